```python
import math
import jax, jax.numpy as jnp
from jax import lax
import numpy as np

D_MODEL = 1024
BATCH = 4
SEQ = 8192
DEPTH = 1

D_SSM = 512
SSM_GROUP = 16
N_SSM_GROUPS = D_SSM // SSM_GROUP
SSM_STATE = 64
N_ATT_HEADS = 8
ATT_HEAD_DIM = 64
D_ATT = N_ATT_HEADS * ATT_HEAD_DIM
D_MIX = D_SSM + D_ATT
D_IN_PROJ = D_SSM + 3 * D_ATT
MOBA_BLOCK = 256
MOBA_TOPK = 3
Q_CHUNK = 32
PEER_HEADS = 8
PEER_KEY_DIM = 256
PEER_HALF = PEER_KEY_DIM // 2
PEER_N_KEYS = 128
PEER_N_EXPERTS = PEER_N_KEYS * PEER_N_KEYS
PEER_TOPK = 16
TOKEN_CHUNK = 128
LN_EPS = 1e-5
DN_ALPHA = (2.0 * DEPTH) ** 0.25
DN_BETA = (8.0 * DEPTH) ** -0.25
NEG = -1e30

kernel_name = "hymba_s5_moba_peer_deepnorm"


def _layernorm(x, g, b):
    xf = x.astype(jnp.float32)
    mu = jnp.mean(xf, axis=-1, keepdims=True)
    var = jnp.mean(jnp.square(xf - mu), axis=-1, keepdims=True)
    y = (xf - mu) * lax.rsqrt(var + LN_EPS) * g.astype(jnp.float32) + b.astype(jnp.float32)
    return y.astype(x.dtype)


def _diag_op(e1, e2):
    a1, b1 = e1
    a2, b2 = e2
    return a1 * a2, a2 * b1 + b2


def _s5(u, a_re, a_im, log_dt, b_re, b_im, c_re, c_im, d_skip, w_glu):
    bsz, L, _ = u.shape
    uf = u.astype(jnp.float32).reshape(bsz, L, N_SSM_GROUPS, SSM_GROUP)
    lam = lax.complex(a_re.astype(jnp.float32), a_im.astype(jnp.float32))
    dt = jnp.exp(log_dt.astype(jnp.float32))[:, None]
    lam_bar = jnp.exp(lam * dt)
    b_c = lax.complex(b_re.astype(jnp.float32), b_im.astype(jnp.float32))
    b_bar = ((lam_bar - 1.0) / lam)[:, :, None] * b_c
    bu = jnp.einsum('blgh,gph->blgp', uf.astype(jnp.complex64), b_bar)
    a = jnp.broadcast_to(lam_bar, (1, L) + lam_bar.shape)
    _, states = lax.associative_scan(_diag_op, (a, bu), axis=1)
    c_c = lax.complex(c_re.astype(jnp.float32), c_im.astype(jnp.float32))
    y = jnp.real(jnp.einsum('blgp,ghp->blgh', states, c_c)) + d_skip.astype(jnp.float32) * uf
    y = jax.nn.gelu(y.reshape(bsz, L, D_SSM))
    y = y * jax.nn.sigmoid(y @ w_glu.astype(jnp.float32))
    return y.astype(u.dtype)


def _moba(q, k, v):
    bsz, L, _ = q.shape
    n_blk = -(-L // MOBA_BLOCK)
    lp = n_blk * MOBA_BLOCK
    pad = lp - L

    def heads(t):
        t = jnp.pad(t, ((0, 0), (0, pad), (0, 0)))
        return t.reshape(bsz, lp, N_ATT_HEADS, ATT_HEAD_DIM).transpose(0, 2, 1, 3)

    qh, kh, vh = heads(q), heads(k), heads(v)
    kb = kh.reshape(bsz, N_ATT_HEADS, n_blk, MOBA_BLOCK, ATT_HEAD_DIM)
    vb = vh.reshape(bsz, N_ATT_HEADS, n_blk, MOBA_BLOCK, ATT_HEAD_DIM)
    kmean = jnp.mean(kb.astype(jnp.float32), axis=3)
    n_sel = min(MOBA_TOPK, n_blk)
    n_chunks = lp // Q_CHUNK
    qc = qh.reshape(bsz, N_ATT_HEADS, n_chunks, Q_CHUNK, ATT_HEAD_DIM).transpose(2, 0, 1, 3, 4)
    bi = jnp.arange(bsz)[:, None, None, None]
    hi = jnp.arange(N_ATT_HEADS)[None, :, None, None]
    blk_ids = jnp.arange(n_blk)
    scale = ATT_HEAD_DIM ** -0.5

    def chunk(args):
        c, qq = args
        start = c * Q_CHUNK
        own = start // MOBA_BLOCK
        q_pos = start + jnp.arange(Q_CHUNK)
        qf = qq.astype(jnp.float32)
        gate = jnp.einsum('bhqd,bhnd->bhqn', qf, kmean)
        gate = jnp.where(blk_ids < own, gate, NEG)
        _, sel = lax.top_k(gate, n_sel)
        valid = sel < own
        ks = kb[bi, hi, sel].astype(jnp.float32)
        vs = vb[bi, hi, sel].astype(jnp.float32)
        s_sel = jnp.einsum('bhqd,bhqskd->bhqsk', qf, ks) * scale
        s_sel = jnp.where(valid[..., None], s_sel, NEG)
        s_sel = s_sel.reshape(bsz, N_ATT_HEADS, Q_CHUNK, n_sel * MOBA_BLOCK)
        ko = lax.dynamic_index_in_dim(kb, own, axis=2, keepdims=False).astype(jnp.float32)
        vo = lax.dynamic_index_in_dim(vb, own, axis=2, keepdims=False).astype(jnp.float32)
        k_pos = own * MOBA_BLOCK + jnp.arange(MOBA_BLOCK)
        s_own = jnp.einsum('bhqd,bhkd->bhqk', qf, ko) * scale
        s_own = jnp.where(k_pos[None, :] <= q_pos[:, None], s_own, NEG)
        p = jax.nn.softmax(jnp.concatenate([s_sel, s_own], axis=-1), axis=-1)
        p_sel = p[..., :n_sel * MOBA_BLOCK].reshape(bsz, N_ATT_HEADS, Q_CHUNK, n_sel, MOBA_BLOCK)
        p_own = p[..., n_sel * MOBA_BLOCK:]
        o = jnp.einsum('bhqsk,bhqskd->bhqd', p_sel, vs) + jnp.einsum('bhqk,bhkd->bhqd', p_own, vo)
        return o.astype(qq.dtype)

    out = lax.map(chunk, (jnp.arange(n_chunks), qc))
    out = out.transpose(1, 2, 0, 3, 4).reshape(bsz, N_ATT_HEADS, lp, ATT_HEAD_DIM)[:, :, :L]
    return out.transpose(0, 2, 1, 3).reshape(bsz, L, D_ATT)


def _hybrid_mixer(h, w_in, a_re, a_im, log_dt, b_re, b_im, c_re, c_im, d_skip, w_glu, w_out):
    proj = h @ w_in
    u = proj[..., :D_SSM]
    q = proj[..., D_SSM:D_SSM + D_ATT]
    k = proj[..., D_SSM + D_ATT:D_SSM + 2 * D_ATT]
    v = proj[..., D_SSM + 2 * D_ATT:]
    y_ssm = _s5(u, a_re, a_im, log_dt, b_re, b_im, c_re, c_im, d_skip, w_glu)
    y_att = _moba(q, k, v)
    return jnp.concatenate([y_ssm, y_att], axis=-1) @ w_out


def _peer(h, w_q, sub_keys, expert_u, expert_v):
    bsz, L, d = h.shape
    n_tok = bsz * L
    xt = h.reshape(n_tok // TOKEN_CHUNK, TOKEN_CHUNK, d)

    def chunk(xc):
        q = (xc @ w_q).astype(jnp.float32).reshape(TOKEN_CHUNK, PEER_HEADS, 2, PEER_HALF)
        s = jnp.einsum('thcd,hcnd->thcn', q, sub_keys.astype(jnp.float32))
        s1, i1 = lax.top_k(s[:, :, 0], PEER_TOPK)
        s2, i2 = lax.top_k(s[:, :, 1], PEER_TOPK)
        cand = (s1[..., :, None] + s2[..., None, :]).reshape(TOKEN_CHUNK, PEER_HEADS, PEER_TOPK * PEER_TOPK)
        cidx = (i1[..., :, None] * PEER_N_KEYS + i2[..., None, :]).reshape(TOKEN_CHUNK, PEER_HEADS, PEER_TOPK * PEER_TOPK)
        top, pos = lax.top_k(cand, PEER_TOPK)
        eidx = jnp.take_along_axis(cidx, pos, axis=-1)
        g = jax.nn.softmax(top, axis=-1)
        u = expert_u[eidx].astype(jnp.float32)
        act = jax.nn.gelu(jnp.einsum('td,thkd->thk', xc.astype(jnp.float32), u))
        out = jnp.einsum('thk,thkd->td', g * act, expert_v[eidx].astype(jnp.float32))
        return out.astype(xc.dtype)

    return lax.map(chunk, xt).reshape(bsz, L, d)


def setup_inputs(seed: int = 0) -> dict:
    key = jax.random.key(seed)
    ks = jax.random.split(key, 20)
    f32 = jnp.float32
    G, P, H = N_SSM_GROUPS, SSM_STATE, SSM_GROUP
    x = jax.random.normal(ks[0], (BATCH, SEQ, D_MODEL), f32)
    w_in = jax.random.normal(ks[1], (DEPTH, D_MODEL, D_IN_PROJ), f32) * D_MODEL ** -0.5
    ssm_a_re = -0.5 + 0.01 * jax.random.normal(ks[2], (DEPTH, G, P), f32)
    ssm_a_im = jnp.pi * jnp.arange(P, dtype=f32)[None, None, :] + 0.01 * jax.random.normal(ks[3], (DEPTH, G, P), f32)
    ssm_log_dt = jax.random.uniform(ks[4], (DEPTH, G), f32, math.log(1e-3), math.log(1e-1))
    ssm_b_re = jax.random.normal(ks[5], (DEPTH, G, P, H), f32) * (2.0 * H) ** -0.5
    ssm_b_im = jax.random.normal(ks[6], (DEPTH, G, P, H), f32) * (2.0 * H) ** -0.5
    ssm_c_re = jax.random.normal(ks[7], (DEPTH, G, H, P), f32) * (2.0 * P) ** -0.5
    ssm_c_im = jax.random.normal(ks[8], (DEPTH, G, H, P), f32) * (2.0 * P) ** -0.5
    ssm_d = jax.random.normal(ks[9], (DEPTH, G, H), f32)
    ssm_w_glu = jax.random.normal(ks[10], (DEPTH, D_SSM, D_SSM), f32) * D_SSM ** -0.5
    w_out = jax.random.normal(ks[11], (DEPTH, D_MIX, D_MODEL), f32) * (D_MIX ** -0.5) * DN_BETA
    ln1_g = 1.0 + 0.02 * jax.random.normal(ks[12], (DEPTH, D_MODEL), f32)
    ln1_b = 0.02 * jax.random.normal(ks[13], (DEPTH, D_MODEL), f32)
    peer_w_q = jax.random.normal(ks[14], (DEPTH, D_MODEL, PEER_HEADS * PEER_KEY_DIM), f32) * D_MODEL ** -0.5
    peer_sub_keys = jax.random.normal(ks[15], (DEPTH, PEER_HEADS, 2, PEER_N_KEYS, PEER_HALF), f32) * PEER_HALF ** -0.5
    peer_u = jax.random.normal(ks[16], (DEPTH, PEER_N_EXPERTS, D_MODEL), f32) * D_MODEL ** -0.5
    peer_v = jax.random.normal(ks[17], (DEPTH, PEER_N_EXPERTS, D_MODEL), f32) * DN_BETA * PEER_HEADS ** -0.5
    ln2_g = 1.0 + 0.02 * jax.random.normal(ks[18], (DEPTH, D_MODEL), f32)
    ln2_b = 0.02 * jax.random.normal(ks[19], (DEPTH, D_MODEL), f32)
    return {"x": x, "w_in": w_in, "ssm_a_re": ssm_a_re, "ssm_a_im": ssm_a_im,
            "ssm_log_dt": ssm_log_dt, "ssm_b_re": ssm_b_re, "ssm_b_im": ssm_b_im,
            "ssm_c_re": ssm_c_re, "ssm_c_im": ssm_c_im, "ssm_d": ssm_d,
            "ssm_w_glu": ssm_w_glu, "w_out": w_out, "ln1_g": ln1_g, "ln1_b": ln1_b,
            "peer_w_q": peer_w_q, "peer_sub_keys": peer_sub_keys, "peer_u": peer_u,
            "peer_v": peer_v, "ln2_g": ln2_g, "ln2_b": ln2_b}


def reference(x, w_in, ssm_a_re, ssm_a_im, ssm_log_dt, ssm_b_re, ssm_b_im, ssm_c_re, ssm_c_im,
              ssm_d, ssm_w_glu, w_out, ln1_g, ln1_b, peer_w_q, peer_sub_keys, peer_u, peer_v,
              ln2_g, ln2_b):
    h = x
    for l in range(DEPTH):
        mix = _hybrid_mixer(h, w_in[l], ssm_a_re[l], ssm_a_im[l], ssm_log_dt[l], ssm_b_re[l],
                            ssm_b_im[l], ssm_c_re[l], ssm_c_im[l], ssm_d[l], ssm_w_glu[l], w_out[l])
        h = _layernorm(DN_ALPHA * h + mix, ln1_g[l], ln1_b[l])
        ffn = _peer(h, peer_w_q[l], peer_sub_keys[l], peer_u[l], peer_v[l])
        h = _layernorm(DN_ALPHA * h + ffn, ln2_g[l], ln2_b[l])
    return h
```

```python
import functools
import math

import jax
import jax.numpy as jnp
from jax import lax
from jax.experimental import pallas as pl
from jax.experimental.pallas import tpu as pltpu

f32 = jnp.float32
bf16 = jnp.bfloat16

D_SSM = 512
SSM_GROUP = 16
SSM_STATE = 64
N_ATT_HEADS = 8
ATT_HEAD_DIM = 64
D_ATT = N_ATT_HEADS * ATT_HEAD_DIM
MOBA_BLOCK = 256
MOBA_TOPK = 3
PEER_HEADS = 8
PEER_HALF = 128
PEER_N_KEYS = 128
PEER_TOPK = 16
LN_EPS = 1e-5
DEPTH = 1
DN_ALPHA = (2.0 * DEPTH) ** 0.25
NEG = -1e30

LANES = 128
SUBLANES = 8
VMEM_LIMIT = 48 * 1024 * 1024

TM_PROJ = 512
S5_CHUNK = 8
S5_LANE_GROUPS = LANES // SSM_GROUP
TT_PEER = 512
EB_PEER = 256


def _dot(a, b):
    return jnp.dot(a, b, preferred_element_type=f32)


def _dot_nt(a, b):
    return lax.dot_general(a, b, (((1,), (1,)), ((), ())), preferred_element_type=f32)


def _split_bf16(x):
    hi = x.astype(bf16)
    lo = (x - hi.astype(f32)).astype(bf16)
    return hi, lo


def _layernorm(r, g, b):
    mu = jnp.mean(r, axis=-1, keepdims=True)
    d = r - mu
    var = jnp.mean(d * d, axis=-1, keepdims=True)
    return d * lax.rsqrt(var + LN_EPS) * g + b


def _inproj_kernel(x_ref, w_ref, u_ref, q_ref, k_ref, v_ref, km_ref):
    acc = _dot(x_ref[...].astype(bf16), w_ref[...])
    u_ref[...] = acc[:, :D_SSM]
    q_ref[...] = acc[:, D_SSM:D_SSM + D_ATT].astype(bf16)
    kk = acc[:, D_SSM + D_ATT:D_SSM + 2 * D_ATT]
    k_ref[...] = kk.astype(bf16)
    v_ref[...] = acc[:, D_SSM + 2 * D_ATT:].astype(bf16)
    nb = TM_PROJ // MOBA_BLOCK
    for r in range(nb):
        km_ref[r] = jnp.mean(kk[r * MOBA_BLOCK:(r + 1) * MOBA_BLOCK], axis=0, keepdims=True)


def _inproj(x2, w_in_b):
    n = x2.shape[0]
    d_model = x2.shape[1]
    nb = TM_PROJ // MOBA_BLOCK
    return pl.pallas_call(
        _inproj_kernel,
        grid=(n // TM_PROJ,),
        in_specs=[pl.BlockSpec((TM_PROJ, d_model), lambda i: (i, 0)),
                  pl.BlockSpec(w_in_b.shape, lambda i: (0, 0))],
        out_specs=[pl.BlockSpec((TM_PROJ, D_SSM), lambda i: (i, 0)),
                   pl.BlockSpec((TM_PROJ, D_ATT), lambda i: (i, 0)),
                   pl.BlockSpec((TM_PROJ, D_ATT), lambda i: (i, 0)),
                   pl.BlockSpec((TM_PROJ, D_ATT), lambda i: (i, 0)),
                   pl.BlockSpec((nb, 1, D_ATT), lambda i: (i, 0, 0))],
        out_shape=[jax.ShapeDtypeStruct((n, D_SSM), f32),
                   jax.ShapeDtypeStruct((n, D_ATT), bf16),
                   jax.ShapeDtypeStruct((n, D_ATT), bf16),
                   jax.ShapeDtypeStruct((n, D_ATT), bf16),
                   jax.ShapeDtypeStruct((n // MOBA_BLOCK, 1, D_ATT), f32)],
        compiler_params=pltpu.CompilerParams(dimension_semantics=("parallel",),
                                             vmem_limit_bytes=VMEM_LIMIT),
        name="inproj",
    )(x2, w_in_b)


def _s5_operators(a_re, a_im, log_dt, b_re, b_im, c_re, c_im, d_skip):
    T = S5_CHUNK
    G, P = a_re.shape
    H = SSM_GROUP
    nj = G // S5_LANE_GROUPS
    g8 = S5_LANE_GROUPS
    lam = lax.complex(a_re.astype(f32), a_im.astype(f32))
    dt = jnp.exp(log_dt.astype(f32))[:, None]
    lam_bar = jnp.exp(lam * dt)
    b_c = lax.complex(b_re.astype(f32), b_im.astype(f32))
    b_bar = ((lam_bar - 1.0) / lam)[:, :, None] * b_c
    c_c = lax.complex(c_re.astype(f32), c_im.astype(f32))
    pows = jnp.exp((lam * dt)[None] * jnp.arange(T + 1, dtype=f32)[:, None, None])
    kern = jnp.real(jnp.einsum('ghp,dgp,gpk->dghk', c_c, pows[:T], b_bar))
    sig = jnp.arange(T)[:, None]
    tau = jnp.arange(T)[None, :]
    dd = tau - sig
    kst = jnp.where((dd >= 0)[:, :, None, None, None], kern[jnp.clip(dd, 0, T - 1)], 0.0)
    eye = jnp.eye(g8, dtype=f32)
    kst = kst.reshape(T, T, nj, g8, H, H)
    toep = jnp.einsum('stjahk,ab->jsaktbh', kst, eye).reshape(nj, T * LANES, T * LANES)
    win = pows[T - 1 - jnp.arange(T)][:, :, :, None] * b_bar[None]
    win = jnp.stack([jnp.real(win), jnp.imag(win)], 0).reshape(2, T, nj, g8, P, H)
    m_in = jnp.einsum('rsjapk,ab->jsakrbp', win, eye).reshape(nj, T * LANES, 2 * g8 * P)
    wout = c_c[None] * jnp.transpose(pows[1:T + 1], (0, 1, 2))[:, :, None, :]
    wout = jnp.stack([jnp.real(wout), -jnp.imag(wout)], 0).reshape(2, T, nj, g8, H, P)
    m_out = jnp.einsum('rtjahp,ab->jraptbh', wout, eye).reshape(nj, 2 * g8 * P, T * LANES)
    a_chunk = pows[T].reshape(nj, 1, g8 * P)
    a_ri = jnp.concatenate([jnp.real(a_chunk), jnp.imag(a_chunk)], axis=1)
    d_flat = jnp.tile(d_skip.astype(f32).reshape(nj, 1, LANES), (1, 1, T))
    return toep.astype(bf16), m_in.astype(bf16), m_out.astype(bf16), a_ri, d_flat


def _s5_kernel(u_ref, toep_ref, min_ref, mout_ref, a_ref, d_ref, y_ref, in_scr, sp_scr):
    T = S5_CHUNK
    n_c = u_ref.shape[0] // T
    half = in_scr.shape[1] // 2
    uf = jnp.concatenate([u_ref[pl.ds(t, n_c, stride=T), :] for t in range(T)], axis=1)
    ufb = uf.astype(bf16)
    y = _dot(ufb, toep_ref[0]) + uf * d_ref[0]
    in_scr[...] = _dot(ufb, min_ref[0])
    ar = a_ref[0, 0:1, :]
    ai = a_ref[0, 1:2, :]

    def step(c, carry):
        sr, si = carry
        sp_scr[pl.ds(c, 1), :half] = sr
        sp_scr[pl.ds(c, 1), half:] = si
        ir = in_scr[pl.ds(c, 1), :half]
        ii = in_scr[pl.ds(c, 1), half:]
        return ar * sr - ai * si + ir, ar * si + ai * sr + ii

    z = jnp.zeros((1, half), f32)
    lax.fori_loop(0, n_c, step, (z, z), unroll=4)
    y = y + _dot(sp_scr[...].astype(bf16), mout_ref[0])
    yg = jax.nn.gelu(y)
    for t in range(T):
        y_ref[pl.ds(t, n_c, stride=T), :] = yg[:, t * LANES:(t + 1) * LANES]


def _s5(u2, bsz, seq, ops):
    toep, m_in, m_out, a_ri, d_flat = ops
    nj = toep.shape[0]
    n_c = seq // S5_CHUNK
    kdim = S5_CHUNK * LANES
    sdim = m_in.shape[2]
    return pl.pallas_call(
        _s5_kernel,
        grid=(nj, bsz),
        in_specs=[pl.BlockSpec((seq, LANES), lambda j, b: (b, j)),
                  pl.BlockSpec((1, kdim, kdim), lambda j, b: (j, 0, 0)),
                  pl.BlockSpec((1, kdim, sdim), lambda j, b: (j, 0, 0)),
                  pl.BlockSpec((1, sdim, kdim), lambda j, b: (j, 0, 0)),
                  pl.BlockSpec((1, 2, sdim // 2), lambda j, b: (j, 0, 0)),
                  pl.BlockSpec((1, 1, kdim), lambda j, b: (j, 0, 0))],
        out_specs=pl.BlockSpec((seq, LANES), lambda j, b: (b, j)),
        out_shape=jax.ShapeDtypeStruct(u2.shape, f32),
        scratch_shapes=[pltpu.VMEM((n_c, sdim), f32), pltpu.VMEM((n_c, sdim), f32)],
        compiler_params=pltpu.CompilerParams(dimension_semantics=("parallel", "parallel"),
                                             vmem_limit_bytes=VMEM_LIMIT),
        name="s5",
    )(u2, toep, m_in, m_out, a_ri, d_flat)


def _moba_kernel(q_ref, k_ref, v_ref, km_ref, o_ref, kaug_ref):
    blk = MOBA_BLOCK
    hd = ATT_HEAD_DIM
    nb = k_ref.shape[0] // blk
    i = pl.program_id(2)
    lane = lax.broadcasted_iota(jnp.int32, (blk, LANES), 1)
    lane_f = lane.astype(f32)

    @pl.when(i == 0)
    def _build():
        def build(j, carry):
            rows = pl.ds(pl.multiple_of(j * blk, blk), blk)
            kk = k_ref[rows, :].astype(f32)
            kaug_ref[0, rows, :] = jnp.where(lane < hd, kk, jnp.where(lane == hd + j, 1.0, 0.0)).astype(bf16)
            kaug_ref[1, rows, :] = jnp.where(lane >= hd, kk, jnp.where(lane == j, 1.0, 0.0)).astype(bf16)
            return carry
        lax.fori_loop(0, nb, build, 0)

    qb = q_ref[...]
    qf = qb.astype(f32) * (hd ** -0.5)
    km = km_ref[0]
    lane_nb = lax.broadcasted_iota(jnp.int32, (nb, LANES), 1)
    row = lax.broadcasted_iota(jnp.int32, (blk, blk), 0)
    col = lax.broadcasted_iota(jnp.int32, (blk, blk), 1)
    own_rows = pl.ds(pl.multiple_of(i * blk, blk), blk)
    outs = []
    for hh in range(2):
        fmask = (lane < hd) if hh == 0 else (lane >= hd)
        off = hd if hh == 0 else 0
        kmh = jnp.where((lane_nb < hd) if hh == 0 else (lane_nb >= hd), km, 0.0)
        pieces = []
        if off:
            pieces.append(jnp.zeros((off, LANES), f32))
        pieces.append(kmh)
        if LANES - off - nb:
            pieces.append(jnp.zeros((LANES - off - nb, LANES), f32))
        rhs_hi, rhs_lo = _split_bf16(jnp.concatenate(pieces, axis=0))
        gate = _dot_nt(qb, rhs_hi) + _dot_nt(qb, rhs_lo)
        jj = lane - off
        g = jnp.where((jj >= 0) & (jj < i), gate, -jnp.inf)
        sel = jj == i
        for _ in range(MOBA_TOPK):
            mx = jnp.max(g, axis=1, keepdims=True)
            first = jnp.min(jnp.where(g == mx, lane_f, float(LANES)), axis=1, keepdims=True)
            pick = (lane_f == first) & (mx > -jnp.inf)
            sel = sel | pick
            g = jnp.where(pick, -jnp.inf, g)
        bias = jnp.where(sel, 0.0, NEG)
        qa = jnp.where(fmask, qf, jnp.where((jj >= 0) & (jj < nb), bias, 0.0)).astype(bf16)

        def body(j, carry):
            m, l, acc = carry
            rows = pl.ds(pl.multiple_of(j * blk, blk), blk)
            s = _dot_nt(qa, kaug_ref[hh, rows, :])
            m_new = jnp.maximum(m, jnp.max(s, axis=1, keepdims=True))
            alpha = jnp.exp(m - m_new)
            p = jnp.exp(s - m_new)
            l = alpha * l + jnp.sum(p, axis=1, keepdims=True)
            acc = alpha * acc + _dot(p.astype(bf16), v_ref[rows, :])
            return m_new, l, acc

        init = (jnp.full((blk, 1), -jnp.inf, f32), jnp.zeros((blk, 1), f32), jnp.zeros((blk, LANES), f32))
        m, l, acc = lax.fori_loop(0, i, body, init)
        s = _dot_nt(qa, kaug_ref[hh, own_rows, :])
        s = jnp.where(col <= row, s, NEG)
        m_new = jnp.maximum(m, jnp.max(s, axis=1, keepdims=True))
        alpha = jnp.exp(m - m_new)
        p = jnp.exp(s - m_new)
        l = alpha * l + jnp.sum(p, axis=1, keepdims=True)
        acc = alpha * acc + _dot(p.astype(bf16), v_ref[own_rows, :])
        outs.append(acc / l)
    o_ref[...] = jnp.where(lane < hd, outs[0], outs[1]).astype(o_ref.dtype)


def _moba(q2, k2, v2, kmean, bsz, seq):
    nb = seq // MOBA_BLOCK
    assert seq % MOBA_BLOCK == 0 and nb % SUBLANES == 0 and nb <= LANES - ATT_HEAD_DIM
    npair = D_ATT // LANES
    return pl.pallas_call(
        _moba_kernel,
        grid=(bsz, npair, nb),
        in_specs=[pl.BlockSpec((MOBA_BLOCK, LANES), lambda b, h, i: (b * nb + i, h)),
                  pl.BlockSpec((seq, LANES), lambda b, h, i: (b, h)),
                  pl.BlockSpec((seq, LANES), lambda b, h, i: (b, h)),
                  pl.BlockSpec((1, nb, LANES), lambda b, h, i: (b, 0, h))],
        out_specs=pl.BlockSpec((MOBA_BLOCK, LANES), lambda b, h, i: (b * nb + i, h)),
        out_shape=jax.ShapeDtypeStruct(q2.shape, bf16),
        scratch_shapes=[pltpu.VMEM((2, seq, LANES), bf16)],
        compiler_params=pltpu.CompilerParams(dimension_semantics=("parallel", "parallel", "arbitrary"),
                                             vmem_limit_bytes=VMEM_LIMIT),
        name="moba",
    )(q2, k2, v2, kmean)


def _outproj_kernel(x_ref, yg_ref, att_ref, wglu_ref, wout_ref, g_ref, b_ref, h_ref):
    yg = yg_ref[...]
    z = yg * jax.nn.sigmoid(_dot(yg.astype(bf16), wglu_ref[...]))
    cat = jnp.concatenate([z.astype(bf16), att_ref[...]], axis=1)
    r = DN_ALPHA * x_ref[...] + _dot(cat, wout_ref[...])
    h_ref[...] = _layernorm(r, g_ref[...], b_ref[...])


def _outproj(x2, yg, att, wglu_b, wout_b, g, b):
    n, d_model = x2.shape
    return pl.pallas_call(
        _outproj_kernel,
        grid=(n // TM_PROJ,),
        in_specs=[pl.BlockSpec((TM_PROJ, d_model), lambda i: (i, 0)),
                  pl.BlockSpec((TM_PROJ, D_SSM), lambda i: (i, 0)),
                  pl.BlockSpec((TM_PROJ, D_ATT), lambda i: (i, 0)),
                  pl.BlockSpec(wglu_b.shape, lambda i: (0, 0)),
                  pl.BlockSpec(wout_b.shape, lambda i: (0, 0)),
                  pl.BlockSpec((1, d_model), lambda i: (0, 0)),
                  pl.BlockSpec((1, d_model), lambda i: (0, 0))],
        out_specs=pl.BlockSpec((TM_PROJ, d_model), lambda i: (i, 0)),
        out_shape=jax.ShapeDtypeStruct((n, d_model), f32),
        compiler_params=pltpu.CompilerParams(dimension_semantics=("parallel",),
                                             vmem_limit_bytes=VMEM_LIMIT),
        name="outproj",
    )(x2, yg, att, wglu_b, wout_b, g, b)


def _bitonic_sort_desc(v):
    n = len(v)
    k = 2
    while k <= n:
        j = k // 2
        while j >= 1:
            for a in range(n):
                b = a ^ j
                if b > a:
                    hi = jnp.maximum(v[a], v[b])
                    lo = jnp.minimum(v[a], v[b])
                    v[a], v[b] = (hi, lo) if (a & k) == 0 else (lo, hi)
            j //= 2
        k *= 2
    return v


def _bitonic_merge_desc(v):
    j = len(v) // 2
    while j >= 1:
        for a in range(len(v)):
            b = a ^ j
            if b > a:
                v[a], v[b] = jnp.maximum(v[a], v[b]), jnp.minimum(v[a], v[b])
        j //= 2
    return v


def _top16_sorted(st):
    k = PEER_TOPK
    x = _bitonic_sort_desc([st[SUBLANES * a:SUBLANES * (a + 1), :] for a in range(PEER_N_KEYS // SUBLANES)])
    for shift in (4, 2, 1):
        y = [pltpu.roll(x[k - 1 - a], shift, 0) for a in range(k)]
        x = _bitonic_merge_desc([jnp.maximum(x[a], y[a]) for a in range(k)])
    return x


def _peer_kernel(h_ref, wq_ref, sk_ref, u_ref, vt_ref, g_ref, b_ref, o_ref,
                 q_scr, ab_scr, pt_scr, bq_scr, srt_scr, row_scr, acc_scr):
    eb = pl.program_id(1)
    n_eb = pl.num_programs(1)
    tt = h_ref.shape[0]
    nk = PEER_N_KEYS
    k = PEER_TOPK

    @pl.when(eb == 0)
    def _prologue():
        q = _dot(h_ref[...].astype(bf16), wq_ref[...])
        for hc in range(2 * PEER_HEADS):
            q_scr[hc] = q[:, hc * PEER_HALF:(hc + 1) * PEER_HALF]

        def head_sort(h, carry):
            for c in range(2):
                q_hi, q_lo = _split_bf16(q_scr[2 * h + c])
                k_hi, k_lo = _split_bf16(sk_ref[2 * h + c])
                st = _dot_nt(k_hi, q_hi) + _dot_nt(k_hi, q_lo) + _dot_nt(k_lo, q_hi)
                ab_scr[c, h] = st
                srt = _top16_sorted(st)
                for a in range(k):
                    srt_scr[c, a, pl.ds(h, 1), :] = srt[a][0:1, :]
            return carry
        lax.fori_loop(0, PEER_HEADS, head_sort, 0)

        a_s = [srt_scr[0, a] for a in range(k)]
        b_s = [srt_scr[1, a] for a in range(k)]
        cands = [a_s[a] + b_s[b] for a in range(k) for b in range(k) if (a + 1) * (b + 1) <= k]
        cands += [jnp.full_like(cands[0], -jnp.inf)] * (64 - len(cands))
        cs = _bitonic_sort_desc(cands)
        zsum = jnp.zeros_like(cs[0])
        for a in range(k):
            zsum = zsum + jnp.exp(cs[a] - cs[0])
        row_scr[0] = 0.5 * (cs[k - 1] + cs[k])
        row_scr[1] = 1.0 / zsum
        row_scr[2] = a_s[0]
        row_scr[3] = b_s[0]
        row_scr[4] = a_s[k - 1]
        row_scr[5] = b_s[k - 1]

        def head_prep(h, carry):
            hrow = pl.ds(h, 1)
            tau = row_scr[0, hrow, :]
            at = ab_scr[0, h]
            bt = ab_scr[1, h]
            pt_scr[0, h] = jnp.where(at >= row_scr[4, hrow, :],
                                     jnp.exp(at - row_scr[2, hrow, :]), 0.0) * row_scr[1, hrow, :]
            pt_scr[1, h] = tau - at
            bq_scr[0, h] = bt
            bq_scr[1, h] = jnp.where(bt >= row_scr[5, hrow, :], jnp.exp(bt - row_scr[3, hrow, :]), 0.0)
            return carry
        lax.fori_loop(0, PEER_HEADS, head_prep, 0)
        acc_scr[...] = jnp.zeros_like(acc_scr)

    hb = h_ref[...].astype(bf16)
    act = jax.nn.gelu(_dot_nt(u_ref[...], hb))
    halves = []
    for half in range(EB_PEER // nk):
        n1 = pl.ds((EB_PEER // nk) * eb + half, 1)
        g = jnp.zeros((nk, tt), f32)
        for h in range(PEER_HEADS):
            p_row = pt_scr[0, h, n1, :]
            thr_row = pt_scr[1, h, n1, :]
            g = g + p_row * jnp.where(bq_scr[0, h] >= thr_row, bq_scr[1, h], 0.0)
        halves.append(g)
    w = (jnp.concatenate(halves, axis=0) * act).astype(bf16)
    acc_scr[...] += _dot(vt_ref[...], w)

    @pl.when(eb == n_eb - 1)
    def _epilogue():
        r = DN_ALPHA * h_ref[...] + acc_scr[...].T
        o_ref[...] = _layernorm(r, g_ref[...], b_ref[...])


def _peer(h1, wq_b, sub_keys, u_b, vt_b, g, b):
    n, d_model = h1.shape
    n_exp = u_b.shape[0]
    tt = TT_PEER
    sk = sub_keys.reshape(2 * PEER_HEADS, PEER_N_KEYS, PEER_HALF).astype(f32)
    return pl.pallas_call(
        _peer_kernel,
        grid=(n // tt, n_exp // EB_PEER),
        in_specs=[pl.BlockSpec((tt, d_model), lambda t, e: (t, 0)),
                  pl.BlockSpec(wq_b.shape, lambda t, e: (0, 0)),
                  pl.BlockSpec(sk.shape, lambda t, e: (0, 0, 0)),
                  pl.BlockSpec((EB_PEER, d_model), lambda t, e: (e, 0)),
                  pl.BlockSpec((d_model, EB_PEER), lambda t, e: (0, e)),
                  pl.BlockSpec((1, d_model), lambda t, e: (0, 0)),
                  pl.BlockSpec((1, d_model), lambda t, e: (0, 0))],
        out_specs=pl.BlockSpec((tt, d_model), lambda t, e: (t, 0)),
        out_shape=jax.ShapeDtypeStruct((n, d_model), f32),
        scratch_shapes=[pltpu.VMEM((2 * PEER_HEADS, tt, PEER_HALF), f32),
                        pltpu.VMEM((2, PEER_HEADS, PEER_N_KEYS, tt), f32),
                        pltpu.VMEM((2, PEER_HEADS, PEER_N_KEYS, tt), f32),
                        pltpu.VMEM((2, PEER_HEADS, PEER_N_KEYS, tt), f32),
                        pltpu.VMEM((2, PEER_TOPK, PEER_HEADS, tt), f32),
                        pltpu.VMEM((6, PEER_HEADS, tt), f32),
                        pltpu.VMEM((d_model, tt), f32)],
        compiler_params=pltpu.CompilerParams(dimension_semantics=("parallel", "arbitrary"),
                                             vmem_limit_bytes=VMEM_LIMIT),
        name="peer",
    )(h1, wq_b, sk, u_b, vt_b, g, b)


def kernel(x, w_in, ssm_a_re, ssm_a_im, ssm_log_dt, ssm_b_re, ssm_b_im, ssm_c_re, ssm_c_im, ssm_d,
           ssm_w_glu, w_out, ln1_g, ln1_b, peer_w_q, peer_sub_keys, peer_u, peer_v, ln2_g, ln2_b):
    bsz, seq, d_model = x.shape
    n = bsz * seq
    h = x.reshape(n, d_model)
    for l in range(w_in.shape[0]):
        u2, q2, k2, v2, kmean = _inproj(h, w_in[l].astype(bf16))
        ops = _s5_operators(ssm_a_re[l], ssm_a_im[l], ssm_log_dt[l], ssm_b_re[l], ssm_b_im[l],
                            ssm_c_re[l], ssm_c_im[l], ssm_d[l])
        yg = _s5(u2, bsz, seq, ops)
        att = _moba(q2, k2, v2, kmean.reshape(bsz, seq // MOBA_BLOCK, D_ATT), bsz, seq)
        h1 = _outproj(h, yg, att, ssm_w_glu[l].astype(bf16), w_out[l].astype(bf16),
                      ln1_g[l].reshape(1, d_model), ln1_b[l].reshape(1, d_model))
        h = _peer(h1, peer_w_q[l].astype(bf16), peer_sub_keys[l], peer_u[l].astype(bf16),
                  peer_v[l].T.astype(bf16), ln2_g[l].reshape(1, d_model), ln2_b[l].reshape(1, d_model))
    return h.reshape(bsz, seq, d_model)
```

```python
import functools
import math

import jax
import jax.numpy as jnp
from jax import lax
from jax.experimental import pallas as pl
from jax.experimental.pallas import tpu as pltpu

f32 = jnp.float32
bf16 = jnp.bfloat16

D_SSM = 512
SSM_GROUP = 16
SSM_STATE = 64
N_ATT_HEADS = 8
ATT_HEAD_DIM = 64
D_ATT = N_ATT_HEADS * ATT_HEAD_DIM
MOBA_BLOCK = 256
MOBA_TOPK = 3
PEER_HEADS = 8
PEER_HALF = 128
PEER_N_KEYS = 128
PEER_TOPK = 16
LN_EPS = 1e-5
DEPTH = 1
DN_ALPHA = (2.0 * DEPTH) ** 0.25
NEG = -1e30

LANES = 128
SUBLANES = 8
VMEM_LIMIT = 48 * 1024 * 1024

TM_PROJ = 512
S5_CHUNK = 8
S5_LANE_GROUPS = LANES // SSM_GROUP
TT_PEER = 512
EB_PEER = 1024
PEER_SUB = 256


def _dot(a, b):
    return jnp.dot(a, b, preferred_element_type=f32)


def _dot_nt(a, b):
    return lax.dot_general(a, b, (((1,), (1,)), ((), ())), preferred_element_type=f32)


def _split_bf16(x):
    hi = x.astype(bf16)
    lo = (x - hi.astype(f32)).astype(bf16)
    return hi, lo


def _layernorm(r, g, b):
    mu = jnp.mean(r, axis=-1, keepdims=True)
    d = r - mu
    var = jnp.mean(d * d, axis=-1, keepdims=True)
    return d * lax.rsqrt(var + LN_EPS) * g + b


def _inproj_kernel(x_ref, wuk_ref, wqvt_ref, u_ref, k_ref, qt_ref, vt_ref, km_ref):
    xb = x_ref[...].astype(bf16)
    acc = _dot(xb, wuk_ref[...])
    u_ref[...] = acc[:, :D_SSM]
    kk = acc[:, D_SSM:]
    k_ref[...] = kk.astype(bf16)
    qvt = _dot_nt(wqvt_ref[...], xb)
    for r in range(TM_PROJ // MOBA_BLOCK):
        cols = slice(r * MOBA_BLOCK, (r + 1) * MOBA_BLOCK)
        km_ref[r] = jnp.mean(kk[cols], axis=0, keepdims=True)
        qt_ref[r] = qvt[:D_ATT, cols].astype(bf16)
        vt_ref[r] = qvt[D_ATT:, cols].astype(bf16)


def _inproj(x2, w_in):
    n, d_model = x2.shape
    nb = TM_PROJ // MOBA_BLOCK
    w_uk = jnp.concatenate([w_in[:, :D_SSM], w_in[:, D_SSM + D_ATT:D_SSM + 2 * D_ATT]], axis=1).astype(bf16)
    w_qvt = jnp.concatenate([w_in[:, D_SSM:D_SSM + D_ATT], w_in[:, D_SSM + 2 * D_ATT:]], axis=1).T.astype(bf16)
    return pl.pallas_call(
        _inproj_kernel,
        grid=(n // TM_PROJ,),
        in_specs=[pl.BlockSpec((TM_PROJ, d_model), lambda i: (i, 0)),
                  pl.BlockSpec(w_uk.shape, lambda i: (0, 0)),
                  pl.BlockSpec(w_qvt.shape, lambda i: (0, 0))],
        out_specs=[pl.BlockSpec((TM_PROJ, D_SSM), lambda i: (i, 0)),
                   pl.BlockSpec((TM_PROJ, D_ATT), lambda i: (i, 0)),
                   pl.BlockSpec((nb, D_ATT, MOBA_BLOCK), lambda i: (i, 0, 0)),
                   pl.BlockSpec((nb, D_ATT, MOBA_BLOCK), lambda i: (i, 0, 0)),
                   pl.BlockSpec((nb, 1, D_ATT), lambda i: (i, 0, 0))],
        out_shape=[jax.ShapeDtypeStruct((n, D_SSM), f32),
                   jax.ShapeDtypeStruct((n, D_ATT), bf16),
                   jax.ShapeDtypeStruct((n // MOBA_BLOCK, D_ATT, MOBA_BLOCK), bf16),
                   jax.ShapeDtypeStruct((n // MOBA_BLOCK, D_ATT, MOBA_BLOCK), bf16),
                   jax.ShapeDtypeStruct((n // MOBA_BLOCK, 1, D_ATT), f32)],
        compiler_params=pltpu.CompilerParams(dimension_semantics=("parallel",),
                                             vmem_limit_bytes=VMEM_LIMIT),
        name="inproj",
    )(x2, w_uk, w_qvt)


def _s5_operators(a_re, a_im, log_dt, b_re, b_im, c_re, c_im, d_skip):
    T = S5_CHUNK
    G, P = a_re.shape
    H = SSM_GROUP
    nj = G // S5_LANE_GROUPS
    g8 = S5_LANE_GROUPS
    lam = lax.complex(a_re.astype(f32), a_im.astype(f32))
    dt = jnp.exp(log_dt.astype(f32))[:, None]
    lam_bar = jnp.exp(lam * dt)
    b_c = lax.complex(b_re.astype(f32), b_im.astype(f32))
    b_bar = ((lam_bar - 1.0) / lam)[:, :, None] * b_c
    c_c = lax.complex(c_re.astype(f32), c_im.astype(f32))
    pows = jnp.exp((lam * dt)[None] * jnp.arange(T + 1, dtype=f32)[:, None, None])
    kern = jnp.real(jnp.einsum('ghp,dgp,gpk->dghk', c_c, pows[:T], b_bar))
    sig = jnp.arange(T)[:, None]
    tau = jnp.arange(T)[None, :]
    dd = tau - sig
    kst = jnp.where((dd >= 0)[:, :, None, None, None], kern[jnp.clip(dd, 0, T - 1)], 0.0)
    eye = jnp.eye(g8, dtype=f32)
    kst = kst.reshape(T, T, nj, g8, H, H)
    toep = jnp.einsum('stjahk,ab->jsaktbh', kst, eye).reshape(nj, T * LANES, T * LANES)
    win = pows[T - 1 - jnp.arange(T)][:, :, :, None] * b_bar[None]
    win = jnp.stack([jnp.real(win), jnp.imag(win)], 0).reshape(2, T, nj, g8, P, H)
    m_in = jnp.einsum('rsjapk,ab->jsakrbp', win, eye).reshape(nj, T * LANES, 2 * g8 * P)
    wout = c_c[None] * jnp.transpose(pows[1:T + 1], (0, 1, 2))[:, :, None, :]
    wout = jnp.stack([jnp.real(wout), -jnp.imag(wout)], 0).reshape(2, T, nj, g8, H, P)
    m_out = jnp.einsum('rtjahp,ab->jraptbh', wout, eye).reshape(nj, 2 * g8 * P, T * LANES)
    a_chunk = pows[T].reshape(nj, 1, g8 * P)
    a_ri = jnp.concatenate([jnp.real(a_chunk), jnp.imag(a_chunk)], axis=1)
    d_flat = jnp.tile(d_skip.astype(f32).reshape(nj, 1, LANES), (1, 1, T))
    return toep.astype(bf16), m_in.astype(bf16), m_out.astype(bf16), a_ri, d_flat


def _s5_kernel(u_ref, toep_ref, min_ref, mout_ref, a_ref, d_ref, y_ref, in_scr, sp_scr):
    T = S5_CHUNK
    n_c = u_ref.shape[0] // T
    half = in_scr.shape[1] // 2
    uf = jnp.concatenate([u_ref[pl.ds(t, n_c, stride=T), :] for t in range(T)], axis=1)
    ufb = uf.astype(bf16)
    y = _dot(ufb, toep_ref[0]) + uf * d_ref[0]
    in_scr[...] = _dot(ufb, min_ref[0])
    ar = a_ref[0, 0:1, :]
    ai = a_ref[0, 1:2, :]

    def step(c, carry):
        sr, si = carry
        sp_scr[pl.ds(c, 1), :half] = sr
        sp_scr[pl.ds(c, 1), half:] = si
        ir = in_scr[pl.ds(c, 1), :half]
        ii = in_scr[pl.ds(c, 1), half:]
        return ar * sr - ai * si + ir, ar * si + ai * sr + ii

    z = jnp.zeros((1, half), f32)
    lax.fori_loop(0, n_c, step, (z, z), unroll=4)
    y = y + _dot(sp_scr[...].astype(bf16), mout_ref[0])
    yg = jax.nn.gelu(y)
    for t in range(T):
        y_ref[pl.ds(t, n_c, stride=T), :] = yg[:, t * LANES:(t + 1) * LANES]


def _s5(u2, bsz, seq, ops):
    toep, m_in, m_out, a_ri, d_flat = ops
    nj = toep.shape[0]
    n_c = seq // S5_CHUNK
    kdim = S5_CHUNK * LANES
    sdim = m_in.shape[2]
    return pl.pallas_call(
        _s5_kernel,
        grid=(nj, bsz),
        in_specs=[pl.BlockSpec((seq, LANES), lambda j, b: (b, j)),
                  pl.BlockSpec((1, kdim, kdim), lambda j, b: (j, 0, 0)),
                  pl.BlockSpec((1, kdim, sdim), lambda j, b: (j, 0, 0)),
                  pl.BlockSpec((1, sdim, kdim), lambda j, b: (j, 0, 0)),
                  pl.BlockSpec((1, 2, sdim // 2), lambda j, b: (j, 0, 0)),
                  pl.BlockSpec((1, 1, kdim), lambda j, b: (j, 0, 0))],
        out_specs=pl.BlockSpec((seq, LANES), lambda j, b: (b, j)),
        out_shape=jax.ShapeDtypeStruct(u2.shape, f32),
        scratch_shapes=[pltpu.VMEM((n_c, sdim), f32), pltpu.VMEM((n_c, sdim), f32)],
        compiler_params=pltpu.CompilerParams(dimension_semantics=("parallel", "parallel"),
                                             vmem_limit_bytes=VMEM_LIMIT),
        name="s5",
    )(u2, toep, m_in, m_out, a_ri, d_flat)


def _moba_kernel(qt_ref, k_ref, vt_ref, km_ref, o_ref, kaug_ref, s0_scr, s1_scr):
    blk = MOBA_BLOCK
    hd = ATT_HEAD_DIM
    nb = k_ref.shape[0] // blk
    i = pl.program_id(2)

    @pl.when(i == 0)
    def _build():
        lane = lax.broadcasted_iota(jnp.int32, (blk, LANES), 1)

        def build(j, carry):
            rows = pl.ds(pl.multiple_of(j * blk, blk), blk)
            kaug_ref[rows, :LANES] = k_ref[rows, :]
            kaug_ref[rows, LANES:] = jnp.where(lane == j, 1.0, 0.0).astype(bf16)
            return carry
        lax.fori_loop(0, nb, build, 0)

    qt = qt_ref[0]
    qtf = qt.astype(f32) * (hd ** -0.5)
    km = km_ref[0]
    lane_nb = lax.broadcasted_iota(jnp.int32, (nb, LANES), 1)
    feat = lax.broadcasted_iota(jnp.int32, (LANES, blk), 0)
    blk_id = lax.broadcasted_iota(jnp.int32, (nb, blk), 0)
    blk_f = blk_id.astype(f32)
    q_cols, bias_cols = [], []
    for hh in range(2):
        km_hi, km_lo = _split_bf16(jnp.where((lane_nb < hd) if hh == 0 else (lane_nb >= hd), km, 0.0))
        gate = _dot(km_hi, qt) + _dot(km_lo, qt)
        g = jnp.where(blk_id < i, gate, -jnp.inf)
        sel = blk_id == i
        for _ in range(MOBA_TOPK):
            mx = jnp.max(g, axis=0, keepdims=True)
            first = jnp.min(jnp.where(g == mx, blk_f, float(nb)), axis=0, keepdims=True)
            pick = (blk_f == first) & (mx > -jnp.inf)
            sel = sel | pick
            g = jnp.where(pick, -jnp.inf, g)
        bias_cols.append(jnp.where(sel, 0.0, NEG))
        q_cols.append(jnp.where((feat < hd) if hh == 0 else (feat >= hd), qtf, 0.0))
    qa = jnp.concatenate([jnp.concatenate(q_cols, axis=1), jnp.concatenate(bias_cols, axis=1),
                          jnp.zeros((LANES - nb, 2 * blk), f32)], axis=0).astype(bf16)

    key_pos = lax.broadcasted_iota(jnp.int32, (blk, 2 * blk), 0)
    q_pos = lax.broadcasted_iota(jnp.int32, (blk, 2 * blk), 1) % blk

    def scores(j):
        j = jnp.minimum(j, nb - 1)
        return _dot(kaug_ref[pl.ds(pl.multiple_of(j * blk, blk), blk), :], qa)

    def update(state, blocks):
        m, l, acc = state
        m_new = m
        for _, s in blocks:
            m_new = jnp.maximum(m_new, jnp.max(s, axis=0, keepdims=True))
        alpha = jnp.exp(m - m_new)
        l = alpha * l
        acc = alpha * acc
        for j, s in blocks:
            p = jnp.exp(s - m_new)
            l = l + jnp.sum(p, axis=0, keepdims=True)
            pb = p.astype(bf16)
            acc = acc + jnp.concatenate([_dot(vt_ref[j, :hd, :], pb[:, :blk]),
                                         _dot(vt_ref[j, hd:, :], pb[:, blk:])], axis=1)
        return m_new, l, acc

    def pair_scores(p, dst):
        dst[0] = scores(2 * p)
        dst[1] = scores(2 * p + 1)

    def pair_update(state, p, src, masked):
        blocks = []
        for r in range(2):
            j = 2 * p + r
            s = src[r]
            if masked:
                off = jnp.where(j < i, blk, jnp.where(j == i, 0, -2 * blk))
                s = jnp.where(key_pos <= q_pos + off, s, NEG)
            blocks.append((jnp.minimum(j, nb - 1), s))
        return update(state, blocks)

    def quad_body(q, state):
        pair_scores(2 * q + 1, s1_scr)
        state = pair_update(state, 2 * q, s0_scr, False)
        pair_scores(2 * q + 2, s0_scr)
        return pair_update(state, 2 * q + 1, s1_scr, False)

    state = (jnp.full((1, 2 * blk), -jnp.inf, f32), jnp.zeros((1, 2 * blk), f32), jnp.zeros((hd, 2 * blk), f32))
    n_quad = i // 4
    pair_scores(0, s0_scr)
    state = lax.fori_loop(0, n_quad, quad_body, state)
    pair_scores(2 * n_quad + 1, s1_scr)
    state = pair_update(state, 2 * n_quad, s0_scr, True)
    m, l, acc = pair_update(state, 2 * n_quad + 1, s1_scr, True)
    out = acc / l
    o_ref[...] = jnp.concatenate([out[:, :blk], out[:, blk:]], axis=0).T.astype(o_ref.dtype)


def _moba(qt, k2, vt, kmean, bsz, seq):
    nb = seq // MOBA_BLOCK
    assert seq % MOBA_BLOCK == 0 and nb % SUBLANES == 0 and nb <= LANES
    npair = D_ATT // LANES
    return pl.pallas_call(
        _moba_kernel,
        grid=(bsz, npair, nb),
        in_specs=[pl.BlockSpec((1, LANES, MOBA_BLOCK), lambda b, h, i: (b * nb + i, h, 0)),
                  pl.BlockSpec((seq, LANES), lambda b, h, i: (b, h)),
                  pl.BlockSpec((nb, LANES, MOBA_BLOCK), lambda b, h, i: (b, h, 0)),
                  pl.BlockSpec((1, nb, LANES), lambda b, h, i: (b, 0, h))],
        out_specs=pl.BlockSpec((MOBA_BLOCK, LANES), lambda b, h, i: (b * nb + i, h)),
        out_shape=jax.ShapeDtypeStruct(k2.shape, bf16),
        scratch_shapes=[pltpu.VMEM((seq, 2 * LANES), bf16),
                        pltpu.VMEM((2, MOBA_BLOCK, 2 * MOBA_BLOCK), f32),
                        pltpu.VMEM((2, MOBA_BLOCK, 2 * MOBA_BLOCK), f32)],
        compiler_params=pltpu.CompilerParams(dimension_semantics=("parallel", "parallel", "arbitrary"),
                                             vmem_limit_bytes=VMEM_LIMIT),
        name="moba",
    )(qt, k2, vt, kmean)


def _outproj_kernel(x_ref, yg_ref, att_ref, wglu_ref, wout_ref, g_ref, b_ref, h_ref):
    yg = yg_ref[...]
    z = yg * jax.nn.sigmoid(_dot(yg.astype(bf16), wglu_ref[...]))
    cat = jnp.concatenate([z.astype(bf16), att_ref[...]], axis=1)
    r = DN_ALPHA * x_ref[...] + _dot(cat, wout_ref[...])
    h_ref[...] = _layernorm(r, g_ref[...], b_ref[...])


def _outproj(x2, yg, att, wglu_b, wout_b, g, b):
    n, d_model = x2.shape
    return pl.pallas_call(
        _outproj_kernel,
        grid=(n // TM_PROJ,),
        in_specs=[pl.BlockSpec((TM_PROJ, d_model), lambda i: (i, 0)),
                  pl.BlockSpec((TM_PROJ, D_SSM), lambda i: (i, 0)),
                  pl.BlockSpec((TM_PROJ, D_ATT), lambda i: (i, 0)),
                  pl.BlockSpec(wglu_b.shape, lambda i: (0, 0)),
                  pl.BlockSpec(wout_b.shape, lambda i: (0, 0)),
                  pl.BlockSpec((1, d_model), lambda i: (0, 0)),
                  pl.BlockSpec((1, d_model), lambda i: (0, 0))],
        out_specs=pl.BlockSpec((TM_PROJ, d_model), lambda i: (i, 0)),
        out_shape=jax.ShapeDtypeStruct((n, d_model), f32),
        compiler_params=pltpu.CompilerParams(dimension_semantics=("parallel",),
                                             vmem_limit_bytes=VMEM_LIMIT),
        name="outproj",
    )(x2, yg, att, wglu_b, wout_b, g, b)


def _bitonic_sort_desc(v):
    n = len(v)
    k = 2
    while k <= n:
        j = k // 2
        while j >= 1:
            for a in range(n):
                b = a ^ j
                if b > a:
                    hi = jnp.maximum(v[a], v[b])
                    lo = jnp.minimum(v[a], v[b])
                    v[a], v[b] = (hi, lo) if (a & k) == 0 else (lo, hi)
            j //= 2
        k *= 2
    return v


def _bitonic_merge_desc(v):
    j = len(v) // 2
    while j >= 1:
        for a in range(len(v)):
            b = a ^ j
            if b > a:
                v[a], v[b] = jnp.maximum(v[a], v[b]), jnp.minimum(v[a], v[b])
        j //= 2
    return v


def _top16_sorted(st):
    k = PEER_TOPK
    x = _bitonic_sort_desc([st[SUBLANES * a:SUBLANES * (a + 1), :] for a in range(PEER_N_KEYS // SUBLANES)])
    for shift in (4, 2, 1):
        y = [pltpu.roll(x[k - 1 - a], shift, 0) for a in range(k)]
        x = _bitonic_merge_desc([jnp.maximum(x[a], y[a]) for a in range(k)])
    return x


def _peer_kernel(h_ref, wq_ref, sk_ref, u_ref, vt_ref, g_ref, b_ref, o_ref,
                 q_scr, ab_scr, pt_scr, bq_scr, srt_scr, row_scr, acc_scr, hb_scr, a_scr):
    eb = pl.program_id(1)
    n_eb = pl.num_programs(1)
    tt = h_ref.shape[0]
    nk = PEER_N_KEYS
    k = PEER_TOPK

    @pl.when(eb == 0)
    def _prologue():
        q = _dot(h_ref[...].astype(bf16), wq_ref[...])
        for hc in range(2 * PEER_HEADS):
            q_scr[hc] = q[:, hc * PEER_HALF:(hc + 1) * PEER_HALF]

        def head_sort(h, carry):
            for c in range(2):
                q_hi, q_lo = _split_bf16(q_scr[2 * h + c])
                k_hi, k_lo = _split_bf16(sk_ref[2 * h + c])
                st = _dot_nt(k_hi, q_hi) + _dot_nt(k_hi, q_lo) + _dot_nt(k_lo, q_hi)
                ab_scr[c, h] = st
                srt = _top16_sorted(st)
                for a in range(k):
                    srt_scr[c, a, pl.ds(h, 1), :] = srt[a][0:1, :]
            return carry
        lax.fori_loop(0, PEER_HEADS, head_sort, 0)

        a_s = [srt_scr[0, a] for a in range(k)]
        b_s = [srt_scr[1, a] for a in range(k)]
        cands = [a_s[a] + b_s[b] for a in range(k) for b in range(k) if (a + 1) * (b + 1) <= k]
        cands += [jnp.full_like(cands[0], -jnp.inf)] * (64 - len(cands))
        cs = _bitonic_sort_desc(cands)
        zsum = jnp.zeros_like(cs[0])
        for a in range(k):
            zsum = zsum + jnp.exp(cs[a] - cs[0])
        row_scr[0] = 0.5 * (cs[k - 1] + cs[k])
        row_scr[1] = 1.0 / zsum
        row_scr[2] = a_s[0]
        row_scr[3] = b_s[0]
        row_scr[4] = a_s[k - 1]
        row_scr[5] = b_s[k - 1]

        def head_prep(h, carry):
            hrow = pl.ds(h, 1)
            tau = row_scr[0, hrow, :]
            at = ab_scr[0, h]
            bt = ab_scr[1, h]
            pt_scr[0, h] = jnp.where(at >= row_scr[4, hrow, :],
                                     jnp.exp(at - row_scr[2, hrow, :]), 0.0) * row_scr[1, hrow, :]
            pt_scr[1, h] = tau - at
            bq_scr[0, h] = bt
            bq_scr[1, h] = jnp.where(bt >= row_scr[5, hrow, :], jnp.exp(bt - row_scr[3, hrow, :]), 0.0)
            return carry
        lax.fori_loop(0, PEER_HEADS, head_prep, 0)
        acc_scr[...] = jnp.zeros_like(acc_scr)
        hb_scr[...] = h_ref[...].astype(bf16)

    n_sub = EB_PEER // PEER_SUB
    n_half = PEER_SUB // nk
    hb = hb_scr[...]
    for sub in range(n_sub):
        a_scr[sub] = _dot_nt(u_ref[sub * PEER_SUB:(sub + 1) * PEER_SUB, :], hb)
    for sub in range(n_sub):
        halves = []
        for half in range(n_half):
            n1 = pl.ds((EB_PEER // nk) * eb + sub * n_half + half, 1)
            g = jnp.zeros((nk, tt), f32)
            for h in range(PEER_HEADS):
                g = g + pt_scr[0, h, n1, :] * jnp.where(bq_scr[0, h] >= pt_scr[1, h, n1, :], bq_scr[1, h], 0.0)
            halves.append(g)
        w = (jnp.concatenate(halves, axis=0) * jax.nn.gelu(a_scr[sub])).astype(bf16)
        acc_scr[...] += _dot(vt_ref[:, sub * PEER_SUB:(sub + 1) * PEER_SUB], w)

    @pl.when(eb == n_eb - 1)
    def _epilogue():
        r = DN_ALPHA * h_ref[...] + acc_scr[...].T
        o_ref[...] = _layernorm(r, g_ref[...], b_ref[...])


def _peer(h1, wq_b, sub_keys, u_b, vt_b, g, b):
    n, d_model = h1.shape
    n_exp = u_b.shape[0]
    tt = TT_PEER
    sk = sub_keys.reshape(2 * PEER_HEADS, PEER_N_KEYS, PEER_HALF).astype(f32)
    return pl.pallas_call(
        _peer_kernel,
        grid=(n // tt, n_exp // EB_PEER),
        in_specs=[pl.BlockSpec((tt, d_model), lambda t, e: (t, 0)),
                  pl.BlockSpec(wq_b.shape, lambda t, e: (0, 0)),
                  pl.BlockSpec(sk.shape, lambda t, e: (0, 0, 0)),
                  pl.BlockSpec((EB_PEER, d_model), lambda t, e: (e, 0)),
                  pl.BlockSpec((d_model, EB_PEER), lambda t, e: (0, e)),
                  pl.BlockSpec((1, d_model), lambda t, e: (0, 0)),
                  pl.BlockSpec((1, d_model), lambda t, e: (0, 0))],
        out_specs=pl.BlockSpec((tt, d_model), lambda t, e: (t, 0)),
        out_shape=jax.ShapeDtypeStruct((n, d_model), f32),
        scratch_shapes=[pltpu.VMEM((2 * PEER_HEADS, tt, PEER_HALF), f32),
                        pltpu.VMEM((2, PEER_HEADS, PEER_N_KEYS, tt), f32),
                        pltpu.VMEM((2, PEER_HEADS, PEER_N_KEYS, tt), f32),
                        pltpu.VMEM((2, PEER_HEADS, PEER_N_KEYS, tt), f32),
                        pltpu.VMEM((2, PEER_TOPK, PEER_HEADS, tt), f32),
                        pltpu.VMEM((6, PEER_HEADS, tt), f32),
                        pltpu.VMEM((d_model, tt), f32),
                        pltpu.VMEM((tt, d_model), bf16),
                        pltpu.VMEM((EB_PEER // PEER_SUB, PEER_SUB, tt), f32)],
        compiler_params=pltpu.CompilerParams(dimension_semantics=("parallel", "arbitrary"),
                                             vmem_limit_bytes=VMEM_LIMIT),
        name="peer",
    )(h1, wq_b, sk, u_b, vt_b, g, b)


def kernel(x, w_in, ssm_a_re, ssm_a_im, ssm_log_dt, ssm_b_re, ssm_b_im, ssm_c_re, ssm_c_im, ssm_d,
           ssm_w_glu, w_out, ln1_g, ln1_b, peer_w_q, peer_sub_keys, peer_u, peer_v, ln2_g, ln2_b):
    bsz, seq, d_model = x.shape
    n = bsz * seq
    h = x.reshape(n, d_model)
    for l in range(w_in.shape[0]):
        u2, k2, qt, vt, kmean = _inproj(h, w_in[l])
        ops = _s5_operators(ssm_a_re[l], ssm_a_im[l], ssm_log_dt[l], ssm_b_re[l], ssm_b_im[l],
                            ssm_c_re[l], ssm_c_im[l], ssm_d[l])
        yg = _s5(u2, bsz, seq, ops)
        att = _moba(qt, k2, vt, kmean.reshape(bsz, seq // MOBA_BLOCK, D_ATT), bsz, seq)
        h1 = _outproj(h, yg, att, ssm_w_glu[l].astype(bf16), w_out[l].astype(bf16),
                      ln1_g[l].reshape(1, d_model), ln1_b[l].reshape(1, d_model))
        h = _peer(h1, peer_w_q[l].astype(bf16), peer_sub_keys[l], peer_u[l].astype(bf16),
                  peer_v[l].T.astype(bf16), ln2_g[l].reshape(1, d_model), ln2_b[l].reshape(1, d_model))
    return h.reshape(bsz, seq, d_model)
```

```python
import functools
import math

import jax
import jax.numpy as jnp
from jax import lax
from jax.experimental import pallas as pl
from jax.experimental.pallas import tpu as pltpu

f32 = jnp.float32
bf16 = jnp.bfloat16

D_SSM = 512
SSM_GROUP = 16
SSM_STATE = 64
N_ATT_HEADS = 8
ATT_HEAD_DIM = 64
D_ATT = N_ATT_HEADS * ATT_HEAD_DIM
MOBA_BLOCK = 256
MOBA_TOPK = 3
PEER_HEADS = 8
PEER_HALF = 128
PEER_N_KEYS = 128
PEER_TOPK = 16
LN_EPS = 1e-5
DEPTH = 1
DN_ALPHA = (2.0 * DEPTH) ** 0.25
NEG = -1e30

LANES = 128
SUBLANES = 8
VMEM_LIMIT = 48 * 1024 * 1024

TM_PROJ = 512
S5_CHUNK = 8
S5_LANE_GROUPS = LANES // SSM_GROUP
TT_PEER = 512
EB_PEER = 2048


def _dot(a, b):
    return jnp.dot(a, b, preferred_element_type=f32)


def _dot_nt(a, b):
    return lax.dot_general(a, b, (((1,), (1,)), ((), ())), preferred_element_type=f32)


def _split_bf16(x):
    hi = x.astype(bf16)
    lo = (x - hi.astype(f32)).astype(bf16)
    return hi, lo


def _layernorm(r, g, b):
    mu = jnp.mean(r, axis=-1, keepdims=True)
    d = r - mu
    var = jnp.mean(d * d, axis=-1, keepdims=True)
    return d * lax.rsqrt(var + LN_EPS) * g + b


def _inproj_kernel(x_ref, wuk_ref, wqvt_ref, u_ref, k_ref, qt_ref, vt_ref, km_ref):
    xb = x_ref[...].astype(bf16)
    acc = _dot(xb, wuk_ref[...])
    u_ref[...] = acc[:, :D_SSM]
    kk = acc[:, D_SSM:]
    k_ref[...] = kk.astype(bf16)
    qvt = _dot_nt(wqvt_ref[...], xb)
    for r in range(TM_PROJ // MOBA_BLOCK):
        cols = slice(r * MOBA_BLOCK, (r + 1) * MOBA_BLOCK)
        km_ref[r] = jnp.mean(kk[cols], axis=0, keepdims=True)
        qt_ref[r] = qvt[:D_ATT, cols].astype(bf16)
        vt_ref[r] = qvt[D_ATT:, cols].astype(bf16)


def _inproj(x2, w_in):
    n, d_model = x2.shape
    nb = TM_PROJ // MOBA_BLOCK
    w_uk = jnp.concatenate([w_in[:, :D_SSM], w_in[:, D_SSM + D_ATT:D_SSM + 2 * D_ATT]], axis=1).astype(bf16)
    w_qvt = jnp.concatenate([w_in[:, D_SSM:D_SSM + D_ATT], w_in[:, D_SSM + 2 * D_ATT:]], axis=1).T.astype(bf16)
    return pl.pallas_call(
        _inproj_kernel,
        grid=(n // TM_PROJ,),
        in_specs=[pl.BlockSpec((TM_PROJ, d_model), lambda i: (i, 0)),
                  pl.BlockSpec(w_uk.shape, lambda i: (0, 0)),
                  pl.BlockSpec(w_qvt.shape, lambda i: (0, 0))],
        out_specs=[pl.BlockSpec((TM_PROJ, D_SSM), lambda i: (i, 0)),
                   pl.BlockSpec((TM_PROJ, D_ATT), lambda i: (i, 0)),
                   pl.BlockSpec((nb, D_ATT, MOBA_BLOCK), lambda i: (i, 0, 0)),
                   pl.BlockSpec((nb, D_ATT, MOBA_BLOCK), lambda i: (i, 0, 0)),
                   pl.BlockSpec((nb, 1, D_ATT), lambda i: (i, 0, 0))],
        out_shape=[jax.ShapeDtypeStruct((n, D_SSM), f32),
                   jax.ShapeDtypeStruct((n, D_ATT), bf16),
                   jax.ShapeDtypeStruct((n // MOBA_BLOCK, D_ATT, MOBA_BLOCK), bf16),
                   jax.ShapeDtypeStruct((n // MOBA_BLOCK, D_ATT, MOBA_BLOCK), bf16),
                   jax.ShapeDtypeStruct((n // MOBA_BLOCK, 1, D_ATT), f32)],
        compiler_params=pltpu.CompilerParams(dimension_semantics=("parallel",),
                                             vmem_limit_bytes=VMEM_LIMIT),
        name="inproj",
    )(x2, w_uk, w_qvt)


def _s5_operators(a_re, a_im, log_dt, b_re, b_im, c_re, c_im, d_skip):
    T = S5_CHUNK
    G, P = a_re.shape
    H = SSM_GROUP
    nj = G // S5_LANE_GROUPS
    g8 = S5_LANE_GROUPS
    lam = lax.complex(a_re.astype(f32), a_im.astype(f32))
    dt = jnp.exp(log_dt.astype(f32))[:, None]
    lam_bar = jnp.exp(lam * dt)
    b_c = lax.complex(b_re.astype(f32), b_im.astype(f32))
    b_bar = ((lam_bar - 1.0) / lam)[:, :, None] * b_c
    c_c = lax.complex(c_re.astype(f32), c_im.astype(f32))
    pows = jnp.exp((lam * dt)[None] * jnp.arange(T + 1, dtype=f32)[:, None, None])
    kern = jnp.real(jnp.einsum('ghp,dgp,gpk->dghk', c_c, pows[:T], b_bar))
    sig = jnp.arange(T)[:, None]
    tau = jnp.arange(T)[None, :]
    dd = tau - sig
    kst = jnp.where((dd >= 0)[:, :, None, None, None], kern[jnp.clip(dd, 0, T - 1)], 0.0)
    eye = jnp.eye(g8, dtype=f32)
    kst = kst.reshape(T, T, nj, g8, H, H)
    toep = jnp.einsum('stjahk,ab->jsaktbh', kst, eye).reshape(nj, T * LANES, T * LANES)
    win = pows[T - 1 - jnp.arange(T)][:, :, :, None] * b_bar[None]
    win = jnp.stack([jnp.real(win), jnp.imag(win)], 0).reshape(2, T, nj, g8, P, H)
    m_in = jnp.einsum('rsjapk,ab->jsakrbp', win, eye).reshape(nj, T * LANES, 2 * g8 * P)
    wout = c_c[None] * jnp.transpose(pows[1:T + 1], (0, 1, 2))[:, :, None, :]
    wout = jnp.stack([jnp.real(wout), -jnp.imag(wout)], 0).reshape(2, T, nj, g8, H, P)
    m_out = jnp.einsum('rtjahp,ab->jraptbh', wout, eye).reshape(nj, 2 * g8 * P, T * LANES)
    a_chunk = pows[T].reshape(nj, 1, g8 * P)
    a_ri = jnp.concatenate([jnp.real(a_chunk), jnp.imag(a_chunk)], axis=1)
    d_flat = jnp.tile(d_skip.astype(f32).reshape(nj, 1, LANES), (1, 1, T))
    return toep.astype(bf16), m_in.astype(bf16), m_out.astype(bf16), a_ri, d_flat


def _s5_kernel(u_ref, toep_ref, min_ref, mout_ref, a_ref, d_ref, y_ref, in_scr, sp_scr):
    T = S5_CHUNK
    n_c = u_ref.shape[0] // T
    half = in_scr.shape[1] // 2
    uf = jnp.concatenate([u_ref[pl.ds(t, n_c, stride=T), :] for t in range(T)], axis=1)
    ufb = uf.astype(bf16)
    y = _dot(ufb, toep_ref[0]) + uf * d_ref[0]
    in_scr[...] = _dot(ufb, min_ref[0])
    ar = a_ref[0, 0:1, :]
    ai = a_ref[0, 1:2, :]

    def step(c, carry):
        sr, si = carry
        sp_scr[pl.ds(c, 1), :half] = sr
        sp_scr[pl.ds(c, 1), half:] = si
        ir = in_scr[pl.ds(c, 1), :half]
        ii = in_scr[pl.ds(c, 1), half:]
        return ar * sr - ai * si + ir, ar * si + ai * sr + ii

    z = jnp.zeros((1, half), f32)
    lax.fori_loop(0, n_c, step, (z, z), unroll=4)
    y = y + _dot(sp_scr[...].astype(bf16), mout_ref[0])
    yg = jax.nn.gelu(y)
    for t in range(T):
        y_ref[pl.ds(t, n_c, stride=T), :] = yg[:, t * LANES:(t + 1) * LANES]


def _s5(u2, bsz, seq, ops):
    toep, m_in, m_out, a_ri, d_flat = ops
    nj = toep.shape[0]
    n_c = seq // S5_CHUNK
    kdim = S5_CHUNK * LANES
    sdim = m_in.shape[2]
    return pl.pallas_call(
        _s5_kernel,
        grid=(nj, bsz),
        in_specs=[pl.BlockSpec((seq, LANES), lambda j, b: (b, j)),
                  pl.BlockSpec((1, kdim, kdim), lambda j, b: (j, 0, 0)),
                  pl.BlockSpec((1, kdim, sdim), lambda j, b: (j, 0, 0)),
                  pl.BlockSpec((1, sdim, kdim), lambda j, b: (j, 0, 0)),
                  pl.BlockSpec((1, 2, sdim // 2), lambda j, b: (j, 0, 0)),
                  pl.BlockSpec((1, 1, kdim), lambda j, b: (j, 0, 0))],
        out_specs=pl.BlockSpec((seq, LANES), lambda j, b: (b, j)),
        out_shape=jax.ShapeDtypeStruct(u2.shape, f32),
        scratch_shapes=[pltpu.VMEM((n_c, sdim), f32), pltpu.VMEM((n_c, sdim), f32)],
        compiler_params=pltpu.CompilerParams(dimension_semantics=("parallel", "parallel"),
                                             vmem_limit_bytes=VMEM_LIMIT),
        name="s5",
    )(u2, toep, m_in, m_out, a_ri, d_flat)


def _moba_kernel(qt_ref, k_ref, vt_ref, km_ref, o_ref, kaug_ref, s0_scr, s1_scr):
    blk = MOBA_BLOCK
    hd = ATT_HEAD_DIM
    nb = k_ref.shape[0] // blk
    i = pl.program_id(2)

    @pl.when(i == 0)
    def _build():
        lane = lax.broadcasted_iota(jnp.int32, (blk, LANES), 1)

        def build(j, carry):
            rows = pl.ds(pl.multiple_of(j * blk, blk), blk)
            kaug_ref[rows, :LANES] = k_ref[rows, :]
            kaug_ref[rows, LANES:] = jnp.where(lane == j, 1.0, 0.0).astype(bf16)
            return carry
        lax.fori_loop(0, nb, build, 0)

    qt = qt_ref[0]
    qtf = qt.astype(f32) * (hd ** -0.5)
    km = km_ref[0]
    lane_nb = lax.broadcasted_iota(jnp.int32, (nb, LANES), 1)
    feat = lax.broadcasted_iota(jnp.int32, (LANES, blk), 0)
    blk_id = lax.broadcasted_iota(jnp.int32, (nb, blk), 0)
    blk_f = blk_id.astype(f32)
    q_cols, bias_cols = [], []
    for hh in range(2):
        km_hi, km_lo = _split_bf16(jnp.where((lane_nb < hd) if hh == 0 else (lane_nb >= hd), km, 0.0))
        gate = _dot(km_hi, qt) + _dot(km_lo, qt)
        g = jnp.where(blk_id < i, gate, -jnp.inf)
        sel = blk_id == i
        for _ in range(MOBA_TOPK):
            mx = jnp.max(g, axis=0, keepdims=True)
            first = jnp.min(jnp.where(g == mx, blk_f, float(nb)), axis=0, keepdims=True)
            pick = (blk_f == first) & (mx > -jnp.inf)
            sel = sel | pick
            g = jnp.where(pick, -jnp.inf, g)
        bias_cols.append(jnp.where(sel, 0.0, NEG))
        q_cols.append(jnp.where((feat < hd) if hh == 0 else (feat >= hd), qtf, 0.0))
    qa = jnp.concatenate([jnp.concatenate(q_cols, axis=1), jnp.concatenate(bias_cols, axis=1),
                          jnp.zeros((LANES - nb, 2 * blk), f32)], axis=0).astype(bf16)

    key_pos = lax.broadcasted_iota(jnp.int32, (blk, 2 * blk), 0)
    q_pos = lax.broadcasted_iota(jnp.int32, (blk, 2 * blk), 1) % blk

    def scores(j):
        j = jnp.minimum(j, nb - 1)
        return _dot(kaug_ref[pl.ds(pl.multiple_of(j * blk, blk), blk), :], qa)

    def update(state, blocks):
        m, l, acc = state
        m_new = m
        for _, s in blocks:
            m_new = jnp.maximum(m_new, jnp.max(s, axis=0, keepdims=True))
        alpha = jnp.exp(m - m_new)
        l = alpha * l
        acc = alpha * acc
        for j, s in blocks:
            p = jnp.exp(s - m_new)
            l = l + jnp.sum(p, axis=0, keepdims=True)
            pb = p.astype(bf16)
            acc = acc + jnp.concatenate([_dot(vt_ref[j, :hd, :], pb[:, :blk]),
                                         _dot(vt_ref[j, hd:, :], pb[:, blk:])], axis=1)
        return m_new, l, acc

    def pair_scores(p, dst):
        dst[0] = scores(2 * p)
        dst[1] = scores(2 * p + 1)

    def pair_update(state, p, src, masked):
        blocks = []
        for r in range(2):
            j = 2 * p + r
            s = src[r]
            if masked:
                off = jnp.where(j < i, blk, jnp.where(j == i, 0, -2 * blk))
                s = jnp.where(key_pos <= q_pos + off, s, NEG)
            blocks.append((jnp.minimum(j, nb - 1), s))
        return update(state, blocks)

    def quad_body(q, state):
        pair_scores(2 * q + 1, s1_scr)
        state = pair_update(state, 2 * q, s0_scr, False)
        pair_scores(2 * q + 2, s0_scr)
        return pair_update(state, 2 * q + 1, s1_scr, False)

    state = (jnp.full((1, 2 * blk), -jnp.inf, f32), jnp.zeros((1, 2 * blk), f32), jnp.zeros((hd, 2 * blk), f32))
    n_quad = i // 4
    pair_scores(0, s0_scr)
    state = lax.fori_loop(0, n_quad, quad_body, state)
    pair_scores(2 * n_quad + 1, s1_scr)
    state = pair_update(state, 2 * n_quad, s0_scr, True)
    m, l, acc = pair_update(state, 2 * n_quad + 1, s1_scr, True)
    out = acc / l
    o_ref[...] = jnp.concatenate([out[:, :blk], out[:, blk:]], axis=0).T.astype(o_ref.dtype)


def _moba(qt, k2, vt, kmean, bsz, seq):
    nb = seq // MOBA_BLOCK
    assert seq % MOBA_BLOCK == 0 and nb % SUBLANES == 0 and nb <= LANES
    npair = D_ATT // LANES
    return pl.pallas_call(
        _moba_kernel,
        grid=(bsz, npair, nb),
        in_specs=[pl.BlockSpec((1, LANES, MOBA_BLOCK), lambda b, h, i: (b * nb + i, h, 0)),
                  pl.BlockSpec((seq, LANES), lambda b, h, i: (b, h)),
                  pl.BlockSpec((nb, LANES, MOBA_BLOCK), lambda b, h, i: (b, h, 0)),
                  pl.BlockSpec((1, nb, LANES), lambda b, h, i: (b, 0, h))],
        out_specs=pl.BlockSpec((MOBA_BLOCK, LANES), lambda b, h, i: (b * nb + i, h)),
        out_shape=jax.ShapeDtypeStruct(k2.shape, bf16),
        scratch_shapes=[pltpu.VMEM((seq, 2 * LANES), bf16),
                        pltpu.VMEM((2, MOBA_BLOCK, 2 * MOBA_BLOCK), f32),
                        pltpu.VMEM((2, MOBA_BLOCK, 2 * MOBA_BLOCK), f32)],
        compiler_params=pltpu.CompilerParams(dimension_semantics=("parallel", "parallel", "arbitrary"),
                                             vmem_limit_bytes=VMEM_LIMIT),
        name="moba",
    )(qt, k2, vt, kmean)


def _outproj_kernel(x_ref, yg_ref, att_ref, wglu_ref, wout_ref, g_ref, b_ref, h_ref):
    yg = yg_ref[...]
    z = yg * jax.nn.sigmoid(_dot(yg.astype(bf16), wglu_ref[...]))
    cat = jnp.concatenate([z.astype(bf16), att_ref[...]], axis=1)
    r = DN_ALPHA * x_ref[...] + _dot(cat, wout_ref[...])
    h_ref[...] = _layernorm(r, g_ref[...], b_ref[...])


def _outproj(x2, yg, att, wglu_b, wout_b, g, b):
    n, d_model = x2.shape
    return pl.pallas_call(
        _outproj_kernel,
        grid=(n // TM_PROJ,),
        in_specs=[pl.BlockSpec((TM_PROJ, d_model), lambda i: (i, 0)),
                  pl.BlockSpec((TM_PROJ, D_SSM), lambda i: (i, 0)),
                  pl.BlockSpec((TM_PROJ, D_ATT), lambda i: (i, 0)),
                  pl.BlockSpec(wglu_b.shape, lambda i: (0, 0)),
                  pl.BlockSpec(wout_b.shape, lambda i: (0, 0)),
                  pl.BlockSpec((1, d_model), lambda i: (0, 0)),
                  pl.BlockSpec((1, d_model), lambda i: (0, 0))],
        out_specs=pl.BlockSpec((TM_PROJ, d_model), lambda i: (i, 0)),
        out_shape=jax.ShapeDtypeStruct((n, d_model), f32),
        compiler_params=pltpu.CompilerParams(dimension_semantics=("parallel",),
                                             vmem_limit_bytes=VMEM_LIMIT),
        name="outproj",
    )(x2, yg, att, wglu_b, wout_b, g, b)


def _bitonic_sort_desc(v):
    n = len(v)
    k = 2
    while k <= n:
        j = k // 2
        while j >= 1:
            for a in range(n):
                b = a ^ j
                if b > a:
                    hi = jnp.maximum(v[a], v[b])
                    lo = jnp.minimum(v[a], v[b])
                    v[a], v[b] = (hi, lo) if (a & k) == 0 else (lo, hi)
            j //= 2
        k *= 2
    return v


def _bitonic_merge_desc(v):
    j = len(v) // 2
    while j >= 1:
        for a in range(len(v)):
            b = a ^ j
            if b > a:
                v[a], v[b] = jnp.maximum(v[a], v[b]), jnp.minimum(v[a], v[b])
        j //= 2
    return v


def _top16_sorted(st):
    k = PEER_TOPK
    x = _bitonic_sort_desc([st[SUBLANES * a:SUBLANES * (a + 1), :] for a in range(PEER_N_KEYS // SUBLANES)])
    for shift in (4, 2, 1):
        y = [pltpu.roll(x[k - 1 - a], shift, 0) for a in range(k)]
        x = _bitonic_merge_desc([jnp.maximum(x[a], y[a]) for a in range(k)])
    return x


def _dup_bf16(x):
    u = pltpu.bitcast(x.astype(bf16).astype(f32), jnp.uint32)
    return pltpu.bitcast(u | (u >> 16), f32)


def _bf16_rows(row, n_rows):
    return pltpu.bitcast(jnp.broadcast_to(row, (n_rows // 2, row.shape[1])), bf16)


def _peer_kernel(n_exp, h_ref, wq_ref, sk_ref, u_ref, vt_ref, g_ref, b_ref, o_ref,
                 q_scr, pr_scr, rq_scr, srt_scr, row_scr, acc_scr, hbt_scr, a_scr):
    eb = pl.program_id(1)
    n_blk = n_exp // EB_PEER
    tt = h_ref.shape[0]
    nk = PEER_N_KEYS
    k = PEER_TOPK

    chunk = 2 * nk
    n_chunk = EB_PEER // chunk

    def activations():
        for s in range(n_chunk):
            a_scr[s] = _dot(u_ref[s * chunk:(s + 1) * chunk, :], hbt_scr[...])

    @pl.when(eb == 0)
    def _prologue():
        hbt_scr[...] = h_ref[...].T.astype(bf16)
        q = _dot(h_ref[...].astype(bf16), wq_ref[...])
        for hc in range(2 * PEER_HEADS):
            q_scr[hc] = q[:, hc * PEER_HALF:(hc + 1) * PEER_HALF]

        def head_sort(h, carry):
            for c in range(2):
                q_hi, q_lo = _split_bf16(q_scr[2 * h + c])
                k_hi, k_lo = _split_bf16(sk_ref[2 * h + c])
                st = _dot_nt(k_hi, q_hi) + _dot_nt(k_hi, q_lo) + _dot_nt(k_lo, q_hi)
                pr_scr[c, h] = st
                srt = _top16_sorted(st)
                for a in range(k):
                    srt_scr[c, a, pl.ds(h, 1), :] = srt[a][0:1, :]
            return carry
        lax.fori_loop(0, PEER_HEADS, head_sort, 0)

        a_s = [srt_scr[0, a] for a in range(k)]
        b_s = [srt_scr[1, a] for a in range(k)]
        cands = [a_s[a] + b_s[b] for a in range(k) for b in range(k) if (a + 1) * (b + 1) <= k]
        cands += [jnp.full_like(cands[0], -jnp.inf)] * (64 - len(cands))
        cs = _bitonic_sort_desc(cands)
        zsum = jnp.zeros_like(cs[0])
        for a in range(k):
            zsum = zsum + jnp.exp(cs[a] - cs[0])
        row_scr[0] = 0.5 * (cs[k - 1] + cs[k])
        row_scr[1] = 1.0 / zsum
        row_scr[2] = a_s[0]
        row_scr[3] = b_s[0]
        row_scr[4] = a_s[k - 1]
        row_scr[5] = b_s[k - 1]

        def head_prep(h, carry):
            hrow = pl.ds(h, 1)
            at = pr_scr[0, h]
            bt = pr_scr[1, h]
            thr = row_scr[0, hrow, :] - at
            r_cnt = jnp.zeros_like(at)
            b_rank = jnp.zeros_like(bt)
            for a in range(k):
                b_a = srt_scr[1, a, hrow, :]
                r_cnt = jnp.where(b_a > thr, float(a + 1), r_cnt)
                b_rank = jnp.where(b_a > bt, float(a + 1), b_rank)
            in_a = at >= row_scr[4, hrow, :]
            p_w = jnp.where(in_a, jnp.exp(at - row_scr[2, hrow, :]), 0.0) * row_scr[1, hrow, :]
            pr_scr[0, h] = _dup_bf16(p_w)
            pr_scr[1, h] = _dup_bf16(jnp.where(in_a, r_cnt, 0.0))
            rq_scr[0, h] = b_rank.astype(bf16)
            rq_scr[1, h] = jnp.where(bt >= row_scr[5, hrow, :], jnp.exp(bt - row_scr[3, hrow, :]), 0.0).astype(bf16)
            return carry
        lax.fori_loop(0, PEER_HEADS, head_prep, 0)
        acc_scr[...] = jnp.zeros_like(acc_scr)

    activations()
    for s in range(n_chunk):
        halves = []
        for half in range(chunk // nk):
            n1 = pl.ds((EB_PEER // nk) * eb + s * (chunk // nk) + half, 1)
            g = jnp.zeros((nk, tt), bf16)
            for h in range(PEER_HEADS):
                p_row = _bf16_rows(pr_scr[0, h, n1, :], nk)
                r_row = _bf16_rows(pr_scr[1, h, n1, :], nk)
                g = g + p_row * jnp.where(rq_scr[0, h] < r_row, rq_scr[1, h], 0)
            halves.append(g)
        w = jnp.concatenate(halves, axis=0) * jax.nn.gelu(a_scr[s].astype(bf16))
        acc_scr[...] += _dot(vt_ref[:, s * chunk:(s + 1) * chunk], w)

    @pl.when(eb == n_blk - 1)
    def _last():
        r = DN_ALPHA * h_ref[...] + acc_scr[...].T
        o_ref[...] = _layernorm(r, g_ref[...], b_ref[...])


def _peer(h1, wq_b, sub_keys, u_b, vt_b, g, b):
    n, d_model = h1.shape
    n_exp = u_b.shape[0]
    tt = TT_PEER
    sk = sub_keys.reshape(2 * PEER_HEADS, PEER_N_KEYS, PEER_HALF).astype(f32)
    n_blk = n_exp // EB_PEER
    const = dict(pipeline_mode=pl.Buffered(1))
    return pl.pallas_call(
        functools.partial(_peer_kernel, n_exp),
        grid=(n // tt, n_blk),
        in_specs=[pl.BlockSpec((tt, d_model), lambda t, e: (t, 0)),
                  pl.BlockSpec(wq_b.shape, lambda t, e: (0, 0), **const),
                  pl.BlockSpec(sk.shape, lambda t, e: (0, 0, 0), **const),
                  pl.BlockSpec((EB_PEER, d_model), lambda t, e: (e, 0)),
                  pl.BlockSpec((d_model, EB_PEER), lambda t, e: (0, e)),
                  pl.BlockSpec((1, d_model), lambda t, e: (0, 0), **const),
                  pl.BlockSpec((1, d_model), lambda t, e: (0, 0), **const)],
        out_specs=pl.BlockSpec((tt, d_model), lambda t, e: (t, 0)),
        out_shape=jax.ShapeDtypeStruct((n, d_model), f32),
        scratch_shapes=[pltpu.VMEM((2 * PEER_HEADS, tt, PEER_HALF), f32),
                        pltpu.VMEM((2, PEER_HEADS, PEER_N_KEYS, tt), f32),
                        pltpu.VMEM((2, PEER_HEADS, PEER_N_KEYS, tt), bf16),
                        pltpu.VMEM((2, PEER_TOPK, PEER_HEADS, tt), f32),
                        pltpu.VMEM((6, PEER_HEADS, tt), f32),
                        pltpu.VMEM((d_model, tt), f32),
                        pltpu.VMEM((d_model, tt), bf16),
                        pltpu.VMEM((EB_PEER // (2 * PEER_N_KEYS), 2 * PEER_N_KEYS, tt), f32)],
        compiler_params=pltpu.CompilerParams(dimension_semantics=("parallel", "arbitrary"),
                                             vmem_limit_bytes=VMEM_LIMIT),
        name="peer",
    )(h1, wq_b, sk, u_b, vt_b, g, b)


def kernel(x, w_in, ssm_a_re, ssm_a_im, ssm_log_dt, ssm_b_re, ssm_b_im, ssm_c_re, ssm_c_im, ssm_d,
           ssm_w_glu, w_out, ln1_g, ln1_b, peer_w_q, peer_sub_keys, peer_u, peer_v, ln2_g, ln2_b):
    bsz, seq, d_model = x.shape
    n = bsz * seq
    h = x.reshape(n, d_model)
    for l in range(w_in.shape[0]):
        u2, k2, qt, vt, kmean = _inproj(h, w_in[l])
        ops = _s5_operators(ssm_a_re[l], ssm_a_im[l], ssm_log_dt[l], ssm_b_re[l], ssm_b_im[l],
                            ssm_c_re[l], ssm_c_im[l], ssm_d[l])
        yg = _s5(u2, bsz, seq, ops)
        att = _moba(qt, k2, vt, kmean.reshape(bsz, seq // MOBA_BLOCK, D_ATT), bsz, seq)
        h1 = _outproj(h, yg, att, ssm_w_glu[l].astype(bf16), w_out[l].astype(bf16),
                      ln1_g[l].reshape(1, d_model), ln1_b[l].reshape(1, d_model))
        h = _peer(h1, peer_w_q[l].astype(bf16), peer_sub_keys[l], peer_u[l].astype(bf16),
                  peer_v[l].T.astype(bf16), ln2_g[l].reshape(1, d_model), ln2_b[l].reshape(1, d_model))
    return h.reshape(bsz, seq, d_model)
```

```python
import functools
import math

import jax
import jax.numpy as jnp
from jax import lax
from jax.experimental import pallas as pl
from jax.experimental.pallas import tpu as pltpu

f32 = jnp.float32
bf16 = jnp.bfloat16

D_SSM = 512
SSM_GROUP = 16
SSM_STATE = 64
N_ATT_HEADS = 8
ATT_HEAD_DIM = 64
D_ATT = N_ATT_HEADS * ATT_HEAD_DIM
MOBA_BLOCK = 256
MOBA_TOPK = 3
PEER_HEADS = 8
PEER_HALF = 128
PEER_N_KEYS = 128
PEER_TOPK = 16
LN_EPS = 1e-5
DEPTH = 1
DN_ALPHA = (2.0 * DEPTH) ** 0.25
NEG = -1e30

LANES = 128
SUBLANES = 8
BF16_ROWS = 2 * SUBLANES
VMEM_LIMIT = 48 * 1024 * 1024

TM_PROJ = 512
S5_CHUNK = 8
S5_LANE_GROUPS = LANES // SSM_GROUP
TT_PEER = 512
EB_PEER = 2048


def _dot(a, b):
    return jnp.dot(a, b, preferred_element_type=f32)


def _dot_nt(a, b):
    return lax.dot_general(a, b, (((1,), (1,)), ((), ())), preferred_element_type=f32)


def _split_bf16(x):
    hi = x.astype(bf16)
    lo = (x - hi.astype(f32)).astype(bf16)
    return hi, lo


def _layernorm(r, g, b):
    mu = jnp.mean(r, axis=-1, keepdims=True)
    d = r - mu
    var = jnp.mean(d * d, axis=-1, keepdims=True)
    return d * lax.rsqrt(var + LN_EPS) * g + b


def _inproj_kernel(x_ref, wuk_ref, wqvt_ref, u_ref, k_ref, qt_ref, vt_ref, km_ref):
    xb = x_ref[...].astype(bf16)
    acc = _dot(xb, wuk_ref[...])
    u_ref[...] = acc[:, :D_SSM]
    kk = acc[:, D_SSM:]
    k_ref[...] = kk.astype(bf16)
    qvt = _dot_nt(wqvt_ref[...], xb)
    for r in range(TM_PROJ // MOBA_BLOCK):
        cols = slice(r * MOBA_BLOCK, (r + 1) * MOBA_BLOCK)
        km_ref[r] = jnp.mean(kk[cols], axis=0, keepdims=True)
        qt_ref[r] = qvt[:D_ATT, cols].astype(bf16)
        vt_ref[r] = qvt[D_ATT:, cols].astype(bf16)


def _inproj(x2, w_in):
    n, d_model = x2.shape
    nb = TM_PROJ // MOBA_BLOCK
    w_uk = jnp.concatenate([w_in[:, :D_SSM], w_in[:, D_SSM + D_ATT:D_SSM + 2 * D_ATT]], axis=1).astype(bf16)
    w_qvt = jnp.concatenate([w_in[:, D_SSM:D_SSM + D_ATT], w_in[:, D_SSM + 2 * D_ATT:]], axis=1).T.astype(bf16)
    return pl.pallas_call(
        _inproj_kernel,
        grid=(n // TM_PROJ,),
        in_specs=[pl.BlockSpec((TM_PROJ, d_model), lambda i: (i, 0)),
                  pl.BlockSpec(w_uk.shape, lambda i: (0, 0)),
                  pl.BlockSpec(w_qvt.shape, lambda i: (0, 0))],
        out_specs=[pl.BlockSpec((TM_PROJ, D_SSM), lambda i: (i, 0)),
                   pl.BlockSpec((TM_PROJ, D_ATT), lambda i: (i, 0)),
                   pl.BlockSpec((nb, D_ATT, MOBA_BLOCK), lambda i: (i, 0, 0)),
                   pl.BlockSpec((nb, D_ATT, MOBA_BLOCK), lambda i: (i, 0, 0)),
                   pl.BlockSpec((nb, 1, D_ATT), lambda i: (i, 0, 0))],
        out_shape=[jax.ShapeDtypeStruct((n, D_SSM), f32),
                   jax.ShapeDtypeStruct((n, D_ATT), bf16),
                   jax.ShapeDtypeStruct((n // MOBA_BLOCK, D_ATT, MOBA_BLOCK), bf16),
                   jax.ShapeDtypeStruct((n // MOBA_BLOCK, D_ATT, MOBA_BLOCK), bf16),
                   jax.ShapeDtypeStruct((n // MOBA_BLOCK, 1, D_ATT), f32)],
        compiler_params=pltpu.CompilerParams(dimension_semantics=("parallel",),
                                             vmem_limit_bytes=VMEM_LIMIT),
        name="inproj",
    )(x2, w_uk, w_qvt)


def _s5_operators(a_re, a_im, log_dt, b_re, b_im, c_re, c_im, d_skip):
    T = S5_CHUNK
    G, P = a_re.shape
    H = SSM_GROUP
    nj = G // S5_LANE_GROUPS
    g8 = S5_LANE_GROUPS
    lam = lax.complex(a_re.astype(f32), a_im.astype(f32))
    dt = jnp.exp(log_dt.astype(f32))[:, None]
    lam_bar = jnp.exp(lam * dt)
    b_c = lax.complex(b_re.astype(f32), b_im.astype(f32))
    b_bar = ((lam_bar - 1.0) / lam)[:, :, None] * b_c
    c_c = lax.complex(c_re.astype(f32), c_im.astype(f32))
    pows = jnp.exp((lam * dt)[None] * jnp.arange(T + 1, dtype=f32)[:, None, None])
    same = jnp.arange(g8)[:, None] == (jnp.arange(g8 * H) // H)[None, :]
    same_p = jnp.arange(g8)[:, None] == (jnp.arange(g8 * P) // P)[None, :]
    kern = jnp.real(jnp.einsum('ghp,dgp,gpk->dgkh', c_c, pows[:T], b_bar))
    dd = jnp.arange(T)[None, :] - jnp.arange(T)[:, None]
    kst = jnp.where((dd >= 0)[:, :, None, None, None], kern[jnp.clip(dd, 0, T - 1)], 0.0)
    kst = jnp.tile(kst.reshape(T, T, nj, g8, H, H), (1, 1, 1, 1, 1, g8))
    kst = jnp.where(same[None, None, None, :, None, :], kst, 0.0)
    toep = kst.transpose(2, 0, 3, 4, 1, 5).reshape(nj, T * LANES, T * LANES)
    win = pows[T - 1 - jnp.arange(T)][:, :, None, :] * jnp.swapaxes(b_bar, 1, 2)[None]
    win = jnp.stack([jnp.real(win), jnp.imag(win)], 0).reshape(2, T, nj, g8, H, P)
    win = jnp.where(same_p[None, None, None, :, None, :], jnp.tile(win, (1, 1, 1, 1, 1, g8)), 0.0)
    m_in = win.transpose(2, 1, 3, 4, 0, 5).reshape(nj, T * LANES, 2 * g8 * P)
    wout = jnp.swapaxes(c_c, 1, 2)[None] * pows[1:T + 1][:, :, :, None]
    wout = jnp.stack([jnp.real(wout), -jnp.imag(wout)], 0).reshape(2, T, nj, g8, P, H)
    wout = jnp.where(same[None, None, None, :, None, :], jnp.tile(wout, (1, 1, 1, 1, 1, g8)), 0.0)
    m_out = wout.transpose(2, 0, 3, 4, 1, 5).reshape(nj, 2 * g8 * P, T * LANES)
    a_chunk = pows[T].reshape(nj, 1, g8 * P)
    a_ri = jnp.concatenate([jnp.real(a_chunk), jnp.imag(a_chunk)], axis=1)
    d_flat = jnp.tile(d_skip.astype(f32).reshape(nj, 1, LANES), (1, 1, T))
    return toep.astype(bf16), m_in.astype(bf16), m_out.astype(bf16), a_ri, d_flat


def _s5_kernel(u_ref, toep_ref, min_ref, mout_ref, a_ref, d_ref, y_ref, in_scr, sp_scr):
    T = S5_CHUNK
    n_c = u_ref.shape[0] // T
    half = in_scr.shape[1] // 2
    uf = jnp.concatenate([u_ref[pl.ds(t, n_c, stride=T), :] for t in range(T)], axis=1)
    ufb = uf.astype(bf16)
    y = _dot(ufb, toep_ref[0]) + uf * d_ref[0]
    in_scr[...] = _dot(ufb, min_ref[0])
    ar = a_ref[0, 0:1, :]
    ai = a_ref[0, 1:2, :]

    def step(c, carry):
        sr, si = carry
        sp_scr[pl.ds(c, 1), :half] = sr
        sp_scr[pl.ds(c, 1), half:] = si
        ir = in_scr[pl.ds(c, 1), :half]
        ii = in_scr[pl.ds(c, 1), half:]
        return ar * sr - ai * si + ir, ar * si + ai * sr + ii

    z = jnp.zeros((1, half), f32)
    lax.fori_loop(0, n_c, step, (z, z), unroll=4)
    y = y + _dot(sp_scr[...].astype(bf16), mout_ref[0])
    yg = jax.nn.gelu(y)
    for t in range(T):
        y_ref[pl.ds(t, n_c, stride=T), :] = yg[:, t * LANES:(t + 1) * LANES]


def _s5(u2, bsz, seq, ops):
    toep, m_in, m_out, a_ri, d_flat = ops
    nj = toep.shape[0]
    n_c = seq // S5_CHUNK
    kdim = S5_CHUNK * LANES
    sdim = m_in.shape[2]
    return pl.pallas_call(
        _s5_kernel,
        grid=(nj, bsz),
        in_specs=[pl.BlockSpec((seq, LANES), lambda j, b: (b, j)),
                  pl.BlockSpec((1, kdim, kdim), lambda j, b: (j, 0, 0)),
                  pl.BlockSpec((1, kdim, sdim), lambda j, b: (j, 0, 0)),
                  pl.BlockSpec((1, sdim, kdim), lambda j, b: (j, 0, 0)),
                  pl.BlockSpec((1, 2, sdim // 2), lambda j, b: (j, 0, 0)),
                  pl.BlockSpec((1, 1, kdim), lambda j, b: (j, 0, 0))],
        out_specs=pl.BlockSpec((seq, LANES), lambda j, b: (b, j)),
        out_shape=jax.ShapeDtypeStruct(u2.shape, f32),
        scratch_shapes=[pltpu.VMEM((n_c, sdim), f32), pltpu.VMEM((n_c, sdim), f32)],
        compiler_params=pltpu.CompilerParams(dimension_semantics=("parallel", "parallel"),
                                             vmem_limit_bytes=VMEM_LIMIT),
        name="s5",
    )(u2, toep, m_in, m_out, a_ri, d_flat)


def _moba_kernel(qt_ref, k_ref, vt_ref, km_ref, o_ref, kaug_ref, s0_scr, s1_scr):
    blk = MOBA_BLOCK
    hd = ATT_HEAD_DIM
    nb = k_ref.shape[0] // blk
    i = pl.program_id(2)

    @pl.when(i == 0)
    def _build():
        lane = lax.broadcasted_iota(jnp.int32, (blk, LANES), 1)

        def build(j, carry):
            rows = pl.ds(pl.multiple_of(j * blk, blk), blk)
            kaug_ref[rows, :LANES] = k_ref[rows, :]
            kaug_ref[rows, LANES:] = jnp.where(lane == j, 1.0, 0.0).astype(bf16)
            return carry
        lax.fori_loop(0, nb, build, 0)

    qt = qt_ref[0]
    qtf = qt.astype(f32) * (hd ** -0.5)
    km = km_ref[0]
    lane_nb = lax.broadcasted_iota(jnp.int32, (nb, LANES), 1)
    feat = lax.broadcasted_iota(jnp.int32, (LANES, blk), 0)
    blk_id = lax.broadcasted_iota(jnp.int32, (nb, blk), 0)
    blk_f = blk_id.astype(f32)
    q_cols, bias_cols = [], []
    for hh in range(2):
        km_hi, km_lo = _split_bf16(jnp.where((lane_nb < hd) if hh == 0 else (lane_nb >= hd), km, 0.0))
        gate = _dot(km_hi, qt) + _dot(km_lo, qt)
        g = jnp.where(blk_id < i, gate, -jnp.inf)
        sel = blk_id == i
        for _ in range(MOBA_TOPK):
            mx = jnp.max(g, axis=0, keepdims=True)
            first = jnp.min(jnp.where(g == mx, blk_f, float(nb)), axis=0, keepdims=True)
            pick = (blk_f == first) & (mx > -jnp.inf)
            sel = sel | pick
            g = jnp.where(pick, -jnp.inf, g)
        bias_cols.append(jnp.where(sel, 0.0, NEG))
        q_cols.append(jnp.where((feat < hd) if hh == 0 else (feat >= hd), qtf, 0.0))
    qa = jnp.concatenate([jnp.concatenate(q_cols, axis=1), jnp.concatenate(bias_cols, axis=1),
                          jnp.zeros((LANES - nb, 2 * blk), f32)], axis=0).astype(bf16)

    key_pos = lax.broadcasted_iota(jnp.int32, (blk, 2 * blk), 0)
    q_pos = lax.broadcasted_iota(jnp.int32, (blk, 2 * blk), 1) % blk

    def scores(j):
        j = jnp.minimum(j, nb - 1)
        return _dot(kaug_ref[pl.ds(pl.multiple_of(j * blk, blk), blk), :], qa)

    def update(state, blocks):
        m, l, acc = state
        m_new = m
        for _, s in blocks:
            m_new = jnp.maximum(m_new, jnp.max(s, axis=0, keepdims=True))
        alpha = jnp.exp(m - m_new)
        l = alpha * l
        acc = alpha * acc
        for j, s in blocks:
            p = jnp.exp(s - m_new)
            l = l + jnp.sum(p, axis=0, keepdims=True)
            pb = p.astype(bf16)
            acc = acc + jnp.concatenate([_dot(vt_ref[j, :hd, :], pb[:, :blk]),
                                         _dot(vt_ref[j, hd:, :], pb[:, blk:])], axis=1)
        return m_new, l, acc

    def pair_scores(p, dst):
        dst[0] = scores(2 * p)
        dst[1] = scores(2 * p + 1)

    def pair_update(state, p, src, masked):
        blocks = []
        for r in range(2):
            j = 2 * p + r
            s = src[r]
            if masked:
                off = jnp.where(j < i, blk, jnp.where(j == i, 0, -2 * blk))
                s = jnp.where(key_pos <= q_pos + off, s, NEG)
            blocks.append((jnp.minimum(j, nb - 1), s))
        return update(state, blocks)

    def quad_body(q, state):
        pair_scores(2 * q + 1, s1_scr)
        state = pair_update(state, 2 * q, s0_scr, False)
        pair_scores(2 * q + 2, s0_scr)
        return pair_update(state, 2 * q + 1, s1_scr, False)

    state = (jnp.full((1, 2 * blk), -jnp.inf, f32), jnp.zeros((1, 2 * blk), f32), jnp.zeros((hd, 2 * blk), f32))
    n_quad = i // 4
    pair_scores(0, s0_scr)
    state = lax.fori_loop(0, n_quad, quad_body, state)
    pair_scores(2 * n_quad + 1, s1_scr)
    state = pair_update(state, 2 * n_quad, s0_scr, True)
    m, l, acc = pair_update(state, 2 * n_quad + 1, s1_scr, True)
    out = acc / l
    o_ref[...] = jnp.concatenate([out[:, :blk], out[:, blk:]], axis=0).T.astype(o_ref.dtype)


def _moba(qt, k2, vt, kmean, bsz, seq):
    nb = seq // MOBA_BLOCK
    assert seq % MOBA_BLOCK == 0 and nb % SUBLANES == 0 and nb <= LANES
    npair = D_ATT // LANES
    return pl.pallas_call(
        _moba_kernel,
        grid=(bsz, npair, nb),
        in_specs=[pl.BlockSpec((1, LANES, MOBA_BLOCK), lambda b, h, i: (b * nb + i, h, 0)),
                  pl.BlockSpec((seq, LANES), lambda b, h, i: (b, h)),
                  pl.BlockSpec((nb, LANES, MOBA_BLOCK), lambda b, h, i: (b, h, 0)),
                  pl.BlockSpec((1, nb, LANES), lambda b, h, i: (b, 0, h))],
        out_specs=pl.BlockSpec((MOBA_BLOCK, LANES), lambda b, h, i: (b * nb + i, h)),
        out_shape=jax.ShapeDtypeStruct(k2.shape, bf16),
        scratch_shapes=[pltpu.VMEM((seq, 2 * LANES), bf16),
                        pltpu.VMEM((2, MOBA_BLOCK, 2 * MOBA_BLOCK), f32),
                        pltpu.VMEM((2, MOBA_BLOCK, 2 * MOBA_BLOCK), f32)],
        compiler_params=pltpu.CompilerParams(dimension_semantics=("parallel", "parallel", "arbitrary"),
                                             vmem_limit_bytes=VMEM_LIMIT),
        name="moba",
    )(qt, k2, vt, kmean)


def _outproj_kernel(x_ref, yg_ref, att_ref, wglu_ref, wout_ref, g_ref, b_ref, h_ref):
    yg = yg_ref[...]
    z = yg * jax.nn.sigmoid(_dot(yg.astype(bf16), wglu_ref[...]))
    cat = jnp.concatenate([z.astype(bf16), att_ref[...]], axis=1)
    r = DN_ALPHA * x_ref[...] + _dot(cat, wout_ref[...])
    h_ref[...] = _layernorm(r, g_ref[...], b_ref[...])


def _outproj(x2, yg, att, wglu_b, wout_b, g, b):
    n, d_model = x2.shape
    return pl.pallas_call(
        _outproj_kernel,
        grid=(n // TM_PROJ,),
        in_specs=[pl.BlockSpec((TM_PROJ, d_model), lambda i: (i, 0)),
                  pl.BlockSpec((TM_PROJ, D_SSM), lambda i: (i, 0)),
                  pl.BlockSpec((TM_PROJ, D_ATT), lambda i: (i, 0)),
                  pl.BlockSpec(wglu_b.shape, lambda i: (0, 0)),
                  pl.BlockSpec(wout_b.shape, lambda i: (0, 0)),
                  pl.BlockSpec((1, d_model), lambda i: (0, 0)),
                  pl.BlockSpec((1, d_model), lambda i: (0, 0))],
        out_specs=pl.BlockSpec((TM_PROJ, d_model), lambda i: (i, 0)),
        out_shape=jax.ShapeDtypeStruct((n, d_model), f32),
        compiler_params=pltpu.CompilerParams(dimension_semantics=("parallel",),
                                             vmem_limit_bytes=VMEM_LIMIT),
        name="outproj",
    )(x2, yg, att, wglu_b, wout_b, g, b)


def _bitonic_sort_desc(v):
    n = len(v)
    k = 2
    while k <= n:
        j = k // 2
        while j >= 1:
            for a in range(n):
                b = a ^ j
                if b > a:
                    hi = jnp.maximum(v[a], v[b])
                    lo = jnp.minimum(v[a], v[b])
                    v[a], v[b] = (hi, lo) if (a & k) == 0 else (lo, hi)
            j //= 2
        k *= 2
    return v


def _bitonic_merge_desc(v):
    j = len(v) // 2
    while j >= 1:
        for a in range(len(v)):
            b = a ^ j
            if b > a:
                v[a], v[b] = jnp.maximum(v[a], v[b]), jnp.minimum(v[a], v[b])
        j //= 2
    return v


def _top16_sorted(st):
    k = PEER_TOPK
    x = _bitonic_sort_desc([st[SUBLANES * a:SUBLANES * (a + 1), :] for a in range(PEER_N_KEYS // SUBLANES)])
    for shift in (4, 2, 1):
        y = [pltpu.roll(x[k - 1 - a], shift, 0) for a in range(k)]
        x = _bitonic_merge_desc([jnp.maximum(x[a], y[a]) for a in range(k)])
    return x


def _dup_bf16(x):
    u = pltpu.bitcast(x.astype(bf16).astype(f32), jnp.uint32)
    return pltpu.bitcast(u | (u >> 16), f32)


def _bf16_rows(row):
    return pltpu.bitcast(jnp.broadcast_to(row, (SUBLANES, row.shape[1])), bf16)[None]


def _peer_kernel(n_exp, h_ref, wq_ref, sk_ref, u_ref, vt_ref, g_ref, b_ref, o_ref,
                 q_scr, pr_scr, rq_scr, srt_scr, row_scr, acc_scr, hbt_scr, a_scr):
    eb = pl.program_id(1)
    n_blk = n_exp // EB_PEER
    tt = h_ref.shape[0]
    nk = PEER_N_KEYS
    k = PEER_TOPK

    chunk = 2 * nk
    n_chunk = EB_PEER // chunk

    def activations():
        for s in range(n_chunk):
            a_scr[s] = _dot(u_ref[s * chunk:(s + 1) * chunk, :], hbt_scr[...])

    @pl.when(eb == 0)
    def _prologue():
        hbt_scr[...] = h_ref[...].T.astype(bf16)
        q = _dot(h_ref[...].astype(bf16), wq_ref[...])
        for hc in range(2 * PEER_HEADS):
            q_scr[hc] = q[:, hc * PEER_HALF:(hc + 1) * PEER_HALF]

        def head_sort(h, carry):
            for c in range(2):
                q_hi, q_lo = _split_bf16(q_scr[2 * h + c])
                k_hi, k_lo = _split_bf16(sk_ref[2 * h + c])
                st = _dot_nt(k_hi, q_hi) + _dot_nt(k_hi, q_lo) + _dot_nt(k_lo, q_hi)
                pr_scr[c, h] = st
                srt = _top16_sorted(st)
                for a in range(k):
                    srt_scr[c, a, pl.ds(h, 1), :] = srt[a][0:1, :]
            return carry
        lax.fori_loop(0, PEER_HEADS, head_sort, 0)

        a_s = [srt_scr[0, a] for a in range(k)]
        b_s = [srt_scr[1, a] for a in range(k)]
        cands = [a_s[a] + b_s[b] for a in range(k) for b in range(k) if (a + 1) * (b + 1) <= k]
        cands += [jnp.full_like(cands[0], -jnp.inf)] * (64 - len(cands))
        cs = _bitonic_sort_desc(cands)
        zsum = jnp.zeros_like(cs[0])
        for a in range(k):
            zsum = zsum + jnp.exp(cs[a] - cs[0])
        row_scr[0] = 0.5 * (cs[k - 1] + cs[k])
        row_scr[1] = 1.0 / zsum
        row_scr[2] = a_s[0]
        row_scr[3] = b_s[0]
        row_scr[4] = a_s[k - 1]
        row_scr[5] = b_s[k - 1]

        def head_prep(h, carry):
            hrow = pl.ds(h, 1)
            at = pr_scr[0, h]
            bt = pr_scr[1, h]
            thr = row_scr[0, hrow, :] - at
            r_cnt = jnp.zeros_like(at)
            b_rank = jnp.zeros_like(bt)
            for a in range(k):
                b_a = srt_scr[1, a, hrow, :]
                r_cnt = jnp.where(b_a > thr, float(a + 1), r_cnt)
                b_rank = jnp.where(b_a > bt, float(a + 1), b_rank)
            in_a = at >= row_scr[4, hrow, :]
            p_w = jnp.where(in_a, jnp.exp(at - row_scr[2, hrow, :]), 0.0) * row_scr[1, hrow, :]
            pr_scr[0, h] = _dup_bf16(p_w)
            pr_scr[1, h] = _dup_bf16(jnp.where(in_a, r_cnt, 0.0))
            q_w = jnp.where(bt >= row_scr[5, hrow, :], jnp.exp(bt - row_scr[3, hrow, :]), 0.0)
            rq_scr[0, h] = b_rank.astype(bf16).reshape(nk // BF16_ROWS, BF16_ROWS, tt)
            rq_scr[1, h] = q_w.astype(bf16).reshape(nk // BF16_ROWS, BF16_ROWS, tt)
            return carry
        lax.fori_loop(0, PEER_HEADS, head_prep, 0)
        acc_scr[...] = jnp.zeros_like(acc_scr)

    activations()
    for s in range(n_chunk):
        halves = []
        for half in range(chunk // nk):
            n1 = pl.ds((EB_PEER // nk) * eb + s * (chunk // nk) + half, 1)
            g = jnp.zeros((nk // BF16_ROWS, BF16_ROWS, tt), bf16)
            for h in range(PEER_HEADS):
                p_row = _bf16_rows(pr_scr[0, h, n1, :])
                r_row = _bf16_rows(pr_scr[1, h, n1, :])
                g = g + p_row * jnp.where(rq_scr[0, h] < r_row, rq_scr[1, h], 0)
            halves.append(g.reshape(nk, tt))
        w = jnp.concatenate(halves, axis=0) * jax.nn.gelu(a_scr[s].astype(bf16))
        acc_scr[...] += _dot(vt_ref[:, s * chunk:(s + 1) * chunk], w)

    @pl.when(eb == n_blk - 1)
    def _last():
        r = DN_ALPHA * h_ref[...] + acc_scr[...].T
        o_ref[...] = _layernorm(r, g_ref[...], b_ref[...])


def _peer(h1, wq_b, sub_keys, u_b, vt_b, g, b):
    n, d_model = h1.shape
    n_exp = u_b.shape[0]
    tt = TT_PEER
    sk = sub_keys.reshape(2 * PEER_HEADS, PEER_N_KEYS, PEER_HALF).astype(f32)
    n_blk = n_exp // EB_PEER
    const = dict(pipeline_mode=pl.Buffered(1))
    return pl.pallas_call(
        functools.partial(_peer_kernel, n_exp),
        grid=(n // tt, n_blk),
        in_specs=[pl.BlockSpec((tt, d_model), lambda t, e: (t, 0)),
                  pl.BlockSpec(wq_b.shape, lambda t, e: (0, 0), **const),
                  pl.BlockSpec(sk.shape, lambda t, e: (0, 0, 0), **const),
                  pl.BlockSpec((EB_PEER, d_model), lambda t, e: (e, 0)),
                  pl.BlockSpec((d_model, EB_PEER), lambda t, e: (0, e)),
                  pl.BlockSpec((1, d_model), lambda t, e: (0, 0), **const),
                  pl.BlockSpec((1, d_model), lambda t, e: (0, 0), **const)],
        out_specs=pl.BlockSpec((tt, d_model), lambda t, e: (t, 0)),
        out_shape=jax.ShapeDtypeStruct((n, d_model), f32),
        scratch_shapes=[pltpu.VMEM((2 * PEER_HEADS, tt, PEER_HALF), f32),
                        pltpu.VMEM((2, PEER_HEADS, PEER_N_KEYS, tt), f32),
                        pltpu.VMEM((2, PEER_HEADS, PEER_N_KEYS // BF16_ROWS, BF16_ROWS, tt), bf16),
                        pltpu.VMEM((2, PEER_TOPK, PEER_HEADS, tt), f32),
                        pltpu.VMEM((6, PEER_HEADS, tt), f32),
                        pltpu.VMEM((d_model, tt), f32),
                        pltpu.VMEM((d_model, tt), bf16),
                        pltpu.VMEM((EB_PEER // (2 * PEER_N_KEYS), 2 * PEER_N_KEYS, tt), f32)],
        compiler_params=pltpu.CompilerParams(dimension_semantics=("parallel", "arbitrary"),
                                             vmem_limit_bytes=VMEM_LIMIT),
        name="peer",
    )(h1, wq_b, sk, u_b, vt_b, g, b)


def kernel(x, w_in, ssm_a_re, ssm_a_im, ssm_log_dt, ssm_b_re, ssm_b_im, ssm_c_re, ssm_c_im, ssm_d,
           ssm_w_glu, w_out, ln1_g, ln1_b, peer_w_q, peer_sub_keys, peer_u, peer_v, ln2_g, ln2_b):
    bsz, seq, d_model = x.shape
    n = bsz * seq
    h = x.reshape(n, d_model)
    for l in range(w_in.shape[0]):
        u2, k2, qt, vt, kmean = _inproj(h, w_in[l])
        ops = _s5_operators(ssm_a_re[l], ssm_a_im[l], ssm_log_dt[l], ssm_b_re[l], ssm_b_im[l],
                            ssm_c_re[l], ssm_c_im[l], ssm_d[l])
        yg = _s5(u2, bsz, seq, ops)
        att = _moba(qt, k2, vt, kmean.reshape(bsz, seq // MOBA_BLOCK, D_ATT), bsz, seq)
        h1 = _outproj(h, yg, att, ssm_w_glu[l].astype(bf16), w_out[l].astype(bf16),
                      ln1_g[l].reshape(1, d_model), ln1_b[l].reshape(1, d_model))
        h = _peer(h1, peer_w_q[l].astype(bf16), peer_sub_keys[l], peer_u[l].astype(bf16),
                  peer_v[l].T.astype(bf16), ln2_g[l].reshape(1, d_model), ln2_b[l].reshape(1, d_model))
    return h.reshape(bsz, seq, d_model)
```

```python
import functools
import math

import jax
import jax.numpy as jnp
from jax import lax
from jax.experimental import pallas as pl
from jax.experimental.pallas import tpu as pltpu

f32 = jnp.float32
bf16 = jnp.bfloat16

D_SSM = 512
SSM_GROUP = 16
SSM_STATE = 64
N_ATT_HEADS = 8
ATT_HEAD_DIM = 64
D_ATT = N_ATT_HEADS * ATT_HEAD_DIM
MOBA_BLOCK = 256
MOBA_TOPK = 3
PEER_HEADS = 8
PEER_HALF = 128
PEER_N_KEYS = 128
PEER_TOPK = 16
LN_EPS = 1e-5
DEPTH = 1
DN_ALPHA = (2.0 * DEPTH) ** 0.25
NEG = -1e30

LANES = 128
SUBLANES = 8
BF16_ROWS = 2 * SUBLANES
VMEM_LIMIT = 48 * 1024 * 1024

TM_PROJ = 1024
S5_CHUNK = 8
S5_LANE_GROUPS = LANES // SSM_GROUP
TT_PEER = 512
EB_PEER = 2048


def _dot(a, b):
    return jnp.dot(a, b, preferred_element_type=f32)


def _dot_nt(a, b):
    return lax.dot_general(a, b, (((1,), (1,)), ((), ())), preferred_element_type=f32)


def _split_bf16(x):
    hi = x.astype(bf16)
    lo = (x - hi.astype(f32)).astype(bf16)
    return hi, lo


def _layernorm(r, g, b):
    mu = jnp.mean(r, axis=-1, keepdims=True)
    d = r - mu
    var = jnp.mean(d * d, axis=-1, keepdims=True)
    return d * lax.rsqrt(var + LN_EPS) * g + b


def _inproj_kernel(x_ref, wuk_ref, wqvt_ref, u_ref, k_ref, qt_ref, vt_ref, km_ref):
    xb = x_ref[...].astype(bf16)
    acc = _dot(xb, wuk_ref[...])
    u_ref[...] = acc[:, :D_SSM]
    kk = acc[:, D_SSM:]
    k_ref[...] = kk.astype(bf16)
    qvt = _dot_nt(wqvt_ref[...], xb)
    for r in range(TM_PROJ // MOBA_BLOCK):
        cols = slice(r * MOBA_BLOCK, (r + 1) * MOBA_BLOCK)
        km_ref[r] = jnp.mean(kk[cols], axis=0, keepdims=True)
        qt_ref[r] = qvt[:D_ATT, cols].astype(bf16)
        vt_ref[r] = qvt[D_ATT:, cols].astype(bf16)


def _inproj(x2, w_in):
    n, d_model = x2.shape
    nb = TM_PROJ // MOBA_BLOCK
    w_uk = jnp.concatenate([w_in[:, :D_SSM], w_in[:, D_SSM + D_ATT:D_SSM + 2 * D_ATT]], axis=1).astype(bf16)
    w_qvt = jnp.concatenate([w_in[:, D_SSM:D_SSM + D_ATT], w_in[:, D_SSM + 2 * D_ATT:]], axis=1).T.astype(bf16)
    return pl.pallas_call(
        _inproj_kernel,
        grid=(n // TM_PROJ,),
        in_specs=[pl.BlockSpec((TM_PROJ, d_model), lambda i: (i, 0)),
                  pl.BlockSpec(w_uk.shape, lambda i: (0, 0)),
                  pl.BlockSpec(w_qvt.shape, lambda i: (0, 0))],
        out_specs=[pl.BlockSpec((TM_PROJ, D_SSM), lambda i: (i, 0)),
                   pl.BlockSpec((TM_PROJ, D_ATT), lambda i: (i, 0)),
                   pl.BlockSpec((nb, D_ATT, MOBA_BLOCK), lambda i: (i, 0, 0)),
                   pl.BlockSpec((nb, D_ATT, MOBA_BLOCK), lambda i: (i, 0, 0)),
                   pl.BlockSpec((nb, 1, D_ATT), lambda i: (i, 0, 0))],
        out_shape=[jax.ShapeDtypeStruct((n, D_SSM), f32),
                   jax.ShapeDtypeStruct((n, D_ATT), bf16),
                   jax.ShapeDtypeStruct((n // MOBA_BLOCK, D_ATT, MOBA_BLOCK), bf16),
                   jax.ShapeDtypeStruct((n // MOBA_BLOCK, D_ATT, MOBA_BLOCK), bf16),
                   jax.ShapeDtypeStruct((n // MOBA_BLOCK, 1, D_ATT), f32)],
        compiler_params=pltpu.CompilerParams(dimension_semantics=("parallel",),
                                             vmem_limit_bytes=VMEM_LIMIT),
        name="inproj",
    )(x2, w_uk, w_qvt)


def _s5_operators(a_re, a_im, log_dt, b_re, b_im, c_re, c_im, d_skip):
    T = S5_CHUNK
    G, P = a_re.shape
    H = SSM_GROUP
    nj = G // S5_LANE_GROUPS
    g8 = S5_LANE_GROUPS
    lam = lax.complex(a_re.astype(f32), a_im.astype(f32))
    dt = jnp.exp(log_dt.astype(f32))[:, None]
    lam_bar = jnp.exp(lam * dt)
    b_c = lax.complex(b_re.astype(f32), b_im.astype(f32))
    b_bar = ((lam_bar - 1.0) / lam)[:, :, None] * b_c
    c_c = lax.complex(c_re.astype(f32), c_im.astype(f32))
    pows = jnp.exp((lam * dt)[None] * jnp.arange(T + 1, dtype=f32)[:, None, None])
    def widen(x, reps):
        n = x.shape[-1]
        rep = jnp.tile(jnp.eye(n, dtype=f32), (1, reps))
        return jnp.dot(x.reshape(-1, n), rep).reshape(x.shape[:-1] + (reps * n,))

    same = jnp.arange(g8)[:, None] == (jnp.arange(g8 * H) // H)[None, :]
    same_p = jnp.arange(g8)[:, None] == (jnp.arange(g8 * P) // P)[None, :]
    kern = jnp.real(jnp.einsum('ghp,dgp,gpk->dgkh', c_c, pows[:T], b_bar))
    dd = jnp.arange(T)[None, :] - jnp.arange(T)[:, None]
    kst = jnp.where((dd >= 0)[:, :, None, None, None], kern[jnp.clip(dd, 0, T - 1)], 0.0)
    kst = widen(kst.reshape(T, T, nj, g8, H, H), g8)
    kst = jnp.where(same[None, None, None, :, None, :], kst, 0.0)
    toep = kst.transpose(2, 0, 3, 4, 1, 5).reshape(nj, T * LANES, T * LANES)
    win = pows[T - 1 - jnp.arange(T)][:, :, None, :] * jnp.swapaxes(b_bar, 1, 2)[None]
    win = jnp.stack([jnp.real(win), jnp.imag(win)], 0).reshape(2, T, nj, g8, H, P)
    win = jnp.where(same_p[None, None, None, :, None, :], widen(win, g8), 0.0)
    m_in = win.transpose(2, 1, 3, 4, 0, 5).reshape(nj, T * LANES, 2 * g8 * P)
    wout = jnp.swapaxes(c_c, 1, 2)[None] * pows[1:T + 1][:, :, :, None]
    wout = jnp.stack([jnp.real(wout), -jnp.imag(wout)], 0).reshape(2, T, nj, g8, P, H)
    wout = jnp.where(same[None, None, None, :, None, :], widen(wout, g8), 0.0)
    m_out = wout.transpose(2, 0, 3, 4, 1, 5).reshape(nj, 2 * g8 * P, T * LANES)
    a_chunk = pows[T].reshape(nj, 1, g8 * P)
    a_ri = jnp.concatenate([jnp.real(a_chunk), jnp.imag(a_chunk)], axis=1)
    d_flat = jnp.tile(d_skip.astype(f32).reshape(nj, 1, LANES), (1, 1, T))
    return toep.astype(bf16), m_in.astype(bf16), m_out.astype(bf16), a_ri, d_flat


def _s5_kernel(u_ref, toep_ref, min_ref, mout_ref, a_ref, d_ref, y_ref, in_scr, sp_scr):
    T = S5_CHUNK
    n_c = u_ref.shape[0] // T
    half = in_scr.shape[1] // 2
    uf = jnp.concatenate([u_ref[pl.ds(t, n_c, stride=T), :] for t in range(T)], axis=1)
    ufb = uf.astype(bf16)
    y = _dot(ufb, toep_ref[0]) + uf * d_ref[0]
    in_scr[...] = _dot(ufb, min_ref[0])
    ar = a_ref[0, 0:1, :]
    ai = a_ref[0, 1:2, :]

    def step(c, carry):
        sr, si = carry
        sp_scr[pl.ds(c, 1), :half] = sr
        sp_scr[pl.ds(c, 1), half:] = si
        ir = in_scr[pl.ds(c, 1), :half]
        ii = in_scr[pl.ds(c, 1), half:]
        return ar * sr - ai * si + ir, ar * si + ai * sr + ii

    z = jnp.zeros((1, half), f32)
    lax.fori_loop(0, n_c, step, (z, z), unroll=4)
    y = y + _dot(sp_scr[...].astype(bf16), mout_ref[0])
    yg = jax.nn.gelu(y)
    for t in range(T):
        y_ref[pl.ds(t, n_c, stride=T), :] = yg[:, t * LANES:(t + 1) * LANES]


def _s5(u2, bsz, seq, ops):
    toep, m_in, m_out, a_ri, d_flat = ops
    nj = toep.shape[0]
    n_c = seq // S5_CHUNK
    kdim = S5_CHUNK * LANES
    sdim = m_in.shape[2]
    return pl.pallas_call(
        _s5_kernel,
        grid=(nj, bsz),
        in_specs=[pl.BlockSpec((seq, LANES), lambda j, b: (b, j)),
                  pl.BlockSpec((1, kdim, kdim), lambda j, b: (j, 0, 0)),
                  pl.BlockSpec((1, kdim, sdim), lambda j, b: (j, 0, 0)),
                  pl.BlockSpec((1, sdim, kdim), lambda j, b: (j, 0, 0)),
                  pl.BlockSpec((1, 2, sdim // 2), lambda j, b: (j, 0, 0)),
                  pl.BlockSpec((1, 1, kdim), lambda j, b: (j, 0, 0))],
        out_specs=pl.BlockSpec((seq, LANES), lambda j, b: (b, j)),
        out_shape=jax.ShapeDtypeStruct(u2.shape, f32),
        scratch_shapes=[pltpu.VMEM((n_c, sdim), f32), pltpu.VMEM((n_c, sdim), f32)],
        compiler_params=pltpu.CompilerParams(dimension_semantics=("parallel", "parallel"),
                                             vmem_limit_bytes=VMEM_LIMIT),
        name="s5",
    )(u2, toep, m_in, m_out, a_ri, d_flat)


def _moba_kernel(qt_ref, k_ref, vt_ref, km_ref, o_ref, kaug_ref, s0_scr, s1_scr):
    blk = MOBA_BLOCK
    hd = ATT_HEAD_DIM
    nb = k_ref.shape[0] // blk
    i = pl.program_id(2)

    @pl.when(i == 0)
    def _build():
        lane = lax.broadcasted_iota(jnp.int32, (blk, LANES), 1)

        def build(j, carry):
            rows = pl.ds(pl.multiple_of(j * blk, blk), blk)
            kaug_ref[rows, :LANES] = k_ref[rows, :]
            kaug_ref[rows, LANES:] = jnp.where(lane == j, 1.0, 0.0).astype(bf16)
            return carry
        lax.fori_loop(0, nb, build, 0)

    qt = qt_ref[0]
    qtf = qt.astype(f32) * (hd ** -0.5)
    km = km_ref[0]
    lane_nb = lax.broadcasted_iota(jnp.int32, (nb, LANES), 1)
    feat = lax.broadcasted_iota(jnp.int32, (LANES, blk), 0)
    blk_id = lax.broadcasted_iota(jnp.int32, (nb, blk), 0)
    blk_f = blk_id.astype(f32)
    q_cols, bias_cols = [], []
    for hh in range(2):
        km_hi, km_lo = _split_bf16(jnp.where((lane_nb < hd) if hh == 0 else (lane_nb >= hd), km, 0.0))
        gate = _dot(km_hi, qt) + _dot(km_lo, qt)
        g = jnp.where(blk_id < i, gate, -jnp.inf)
        sel = blk_id == i
        for _ in range(MOBA_TOPK):
            mx = jnp.max(g, axis=0, keepdims=True)
            first = jnp.min(jnp.where(g == mx, blk_f, float(nb)), axis=0, keepdims=True)
            pick = (blk_f == first) & (mx > -jnp.inf)
            sel = sel | pick
            g = jnp.where(pick, -jnp.inf, g)
        bias_cols.append(jnp.where(sel, 0.0, NEG))
        q_cols.append(jnp.where((feat < hd) if hh == 0 else (feat >= hd), qtf, 0.0))
    qa = jnp.concatenate([jnp.concatenate(q_cols, axis=1), jnp.concatenate(bias_cols, axis=1),
                          jnp.zeros((LANES - nb, 2 * blk), f32)], axis=0).astype(bf16)

    key_pos = lax.broadcasted_iota(jnp.int32, (blk, 2 * blk), 0)
    q_pos = lax.broadcasted_iota(jnp.int32, (blk, 2 * blk), 1) % blk

    def scores(j):
        j = jnp.minimum(j, nb - 1)
        return _dot(kaug_ref[pl.ds(pl.multiple_of(j * blk, blk), blk), :], qa)

    def update(state, blocks):
        m, l, acc = state
        m_new = m
        for _, s in blocks:
            m_new = jnp.maximum(m_new, jnp.max(s, axis=0, keepdims=True))
        alpha = jnp.exp(m - m_new)
        l = alpha * l
        acc = alpha * acc
        for j, s in blocks:
            p = jnp.exp(s - m_new)
            l = l + jnp.sum(p, axis=0, keepdims=True)
            pb = p.astype(bf16)
            acc = acc + jnp.concatenate([_dot(vt_ref[j, :hd, :], pb[:, :blk]),
                                         _dot(vt_ref[j, hd:, :], pb[:, blk:])], axis=1)
        return m_new, l, acc

    def pair_scores(p, dst):
        dst[0] = scores(2 * p)
        dst[1] = scores(2 * p + 1)

    def pair_update(state, p, src, masked):
        blocks = []
        for r in range(2):
            j = 2 * p + r
            s = src[r]
            if masked:
                off = jnp.where(j < i, blk, jnp.where(j == i, 0, -2 * blk))
                s = jnp.where(key_pos <= q_pos + off, s, NEG)
            blocks.append((jnp.minimum(j, nb - 1), s))
        return update(state, blocks)

    def quad_body(q, state):
        pair_scores(2 * q + 1, s1_scr)
        state = pair_update(state, 2 * q, s0_scr, False)
        pair_scores(2 * q + 2, s0_scr)
        return pair_update(state, 2 * q + 1, s1_scr, False)

    state = (jnp.full((1, 2 * blk), -jnp.inf, f32), jnp.zeros((1, 2 * blk), f32), jnp.zeros((hd, 2 * blk), f32))
    n_quad = i // 4
    pair_scores(0, s0_scr)
    state = lax.fori_loop(0, n_quad, quad_body, state)
    pair_scores(2 * n_quad + 1, s1_scr)
    state = pair_update(state, 2 * n_quad, s0_scr, True)
    m, l, acc = pair_update(state, 2 * n_quad + 1, s1_scr, True)
    out = acc / l
    o_ref[...] = jnp.concatenate([out[:, :blk], out[:, blk:]], axis=0).T.astype(o_ref.dtype)


def _moba(qt, k2, vt, kmean, bsz, seq):
    nb = seq // MOBA_BLOCK
    assert seq % MOBA_BLOCK == 0 and nb % SUBLANES == 0 and nb <= LANES
    npair = D_ATT // LANES
    return pl.pallas_call(
        _moba_kernel,
        grid=(bsz, npair, nb),
        in_specs=[pl.BlockSpec((1, LANES, MOBA_BLOCK), lambda b, h, i: (b * nb + i, h, 0)),
                  pl.BlockSpec((seq, LANES), lambda b, h, i: (b, h)),
                  pl.BlockSpec((nb, LANES, MOBA_BLOCK), lambda b, h, i: (b, h, 0)),
                  pl.BlockSpec((1, nb, LANES), lambda b, h, i: (b, 0, h))],
        out_specs=pl.BlockSpec((MOBA_BLOCK, LANES), lambda b, h, i: (b * nb + i, h)),
        out_shape=jax.ShapeDtypeStruct(k2.shape, bf16),
        scratch_shapes=[pltpu.VMEM((seq, 2 * LANES), bf16),
                        pltpu.VMEM((2, MOBA_BLOCK, 2 * MOBA_BLOCK), f32),
                        pltpu.VMEM((2, MOBA_BLOCK, 2 * MOBA_BLOCK), f32)],
        compiler_params=pltpu.CompilerParams(dimension_semantics=("parallel", "parallel", "arbitrary"),
                                             vmem_limit_bytes=VMEM_LIMIT),
        name="moba",
    )(qt, k2, vt, kmean)


def _outproj_kernel(x_ref, yg_ref, att_ref, wglu_ref, wout_ref, g_ref, b_ref, h_ref):
    yg = yg_ref[...]
    z = yg * jax.nn.sigmoid(_dot(yg.astype(bf16), wglu_ref[...]))
    cat = jnp.concatenate([z.astype(bf16), att_ref[...]], axis=1)
    r = DN_ALPHA * x_ref[...] + _dot(cat, wout_ref[...])
    h_ref[...] = _layernorm(r, g_ref[...], b_ref[...])


def _outproj(x2, yg, att, wglu_b, wout_b, g, b):
    n, d_model = x2.shape
    return pl.pallas_call(
        _outproj_kernel,
        grid=(n // TM_PROJ,),
        in_specs=[pl.BlockSpec((TM_PROJ, d_model), lambda i: (i, 0)),
                  pl.BlockSpec((TM_PROJ, D_SSM), lambda i: (i, 0)),
                  pl.BlockSpec((TM_PROJ, D_ATT), lambda i: (i, 0)),
                  pl.BlockSpec(wglu_b.shape, lambda i: (0, 0)),
                  pl.BlockSpec(wout_b.shape, lambda i: (0, 0)),
                  pl.BlockSpec((1, d_model), lambda i: (0, 0)),
                  pl.BlockSpec((1, d_model), lambda i: (0, 0))],
        out_specs=pl.BlockSpec((TM_PROJ, d_model), lambda i: (i, 0)),
        out_shape=jax.ShapeDtypeStruct((n, d_model), f32),
        compiler_params=pltpu.CompilerParams(dimension_semantics=("parallel",),
                                             vmem_limit_bytes=VMEM_LIMIT),
        name="outproj",
    )(x2, yg, att, wglu_b, wout_b, g, b)


def _bitonic_sort_desc(v):
    n = len(v)
    k = 2
    while k <= n:
        j = k // 2
        while j >= 1:
            for a in range(n):
                b = a ^ j
                if b > a:
                    hi = jnp.maximum(v[a], v[b])
                    lo = jnp.minimum(v[a], v[b])
                    v[a], v[b] = (hi, lo) if (a & k) == 0 else (lo, hi)
            j //= 2
        k *= 2
    return v


def _bitonic_merge_desc(v):
    j = len(v) // 2
    while j >= 1:
        for a in range(len(v)):
            b = a ^ j
            if b > a:
                v[a], v[b] = jnp.maximum(v[a], v[b]), jnp.minimum(v[a], v[b])
        j //= 2
    return v


def _top16_sorted(st):
    k = PEER_TOPK
    x = _bitonic_sort_desc([st[SUBLANES * a:SUBLANES * (a + 1), :] for a in range(PEER_N_KEYS // SUBLANES)])
    for shift in (4, 2, 1):
        y = [pltpu.roll(x[k - 1 - a], shift, 0) for a in range(k)]
        x = _bitonic_merge_desc([jnp.maximum(x[a], y[a]) for a in range(k)])
    return x


def _dup_bf16(x):
    u = pltpu.bitcast(x.astype(bf16).astype(f32), jnp.uint32)
    return pltpu.bitcast(u | (u >> 16), f32)


def _bf16_rows(row):
    return pltpu.bitcast(jnp.broadcast_to(row, (SUBLANES, row.shape[1])), bf16)[None]


def _peer_kernel(n_exp, h_ref, wq_ref, sk_ref, u_ref, vt_ref, g_ref, b_ref, o_ref,
                 q_scr, pr_scr, rq_scr, srt_scr, row_scr, acc_scr, hbt_scr, a_scr):
    eb = pl.program_id(1)
    n_blk = n_exp // EB_PEER
    tt = h_ref.shape[0]
    nk = PEER_N_KEYS
    k = PEER_TOPK

    chunk = 2 * nk
    n_chunk = EB_PEER // chunk

    def activations():
        for s in range(n_chunk):
            a_scr[s] = _dot(u_ref[s * chunk:(s + 1) * chunk, :], hbt_scr[...])

    @pl.when(eb == 0)
    def _prologue():
        hbt_scr[...] = h_ref[...].T.astype(bf16)
        q = _dot(h_ref[...].astype(bf16), wq_ref[...])
        for hc in range(2 * PEER_HEADS):
            q_scr[hc] = q[:, hc * PEER_HALF:(hc + 1) * PEER_HALF]

        def head_sort(h, carry):
            for c in range(2):
                q_hi, q_lo = _split_bf16(q_scr[2 * h + c])
                k_hi, k_lo = _split_bf16(sk_ref[2 * h + c])
                st = _dot_nt(k_hi, q_hi) + _dot_nt(k_hi, q_lo) + _dot_nt(k_lo, q_hi)
                pr_scr[c, h] = st
                srt = _top16_sorted(st)
                for a in range(k):
                    srt_scr[c, a, pl.ds(h, 1), :] = srt[a][0:1, :]
            return carry
        lax.fori_loop(0, PEER_HEADS, head_sort, 0)

        a_s = [srt_scr[0, a] for a in range(k)]
        b_s = [srt_scr[1, a] for a in range(k)]
        cands = [a_s[a] + b_s[b] for a in range(k) for b in range(k) if (a + 1) * (b + 1) <= k]
        cands += [jnp.full_like(cands[0], -jnp.inf)] * (64 - len(cands))
        cs = _bitonic_sort_desc(cands)
        zsum = jnp.zeros_like(cs[0])
        for a in range(k):
            zsum = zsum + jnp.exp(cs[a] - cs[0])
        row_scr[0] = 0.5 * (cs[k - 1] + cs[k])
        row_scr[1] = 1.0 / zsum
        row_scr[2] = a_s[0]
        row_scr[3] = b_s[0]
        row_scr[4] = a_s[k - 1]
        row_scr[5] = b_s[k - 1]

        def head_prep(h, carry):
            hrow = pl.ds(h, 1)
            at = pr_scr[0, h]
            bt = pr_scr[1, h]
            thr = row_scr[0, hrow, :] - at
            r_cnt = jnp.zeros_like(at)
            b_rank = jnp.zeros_like(bt)
            for a in range(k):
                b_a = srt_scr[1, a, hrow, :]
                r_cnt = jnp.where(b_a > thr, float(a + 1), r_cnt)
                b_rank = jnp.where(b_a > bt, float(a + 1), b_rank)
            in_a = at >= row_scr[4, hrow, :]
            p_w = jnp.where(in_a, jnp.exp(at - row_scr[2, hrow, :]), 0.0) * row_scr[1, hrow, :]
            pr_scr[0, h] = _dup_bf16(p_w)
            pr_scr[1, h] = _dup_bf16(jnp.where(in_a, r_cnt, 0.0))
            q_w = jnp.where(bt >= row_scr[5, hrow, :], jnp.exp(bt - row_scr[3, hrow, :]), 0.0)
            rq_scr[0, h] = b_rank.astype(bf16).reshape(nk // BF16_ROWS, BF16_ROWS, tt)
            rq_scr[1, h] = q_w.astype(bf16).reshape(nk // BF16_ROWS, BF16_ROWS, tt)
            return carry
        lax.fori_loop(0, PEER_HEADS, head_prep, 0)
        acc_scr[...] = jnp.zeros_like(acc_scr)

    activations()
    for s in range(n_chunk):
        halves = []
        for half in range(chunk // nk):
            n1 = pl.ds((EB_PEER // nk) * eb + s * (chunk // nk) + half, 1)
            g = jnp.zeros((nk // BF16_ROWS, BF16_ROWS, tt), bf16)
            for h in range(PEER_HEADS):
                p_row = _bf16_rows(pr_scr[0, h, n1, :])
                r_row = _bf16_rows(pr_scr[1, h, n1, :])
                g = g + p_row * jnp.where(rq_scr[0, h] < r_row, rq_scr[1, h], 0)
            halves.append(g.reshape(nk, tt))
        w = jnp.concatenate(halves, axis=0) * jax.nn.gelu(a_scr[s].astype(bf16))
        acc_scr[...] += _dot(vt_ref[:, s * chunk:(s + 1) * chunk], w)

    @pl.when(eb == n_blk - 1)
    def _last():
        r = DN_ALPHA * h_ref[...] + acc_scr[...].T
        o_ref[...] = _layernorm(r, g_ref[...], b_ref[...])


def _peer(h1, wq_b, sub_keys, u_b, vt_b, g, b):
    n, d_model = h1.shape
    n_exp = u_b.shape[0]
    tt = TT_PEER
    sk = sub_keys.reshape(2 * PEER_HEADS, PEER_N_KEYS, PEER_HALF).astype(f32)
    n_blk = n_exp // EB_PEER
    const = dict(pipeline_mode=pl.Buffered(1))
    return pl.pallas_call(
        functools.partial(_peer_kernel, n_exp),
        grid=(n // tt, n_blk),
        in_specs=[pl.BlockSpec((tt, d_model), lambda t, e: (t, 0)),
                  pl.BlockSpec(wq_b.shape, lambda t, e: (0, 0), **const),
                  pl.BlockSpec(sk.shape, lambda t, e: (0, 0, 0), **const),
                  pl.BlockSpec((EB_PEER, d_model), lambda t, e: (e, 0)),
                  pl.BlockSpec((d_model, EB_PEER), lambda t, e: (0, e)),
                  pl.BlockSpec((1, d_model), lambda t, e: (0, 0), **const),
                  pl.BlockSpec((1, d_model), lambda t, e: (0, 0), **const)],
        out_specs=pl.BlockSpec((tt, d_model), lambda t, e: (t, 0)),
        out_shape=jax.ShapeDtypeStruct((n, d_model), f32),
        scratch_shapes=[pltpu.VMEM((2 * PEER_HEADS, tt, PEER_HALF), f32),
                        pltpu.VMEM((2, PEER_HEADS, PEER_N_KEYS, tt), f32),
                        pltpu.VMEM((2, PEER_HEADS, PEER_N_KEYS // BF16_ROWS, BF16_ROWS, tt), bf16),
                        pltpu.VMEM((2, PEER_TOPK, PEER_HEADS, tt), f32),
                        pltpu.VMEM((6, PEER_HEADS, tt), f32),
                        pltpu.VMEM((d_model, tt), f32),
                        pltpu.VMEM((d_model, tt), bf16),
                        pltpu.VMEM((EB_PEER // (2 * PEER_N_KEYS), 2 * PEER_N_KEYS, tt), f32)],
        compiler_params=pltpu.CompilerParams(dimension_semantics=("parallel", "arbitrary"),
                                             vmem_limit_bytes=VMEM_LIMIT),
        name="peer",
    )(h1, wq_b, sk, u_b, vt_b, g, b)


def kernel(x, w_in, ssm_a_re, ssm_a_im, ssm_log_dt, ssm_b_re, ssm_b_im, ssm_c_re, ssm_c_im, ssm_d,
           ssm_w_glu, w_out, ln1_g, ln1_b, peer_w_q, peer_sub_keys, peer_u, peer_v, ln2_g, ln2_b):
    bsz, seq, d_model = x.shape
    n = bsz * seq
    h = x.reshape(n, d_model)
    for l in range(w_in.shape[0]):
        u2, k2, qt, vt, kmean = _inproj(h, w_in[l])
        ops = _s5_operators(ssm_a_re[l], ssm_a_im[l], ssm_log_dt[l], ssm_b_re[l], ssm_b_im[l],
                            ssm_c_re[l], ssm_c_im[l], ssm_d[l])
        yg = _s5(u2, bsz, seq, ops)
        att = _moba(qt, k2, vt, kmean.reshape(bsz, seq // MOBA_BLOCK, D_ATT), bsz, seq)
        h1 = _outproj(h, yg, att, ssm_w_glu[l].astype(bf16), w_out[l].astype(bf16),
                      ln1_g[l].reshape(1, d_model), ln1_b[l].reshape(1, d_model))
        h = _peer(h1, peer_w_q[l].astype(bf16), peer_sub_keys[l], peer_u[l].astype(bf16),
                  peer_v[l].T.astype(bf16), ln2_g[l].reshape(1, d_model), ln2_b[l].reshape(1, d_model))
    return h.reshape(bsz, seq, d_model)
```

```python
import functools
import math

import jax
import jax.numpy as jnp
from jax import lax
from jax.experimental import pallas as pl
from jax.experimental.pallas import tpu as pltpu

f32 = jnp.float32
bf16 = jnp.bfloat16

D_SSM = 512
SSM_GROUP = 16
SSM_STATE = 64
N_ATT_HEADS = 8
ATT_HEAD_DIM = 64
D_ATT = N_ATT_HEADS * ATT_HEAD_DIM
MOBA_BLOCK = 256
MOBA_TOPK = 3
PEER_HEADS = 8
PEER_HALF = 128
PEER_N_KEYS = 128
PEER_TOPK = 16
LN_EPS = 1e-5
DEPTH = 1
DN_ALPHA = (2.0 * DEPTH) ** 0.25
NEG = -1e30

LANES = 128
SUBLANES = 8
BF16_ROWS = 2 * SUBLANES
VMEM_LIMIT = 48 * 1024 * 1024

TM_PROJ = 1024
S5_CHUNK = 8
S5_LANE_GROUPS = LANES // SSM_GROUP
TT_PEER = 512
EB_PEER = 2048


def _dot(a, b):
    return jnp.dot(a, b, preferred_element_type=f32)


def _dot_nt(a, b):
    return lax.dot_general(a, b, (((1,), (1,)), ((), ())), preferred_element_type=f32)


def _dot_tn(a, b):
    return lax.dot_general(a, b, (((0,), (0,)), ((), ())), preferred_element_type=f32)


def _split_bf16(x):
    hi = x.astype(bf16)
    lo = (x - hi.astype(f32)).astype(bf16)
    return hi, lo


def _layernorm(r, g, b):
    mu = jnp.mean(r, axis=-1, keepdims=True)
    d = r - mu
    var = jnp.mean(d * d, axis=-1, keepdims=True)
    return d * lax.rsqrt(var + LN_EPS) * g + b


def _inproj_kernel(x_ref, wuk_ref, wqvt_ref, u_ref, k_ref, qt_ref, vt_ref, km_ref):
    xb = x_ref[...].astype(bf16)
    acc = _dot(xb, wuk_ref[...])
    u_ref[...] = acc[:, :D_SSM]
    kk = acc[:, D_SSM:]
    k_ref[...] = kk.astype(bf16)
    qvt = _dot_nt(wqvt_ref[...], xb)
    for r in range(TM_PROJ // MOBA_BLOCK):
        cols = slice(r * MOBA_BLOCK, (r + 1) * MOBA_BLOCK)
        km_ref[r] = jnp.mean(kk[cols], axis=0, keepdims=True)
        qt_ref[r] = qvt[:D_ATT, cols].astype(bf16)
        vt_ref[r] = qvt[D_ATT:, cols].astype(bf16)


def _inproj(x2, w_in):
    n, d_model = x2.shape
    nb = TM_PROJ // MOBA_BLOCK
    w_uk = jnp.concatenate([w_in[:, :D_SSM], w_in[:, D_SSM + D_ATT:D_SSM + 2 * D_ATT]], axis=1).astype(bf16)
    w_qvt = jnp.concatenate([w_in[:, D_SSM:D_SSM + D_ATT], w_in[:, D_SSM + 2 * D_ATT:]], axis=1).T.astype(bf16)
    return pl.pallas_call(
        _inproj_kernel,
        grid=(n // TM_PROJ,),
        in_specs=[pl.BlockSpec((TM_PROJ, d_model), lambda i: (i, 0)),
                  pl.BlockSpec(w_uk.shape, lambda i: (0, 0)),
                  pl.BlockSpec(w_qvt.shape, lambda i: (0, 0))],
        out_specs=[pl.BlockSpec((TM_PROJ, D_SSM), lambda i: (i, 0)),
                   pl.BlockSpec((TM_PROJ, D_ATT), lambda i: (i, 0)),
                   pl.BlockSpec((nb, D_ATT, MOBA_BLOCK), lambda i: (i, 0, 0)),
                   pl.BlockSpec((nb, D_ATT, MOBA_BLOCK), lambda i: (i, 0, 0)),
                   pl.BlockSpec((nb, 1, D_ATT), lambda i: (i, 0, 0))],
        out_shape=[jax.ShapeDtypeStruct((n, D_SSM), f32),
                   jax.ShapeDtypeStruct((n, D_ATT), bf16),
                   jax.ShapeDtypeStruct((n // MOBA_BLOCK, D_ATT, MOBA_BLOCK), bf16),
                   jax.ShapeDtypeStruct((n // MOBA_BLOCK, D_ATT, MOBA_BLOCK), bf16),
                   jax.ShapeDtypeStruct((n // MOBA_BLOCK, 1, D_ATT), f32)],
        compiler_params=pltpu.CompilerParams(dimension_semantics=("parallel",),
                                             vmem_limit_bytes=VMEM_LIMIT),
        name="inproj",
    )(x2, w_uk, w_qvt)


def _s5_operators(a_re, a_im, log_dt, b_re, b_im, c_re, c_im, d_skip):
    T = S5_CHUNK
    G, P = a_re.shape
    H = SSM_GROUP
    nj = G // S5_LANE_GROUPS
    g8 = S5_LANE_GROUPS
    lam = lax.complex(a_re.astype(f32), a_im.astype(f32))
    dt = jnp.exp(log_dt.astype(f32))[:, None]
    lam_bar = jnp.exp(lam * dt)
    b_c = lax.complex(b_re.astype(f32), b_im.astype(f32))
    b_bar = ((lam_bar - 1.0) / lam)[:, :, None] * b_c
    c_c = lax.complex(c_re.astype(f32), c_im.astype(f32))
    pows = jnp.exp((lam * dt)[None] * jnp.arange(T + 1, dtype=f32)[:, None, None])
    def widen(x, reps):
        n = x.shape[-1]
        rep = jnp.tile(jnp.eye(n, dtype=f32), (1, reps))
        return jnp.dot(x.reshape(-1, n), rep).reshape(x.shape[:-1] + (reps * n,))

    same = jnp.arange(g8)[:, None] == (jnp.arange(g8 * H) // H)[None, :]
    same_p = jnp.arange(g8)[:, None] == (jnp.arange(g8 * P) // P)[None, :]
    kern = jnp.real(jnp.einsum('ghp,dgp,gpk->dgkh', c_c, pows[:T], b_bar))
    dd = jnp.arange(T)[None, :] - jnp.arange(T)[:, None]
    kst = jnp.where((dd >= 0)[:, :, None, None, None], kern[jnp.clip(dd, 0, T - 1)], 0.0)
    kst = widen(kst.reshape(T, T, nj, g8, H, H), g8)
    kst = jnp.where(same[None, None, None, :, None, :], kst, 0.0)
    toep = kst.transpose(2, 0, 3, 4, 1, 5).reshape(nj, T * LANES, T * LANES)
    win = pows[T - 1 - jnp.arange(T)][:, :, None, :] * jnp.swapaxes(b_bar, 1, 2)[None]
    win = jnp.stack([jnp.real(win), jnp.imag(win)], 0).reshape(2, T, nj, g8, H, P)
    win = jnp.where(same_p[None, None, None, :, None, :], widen(win, g8), 0.0)
    m_in = win.transpose(2, 1, 3, 4, 0, 5).reshape(nj, T * LANES, 2 * g8 * P)
    wout = jnp.swapaxes(c_c, 1, 2)[None] * pows[1:T + 1][:, :, :, None]
    wout = jnp.stack([jnp.real(wout), -jnp.imag(wout)], 0).reshape(2, T, nj, g8, P, H)
    wout = jnp.where(same[None, None, None, :, None, :], widen(wout, g8), 0.0)
    m_out = wout.transpose(2, 0, 3, 4, 1, 5).reshape(nj, 2 * g8 * P, T * LANES)
    a_chunk = pows[T].reshape(nj, 1, g8 * P)
    a_ri = jnp.concatenate([jnp.real(a_chunk), jnp.imag(a_chunk)], axis=1)
    d_flat = jnp.tile(d_skip.astype(f32).reshape(nj, 1, LANES), (1, 1, T))
    return toep.astype(bf16), m_in.astype(bf16), m_out.astype(bf16), a_ri, d_flat


def _s5_kernel(u_ref, toep_ref, min_ref, mout_ref, a_ref, d_ref, y_ref, in_scr, sp_scr):
    T = S5_CHUNK
    n_c = u_ref.shape[0] // T
    half = in_scr.shape[1] // 2
    uf = jnp.concatenate([u_ref[pl.ds(t, n_c, stride=T), :] for t in range(T)], axis=1)
    ufb = uf.astype(bf16)
    y = _dot(ufb, toep_ref[0]) + uf * d_ref[0]
    in_scr[...] = _dot(ufb, min_ref[0])
    ar = a_ref[0, 0:1, :]
    ai = a_ref[0, 1:2, :]

    def step(c, carry):
        sr, si = carry
        sp_scr[pl.ds(c, 1), :half] = sr
        sp_scr[pl.ds(c, 1), half:] = si
        ir = in_scr[pl.ds(c, 1), :half]
        ii = in_scr[pl.ds(c, 1), half:]
        return ar * sr - ai * si + ir, ar * si + ai * sr + ii

    z = jnp.zeros((1, half), f32)
    lax.fori_loop(0, n_c, step, (z, z), unroll=4)
    y = y + _dot(sp_scr[...].astype(bf16), mout_ref[0])
    yg = jax.nn.gelu(y)
    for t in range(T):
        y_ref[pl.ds(t, n_c, stride=T), :] = yg[:, t * LANES:(t + 1) * LANES]


def _s5(u2, bsz, seq, ops):
    toep, m_in, m_out, a_ri, d_flat = ops
    nj = toep.shape[0]
    n_c = seq // S5_CHUNK
    kdim = S5_CHUNK * LANES
    sdim = m_in.shape[2]
    return pl.pallas_call(
        _s5_kernel,
        grid=(nj, bsz),
        in_specs=[pl.BlockSpec((seq, LANES), lambda j, b: (b, j)),
                  pl.BlockSpec((1, kdim, kdim), lambda j, b: (j, 0, 0)),
                  pl.BlockSpec((1, kdim, sdim), lambda j, b: (j, 0, 0)),
                  pl.BlockSpec((1, sdim, kdim), lambda j, b: (j, 0, 0)),
                  pl.BlockSpec((1, 2, sdim // 2), lambda j, b: (j, 0, 0)),
                  pl.BlockSpec((1, 1, kdim), lambda j, b: (j, 0, 0))],
        out_specs=pl.BlockSpec((seq, LANES), lambda j, b: (b, j)),
        out_shape=jax.ShapeDtypeStruct(u2.shape, f32),
        scratch_shapes=[pltpu.VMEM((n_c, sdim), f32), pltpu.VMEM((n_c, sdim), f32)],
        compiler_params=pltpu.CompilerParams(dimension_semantics=("parallel", "parallel"),
                                             vmem_limit_bytes=VMEM_LIMIT),
        name="s5",
    )(u2, toep, m_in, m_out, a_ri, d_flat)


def _moba_kernel(qt_ref, k_ref, vt_ref, km_ref, o_ref, kaug_ref, s0_scr, s1_scr):
    blk = MOBA_BLOCK
    hd = ATT_HEAD_DIM
    nb = k_ref.shape[0] // blk
    i = pl.program_id(2)

    @pl.when(i == 0)
    def _build():
        lane = lax.broadcasted_iota(jnp.int32, (blk, LANES), 1)

        def build(j, carry):
            rows = pl.ds(pl.multiple_of(j * blk, blk), blk)
            kaug_ref[rows, :LANES] = k_ref[rows, :]
            kaug_ref[rows, LANES:] = jnp.where(lane == j, 1.0, 0.0).astype(bf16)
            return carry
        lax.fori_loop(0, nb, build, 0)

    qt = qt_ref[0]
    qtf = qt.astype(f32) * (hd ** -0.5)
    km = km_ref[0]
    lane_nb = lax.broadcasted_iota(jnp.int32, (nb, LANES), 1)
    feat = lax.broadcasted_iota(jnp.int32, (LANES, blk), 0)
    blk_id = lax.broadcasted_iota(jnp.int32, (nb, blk), 0)
    blk_f = blk_id.astype(f32)
    q_cols, bias_cols = [], []
    for hh in range(2):
        km_hi, km_lo = _split_bf16(jnp.where((lane_nb < hd) if hh == 0 else (lane_nb >= hd), km, 0.0))
        gate = _dot(km_hi, qt) + _dot(km_lo, qt)
        g = jnp.where(blk_id < i, gate, -jnp.inf)
        sel = blk_id == i
        for _ in range(MOBA_TOPK):
            mx = jnp.max(g, axis=0, keepdims=True)
            first = jnp.min(jnp.where(g == mx, blk_f, float(nb)), axis=0, keepdims=True)
            pick = (blk_f == first) & (mx > -jnp.inf)
            sel = sel | pick
            g = jnp.where(pick, -jnp.inf, g)
        bias_cols.append(jnp.where(sel, 0.0, NEG))
        q_cols.append(jnp.where((feat < hd) if hh == 0 else (feat >= hd), qtf, 0.0))
    qa = jnp.concatenate([jnp.concatenate(q_cols, axis=1), jnp.concatenate(bias_cols, axis=1),
                          jnp.zeros((LANES - nb, 2 * blk), f32)], axis=0).astype(bf16)

    key_pos = lax.broadcasted_iota(jnp.int32, (blk, 2 * blk), 0)
    q_pos = lax.broadcasted_iota(jnp.int32, (blk, 2 * blk), 1) % blk

    def scores(j):
        j = jnp.minimum(j, nb - 1)
        return _dot(kaug_ref[pl.ds(pl.multiple_of(j * blk, blk), blk), :], qa)

    def update(state, blocks):
        m, l, acc = state
        m_new = m
        for _, s in blocks:
            m_new = jnp.maximum(m_new, jnp.max(s, axis=0, keepdims=True))
        alpha = jnp.exp(m - m_new)
        l = alpha * l
        acc = alpha * acc
        for j, s in blocks:
            p = jnp.exp(s - m_new)
            l = l + jnp.sum(p, axis=0, keepdims=True)
            pb = p.astype(bf16)
            acc = acc + jnp.concatenate([_dot(vt_ref[j, :hd, :], pb[:, :blk]),
                                         _dot(vt_ref[j, hd:, :], pb[:, blk:])], axis=1)
        return m_new, l, acc

    def pair_scores(p, dst):
        dst[0] = scores(2 * p)
        dst[1] = scores(2 * p + 1)

    def pair_update(state, p, src, masked):
        blocks = []
        for r in range(2):
            j = 2 * p + r
            s = src[r]
            if masked:
                off = jnp.where(j < i, blk, jnp.where(j == i, 0, -2 * blk))
                s = jnp.where(key_pos <= q_pos + off, s, NEG)
            blocks.append((jnp.minimum(j, nb - 1), s))
        return update(state, blocks)

    def quad_body(q, state):
        pair_scores(2 * q + 1, s1_scr)
        state = pair_update(state, 2 * q, s0_scr, False)
        pair_scores(2 * q + 2, s0_scr)
        return pair_update(state, 2 * q + 1, s1_scr, False)

    state = (jnp.full((1, 2 * blk), -jnp.inf, f32), jnp.zeros((1, 2 * blk), f32), jnp.zeros((hd, 2 * blk), f32))
    n_quad = i // 4
    pair_scores(0, s0_scr)
    state = lax.fori_loop(0, n_quad, quad_body, state)
    pair_scores(2 * n_quad + 1, s1_scr)
    state = pair_update(state, 2 * n_quad, s0_scr, True)
    m, l, acc = pair_update(state, 2 * n_quad + 1, s1_scr, True)
    out = acc / l
    o_ref[...] = jnp.concatenate([out[:, :blk], out[:, blk:]], axis=0).T.astype(o_ref.dtype)


def _moba(qt, k2, vt, kmean, bsz, seq):
    nb = seq // MOBA_BLOCK
    assert seq % MOBA_BLOCK == 0 and nb % SUBLANES == 0 and nb <= LANES
    npair = D_ATT // LANES
    return pl.pallas_call(
        _moba_kernel,
        grid=(bsz, npair, nb),
        in_specs=[pl.BlockSpec((1, LANES, MOBA_BLOCK), lambda b, h, i: (b * nb + i, h, 0)),
                  pl.BlockSpec((seq, LANES), lambda b, h, i: (b, h)),
                  pl.BlockSpec((nb, LANES, MOBA_BLOCK), lambda b, h, i: (b, h, 0)),
                  pl.BlockSpec((1, nb, LANES), lambda b, h, i: (b, 0, h))],
        out_specs=pl.BlockSpec((MOBA_BLOCK, LANES), lambda b, h, i: (b * nb + i, h)),
        out_shape=jax.ShapeDtypeStruct(k2.shape, bf16),
        scratch_shapes=[pltpu.VMEM((seq, 2 * LANES), bf16),
                        pltpu.VMEM((2, MOBA_BLOCK, 2 * MOBA_BLOCK), f32),
                        pltpu.VMEM((2, MOBA_BLOCK, 2 * MOBA_BLOCK), f32)],
        compiler_params=pltpu.CompilerParams(dimension_semantics=("parallel", "parallel", "arbitrary"),
                                             vmem_limit_bytes=VMEM_LIMIT),
        name="moba",
    )(qt, k2, vt, kmean)


def _outproj_kernel(x_ref, yg_ref, att_ref, wglu_ref, wout_ref, g_ref, b_ref, h_ref):
    yg = yg_ref[...]
    z = yg * jax.nn.sigmoid(_dot(yg.astype(bf16), wglu_ref[...]))
    cat = jnp.concatenate([z.astype(bf16), att_ref[...]], axis=1)
    r = DN_ALPHA * x_ref[...] + _dot(cat, wout_ref[...])
    h_ref[...] = _layernorm(r, g_ref[...], b_ref[...])


def _outproj(x2, yg, att, wglu_b, wout_b, g, b):
    n, d_model = x2.shape
    return pl.pallas_call(
        _outproj_kernel,
        grid=(n // TM_PROJ,),
        in_specs=[pl.BlockSpec((TM_PROJ, d_model), lambda i: (i, 0)),
                  pl.BlockSpec((TM_PROJ, D_SSM), lambda i: (i, 0)),
                  pl.BlockSpec((TM_PROJ, D_ATT), lambda i: (i, 0)),
                  pl.BlockSpec(wglu_b.shape, lambda i: (0, 0)),
                  pl.BlockSpec(wout_b.shape, lambda i: (0, 0)),
                  pl.BlockSpec((1, d_model), lambda i: (0, 0)),
                  pl.BlockSpec((1, d_model), lambda i: (0, 0))],
        out_specs=pl.BlockSpec((TM_PROJ, d_model), lambda i: (i, 0)),
        out_shape=jax.ShapeDtypeStruct((n, d_model), f32),
        compiler_params=pltpu.CompilerParams(dimension_semantics=("parallel",),
                                             vmem_limit_bytes=VMEM_LIMIT),
        name="outproj",
    )(x2, yg, att, wglu_b, wout_b, g, b)


def _bitonic_sort_desc(v):
    n = len(v)
    k = 2
    while k <= n:
        j = k // 2
        while j >= 1:
            for a in range(n):
                b = a ^ j
                if b > a:
                    hi = jnp.maximum(v[a], v[b])
                    lo = jnp.minimum(v[a], v[b])
                    v[a], v[b] = (hi, lo) if (a & k) == 0 else (lo, hi)
            j //= 2
        k *= 2
    return v


def _bitonic_merge_desc(v):
    j = len(v) // 2
    while j >= 1:
        for a in range(len(v)):
            b = a ^ j
            if b > a:
                v[a], v[b] = jnp.maximum(v[a], v[b]), jnp.minimum(v[a], v[b])
        j //= 2
    return v


def _top16_sorted(st):
    k = PEER_TOPK
    x = _bitonic_sort_desc([st[SUBLANES * a:SUBLANES * (a + 1), :] for a in range(PEER_N_KEYS // SUBLANES)])
    for shift in (4, 2, 1):
        y = [pltpu.roll(x[k - 1 - a], shift, 0) for a in range(k)]
        x = _bitonic_merge_desc([jnp.maximum(x[a], y[a]) for a in range(k)])
    return x


def _dup_bf16(x):
    u = pltpu.bitcast(x.astype(bf16).astype(f32), jnp.uint32)
    return pltpu.bitcast(u | (u >> 16), f32)


def _bf16_rows(row):
    return pltpu.bitcast(jnp.broadcast_to(row, (SUBLANES, row.shape[1])), bf16)[None]


def _peer_kernel(n_exp, h_ref, wq_ref, sk_ref, u_ref, v_ref, g_ref, b_ref, o_ref,
                 q_scr, pr_scr, rq_scr, srt_scr, row_scr, acc_scr, hbt_scr, a_scr):
    eb = pl.program_id(1)
    n_blk = n_exp // EB_PEER
    tt = h_ref.shape[0]
    nk = PEER_N_KEYS
    k = PEER_TOPK

    chunk = 2 * nk
    n_chunk = EB_PEER // chunk

    def activations():
        for s in range(n_chunk):
            a_scr[s] = _dot(u_ref[s * chunk:(s + 1) * chunk, :], hbt_scr[...])

    @pl.when(eb == 0)
    def _prologue():
        hbt_scr[...] = h_ref[...].T.astype(bf16)
        q = _dot(h_ref[...].astype(bf16), wq_ref[...])
        for hc in range(2 * PEER_HEADS):
            q_scr[hc] = q[:, hc * PEER_HALF:(hc + 1) * PEER_HALF]

        def head_sort(h, carry):
            for c in range(2):
                q_hi, q_lo = _split_bf16(q_scr[2 * h + c])
                k_hi, k_lo = _split_bf16(sk_ref[2 * h + c])
                st = _dot_nt(k_hi, q_hi) + _dot_nt(k_hi, q_lo) + _dot_nt(k_lo, q_hi)
                pr_scr[c, h] = st
                srt = _top16_sorted(st)
                for a in range(k):
                    srt_scr[c, a, pl.ds(h, 1), :] = srt[a][0:1, :]
            return carry
        lax.fori_loop(0, PEER_HEADS, head_sort, 0)

        a_s = [srt_scr[0, a] for a in range(k)]
        b_s = [srt_scr[1, a] for a in range(k)]
        cands = [a_s[a] + b_s[b] for a in range(k) for b in range(k) if (a + 1) * (b + 1) <= k]
        cands += [jnp.full_like(cands[0], -jnp.inf)] * (64 - len(cands))
        cs = _bitonic_sort_desc(cands)
        zsum = jnp.zeros_like(cs[0])
        for a in range(k):
            zsum = zsum + jnp.exp(cs[a] - cs[0])
        row_scr[0] = 0.5 * (cs[k - 1] + cs[k])
        row_scr[1] = 1.0 / zsum
        row_scr[2] = a_s[0]
        row_scr[3] = b_s[0]
        row_scr[4] = a_s[k - 1]
        row_scr[5] = b_s[k - 1]

        def head_prep(h, carry):
            hrow = pl.ds(h, 1)
            at = pr_scr[0, h]
            bt = pr_scr[1, h]
            thr = row_scr[0, hrow, :] - at
            r_cnt = jnp.zeros_like(at)
            b_rank = jnp.zeros_like(bt)
            for a in range(k):
                b_a = srt_scr[1, a, hrow, :]
                r_cnt = jnp.where(b_a > thr, float(a + 1), r_cnt)
                b_rank = jnp.where(b_a > bt, float(a + 1), b_rank)
            in_a = at >= row_scr[4, hrow, :]
            p_w = jnp.where(in_a, jnp.exp(at - row_scr[2, hrow, :]), 0.0) * row_scr[1, hrow, :]
            pr_scr[0, h] = _dup_bf16(p_w)
            pr_scr[1, h] = _dup_bf16(jnp.where(in_a, r_cnt, 0.0))
            q_w = jnp.where(bt >= row_scr[5, hrow, :], jnp.exp(bt - row_scr[3, hrow, :]), 0.0)
            rq_scr[0, h] = b_rank.astype(bf16).reshape(nk // BF16_ROWS, BF16_ROWS, tt)
            rq_scr[1, h] = q_w.astype(bf16).reshape(nk // BF16_ROWS, BF16_ROWS, tt)
            return carry
        lax.fori_loop(0, PEER_HEADS, head_prep, 0)
        acc_scr[...] = jnp.zeros_like(acc_scr)

    activations()
    for s in range(n_chunk):
        halves = []
        for half in range(chunk // nk):
            n1 = pl.ds((EB_PEER // nk) * eb + s * (chunk // nk) + half, 1)
            g = jnp.zeros((nk // BF16_ROWS, BF16_ROWS, tt), bf16)
            for h in range(PEER_HEADS):
                p_row = _bf16_rows(pr_scr[0, h, n1, :])
                r_row = _bf16_rows(pr_scr[1, h, n1, :])
                g = g + p_row * jnp.where(rq_scr[0, h] < r_row, rq_scr[1, h], 0)
            halves.append(g.reshape(nk, tt))
        w = jnp.concatenate(halves, axis=0) * jax.nn.gelu(a_scr[s].astype(bf16))
        acc_scr[...] += _dot_tn(v_ref[s * chunk:(s + 1) * chunk, :], w)

    @pl.when(eb == n_blk - 1)
    def _last():
        r = DN_ALPHA * h_ref[...] + acc_scr[...].T
        o_ref[...] = _layernorm(r, g_ref[...], b_ref[...])


def _peer(h1, wq_b, sub_keys, u_b, v_b, g, b):
    n, d_model = h1.shape
    n_exp = u_b.shape[0]
    tt = TT_PEER
    sk = sub_keys.reshape(2 * PEER_HEADS, PEER_N_KEYS, PEER_HALF).astype(f32)
    n_blk = n_exp // EB_PEER
    const = dict(pipeline_mode=pl.Buffered(1))
    return pl.pallas_call(
        functools.partial(_peer_kernel, n_exp),
        grid=(n // tt, n_blk),
        in_specs=[pl.BlockSpec((tt, d_model), lambda t, e: (t, 0)),
                  pl.BlockSpec(wq_b.shape, lambda t, e: (0, 0), **const),
                  pl.BlockSpec(sk.shape, lambda t, e: (0, 0, 0), **const),
                  pl.BlockSpec((EB_PEER, d_model), lambda t, e: (e, 0)),
                  pl.BlockSpec((EB_PEER, d_model), lambda t, e: (e, 0)),
                  pl.BlockSpec((1, d_model), lambda t, e: (0, 0), **const),
                  pl.BlockSpec((1, d_model), lambda t, e: (0, 0), **const)],
        out_specs=pl.BlockSpec((tt, d_model), lambda t, e: (t, 0)),
        out_shape=jax.ShapeDtypeStruct((n, d_model), f32),
        scratch_shapes=[pltpu.VMEM((2 * PEER_HEADS, tt, PEER_HALF), f32),
                        pltpu.VMEM((2, PEER_HEADS, PEER_N_KEYS, tt), f32),
                        pltpu.VMEM((2, PEER_HEADS, PEER_N_KEYS // BF16_ROWS, BF16_ROWS, tt), bf16),
                        pltpu.VMEM((2, PEER_TOPK, PEER_HEADS, tt), f32),
                        pltpu.VMEM((6, PEER_HEADS, tt), f32),
                        pltpu.VMEM((d_model, tt), f32),
                        pltpu.VMEM((d_model, tt), bf16),
                        pltpu.VMEM((EB_PEER // (2 * PEER_N_KEYS), 2 * PEER_N_KEYS, tt), f32)],
        compiler_params=pltpu.CompilerParams(dimension_semantics=("parallel", "arbitrary"),
                                             vmem_limit_bytes=VMEM_LIMIT),
        name="peer",
    )(h1, wq_b, sk, u_b, v_b, g, b)


def kernel(x, w_in, ssm_a_re, ssm_a_im, ssm_log_dt, ssm_b_re, ssm_b_im, ssm_c_re, ssm_c_im, ssm_d,
           ssm_w_glu, w_out, ln1_g, ln1_b, peer_w_q, peer_sub_keys, peer_u, peer_v, ln2_g, ln2_b):
    bsz, seq, d_model = x.shape
    n = bsz * seq
    h = x.reshape(n, d_model)
    for l in range(w_in.shape[0]):
        u2, k2, qt, vt, kmean = _inproj(h, w_in[l])
        ops = _s5_operators(ssm_a_re[l], ssm_a_im[l], ssm_log_dt[l], ssm_b_re[l], ssm_b_im[l],
                            ssm_c_re[l], ssm_c_im[l], ssm_d[l])
        yg = _s5(u2, bsz, seq, ops)
        att = _moba(qt, k2, vt, kmean.reshape(bsz, seq // MOBA_BLOCK, D_ATT), bsz, seq)
        h1 = _outproj(h, yg, att, ssm_w_glu[l].astype(bf16), w_out[l].astype(bf16),
                      ln1_g[l].reshape(1, d_model), ln1_b[l].reshape(1, d_model))
        h = _peer(h1, peer_w_q[l].astype(bf16), peer_sub_keys[l], peer_u[l].astype(bf16),
                  peer_v[l].astype(bf16), ln2_g[l].reshape(1, d_model), ln2_b[l].reshape(1, d_model))
    return h.reshape(bsz, seq, d_model)
```

```python
import functools
import math

import jax
import jax.numpy as jnp
from jax import lax
from jax.experimental import pallas as pl
from jax.experimental.pallas import tpu as pltpu

f32 = jnp.float32
bf16 = jnp.bfloat16
fp8 = jnp.float8_e4m3fn
FP8_TARGET_MAX = 256.0

D_SSM = 512
SSM_GROUP = 16
SSM_STATE = 64
N_ATT_HEADS = 8
ATT_HEAD_DIM = 64
D_ATT = N_ATT_HEADS * ATT_HEAD_DIM
MOBA_BLOCK = 256
MOBA_TOPK = 3
PEER_HEADS = 8
PEER_HALF = 128
PEER_N_KEYS = 128
PEER_TOPK = 16
LN_EPS = 1e-5
DEPTH = 1
DN_ALPHA = (2.0 * DEPTH) ** 0.25
NEG = -1e30

LANES = 128
SUBLANES = 8
BF16_ROWS = 2 * SUBLANES
VMEM_LIMIT = 48 * 1024 * 1024

TM_PROJ = 1024
S5_CHUNK = 8
S5_LANE_GROUPS = LANES // SSM_GROUP
TT_PEER = 512
EB_PEER = 2048


def _dot(a, b):
    return jnp.dot(a, b, preferred_element_type=f32)


def _dot_nt(a, b):
    return lax.dot_general(a, b, (((1,), (1,)), ((), ())), preferred_element_type=f32)


def _split_bf16(x):
    hi = x.astype(bf16)
    lo = (x - hi.astype(f32)).astype(bf16)
    return hi, lo


def _layernorm(r, g, b):
    mu = jnp.mean(r, axis=-1, keepdims=True)
    d = r - mu
    var = jnp.mean(d * d, axis=-1, keepdims=True)
    return d * lax.rsqrt(var + LN_EPS) * g + b


def _inproj_kernel(x_ref, wuk_ref, wqvt_ref, u_ref, k_ref, qt_ref, vt_ref, km_ref):
    xb = x_ref[...].astype(bf16)
    acc = _dot(xb, wuk_ref[...])
    u_ref[...] = acc[:, :D_SSM]
    kk = acc[:, D_SSM:]
    k_ref[...] = kk.astype(bf16)
    qvt = _dot_nt(wqvt_ref[...], xb)
    for r in range(TM_PROJ // MOBA_BLOCK):
        cols = slice(r * MOBA_BLOCK, (r + 1) * MOBA_BLOCK)
        km_ref[r] = jnp.mean(kk[cols], axis=0, keepdims=True)
        qt_ref[r] = qvt[:D_ATT, cols].astype(bf16)
        vt_ref[r] = qvt[D_ATT:, cols].astype(bf16)


def _inproj(x2, w_in):
    n, d_model = x2.shape
    nb = TM_PROJ // MOBA_BLOCK
    w_uk = jnp.concatenate([w_in[:, :D_SSM], w_in[:, D_SSM + D_ATT:D_SSM + 2 * D_ATT]], axis=1).astype(bf16)
    w_qvt = jnp.concatenate([w_in[:, D_SSM:D_SSM + D_ATT], w_in[:, D_SSM + 2 * D_ATT:]], axis=1).T.astype(bf16)
    return pl.pallas_call(
        _inproj_kernel,
        grid=(n // TM_PROJ,),
        in_specs=[pl.BlockSpec((TM_PROJ, d_model), lambda i: (i, 0)),
                  pl.BlockSpec(w_uk.shape, lambda i: (0, 0)),
                  pl.BlockSpec(w_qvt.shape, lambda i: (0, 0))],
        out_specs=[pl.BlockSpec((TM_PROJ, D_SSM), lambda i: (i, 0)),
                   pl.BlockSpec((TM_PROJ, D_ATT), lambda i: (i, 0)),
                   pl.BlockSpec((nb, D_ATT, MOBA_BLOCK), lambda i: (i, 0, 0)),
                   pl.BlockSpec((nb, D_ATT, MOBA_BLOCK), lambda i: (i, 0, 0)),
                   pl.BlockSpec((nb, 1, D_ATT), lambda i: (i, 0, 0))],
        out_shape=[jax.ShapeDtypeStruct((n, D_SSM), f32),
                   jax.ShapeDtypeStruct((n, D_ATT), bf16),
                   jax.ShapeDtypeStruct((n // MOBA_BLOCK, D_ATT, MOBA_BLOCK), bf16),
                   jax.ShapeDtypeStruct((n // MOBA_BLOCK, D_ATT, MOBA_BLOCK), bf16),
                   jax.ShapeDtypeStruct((n // MOBA_BLOCK, 1, D_ATT), f32)],
        compiler_params=pltpu.CompilerParams(dimension_semantics=("parallel",),
                                             vmem_limit_bytes=VMEM_LIMIT),
        name="inproj",
    )(x2, w_uk, w_qvt)


def _s5_operators(a_re, a_im, log_dt, b_re, b_im, c_re, c_im, d_skip):
    T = S5_CHUNK
    G, P = a_re.shape
    H = SSM_GROUP
    nj = G // S5_LANE_GROUPS
    g8 = S5_LANE_GROUPS
    lam = lax.complex(a_re.astype(f32), a_im.astype(f32))
    dt = jnp.exp(log_dt.astype(f32))[:, None]
    lam_bar = jnp.exp(lam * dt)
    b_c = lax.complex(b_re.astype(f32), b_im.astype(f32))
    b_bar = ((lam_bar - 1.0) / lam)[:, :, None] * b_c
    c_c = lax.complex(c_re.astype(f32), c_im.astype(f32))
    pows = jnp.exp((lam * dt)[None] * jnp.arange(T + 1, dtype=f32)[:, None, None])
    def widen(x, reps):
        n = x.shape[-1]
        rep = jnp.tile(jnp.eye(n, dtype=f32), (1, reps))
        return jnp.dot(x.reshape(-1, n), rep).reshape(x.shape[:-1] + (reps * n,))

    same = jnp.arange(g8)[:, None] == (jnp.arange(g8 * H) // H)[None, :]
    same_p = jnp.arange(g8)[:, None] == (jnp.arange(g8 * P) // P)[None, :]
    kern = jnp.real(jnp.einsum('ghp,dgp,gpk->dgkh', c_c, pows[:T], b_bar))
    dd = jnp.arange(T)[None, :] - jnp.arange(T)[:, None]
    kst = jnp.where((dd >= 0)[:, :, None, None, None], kern[jnp.clip(dd, 0, T - 1)], 0.0)
    kst = widen(kst.reshape(T, T, nj, g8, H, H), g8)
    kst = jnp.where(same[None, None, None, :, None, :], kst, 0.0)
    toep = kst.transpose(2, 0, 3, 4, 1, 5).reshape(nj, T * LANES, T * LANES)
    win = pows[T - 1 - jnp.arange(T)][:, :, None, :] * jnp.swapaxes(b_bar, 1, 2)[None]
    win = jnp.stack([jnp.real(win), jnp.imag(win)], 0).reshape(2, T, nj, g8, H, P)
    win = jnp.where(same_p[None, None, None, :, None, :], widen(win, g8), 0.0)
    m_in = win.transpose(2, 1, 3, 4, 0, 5).reshape(nj, T * LANES, 2 * g8 * P)
    wout = jnp.swapaxes(c_c, 1, 2)[None] * pows[1:T + 1][:, :, :, None]
    wout = jnp.stack([jnp.real(wout), -jnp.imag(wout)], 0).reshape(2, T, nj, g8, P, H)
    wout = jnp.where(same[None, None, None, :, None, :], widen(wout, g8), 0.0)
    m_out = wout.transpose(2, 0, 3, 4, 1, 5).reshape(nj, 2 * g8 * P, T * LANES)
    a_chunk = pows[T].reshape(nj, 1, g8 * P)
    a_ri = jnp.concatenate([jnp.real(a_chunk), jnp.imag(a_chunk)], axis=1)
    d_flat = jnp.tile(d_skip.astype(f32).reshape(nj, 1, LANES), (1, 1, T))
    return toep.astype(bf16), m_in.astype(bf16), m_out.astype(bf16), a_ri, d_flat


def _s5_kernel(u_ref, toep_ref, min_ref, mout_ref, a_ref, d_ref, y_ref, in_scr, sp_scr):
    T = S5_CHUNK
    n_c = u_ref.shape[0] // T
    half = in_scr.shape[1] // 2
    uf = jnp.concatenate([u_ref[pl.ds(t, n_c, stride=T), :] for t in range(T)], axis=1)
    ufb = uf.astype(bf16)
    y = _dot(ufb, toep_ref[0]) + uf * d_ref[0]
    in_scr[...] = _dot(ufb, min_ref[0])
    ar = a_ref[0, 0:1, :]
    ai = a_ref[0, 1:2, :]

    def step(c, carry):
        sr, si = carry
        sp_scr[pl.ds(c, 1), :half] = sr
        sp_scr[pl.ds(c, 1), half:] = si
        ir = in_scr[pl.ds(c, 1), :half]
        ii = in_scr[pl.ds(c, 1), half:]
        return ar * sr - ai * si + ir, ar * si + ai * sr + ii

    z = jnp.zeros((1, half), f32)
    lax.fori_loop(0, n_c, step, (z, z), unroll=4)
    y = y + _dot(sp_scr[...].astype(bf16), mout_ref[0])
    yg = jax.nn.gelu(y)
    for t in range(T):
        y_ref[pl.ds(t, n_c, stride=T), :] = yg[:, t * LANES:(t + 1) * LANES]


def _s5(u2, bsz, seq, ops):
    toep, m_in, m_out, a_ri, d_flat = ops
    nj = toep.shape[0]
    n_c = seq // S5_CHUNK
    kdim = S5_CHUNK * LANES
    sdim = m_in.shape[2]
    return pl.pallas_call(
        _s5_kernel,
        grid=(nj, bsz),
        in_specs=[pl.BlockSpec((seq, LANES), lambda j, b: (b, j)),
                  pl.BlockSpec((1, kdim, kdim), lambda j, b: (j, 0, 0)),
                  pl.BlockSpec((1, kdim, sdim), lambda j, b: (j, 0, 0)),
                  pl.BlockSpec((1, sdim, kdim), lambda j, b: (j, 0, 0)),
                  pl.BlockSpec((1, 2, sdim // 2), lambda j, b: (j, 0, 0)),
                  pl.BlockSpec((1, 1, kdim), lambda j, b: (j, 0, 0))],
        out_specs=pl.BlockSpec((seq, LANES), lambda j, b: (b, j)),
        out_shape=jax.ShapeDtypeStruct(u2.shape, f32),
        scratch_shapes=[pltpu.VMEM((n_c, sdim), f32), pltpu.VMEM((n_c, sdim), f32)],
        compiler_params=pltpu.CompilerParams(dimension_semantics=("parallel", "parallel"),
                                             vmem_limit_bytes=VMEM_LIMIT),
        name="s5",
    )(u2, toep, m_in, m_out, a_ri, d_flat)


def _moba_kernel(qt_ref, k_ref, vt_ref, km_ref, o_ref, kaug_ref, s0_scr, s1_scr):
    blk = MOBA_BLOCK
    hd = ATT_HEAD_DIM
    nb = k_ref.shape[0] // blk
    i = pl.program_id(2)

    @pl.when(i == 0)
    def _build():
        lane = lax.broadcasted_iota(jnp.int32, (blk, LANES), 1)

        def build(j, carry):
            rows = pl.ds(pl.multiple_of(j * blk, blk), blk)
            kaug_ref[rows, :LANES] = k_ref[rows, :]
            kaug_ref[rows, LANES:] = jnp.where(lane == j, 1.0, 0.0).astype(bf16)
            return carry
        lax.fori_loop(0, nb, build, 0)

    qt = qt_ref[0]
    qtf = qt.astype(f32) * (hd ** -0.5)
    km = km_ref[0]
    lane_nb = lax.broadcasted_iota(jnp.int32, (nb, LANES), 1)
    feat = lax.broadcasted_iota(jnp.int32, (LANES, blk), 0)
    blk_id = lax.broadcasted_iota(jnp.int32, (nb, blk), 0)
    blk_f = blk_id.astype(f32)
    q_cols, bias_cols = [], []
    for hh in range(2):
        km_hi, km_lo = _split_bf16(jnp.where((lane_nb < hd) if hh == 0 else (lane_nb >= hd), km, 0.0))
        gate = _dot(km_hi, qt) + _dot(km_lo, qt)
        g = jnp.where(blk_id < i, gate, -jnp.inf)
        sel = blk_id == i
        for _ in range(MOBA_TOPK):
            mx = jnp.max(g, axis=0, keepdims=True)
            first = jnp.min(jnp.where(g == mx, blk_f, float(nb)), axis=0, keepdims=True)
            pick = (blk_f == first) & (mx > -jnp.inf)
            sel = sel | pick
            g = jnp.where(pick, -jnp.inf, g)
        bias_cols.append(jnp.where(sel, 0.0, NEG))
        q_cols.append(jnp.where((feat < hd) if hh == 0 else (feat >= hd), qtf, 0.0))
    qa = jnp.concatenate([jnp.concatenate(q_cols, axis=1), jnp.concatenate(bias_cols, axis=1),
                          jnp.zeros((LANES - nb, 2 * blk), f32)], axis=0).astype(bf16)

    key_pos = lax.broadcasted_iota(jnp.int32, (blk, 2 * blk), 0)
    q_pos = lax.broadcasted_iota(jnp.int32, (blk, 2 * blk), 1) % blk

    def scores(j):
        j = jnp.minimum(j, nb - 1)
        return _dot(kaug_ref[pl.ds(pl.multiple_of(j * blk, blk), blk), :], qa)

    def update(state, blocks):
        m, l, acc = state
        m_new = m
        for _, s in blocks:
            m_new = jnp.maximum(m_new, jnp.max(s, axis=0, keepdims=True))
        alpha = jnp.exp(m - m_new)
        l = alpha * l
        acc = alpha * acc
        for j, s in blocks:
            p = jnp.exp(s - m_new)
            l = l + jnp.sum(p, axis=0, keepdims=True)
            pb = p.astype(bf16)
            acc = acc + jnp.concatenate([_dot(vt_ref[j, :hd, :], pb[:, :blk]),
                                         _dot(vt_ref[j, hd:, :], pb[:, blk:])], axis=1)
        return m_new, l, acc

    def pair_scores(p, dst):
        dst[0] = scores(2 * p)
        dst[1] = scores(2 * p + 1)

    def pair_update(state, p, src, masked):
        blocks = []
        for r in range(2):
            j = 2 * p + r
            s = src[r]
            if masked:
                off = jnp.where(j < i, blk, jnp.where(j == i, 0, -2 * blk))
                s = jnp.where(key_pos <= q_pos + off, s, NEG)
            blocks.append((jnp.minimum(j, nb - 1), s))
        return update(state, blocks)

    def quad_body(q, state):
        pair_scores(2 * q + 1, s1_scr)
        state = pair_update(state, 2 * q, s0_scr, False)
        pair_scores(2 * q + 2, s0_scr)
        return pair_update(state, 2 * q + 1, s1_scr, False)

    state = (jnp.full((1, 2 * blk), -jnp.inf, f32), jnp.zeros((1, 2 * blk), f32), jnp.zeros((hd, 2 * blk), f32))
    n_quad = i // 4
    pair_scores(0, s0_scr)
    state = lax.fori_loop(0, n_quad, quad_body, state)
    pair_scores(2 * n_quad + 1, s1_scr)
    state = pair_update(state, 2 * n_quad, s0_scr, True)
    m, l, acc = pair_update(state, 2 * n_quad + 1, s1_scr, True)
    out = acc / l
    o_ref[...] = jnp.concatenate([out[:, :blk], out[:, blk:]], axis=0).T.astype(o_ref.dtype)


def _moba(qt, k2, vt, kmean, bsz, seq):
    nb = seq // MOBA_BLOCK
    assert seq % MOBA_BLOCK == 0 and nb % SUBLANES == 0 and nb <= LANES
    npair = D_ATT // LANES
    return pl.pallas_call(
        _moba_kernel,
        grid=(bsz, npair, nb),
        in_specs=[pl.BlockSpec((1, LANES, MOBA_BLOCK), lambda b, h, i: (b * nb + i, h, 0)),
                  pl.BlockSpec((seq, LANES), lambda b, h, i: (b, h)),
                  pl.BlockSpec((nb, LANES, MOBA_BLOCK), lambda b, h, i: (b, h, 0)),
                  pl.BlockSpec((1, nb, LANES), lambda b, h, i: (b, 0, h))],
        out_specs=pl.BlockSpec((MOBA_BLOCK, LANES), lambda b, h, i: (b * nb + i, h)),
        out_shape=jax.ShapeDtypeStruct(k2.shape, bf16),
        scratch_shapes=[pltpu.VMEM((seq, 2 * LANES), bf16),
                        pltpu.VMEM((2, MOBA_BLOCK, 2 * MOBA_BLOCK), f32),
                        pltpu.VMEM((2, MOBA_BLOCK, 2 * MOBA_BLOCK), f32)],
        compiler_params=pltpu.CompilerParams(dimension_semantics=("parallel", "parallel", "arbitrary"),
                                             vmem_limit_bytes=VMEM_LIMIT),
        name="moba",
    )(qt, k2, vt, kmean)


def _outproj_kernel(x_ref, yg_ref, att_ref, wglu_ref, wout_ref, g_ref, b_ref, h_ref):
    yg = yg_ref[...]
    z = yg * jax.nn.sigmoid(_dot(yg.astype(bf16), wglu_ref[...]))
    cat = jnp.concatenate([z.astype(bf16), att_ref[...]], axis=1)
    r = DN_ALPHA * x_ref[...] + _dot(cat, wout_ref[...])
    h_ref[...] = _layernorm(r, g_ref[...], b_ref[...])


def _outproj(x2, yg, att, wglu_b, wout_b, g, b):
    n, d_model = x2.shape
    return pl.pallas_call(
        _outproj_kernel,
        grid=(n // TM_PROJ,),
        in_specs=[pl.BlockSpec((TM_PROJ, d_model), lambda i: (i, 0)),
                  pl.BlockSpec((TM_PROJ, D_SSM), lambda i: (i, 0)),
                  pl.BlockSpec((TM_PROJ, D_ATT), lambda i: (i, 0)),
                  pl.BlockSpec(wglu_b.shape, lambda i: (0, 0)),
                  pl.BlockSpec(wout_b.shape, lambda i: (0, 0)),
                  pl.BlockSpec((1, d_model), lambda i: (0, 0)),
                  pl.BlockSpec((1, d_model), lambda i: (0, 0))],
        out_specs=pl.BlockSpec((TM_PROJ, d_model), lambda i: (i, 0)),
        out_shape=jax.ShapeDtypeStruct((n, d_model), f32),
        compiler_params=pltpu.CompilerParams(dimension_semantics=("parallel",),
                                             vmem_limit_bytes=VMEM_LIMIT),
        name="outproj",
    )(x2, yg, att, wglu_b, wout_b, g, b)


def _bitonic_sort_desc(v):
    n = len(v)
    k = 2
    while k <= n:
        j = k // 2
        while j >= 1:
            for a in range(n):
                b = a ^ j
                if b > a:
                    hi = jnp.maximum(v[a], v[b])
                    lo = jnp.minimum(v[a], v[b])
                    v[a], v[b] = (hi, lo) if (a & k) == 0 else (lo, hi)
            j //= 2
        k *= 2
    return v


def _bitonic_merge_desc(v):
    j = len(v) // 2
    while j >= 1:
        for a in range(len(v)):
            b = a ^ j
            if b > a:
                v[a], v[b] = jnp.maximum(v[a], v[b]), jnp.minimum(v[a], v[b])
        j //= 2
    return v


def _top16_sorted(st):
    k = PEER_TOPK
    x = _bitonic_sort_desc([st[SUBLANES * a:SUBLANES * (a + 1), :] for a in range(PEER_N_KEYS // SUBLANES)])
    for shift in (4, 2, 1):
        y = [pltpu.roll(x[k - 1 - a], shift, 0) for a in range(k)]
        x = _bitonic_merge_desc([jnp.maximum(x[a], y[a]) for a in range(k)])
    return x


def _dup_bf16(x):
    u = pltpu.bitcast(x.astype(bf16).astype(f32), jnp.uint32)
    return pltpu.bitcast(u | (u >> 16), f32)


def _bf16_rows(row):
    return pltpu.bitcast(jnp.broadcast_to(row, (SUBLANES, row.shape[1])), bf16)[None]


def _peer_kernel(n_exp, us_ref, h_ref, wq_ref, sk_ref, u_ref, vt_ref, g_ref, b_ref, o_ref,
                 q_scr, pr_scr, rq_scr, srt_scr, row_scr, acc_scr, hbt_scr, a_scr):
    eb = pl.program_id(1)
    n_blk = n_exp // EB_PEER
    u_unscale = us_ref[0, 0].astype(bf16)
    tt = h_ref.shape[0]
    nk = PEER_N_KEYS
    k = PEER_TOPK

    chunk = 2 * nk
    n_chunk = EB_PEER // chunk

    def activations():
        for s in range(n_chunk):
            a_scr[s] = _dot(u_ref[s * chunk:(s + 1) * chunk, :], hbt_scr[...])

    @pl.when(eb == 0)
    def _prologue():
        hbt_scr[...] = h_ref[...].T.astype(fp8)
        q = _dot(h_ref[...].astype(bf16), wq_ref[...])
        for hc in range(2 * PEER_HEADS):
            q_scr[hc] = q[:, hc * PEER_HALF:(hc + 1) * PEER_HALF]

        def head_sort(h, carry):
            for c in range(2):
                q_hi, q_lo = _split_bf16(q_scr[2 * h + c])
                k_hi, k_lo = _split_bf16(sk_ref[2 * h + c])
                st = _dot_nt(k_hi, q_hi) + _dot_nt(k_hi, q_lo) + _dot_nt(k_lo, q_hi)
                pr_scr[c, h] = st
                srt = _top16_sorted(st)
                for a in range(k):
                    srt_scr[c, a, pl.ds(h, 1), :] = srt[a][0:1, :]
            return carry
        lax.fori_loop(0, PEER_HEADS, head_sort, 0)

        a_s = [srt_scr[0, a] for a in range(k)]
        b_s = [srt_scr[1, a] for a in range(k)]
        cands = [a_s[a] + b_s[b] for a in range(k) for b in range(k) if (a + 1) * (b + 1) <= k]
        cands += [jnp.full_like(cands[0], -jnp.inf)] * (64 - len(cands))
        cs = _bitonic_sort_desc(cands)
        zsum = jnp.zeros_like(cs[0])
        for a in range(k):
            zsum = zsum + jnp.exp(cs[a] - cs[0])
        row_scr[0] = 0.5 * (cs[k - 1] + cs[k])
        row_scr[1] = 1.0 / zsum
        row_scr[2] = a_s[0]
        row_scr[3] = b_s[0]
        row_scr[4] = a_s[k - 1]
        row_scr[5] = b_s[k - 1]

        def head_prep(h, carry):
            hrow = pl.ds(h, 1)
            at = pr_scr[0, h]
            bt = pr_scr[1, h]
            thr = row_scr[0, hrow, :] - at
            r_cnt = jnp.zeros_like(at)
            b_rank = jnp.zeros_like(bt)
            for a in range(k):
                b_a = srt_scr[1, a, hrow, :]
                r_cnt = jnp.where(b_a > thr, float(a + 1), r_cnt)
                b_rank = jnp.where(b_a > bt, float(a + 1), b_rank)
            in_a = at >= row_scr[4, hrow, :]
            p_w = jnp.where(in_a, jnp.exp(at - row_scr[2, hrow, :]), 0.0) * row_scr[1, hrow, :]
            pr_scr[0, h] = _dup_bf16(p_w)
            pr_scr[1, h] = _dup_bf16(jnp.where(in_a, r_cnt, 0.0))
            q_w = jnp.where(bt >= row_scr[5, hrow, :], jnp.exp(bt - row_scr[3, hrow, :]), 0.0)
            rq_scr[0, h] = b_rank.astype(bf16).reshape(nk // BF16_ROWS, BF16_ROWS, tt)
            rq_scr[1, h] = q_w.astype(bf16).reshape(nk // BF16_ROWS, BF16_ROWS, tt)
            return carry
        lax.fori_loop(0, PEER_HEADS, head_prep, 0)
        acc_scr[...] = jnp.zeros_like(acc_scr)

    activations()
    for s in range(n_chunk):
        halves = []
        for half in range(chunk // nk):
            n1 = pl.ds((EB_PEER // nk) * eb + s * (chunk // nk) + half, 1)
            g = jnp.zeros((nk // BF16_ROWS, BF16_ROWS, tt), bf16)
            for h in range(PEER_HEADS):
                p_row = _bf16_rows(pr_scr[0, h, n1, :])
                r_row = _bf16_rows(pr_scr[1, h, n1, :])
                g = g + p_row * jnp.where(rq_scr[0, h] < r_row, rq_scr[1, h], 0)
            halves.append(g.reshape(nk, tt))
        w = jnp.concatenate(halves, axis=0) * jax.nn.gelu(a_scr[s].astype(bf16) * u_unscale)
        acc_scr[...] += _dot(vt_ref[:, s * chunk:(s + 1) * chunk], w)

    @pl.when(eb == n_blk - 1)
    def _last():
        r = DN_ALPHA * h_ref[...] + acc_scr[...].T
        o_ref[...] = _layernorm(r, g_ref[...], b_ref[...])


def _peer(h1, wq_b, sub_keys, u_q, u_unscale, vt_b, g, b):
    u_b = u_q
    n, d_model = h1.shape
    n_exp = u_b.shape[0]
    tt = TT_PEER
    sk = sub_keys.reshape(2 * PEER_HEADS, PEER_N_KEYS, PEER_HALF).astype(f32)
    n_blk = n_exp // EB_PEER
    const = dict(pipeline_mode=pl.Buffered(1))
    return pl.pallas_call(
        functools.partial(_peer_kernel, n_exp),
        grid=(n // tt, n_blk),
        in_specs=[pl.BlockSpec(memory_space=pltpu.SMEM),
                  pl.BlockSpec((tt, d_model), lambda t, e: (t, 0)),
                  pl.BlockSpec(wq_b.shape, lambda t, e: (0, 0), **const),
                  pl.BlockSpec(sk.shape, lambda t, e: (0, 0, 0), **const),
                  pl.BlockSpec((EB_PEER, d_model), lambda t, e: (e, 0)),
                  pl.BlockSpec((d_model, EB_PEER), lambda t, e: (0, e)),
                  pl.BlockSpec((1, d_model), lambda t, e: (0, 0), **const),
                  pl.BlockSpec((1, d_model), lambda t, e: (0, 0), **const)],
        out_specs=pl.BlockSpec((tt, d_model), lambda t, e: (t, 0)),
        out_shape=jax.ShapeDtypeStruct((n, d_model), f32),
        scratch_shapes=[pltpu.VMEM((2 * PEER_HEADS, tt, PEER_HALF), f32),
                        pltpu.VMEM((2, PEER_HEADS, PEER_N_KEYS, tt), f32),
                        pltpu.VMEM((2, PEER_HEADS, PEER_N_KEYS // BF16_ROWS, BF16_ROWS, tt), bf16),
                        pltpu.VMEM((2, PEER_TOPK, PEER_HEADS, tt), f32),
                        pltpu.VMEM((6, PEER_HEADS, tt), f32),
                        pltpu.VMEM((d_model, tt), f32),
                        pltpu.VMEM((d_model, tt), fp8),
                        pltpu.VMEM((EB_PEER // (2 * PEER_N_KEYS), 2 * PEER_N_KEYS, tt), f32)],
        compiler_params=pltpu.CompilerParams(dimension_semantics=("parallel", "arbitrary"),
                                             vmem_limit_bytes=VMEM_LIMIT),
        name="peer",
    )(u_unscale.reshape(1, 1), h1, wq_b, sk, u_b, vt_b, g, b)


def _quantize_table(t):
    amax = jnp.maximum(jnp.max(jnp.abs(t)), jnp.finfo(f32).tiny)
    e = jnp.floor(jnp.log2(FP8_TARGET_MAX / amax))
    return (t * jnp.exp2(e)).astype(fp8), jnp.exp2(-e).astype(f32)


def kernel(x, w_in, ssm_a_re, ssm_a_im, ssm_log_dt, ssm_b_re, ssm_b_im, ssm_c_re, ssm_c_im, ssm_d,
           ssm_w_glu, w_out, ln1_g, ln1_b, peer_w_q, peer_sub_keys, peer_u, peer_v, ln2_g, ln2_b):
    bsz, seq, d_model = x.shape
    n = bsz * seq
    h = x.reshape(n, d_model)
    for l in range(w_in.shape[0]):
        u2, k2, qt, vt, kmean = _inproj(h, w_in[l])
        ops = _s5_operators(ssm_a_re[l], ssm_a_im[l], ssm_log_dt[l], ssm_b_re[l], ssm_b_im[l],
                            ssm_c_re[l], ssm_c_im[l], ssm_d[l])
        yg = _s5(u2, bsz, seq, ops)
        att = _moba(qt, k2, vt, kmean.reshape(bsz, seq // MOBA_BLOCK, D_ATT), bsz, seq)
        h1 = _outproj(h, yg, att, ssm_w_glu[l].astype(bf16), w_out[l].astype(bf16),
                      ln1_g[l].reshape(1, d_model), ln1_b[l].reshape(1, d_model))
        u_q, u_unscale = _quantize_table(peer_u[l])
        h = _peer(h1, peer_w_q[l].astype(bf16), peer_sub_keys[l], u_q, u_unscale,
                  peer_v[l].T.astype(bf16), ln2_g[l].reshape(1, d_model), ln2_b[l].reshape(1, d_model))
    return h.reshape(bsz, seq, d_model)
```

```python
import functools
import math

import jax
import jax.numpy as jnp
from jax import lax
from jax.experimental import pallas as pl
from jax.experimental.pallas import tpu as pltpu

f32 = jnp.float32
bf16 = jnp.bfloat16
fp8 = jnp.float8_e4m3fn
FP8_TARGET_MAX = 256.0
FP8_MAX_SHIFT = 48.0

D_SSM = 512
SSM_GROUP = 16
SSM_STATE = 64
N_ATT_HEADS = 8
ATT_HEAD_DIM = 64
D_ATT = N_ATT_HEADS * ATT_HEAD_DIM
MOBA_BLOCK = 256
MOBA_TOPK = 3
PEER_HEADS = 8
PEER_HALF = 128
PEER_N_KEYS = 128
PEER_TOPK = 16
LN_EPS = 1e-5
DEPTH = 1
DN_ALPHA = (2.0 * DEPTH) ** 0.25
NEG = -1e30

LANES = 128
SUBLANES = 8
BF16_ROWS = 2 * SUBLANES
VMEM_LIMIT = 48 * 1024 * 1024

TM_PROJ = 1024
S5_CHUNK = 8
S5_LANE_GROUPS = LANES // SSM_GROUP
TT_PEER = 512
EB_PEER = 2048


def _dot(a, b):
    return jnp.dot(a, b, preferred_element_type=f32)


def _dot_nt(a, b):
    return lax.dot_general(a, b, (((1,), (1,)), ((), ())), preferred_element_type=f32)


def _split_bf16(x):
    hi = x.astype(bf16)
    lo = (x - hi.astype(f32)).astype(bf16)
    return hi, lo


def _layernorm(r, g, b):
    mu = jnp.mean(r, axis=-1, keepdims=True)
    d = r - mu
    var = jnp.mean(d * d, axis=-1, keepdims=True)
    return d * lax.rsqrt(var + LN_EPS) * g + b


def _inproj_kernel(x_ref, wuk_ref, wqvt_ref, u_ref, k_ref, qt_ref, vt_ref, km_ref):
    xb = x_ref[...].astype(bf16)
    acc = _dot(xb, wuk_ref[...])
    u_ref[...] = acc[:, :D_SSM]
    kk = acc[:, D_SSM:]
    k_ref[...] = kk.astype(bf16)
    qvt = _dot_nt(wqvt_ref[...], xb)
    for r in range(TM_PROJ // MOBA_BLOCK):
        cols = slice(r * MOBA_BLOCK, (r + 1) * MOBA_BLOCK)
        km_ref[r] = jnp.mean(kk[cols], axis=0, keepdims=True)
        qt_ref[r] = qvt[:D_ATT, cols].astype(bf16)
        vt_ref[r] = qvt[D_ATT:, cols].astype(bf16)


def _inproj(x2, w_in):
    n, d_model = x2.shape
    nb = TM_PROJ // MOBA_BLOCK
    w_uk = jnp.concatenate([w_in[:, :D_SSM], w_in[:, D_SSM + D_ATT:D_SSM + 2 * D_ATT]], axis=1).astype(bf16)
    w_qvt = jnp.concatenate([w_in[:, D_SSM:D_SSM + D_ATT], w_in[:, D_SSM + 2 * D_ATT:]], axis=1).T.astype(bf16)
    return pl.pallas_call(
        _inproj_kernel,
        grid=(n // TM_PROJ,),
        in_specs=[pl.BlockSpec((TM_PROJ, d_model), lambda i: (i, 0)),
                  pl.BlockSpec(w_uk.shape, lambda i: (0, 0)),
                  pl.BlockSpec(w_qvt.shape, lambda i: (0, 0))],
        out_specs=[pl.BlockSpec((TM_PROJ, D_SSM), lambda i: (i, 0)),
                   pl.BlockSpec((TM_PROJ, D_ATT), lambda i: (i, 0)),
                   pl.BlockSpec((nb, D_ATT, MOBA_BLOCK), lambda i: (i, 0, 0)),
                   pl.BlockSpec((nb, D_ATT, MOBA_BLOCK), lambda i: (i, 0, 0)),
                   pl.BlockSpec((nb, 1, D_ATT), lambda i: (i, 0, 0))],
        out_shape=[jax.ShapeDtypeStruct((n, D_SSM), f32),
                   jax.ShapeDtypeStruct((n, D_ATT), bf16),
                   jax.ShapeDtypeStruct((n // MOBA_BLOCK, D_ATT, MOBA_BLOCK), bf16),
                   jax.ShapeDtypeStruct((n // MOBA_BLOCK, D_ATT, MOBA_BLOCK), bf16),
                   jax.ShapeDtypeStruct((n // MOBA_BLOCK, 1, D_ATT), f32)],
        compiler_params=pltpu.CompilerParams(dimension_semantics=("parallel",),
                                             vmem_limit_bytes=VMEM_LIMIT),
        name="inproj",
    )(x2, w_uk, w_qvt)


def _s5_operators(a_re, a_im, log_dt, b_re, b_im, c_re, c_im, d_skip):
    T = S5_CHUNK
    G, P = a_re.shape
    H = SSM_GROUP
    nj = G // S5_LANE_GROUPS
    g8 = S5_LANE_GROUPS
    lam = lax.complex(a_re.astype(f32), a_im.astype(f32))
    dt = jnp.exp(log_dt.astype(f32))[:, None]
    lam_bar = jnp.exp(lam * dt)
    b_c = lax.complex(b_re.astype(f32), b_im.astype(f32))
    b_bar = ((lam_bar - 1.0) / lam)[:, :, None] * b_c
    c_c = lax.complex(c_re.astype(f32), c_im.astype(f32))
    pows = jnp.exp((lam * dt)[None] * jnp.arange(T + 1, dtype=f32)[:, None, None])
    def widen(x, reps):
        n = x.shape[-1]
        rep = jnp.tile(jnp.eye(n, dtype=f32), (1, reps))
        return jnp.dot(x.reshape(-1, n), rep).reshape(x.shape[:-1] + (reps * n,))

    same = jnp.arange(g8)[:, None] == (jnp.arange(g8 * H) // H)[None, :]
    same_p = jnp.arange(g8)[:, None] == (jnp.arange(g8 * P) // P)[None, :]
    kern = jnp.real(jnp.einsum('ghp,dgp,gpk->dgkh', c_c, pows[:T], b_bar))
    dd = jnp.arange(T)[None, :] - jnp.arange(T)[:, None]
    kst = jnp.where((dd >= 0)[:, :, None, None, None], kern[jnp.clip(dd, 0, T - 1)], 0.0)
    kst = widen(kst.reshape(T, T, nj, g8, H, H), g8)
    kst = jnp.where(same[None, None, None, :, None, :], kst, 0.0)
    toep = kst.transpose(2, 0, 3, 4, 1, 5).reshape(nj, T * LANES, T * LANES)
    win = pows[T - 1 - jnp.arange(T)][:, :, None, :] * jnp.swapaxes(b_bar, 1, 2)[None]
    win = jnp.stack([jnp.real(win), jnp.imag(win)], 0).reshape(2, T, nj, g8, H, P)
    win = jnp.where(same_p[None, None, None, :, None, :], widen(win, g8), 0.0)
    m_in = win.transpose(2, 1, 3, 4, 0, 5).reshape(nj, T * LANES, 2 * g8 * P)
    wout = jnp.swapaxes(c_c, 1, 2)[None] * pows[1:T + 1][:, :, :, None]
    wout = jnp.stack([jnp.real(wout), -jnp.imag(wout)], 0).reshape(2, T, nj, g8, P, H)
    wout = jnp.where(same[None, None, None, :, None, :], widen(wout, g8), 0.0)
    m_out = wout.transpose(2, 0, 3, 4, 1, 5).reshape(nj, 2 * g8 * P, T * LANES)
    a_chunk = pows[T].reshape(nj, 1, g8 * P)
    a_ri = jnp.concatenate([jnp.real(a_chunk), jnp.imag(a_chunk)], axis=1)
    d_flat = jnp.tile(d_skip.astype(f32).reshape(nj, 1, LANES), (1, 1, T))
    return toep.astype(bf16), m_in.astype(bf16), m_out.astype(bf16), a_ri, d_flat


def _s5_kernel(u_ref, toep_ref, min_ref, mout_ref, a_ref, d_ref, y_ref, in_scr, sp_scr):
    T = S5_CHUNK
    n_c = u_ref.shape[0] // T
    half = in_scr.shape[1] // 2
    uf = jnp.concatenate([u_ref[pl.ds(t, n_c, stride=T), :] for t in range(T)], axis=1)
    ufb = uf.astype(bf16)
    y = _dot(ufb, toep_ref[0]) + uf * d_ref[0]
    in_scr[...] = _dot(ufb, min_ref[0])
    ar = a_ref[0, 0:1, :]
    ai = a_ref[0, 1:2, :]

    def step(c, carry):
        sr, si = carry
        sp_scr[pl.ds(c, 1), :half] = sr
        sp_scr[pl.ds(c, 1), half:] = si
        ir = in_scr[pl.ds(c, 1), :half]
        ii = in_scr[pl.ds(c, 1), half:]
        return ar * sr - ai * si + ir, ar * si + ai * sr + ii

    z = jnp.zeros((1, half), f32)
    lax.fori_loop(0, n_c, step, (z, z), unroll=4)
    y = y + _dot(sp_scr[...].astype(bf16), mout_ref[0])
    yg = jax.nn.gelu(y)
    for t in range(T):
        y_ref[pl.ds(t, n_c, stride=T), :] = yg[:, t * LANES:(t + 1) * LANES]


def _s5(u2, bsz, seq, ops):
    toep, m_in, m_out, a_ri, d_flat = ops
    nj = toep.shape[0]
    n_c = seq // S5_CHUNK
    kdim = S5_CHUNK * LANES
    sdim = m_in.shape[2]
    return pl.pallas_call(
        _s5_kernel,
        grid=(nj, bsz),
        in_specs=[pl.BlockSpec((seq, LANES), lambda j, b: (b, j)),
                  pl.BlockSpec((1, kdim, kdim), lambda j, b: (j, 0, 0)),
                  pl.BlockSpec((1, kdim, sdim), lambda j, b: (j, 0, 0)),
                  pl.BlockSpec((1, sdim, kdim), lambda j, b: (j, 0, 0)),
                  pl.BlockSpec((1, 2, sdim // 2), lambda j, b: (j, 0, 0)),
                  pl.BlockSpec((1, 1, kdim), lambda j, b: (j, 0, 0))],
        out_specs=pl.BlockSpec((seq, LANES), lambda j, b: (b, j)),
        out_shape=jax.ShapeDtypeStruct(u2.shape, f32),
        scratch_shapes=[pltpu.VMEM((n_c, sdim), f32), pltpu.VMEM((n_c, sdim), f32)],
        compiler_params=pltpu.CompilerParams(dimension_semantics=("parallel", "parallel"),
                                             vmem_limit_bytes=VMEM_LIMIT),
        name="s5",
    )(u2, toep, m_in, m_out, a_ri, d_flat)


def _moba_kernel(qt_ref, k_ref, vt_ref, km_ref, o_ref, kaug_ref, s0_scr, s1_scr):
    blk = MOBA_BLOCK
    hd = ATT_HEAD_DIM
    nb = k_ref.shape[0] // blk
    i = pl.program_id(2)

    @pl.when(i == 0)
    def _build():
        lane = lax.broadcasted_iota(jnp.int32, (blk, LANES), 1)

        def build(j, carry):
            rows = pl.ds(pl.multiple_of(j * blk, blk), blk)
            kaug_ref[rows, :LANES] = k_ref[rows, :]
            kaug_ref[rows, LANES:] = jnp.where(lane == j, 1.0, 0.0).astype(bf16)
            return carry
        lax.fori_loop(0, nb, build, 0)

    qt = qt_ref[0]
    qtf = qt.astype(f32) * (hd ** -0.5)
    km = km_ref[0]
    lane_nb = lax.broadcasted_iota(jnp.int32, (nb, LANES), 1)
    feat = lax.broadcasted_iota(jnp.int32, (LANES, blk), 0)
    blk_id = lax.broadcasted_iota(jnp.int32, (nb, blk), 0)
    blk_f = blk_id.astype(f32)
    q_cols, bias_cols = [], []
    for hh in range(2):
        km_hi, km_lo = _split_bf16(jnp.where((lane_nb < hd) if hh == 0 else (lane_nb >= hd), km, 0.0))
        gate = _dot(km_hi, qt) + _dot(km_lo, qt)
        g = jnp.where(blk_id < i, gate, -jnp.inf)
        sel = blk_id == i
        for _ in range(MOBA_TOPK):
            mx = jnp.max(g, axis=0, keepdims=True)
            first = jnp.min(jnp.where(g == mx, blk_f, float(nb)), axis=0, keepdims=True)
            pick = (blk_f == first) & (mx > -jnp.inf)
            sel = sel | pick
            g = jnp.where(pick, -jnp.inf, g)
        bias_cols.append(jnp.where(sel, 0.0, NEG))
        q_cols.append(jnp.where((feat < hd) if hh == 0 else (feat >= hd), qtf, 0.0))
    qa = jnp.concatenate([jnp.concatenate(q_cols, axis=1), jnp.concatenate(bias_cols, axis=1),
                          jnp.zeros((LANES - nb, 2 * blk), f32)], axis=0).astype(bf16)

    key_pos = lax.broadcasted_iota(jnp.int32, (blk, 2 * blk), 0)
    q_pos = lax.broadcasted_iota(jnp.int32, (blk, 2 * blk), 1) % blk

    def scores(j):
        j = jnp.minimum(j, nb - 1)
        return _dot(kaug_ref[pl.ds(pl.multiple_of(j * blk, blk), blk), :], qa)

    def update(state, blocks):
        m, l, acc = state
        m_new = m
        for _, s in blocks:
            m_new = jnp.maximum(m_new, jnp.max(s, axis=0, keepdims=True))
        alpha = jnp.exp(m - m_new)
        l = alpha * l
        acc = alpha * acc
        for j, s in blocks:
            p = jnp.exp(s - m_new)
            l = l + jnp.sum(p, axis=0, keepdims=True)
            pb = p.astype(bf16)
            acc = acc + jnp.concatenate([_dot(vt_ref[j, :hd, :], pb[:, :blk]),
                                         _dot(vt_ref[j, hd:, :], pb[:, blk:])], axis=1)
        return m_new, l, acc

    def pair_scores(p, dst):
        dst[0] = scores(2 * p)
        dst[1] = scores(2 * p + 1)

    def pair_update(state, p, src, masked):
        blocks = []
        for r in range(2):
            j = 2 * p + r
            s = src[r]
            if masked:
                off = jnp.where(j < i, blk, jnp.where(j == i, 0, -2 * blk))
                s = jnp.where(key_pos <= q_pos + off, s, NEG)
            blocks.append((jnp.minimum(j, nb - 1), s))
        return update(state, blocks)

    def quad_body(q, state):
        pair_scores(2 * q + 1, s1_scr)
        state = pair_update(state, 2 * q, s0_scr, False)
        pair_scores(2 * q + 2, s0_scr)
        return pair_update(state, 2 * q + 1, s1_scr, False)

    state = (jnp.full((1, 2 * blk), -jnp.inf, f32), jnp.zeros((1, 2 * blk), f32), jnp.zeros((hd, 2 * blk), f32))
    n_quad = i // 4
    pair_scores(0, s0_scr)
    state = lax.fori_loop(0, n_quad, quad_body, state)
    pair_scores(2 * n_quad + 1, s1_scr)
    state = pair_update(state, 2 * n_quad, s0_scr, True)
    m, l, acc = pair_update(state, 2 * n_quad + 1, s1_scr, True)
    out = acc / l
    o_ref[...] = jnp.concatenate([out[:, :blk], out[:, blk:]], axis=0).T.astype(o_ref.dtype)


def _moba(qt, k2, vt, kmean, bsz, seq):
    nb = seq // MOBA_BLOCK
    assert seq % MOBA_BLOCK == 0 and nb % SUBLANES == 0 and nb <= LANES
    npair = D_ATT // LANES
    return pl.pallas_call(
        _moba_kernel,
        grid=(bsz, npair, nb),
        in_specs=[pl.BlockSpec((1, LANES, MOBA_BLOCK), lambda b, h, i: (b * nb + i, h, 0)),
                  pl.BlockSpec((seq, LANES), lambda b, h, i: (b, h)),
                  pl.BlockSpec((nb, LANES, MOBA_BLOCK), lambda b, h, i: (b, h, 0)),
                  pl.BlockSpec((1, nb, LANES), lambda b, h, i: (b, 0, h))],
        out_specs=pl.BlockSpec((MOBA_BLOCK, LANES), lambda b, h, i: (b * nb + i, h)),
        out_shape=jax.ShapeDtypeStruct(k2.shape, bf16),
        scratch_shapes=[pltpu.VMEM((seq, 2 * LANES), bf16),
                        pltpu.VMEM((2, MOBA_BLOCK, 2 * MOBA_BLOCK), f32),
                        pltpu.VMEM((2, MOBA_BLOCK, 2 * MOBA_BLOCK), f32)],
        compiler_params=pltpu.CompilerParams(dimension_semantics=("parallel", "parallel", "arbitrary"),
                                             vmem_limit_bytes=VMEM_LIMIT),
        name="moba",
    )(qt, k2, vt, kmean)


def _outproj_kernel(x_ref, yg_ref, att_ref, wglu_ref, wout_ref, g_ref, b_ref, h_ref):
    yg = yg_ref[...]
    z = yg * jax.nn.sigmoid(_dot(yg.astype(bf16), wglu_ref[...]))
    cat = jnp.concatenate([z.astype(bf16), att_ref[...]], axis=1)
    r = DN_ALPHA * x_ref[...] + _dot(cat, wout_ref[...])
    h_ref[...] = _layernorm(r, g_ref[...], b_ref[...])


def _outproj(x2, yg, att, wglu_b, wout_b, g, b):
    n, d_model = x2.shape
    return pl.pallas_call(
        _outproj_kernel,
        grid=(n // TM_PROJ,),
        in_specs=[pl.BlockSpec((TM_PROJ, d_model), lambda i: (i, 0)),
                  pl.BlockSpec((TM_PROJ, D_SSM), lambda i: (i, 0)),
                  pl.BlockSpec((TM_PROJ, D_ATT), lambda i: (i, 0)),
                  pl.BlockSpec(wglu_b.shape, lambda i: (0, 0)),
                  pl.BlockSpec(wout_b.shape, lambda i: (0, 0)),
                  pl.BlockSpec((1, d_model), lambda i: (0, 0)),
                  pl.BlockSpec((1, d_model), lambda i: (0, 0))],
        out_specs=pl.BlockSpec((TM_PROJ, d_model), lambda i: (i, 0)),
        out_shape=jax.ShapeDtypeStruct((n, d_model), f32),
        compiler_params=pltpu.CompilerParams(dimension_semantics=("parallel",),
                                             vmem_limit_bytes=VMEM_LIMIT),
        name="outproj",
    )(x2, yg, att, wglu_b, wout_b, g, b)


def _bitonic_sort_desc(v):
    n = len(v)
    k = 2
    while k <= n:
        j = k // 2
        while j >= 1:
            for a in range(n):
                b = a ^ j
                if b > a:
                    hi = jnp.maximum(v[a], v[b])
                    lo = jnp.minimum(v[a], v[b])
                    v[a], v[b] = (hi, lo) if (a & k) == 0 else (lo, hi)
            j //= 2
        k *= 2
    return v


def _bitonic_merge_desc(v):
    j = len(v) // 2
    while j >= 1:
        for a in range(len(v)):
            b = a ^ j
            if b > a:
                v[a], v[b] = jnp.maximum(v[a], v[b]), jnp.minimum(v[a], v[b])
        j //= 2
    return v


def _top16_sorted(st):
    k = PEER_TOPK
    x = _bitonic_sort_desc([st[SUBLANES * a:SUBLANES * (a + 1), :] for a in range(PEER_N_KEYS // SUBLANES)])
    for shift in (4, 2, 1):
        y = [pltpu.roll(x[k - 1 - a], shift, 0) for a in range(k)]
        x = _bitonic_merge_desc([jnp.maximum(x[a], y[a]) for a in range(k)])
    return x


def _dup_bf16(x):
    u = pltpu.bitcast(x.astype(bf16).astype(f32), jnp.uint32)
    return pltpu.bitcast(u | (u >> 16), f32)


def _bf16_rows(row):
    return pltpu.bitcast(jnp.broadcast_to(row, (SUBLANES, row.shape[1])), bf16)[None]


def _peer_kernel(n_exp, us_ref, h_ref, wq_ref, sk_ref, u_ref, vt_ref, g_ref, b_ref, o_ref,
                 q_scr, pr_scr, rq_scr, srt_scr, row_scr, acc_scr, hbt_scr, a_scr, unscale_scr):
    eb = pl.program_id(1)
    n_blk = n_exp // EB_PEER
    tt = h_ref.shape[0]
    nk = PEER_N_KEYS
    k = PEER_TOPK

    chunk = 2 * nk
    n_chunk = EB_PEER // chunk

    def activations():
        for s in range(n_chunk):
            a_scr[s] = _dot(u_ref[s * chunk:(s + 1) * chunk, :], hbt_scr[...])

    @pl.when(eb == 0)
    def _prologue():
        ht = h_ref[...].T
        amax = jnp.maximum(jnp.max(jnp.abs(ht), axis=0, keepdims=True), jnp.finfo(f32).tiny)
        e_tok = jnp.minimum(jnp.floor(jnp.log2(FP8_TARGET_MAX / amax)), FP8_MAX_SHIFT)
        hbt_scr[...] = (ht * jnp.exp2(e_tok)).astype(fp8)
        unscale_scr[...] = jnp.exp2(-e_tok) * us_ref[0, 0]
        q = _dot(h_ref[...].astype(bf16), wq_ref[...])
        for hc in range(2 * PEER_HEADS):
            q_scr[hc] = q[:, hc * PEER_HALF:(hc + 1) * PEER_HALF]

        def head_sort(h, carry):
            for c in range(2):
                q_hi, q_lo = _split_bf16(q_scr[2 * h + c])
                k_hi, k_lo = _split_bf16(sk_ref[2 * h + c])
                st = _dot_nt(k_hi, q_hi) + _dot_nt(k_hi, q_lo) + _dot_nt(k_lo, q_hi)
                pr_scr[c, h] = st
                srt = _top16_sorted(st)
                for a in range(k):
                    srt_scr[c, a, pl.ds(h, 1), :] = srt[a][0:1, :]
            return carry
        lax.fori_loop(0, PEER_HEADS, head_sort, 0)

        a_s = [srt_scr[0, a] for a in range(k)]
        b_s = [srt_scr[1, a] for a in range(k)]
        cands = [a_s[a] + b_s[b] for a in range(k) for b in range(k) if (a + 1) * (b + 1) <= k]
        cands += [jnp.full_like(cands[0], -jnp.inf)] * (64 - len(cands))
        cs = _bitonic_sort_desc(cands)
        zsum = jnp.zeros_like(cs[0])
        for a in range(k):
            zsum = zsum + jnp.exp(cs[a] - cs[0])
        row_scr[0] = 0.5 * (cs[k - 1] + cs[k])
        row_scr[1] = 1.0 / zsum
        row_scr[2] = a_s[0]
        row_scr[3] = b_s[0]
        row_scr[4] = a_s[k - 1]
        row_scr[5] = b_s[k - 1]

        def head_prep(h, carry):
            hrow = pl.ds(h, 1)
            at = pr_scr[0, h]
            bt = pr_scr[1, h]
            thr = row_scr[0, hrow, :] - at
            r_cnt = jnp.zeros_like(at)
            b_rank = jnp.zeros_like(bt)
            for a in range(k):
                b_a = srt_scr[1, a, hrow, :]
                r_cnt = jnp.where(b_a > thr, float(a + 1), r_cnt)
                b_rank = jnp.where(b_a > bt, float(a + 1), b_rank)
            in_a = at >= row_scr[4, hrow, :]
            p_w = jnp.where(in_a, jnp.exp(at - row_scr[2, hrow, :]), 0.0) * row_scr[1, hrow, :]
            pr_scr[0, h] = _dup_bf16(p_w)
            pr_scr[1, h] = _dup_bf16(jnp.where(in_a, r_cnt, 0.0))
            q_w = jnp.where(bt >= row_scr[5, hrow, :], jnp.exp(bt - row_scr[3, hrow, :]), 0.0)
            rq_scr[0, h] = b_rank.astype(bf16).reshape(nk // BF16_ROWS, BF16_ROWS, tt)
            rq_scr[1, h] = q_w.astype(bf16).reshape(nk // BF16_ROWS, BF16_ROWS, tt)
            return carry
        lax.fori_loop(0, PEER_HEADS, head_prep, 0)
        acc_scr[...] = jnp.zeros_like(acc_scr)

    activations()
    for s in range(n_chunk):
        halves = []
        for half in range(chunk // nk):
            n1 = pl.ds((EB_PEER // nk) * eb + s * (chunk // nk) + half, 1)
            g = jnp.zeros((nk // BF16_ROWS, BF16_ROWS, tt), bf16)
            for h in range(PEER_HEADS):
                p_row = _bf16_rows(pr_scr[0, h, n1, :])
                r_row = _bf16_rows(pr_scr[1, h, n1, :])
                g = g + p_row * jnp.where(rq_scr[0, h] < r_row, rq_scr[1, h], 0)
            halves.append(g.reshape(nk, tt))
        act = a_scr[s].astype(bf16) * unscale_scr[...].astype(bf16)
        w = jnp.concatenate(halves, axis=0) * jax.nn.gelu(act)
        acc_scr[...] += _dot(vt_ref[:, s * chunk:(s + 1) * chunk], w)

    @pl.when(eb == n_blk - 1)
    def _last():
        r = DN_ALPHA * h_ref[...] + acc_scr[...].T
        o_ref[...] = _layernorm(r, g_ref[...], b_ref[...])


def _peer(h1, wq_b, sub_keys, u_q, u_unscale, vt_b, g, b):
    u_b = u_q
    n, d_model = h1.shape
    n_exp = u_b.shape[0]
    tt = TT_PEER
    sk = sub_keys.reshape(2 * PEER_HEADS, PEER_N_KEYS, PEER_HALF).astype(f32)
    n_blk = n_exp // EB_PEER
    const = dict(pipeline_mode=pl.Buffered(1))
    return pl.pallas_call(
        functools.partial(_peer_kernel, n_exp),
        grid=(n // tt, n_blk),
        in_specs=[pl.BlockSpec(memory_space=pltpu.SMEM),
                  pl.BlockSpec((tt, d_model), lambda t, e: (t, 0)),
                  pl.BlockSpec(wq_b.shape, lambda t, e: (0, 0), **const),
                  pl.BlockSpec(sk.shape, lambda t, e: (0, 0, 0), **const),
                  pl.BlockSpec((EB_PEER, d_model), lambda t, e: (e, 0)),
                  pl.BlockSpec((d_model, EB_PEER), lambda t, e: (0, e)),
                  pl.BlockSpec((1, d_model), lambda t, e: (0, 0), **const),
                  pl.BlockSpec((1, d_model), lambda t, e: (0, 0), **const)],
        out_specs=pl.BlockSpec((tt, d_model), lambda t, e: (t, 0)),
        out_shape=jax.ShapeDtypeStruct((n, d_model), f32),
        scratch_shapes=[pltpu.VMEM((2 * PEER_HEADS, tt, PEER_HALF), f32),
                        pltpu.VMEM((2, PEER_HEADS, PEER_N_KEYS, tt), f32),
                        pltpu.VMEM((2, PEER_HEADS, PEER_N_KEYS // BF16_ROWS, BF16_ROWS, tt), bf16),
                        pltpu.VMEM((2, PEER_TOPK, PEER_HEADS, tt), f32),
                        pltpu.VMEM((6, PEER_HEADS, tt), f32),
                        pltpu.VMEM((d_model, tt), f32),
                        pltpu.VMEM((d_model, tt), fp8),
                        pltpu.VMEM((EB_PEER // (2 * PEER_N_KEYS), 2 * PEER_N_KEYS, tt), f32),
                        pltpu.VMEM((1, tt), f32)],
        compiler_params=pltpu.CompilerParams(dimension_semantics=("parallel", "arbitrary"),
                                             vmem_limit_bytes=VMEM_LIMIT),
        name="peer",
    )(u_unscale.reshape(1, 1), h1, wq_b, sk, u_b, vt_b, g, b)


def _quantize_table(t):
    amax = jnp.maximum(jnp.max(jnp.abs(t)), jnp.finfo(f32).tiny)
    e = jnp.minimum(jnp.floor(jnp.log2(FP8_TARGET_MAX / amax)), FP8_MAX_SHIFT)
    return (t * jnp.exp2(e)).astype(fp8), jnp.exp2(-e).astype(f32)


def kernel(x, w_in, ssm_a_re, ssm_a_im, ssm_log_dt, ssm_b_re, ssm_b_im, ssm_c_re, ssm_c_im, ssm_d,
           ssm_w_glu, w_out, ln1_g, ln1_b, peer_w_q, peer_sub_keys, peer_u, peer_v, ln2_g, ln2_b):
    bsz, seq, d_model = x.shape
    n = bsz * seq
    h = x.reshape(n, d_model)
    for l in range(w_in.shape[0]):
        u2, k2, qt, vt, kmean = _inproj(h, w_in[l])
        ops = _s5_operators(ssm_a_re[l], ssm_a_im[l], ssm_log_dt[l], ssm_b_re[l], ssm_b_im[l],
                            ssm_c_re[l], ssm_c_im[l], ssm_d[l])
        yg = _s5(u2, bsz, seq, ops)
        att = _moba(qt, k2, vt, kmean.reshape(bsz, seq // MOBA_BLOCK, D_ATT), bsz, seq)
        h1 = _outproj(h, yg, att, ssm_w_glu[l].astype(bf16), w_out[l].astype(bf16),
                      ln1_g[l].reshape(1, d_model), ln1_b[l].reshape(1, d_model))
        u_q, u_unscale = _quantize_table(peer_u[l])
        h = _peer(h1, peer_w_q[l].astype(bf16), peer_sub_keys[l], u_q, u_unscale,
                  peer_v[l].T.astype(bf16), ln2_g[l].reshape(1, d_model), ln2_b[l].reshape(1, d_model))
    return h.reshape(bsz, seq, d_model)
```

```python
import functools
import math

import jax
import jax.numpy as jnp
from jax import lax
from jax.experimental import pallas as pl
from jax.experimental.pallas import tpu as pltpu

f32 = jnp.float32
bf16 = jnp.bfloat16
fp8 = jnp.float8_e4m3fn
FP8_TARGET_MAX = 256.0
FP8_MAX_SHIFT = 48.0

D_SSM = 512
SSM_GROUP = 16
SSM_STATE = 64
N_ATT_HEADS = 8
ATT_HEAD_DIM = 64
D_ATT = N_ATT_HEADS * ATT_HEAD_DIM
MOBA_BLOCK = 256
MOBA_TOPK = 3
PEER_HEADS = 8
PEER_HALF = 128
PEER_N_KEYS = 128
PEER_TOPK = 16
LN_EPS = 1e-5
DEPTH = 1
DN_ALPHA = (2.0 * DEPTH) ** 0.25
NEG = -1e30

LANES = 128
SUBLANES = 8
BF16_ROWS = 2 * SUBLANES
VMEM_LIMIT = 48 * 1024 * 1024

TM_PROJ = 1024
S5_CHUNK = 8
S5_LANE_GROUPS = LANES // SSM_GROUP
TT_PEER = 512
EB_PEER = 2048


def _dot(a, b):
    return jnp.dot(a, b, preferred_element_type=f32)


def _dot_nt(a, b):
    return lax.dot_general(a, b, (((1,), (1,)), ((), ())), preferred_element_type=f32)


def _gelu_tanh(x):
    k1 = -2.0 * math.sqrt(2.0 / math.pi) * math.log2(math.e)
    e = jnp.exp2(x * (k1 + (k1 * 0.044715) * (x * x)))
    return x / (1.0 + e)


def _split_bf16(x):
    hi = x.astype(bf16)
    lo = (x - hi.astype(f32)).astype(bf16)
    return hi, lo


def _layernorm(r, g, b):
    mu = jnp.mean(r, axis=-1, keepdims=True)
    d = r - mu
    var = jnp.mean(d * d, axis=-1, keepdims=True)
    return d * lax.rsqrt(var + LN_EPS) * g + b


def _inproj_kernel(x_ref, wuk_ref, wqvt_ref, u_ref, k_ref, qt_ref, vt_ref, km_ref):
    xb = x_ref[...].astype(bf16)
    acc = _dot(xb, wuk_ref[...])
    u_ref[...] = acc[:, :D_SSM]
    kk = acc[:, D_SSM:]
    k_ref[...] = kk.astype(bf16)
    qvt = _dot_nt(wqvt_ref[...], xb)
    for r in range(TM_PROJ // MOBA_BLOCK):
        cols = slice(r * MOBA_BLOCK, (r + 1) * MOBA_BLOCK)
        km_ref[r] = jnp.mean(kk[cols], axis=0, keepdims=True)
        qt_ref[r] = qvt[:D_ATT, cols].astype(bf16)
        vt_ref[r] = qvt[D_ATT:, cols].astype(bf16)


def _inproj(x2, w_in):
    n, d_model = x2.shape
    nb = TM_PROJ // MOBA_BLOCK
    w_uk = jnp.concatenate([w_in[:, :D_SSM], w_in[:, D_SSM + D_ATT:D_SSM + 2 * D_ATT]], axis=1).astype(bf16)
    w_qvt = jnp.concatenate([w_in[:, D_SSM:D_SSM + D_ATT], w_in[:, D_SSM + 2 * D_ATT:]], axis=1).T.astype(bf16)
    return pl.pallas_call(
        _inproj_kernel,
        grid=(n // TM_PROJ,),
        in_specs=[pl.BlockSpec((TM_PROJ, d_model), lambda i: (i, 0)),
                  pl.BlockSpec(w_uk.shape, lambda i: (0, 0)),
                  pl.BlockSpec(w_qvt.shape, lambda i: (0, 0))],
        out_specs=[pl.BlockSpec((TM_PROJ, D_SSM), lambda i: (i, 0)),
                   pl.BlockSpec((TM_PROJ, D_ATT), lambda i: (i, 0)),
                   pl.BlockSpec((nb, D_ATT, MOBA_BLOCK), lambda i: (i, 0, 0)),
                   pl.BlockSpec((nb, D_ATT, MOBA_BLOCK), lambda i: (i, 0, 0)),
                   pl.BlockSpec((nb, 1, D_ATT), lambda i: (i, 0, 0))],
        out_shape=[jax.ShapeDtypeStruct((n, D_SSM), f32),
                   jax.ShapeDtypeStruct((n, D_ATT), bf16),
                   jax.ShapeDtypeStruct((n // MOBA_BLOCK, D_ATT, MOBA_BLOCK), bf16),
                   jax.ShapeDtypeStruct((n // MOBA_BLOCK, D_ATT, MOBA_BLOCK), bf16),
                   jax.ShapeDtypeStruct((n // MOBA_BLOCK, 1, D_ATT), f32)],
        compiler_params=pltpu.CompilerParams(dimension_semantics=("parallel",),
                                             vmem_limit_bytes=VMEM_LIMIT),
        name="inproj",
    )(x2, w_uk, w_qvt)


def _s5_operators(a_re, a_im, log_dt, b_re, b_im, c_re, c_im, d_skip):
    T = S5_CHUNK
    G, P = a_re.shape
    H = SSM_GROUP
    nj = G // S5_LANE_GROUPS
    g8 = S5_LANE_GROUPS
    lam = lax.complex(a_re.astype(f32), a_im.astype(f32))
    dt = jnp.exp(log_dt.astype(f32))[:, None]
    lam_bar = jnp.exp(lam * dt)
    b_c = lax.complex(b_re.astype(f32), b_im.astype(f32))
    b_bar = ((lam_bar - 1.0) / lam)[:, :, None] * b_c
    c_c = lax.complex(c_re.astype(f32), c_im.astype(f32))
    pows = jnp.exp((lam * dt)[None] * jnp.arange(T + 1, dtype=f32)[:, None, None])
    def widen(x, reps):
        n = x.shape[-1]
        rep = jnp.tile(jnp.eye(n, dtype=f32), (1, reps))
        return jnp.dot(x.reshape(-1, n), rep).reshape(x.shape[:-1] + (reps * n,))

    same = jnp.arange(g8)[:, None] == (jnp.arange(g8 * H) // H)[None, :]
    same_p = jnp.arange(g8)[:, None] == (jnp.arange(g8 * P) // P)[None, :]
    kern = jnp.real(jnp.einsum('ghp,dgp,gpk->dgkh', c_c, pows[:T], b_bar))
    dd = jnp.arange(T)[None, :] - jnp.arange(T)[:, None]
    kst = jnp.where((dd >= 0)[:, :, None, None, None], kern[jnp.clip(dd, 0, T - 1)], 0.0)
    kst = widen(kst.reshape(T, T, nj, g8, H, H), g8)
    kst = jnp.where(same[None, None, None, :, None, :], kst, 0.0)
    toep = kst.transpose(2, 0, 3, 4, 1, 5).reshape(nj, T * LANES, T * LANES)
    win = pows[T - 1 - jnp.arange(T)][:, :, None, :] * jnp.swapaxes(b_bar, 1, 2)[None]
    win = jnp.stack([jnp.real(win), jnp.imag(win)], 0).reshape(2, T, nj, g8, H, P)
    win = jnp.where(same_p[None, None, None, :, None, :], widen(win, g8), 0.0)
    m_in = win.transpose(2, 1, 3, 4, 0, 5).reshape(nj, T * LANES, 2 * g8 * P)
    wout = jnp.swapaxes(c_c, 1, 2)[None] * pows[1:T + 1][:, :, :, None]
    wout = jnp.stack([jnp.real(wout), -jnp.imag(wout)], 0).reshape(2, T, nj, g8, P, H)
    wout = jnp.where(same[None, None, None, :, None, :], widen(wout, g8), 0.0)
    m_out = wout.transpose(2, 0, 3, 4, 1, 5).reshape(nj, 2 * g8 * P, T * LANES)
    a_chunk = pows[T].reshape(nj, 1, g8 * P)
    a_ri = jnp.concatenate([jnp.real(a_chunk), jnp.imag(a_chunk)], axis=1)
    d_flat = jnp.tile(d_skip.astype(f32).reshape(nj, 1, LANES), (1, 1, T))
    return toep.astype(bf16), m_in.astype(bf16), m_out.astype(bf16), a_ri, d_flat


def _s5_kernel(u_ref, toep_ref, min_ref, mout_ref, a_ref, d_ref, y_ref, in_scr, sp_scr):
    T = S5_CHUNK
    n_c = u_ref.shape[0] // T
    half = in_scr.shape[1] // 2
    uf = jnp.concatenate([u_ref[pl.ds(t, n_c, stride=T), :] for t in range(T)], axis=1)
    ufb = uf.astype(bf16)
    y = _dot(ufb, toep_ref[0]) + uf * d_ref[0]
    in_scr[...] = _dot(ufb, min_ref[0])
    ar = a_ref[0, 0:1, :]
    ai = a_ref[0, 1:2, :]

    def step(c, carry):
        sr, si = carry
        sp_scr[pl.ds(c, 1), :half] = sr
        sp_scr[pl.ds(c, 1), half:] = si
        ir = in_scr[pl.ds(c, 1), :half]
        ii = in_scr[pl.ds(c, 1), half:]
        return ar * sr - ai * si + ir, ar * si + ai * sr + ii

    z = jnp.zeros((1, half), f32)
    lax.fori_loop(0, n_c, step, (z, z), unroll=4)
    y = y + _dot(sp_scr[...].astype(bf16), mout_ref[0])
    yg = jax.nn.gelu(y)
    for t in range(T):
        y_ref[pl.ds(t, n_c, stride=T), :] = yg[:, t * LANES:(t + 1) * LANES]


def _s5(u2, bsz, seq, ops):
    toep, m_in, m_out, a_ri, d_flat = ops
    nj = toep.shape[0]
    n_c = seq // S5_CHUNK
    kdim = S5_CHUNK * LANES
    sdim = m_in.shape[2]
    return pl.pallas_call(
        _s5_kernel,
        grid=(nj, bsz),
        in_specs=[pl.BlockSpec((seq, LANES), lambda j, b: (b, j)),
                  pl.BlockSpec((1, kdim, kdim), lambda j, b: (j, 0, 0)),
                  pl.BlockSpec((1, kdim, sdim), lambda j, b: (j, 0, 0)),
                  pl.BlockSpec((1, sdim, kdim), lambda j, b: (j, 0, 0)),
                  pl.BlockSpec((1, 2, sdim // 2), lambda j, b: (j, 0, 0)),
                  pl.BlockSpec((1, 1, kdim), lambda j, b: (j, 0, 0))],
        out_specs=pl.BlockSpec((seq, LANES), lambda j, b: (b, j)),
        out_shape=jax.ShapeDtypeStruct(u2.shape, f32),
        scratch_shapes=[pltpu.VMEM((n_c, sdim), f32), pltpu.VMEM((n_c, sdim), f32)],
        compiler_params=pltpu.CompilerParams(dimension_semantics=("parallel", "parallel"),
                                             vmem_limit_bytes=VMEM_LIMIT),
        name="s5",
    )(u2, toep, m_in, m_out, a_ri, d_flat)


def _moba_kernel(qt_ref, k_ref, vt_ref, km_ref, o_ref, kaug_ref, s0_scr, s1_scr):
    blk = MOBA_BLOCK
    hd = ATT_HEAD_DIM
    nb = k_ref.shape[0] // blk
    i = pl.program_id(2)

    @pl.when(i == 0)
    def _build():
        lane = lax.broadcasted_iota(jnp.int32, (blk, LANES), 1)

        def build(j, carry):
            rows = pl.ds(pl.multiple_of(j * blk, blk), blk)
            kaug_ref[rows, :LANES] = k_ref[rows, :]
            kaug_ref[rows, LANES:] = jnp.where(lane == j, 1.0, 0.0).astype(bf16)
            return carry
        lax.fori_loop(0, nb, build, 0)

    qt = qt_ref[0]
    qtf = qt.astype(f32) * (hd ** -0.5)
    km = km_ref[0]
    lane_nb = lax.broadcasted_iota(jnp.int32, (nb, LANES), 1)
    feat = lax.broadcasted_iota(jnp.int32, (LANES, blk), 0)
    blk_id = lax.broadcasted_iota(jnp.int32, (nb, blk), 0)
    blk_f = blk_id.astype(f32)
    q_cols, bias_cols = [], []
    for hh in range(2):
        km_hi, km_lo = _split_bf16(jnp.where((lane_nb < hd) if hh == 0 else (lane_nb >= hd), km, 0.0))
        gate = _dot(km_hi, qt) + _dot(km_lo, qt)
        g = jnp.where(blk_id < i, gate, -jnp.inf)
        sel = blk_id == i
        for _ in range(MOBA_TOPK):
            mx = jnp.max(g, axis=0, keepdims=True)
            first = jnp.min(jnp.where(g == mx, blk_f, float(nb)), axis=0, keepdims=True)
            pick = (blk_f == first) & (mx > -jnp.inf)
            sel = sel | pick
            g = jnp.where(pick, -jnp.inf, g)
        bias_cols.append(jnp.where(sel, 0.0, NEG))
        q_cols.append(jnp.where((feat < hd) if hh == 0 else (feat >= hd), qtf, 0.0))
    qa = jnp.concatenate([jnp.concatenate(q_cols, axis=1), jnp.concatenate(bias_cols, axis=1),
                          jnp.zeros((LANES - nb, 2 * blk), f32)], axis=0).astype(bf16)

    key_pos = lax.broadcasted_iota(jnp.int32, (blk, 2 * blk), 0)
    q_pos = lax.broadcasted_iota(jnp.int32, (blk, 2 * blk), 1) % blk

    def scores(j):
        j = jnp.minimum(j, nb - 1)
        return _dot(kaug_ref[pl.ds(pl.multiple_of(j * blk, blk), blk), :], qa)

    def update(state, blocks):
        m, l, acc = state
        m_new = m
        for _, s in blocks:
            m_new = jnp.maximum(m_new, jnp.max(s, axis=0, keepdims=True))
        alpha = jnp.exp(m - m_new)
        l = alpha * l
        acc = alpha * acc
        for j, s in blocks:
            p = jnp.exp(s - m_new)
            l = l + jnp.sum(p, axis=0, keepdims=True)
            pb = p.astype(bf16)
            acc = acc + jnp.concatenate([_dot(vt_ref[j, :hd, :], pb[:, :blk]),
                                         _dot(vt_ref[j, hd:, :], pb[:, blk:])], axis=1)
        return m_new, l, acc

    def pair_scores(p, dst):
        dst[0] = scores(2 * p)
        dst[1] = scores(2 * p + 1)

    def pair_update(state, p, src, masked):
        blocks = []
        for r in range(2):
            j = 2 * p + r
            s = src[r]
            if masked:
                off = jnp.where(j < i, blk, jnp.where(j == i, 0, -2 * blk))
                s = jnp.where(key_pos <= q_pos + off, s, NEG)
            blocks.append((jnp.minimum(j, nb - 1), s))
        return update(state, blocks)

    def quad_body(q, state):
        pair_scores(2 * q + 1, s1_scr)
        state = pair_update(state, 2 * q, s0_scr, False)
        pair_scores(2 * q + 2, s0_scr)
        return pair_update(state, 2 * q + 1, s1_scr, False)

    state = (jnp.full((1, 2 * blk), -jnp.inf, f32), jnp.zeros((1, 2 * blk), f32), jnp.zeros((hd, 2 * blk), f32))
    n_quad = i // 4
    pair_scores(0, s0_scr)
    state = lax.fori_loop(0, n_quad, quad_body, state)
    pair_scores(2 * n_quad + 1, s1_scr)
    state = pair_update(state, 2 * n_quad, s0_scr, True)
    m, l, acc = pair_update(state, 2 * n_quad + 1, s1_scr, True)
    out = acc / l
    o_ref[...] = jnp.concatenate([out[:, :blk], out[:, blk:]], axis=0).T.astype(o_ref.dtype)


def _moba(qt, k2, vt, kmean, bsz, seq):
    nb = seq // MOBA_BLOCK
    assert seq % MOBA_BLOCK == 0 and nb % SUBLANES == 0 and nb <= LANES
    npair = D_ATT // LANES
    return pl.pallas_call(
        _moba_kernel,
        grid=(bsz, npair, nb),
        in_specs=[pl.BlockSpec((1, LANES, MOBA_BLOCK), lambda b, h, i: (b * nb + i, h, 0)),
                  pl.BlockSpec((seq, LANES), lambda b, h, i: (b, h)),
                  pl.BlockSpec((nb, LANES, MOBA_BLOCK), lambda b, h, i: (b, h, 0)),
                  pl.BlockSpec((1, nb, LANES), lambda b, h, i: (b, 0, h))],
        out_specs=pl.BlockSpec((MOBA_BLOCK, LANES), lambda b, h, i: (b * nb + i, h)),
        out_shape=jax.ShapeDtypeStruct(k2.shape, bf16),
        scratch_shapes=[pltpu.VMEM((seq, 2 * LANES), bf16),
                        pltpu.VMEM((2, MOBA_BLOCK, 2 * MOBA_BLOCK), f32),
                        pltpu.VMEM((2, MOBA_BLOCK, 2 * MOBA_BLOCK), f32)],
        compiler_params=pltpu.CompilerParams(dimension_semantics=("parallel", "parallel", "arbitrary"),
                                             vmem_limit_bytes=VMEM_LIMIT),
        name="moba",
    )(qt, k2, vt, kmean)


def _outproj_kernel(x_ref, yg_ref, att_ref, wglu_ref, wout_ref, g_ref, b_ref, h_ref):
    yg = yg_ref[...]
    z = yg * jax.nn.sigmoid(_dot(yg.astype(bf16), wglu_ref[...]))
    cat = jnp.concatenate([z.astype(bf16), att_ref[...]], axis=1)
    r = DN_ALPHA * x_ref[...] + _dot(cat, wout_ref[...])
    h_ref[...] = _layernorm(r, g_ref[...], b_ref[...])


def _outproj(x2, yg, att, wglu_b, wout_b, g, b):
    n, d_model = x2.shape
    return pl.pallas_call(
        _outproj_kernel,
        grid=(n // TM_PROJ,),
        in_specs=[pl.BlockSpec((TM_PROJ, d_model), lambda i: (i, 0)),
                  pl.BlockSpec((TM_PROJ, D_SSM), lambda i: (i, 0)),
                  pl.BlockSpec((TM_PROJ, D_ATT), lambda i: (i, 0)),
                  pl.BlockSpec(wglu_b.shape, lambda i: (0, 0)),
                  pl.BlockSpec(wout_b.shape, lambda i: (0, 0)),
                  pl.BlockSpec((1, d_model), lambda i: (0, 0)),
                  pl.BlockSpec((1, d_model), lambda i: (0, 0))],
        out_specs=pl.BlockSpec((TM_PROJ, d_model), lambda i: (i, 0)),
        out_shape=jax.ShapeDtypeStruct((n, d_model), f32),
        compiler_params=pltpu.CompilerParams(dimension_semantics=("parallel",),
                                             vmem_limit_bytes=VMEM_LIMIT),
        name="outproj",
    )(x2, yg, att, wglu_b, wout_b, g, b)


def _bitonic_sort_desc(v):
    n = len(v)
    k = 2
    while k <= n:
        j = k // 2
        while j >= 1:
            for a in range(n):
                b = a ^ j
                if b > a:
                    hi = jnp.maximum(v[a], v[b])
                    lo = jnp.minimum(v[a], v[b])
                    v[a], v[b] = (hi, lo) if (a & k) == 0 else (lo, hi)
            j //= 2
        k *= 2
    return v


def _bitonic_merge_desc(v):
    j = len(v) // 2
    while j >= 1:
        for a in range(len(v)):
            b = a ^ j
            if b > a:
                v[a], v[b] = jnp.maximum(v[a], v[b]), jnp.minimum(v[a], v[b])
        j //= 2
    return v


def _top16_sorted(st):
    k = PEER_TOPK
    x = _bitonic_sort_desc([st[SUBLANES * a:SUBLANES * (a + 1), :] for a in range(PEER_N_KEYS // SUBLANES)])
    for shift in (4, 2, 1):
        y = [pltpu.roll(x[k - 1 - a], shift, 0) for a in range(k)]
        x = _bitonic_merge_desc([jnp.maximum(x[a], y[a]) for a in range(k)])
    return x


def _dup_bf16(x):
    u = pltpu.bitcast(x.astype(bf16).astype(f32), jnp.uint32)
    return pltpu.bitcast(u | (u >> 16), f32)


def _bf16_rows(row):
    return pltpu.bitcast(jnp.broadcast_to(row, (SUBLANES, row.shape[1])), bf16)[None]


def _peer_kernel(n_exp, us_ref, h_ref, wq_ref, sk_ref, u_ref, vt_ref, g_ref, b_ref, o_ref,
                 q_scr, pr_scr, rq_scr, srt_scr, row_scr, acc_scr, hbt_scr, a_scr, unscale_scr):
    eb = pl.program_id(1)
    n_blk = n_exp // EB_PEER
    tt = h_ref.shape[0]
    nk = PEER_N_KEYS
    k = PEER_TOPK

    chunk = 2 * nk
    n_chunk = EB_PEER // chunk

    def activations():
        for s in range(n_chunk):
            a_scr[s] = _dot(u_ref[s * chunk:(s + 1) * chunk, :], hbt_scr[...])

    @pl.when(eb == 0)
    def _prologue():
        ht = h_ref[...].T
        amax = jnp.maximum(jnp.max(jnp.abs(ht), axis=0, keepdims=True), jnp.finfo(f32).tiny)
        e_tok = jnp.minimum(jnp.floor(jnp.log2(FP8_TARGET_MAX / amax)), FP8_MAX_SHIFT)
        hbt_scr[...] = (ht * jnp.exp2(e_tok)).astype(fp8)
        unscale_scr[...] = jnp.exp2(-e_tok) * us_ref[0, 0]
        q = _dot(h_ref[...].astype(bf16), wq_ref[...])
        for hc in range(2 * PEER_HEADS):
            q_scr[hc] = q[:, hc * PEER_HALF:(hc + 1) * PEER_HALF]

        def head_sort(h, carry):
            for c in range(2):
                q_hi, q_lo = _split_bf16(q_scr[2 * h + c])
                k_hi, k_lo = _split_bf16(sk_ref[2 * h + c])
                st = _dot_nt(k_hi, q_hi) + _dot_nt(k_hi, q_lo) + _dot_nt(k_lo, q_hi)
                pr_scr[c, h] = st
                srt = _top16_sorted(st)
                for a in range(k):
                    srt_scr[c, a, pl.ds(h, 1), :] = srt[a][0:1, :]
            return carry
        lax.fori_loop(0, PEER_HEADS, head_sort, 0)

        a_s = [srt_scr[0, a] for a in range(k)]
        b_s = [srt_scr[1, a] for a in range(k)]
        cands = [a_s[a] + b_s[b] for a in range(k) for b in range(k) if (a + 1) * (b + 1) <= k]
        cands += [jnp.full_like(cands[0], -jnp.inf)] * (64 - len(cands))
        cs = _bitonic_sort_desc(cands)
        zsum = jnp.zeros_like(cs[0])
        for a in range(k):
            zsum = zsum + jnp.exp(cs[a] - cs[0])
        row_scr[0] = 0.5 * (cs[k - 1] + cs[k])
        row_scr[1] = 1.0 / zsum
        row_scr[2] = a_s[0]
        row_scr[3] = b_s[0]
        row_scr[4] = a_s[k - 1]
        row_scr[5] = b_s[k - 1]

        def head_prep(h, carry):
            hrow = pl.ds(h, 1)
            at = pr_scr[0, h]
            bt = pr_scr[1, h]
            thr = row_scr[0, hrow, :] - at
            r_cnt = jnp.zeros_like(at)
            b_rank = jnp.zeros_like(bt)
            for a in range(k):
                b_a = srt_scr[1, a, hrow, :]
                r_cnt = jnp.where(b_a > thr, float(a + 1), r_cnt)
                b_rank = jnp.where(b_a > bt, float(a + 1), b_rank)
            in_a = at >= row_scr[4, hrow, :]
            p_w = jnp.where(in_a, jnp.exp(at - row_scr[2, hrow, :]), 0.0) * row_scr[1, hrow, :]
            pr_scr[0, h] = _dup_bf16(p_w)
            pr_scr[1, h] = _dup_bf16(jnp.where(in_a, r_cnt, 0.0))
            q_w = jnp.where(bt >= row_scr[5, hrow, :], jnp.exp(bt - row_scr[3, hrow, :]), 0.0)
            rq_scr[0, h] = b_rank.astype(bf16).reshape(nk // BF16_ROWS, BF16_ROWS, tt)
            rq_scr[1, h] = q_w.astype(bf16).reshape(nk // BF16_ROWS, BF16_ROWS, tt)
            return carry
        lax.fori_loop(0, PEER_HEADS, head_prep, 0)
        acc_scr[...] = jnp.zeros_like(acc_scr)

    activations()
    for s in range(n_chunk):
        halves = []
        for half in range(chunk // nk):
            n1 = pl.ds((EB_PEER // nk) * eb + s * (chunk // nk) + half, 1)
            g = jnp.zeros((nk // BF16_ROWS, BF16_ROWS, tt), bf16)
            for h in range(PEER_HEADS):
                p_row = _bf16_rows(pr_scr[0, h, n1, :])
                r_row = _bf16_rows(pr_scr[1, h, n1, :])
                g = g + p_row * jnp.where(rq_scr[0, h] < r_row, rq_scr[1, h], 0)
            halves.append(g.reshape(nk, tt))
        act = a_scr[s].astype(bf16) * unscale_scr[...].astype(bf16)
        w = jnp.concatenate(halves, axis=0) * _gelu_tanh(act)
        acc_scr[...] += _dot(vt_ref[:, s * chunk:(s + 1) * chunk], w)

    @pl.when(eb == n_blk - 1)
    def _last():
        r = DN_ALPHA * h_ref[...] + acc_scr[...].T
        o_ref[...] = _layernorm(r, g_ref[...], b_ref[...])


def _peer(h1, wq_b, sub_keys, u_q, u_unscale, vt_b, g, b):
    u_b = u_q
    n, d_model = h1.shape
    n_exp = u_b.shape[0]
    tt = TT_PEER
    sk = sub_keys.reshape(2 * PEER_HEADS, PEER_N_KEYS, PEER_HALF).astype(f32)
    n_blk = n_exp // EB_PEER
    const = dict(pipeline_mode=pl.Buffered(1))
    return pl.pallas_call(
        functools.partial(_peer_kernel, n_exp),
        grid=(n // tt, n_blk),
        in_specs=[pl.BlockSpec(memory_space=pltpu.SMEM),
                  pl.BlockSpec((tt, d_model), lambda t, e: (t, 0)),
                  pl.BlockSpec(wq_b.shape, lambda t, e: (0, 0), **const),
                  pl.BlockSpec(sk.shape, lambda t, e: (0, 0, 0), **const),
                  pl.BlockSpec((EB_PEER, d_model), lambda t, e: (e, 0)),
                  pl.BlockSpec((d_model, EB_PEER), lambda t, e: (0, e)),
                  pl.BlockSpec((1, d_model), lambda t, e: (0, 0), **const),
                  pl.BlockSpec((1, d_model), lambda t, e: (0, 0), **const)],
        out_specs=pl.BlockSpec((tt, d_model), lambda t, e: (t, 0)),
        out_shape=jax.ShapeDtypeStruct((n, d_model), f32),
        scratch_shapes=[pltpu.VMEM((2 * PEER_HEADS, tt, PEER_HALF), f32),
                        pltpu.VMEM((2, PEER_HEADS, PEER_N_KEYS, tt), f32),
                        pltpu.VMEM((2, PEER_HEADS, PEER_N_KEYS // BF16_ROWS, BF16_ROWS, tt), bf16),
                        pltpu.VMEM((2, PEER_TOPK, PEER_HEADS, tt), f32),
                        pltpu.VMEM((6, PEER_HEADS, tt), f32),
                        pltpu.VMEM((d_model, tt), f32),
                        pltpu.VMEM((d_model, tt), fp8),
                        pltpu.VMEM((EB_PEER // (2 * PEER_N_KEYS), 2 * PEER_N_KEYS, tt), f32),
                        pltpu.VMEM((1, tt), f32)],
        compiler_params=pltpu.CompilerParams(dimension_semantics=("parallel", "arbitrary"),
                                             vmem_limit_bytes=VMEM_LIMIT),
        name="peer",
    )(u_unscale.reshape(1, 1), h1, wq_b, sk, u_b, vt_b, g, b)


def _quantize_table(t):
    amax = jnp.maximum(jnp.max(jnp.abs(t)), jnp.finfo(f32).tiny)
    e = jnp.minimum(jnp.floor(jnp.log2(FP8_TARGET_MAX / amax)), FP8_MAX_SHIFT)
    return (t * jnp.exp2(e)).astype(fp8), jnp.exp2(-e).astype(f32)


def kernel(x, w_in, ssm_a_re, ssm_a_im, ssm_log_dt, ssm_b_re, ssm_b_im, ssm_c_re, ssm_c_im, ssm_d,
           ssm_w_glu, w_out, ln1_g, ln1_b, peer_w_q, peer_sub_keys, peer_u, peer_v, ln2_g, ln2_b):
    bsz, seq, d_model = x.shape
    n = bsz * seq
    h = x.reshape(n, d_model)
    for l in range(w_in.shape[0]):
        u2, k2, qt, vt, kmean = _inproj(h, w_in[l])
        ops = _s5_operators(ssm_a_re[l], ssm_a_im[l], ssm_log_dt[l], ssm_b_re[l], ssm_b_im[l],
                            ssm_c_re[l], ssm_c_im[l], ssm_d[l])
        yg = _s5(u2, bsz, seq, ops)
        att = _moba(qt, k2, vt, kmean.reshape(bsz, seq // MOBA_BLOCK, D_ATT), bsz, seq)
        h1 = _outproj(h, yg, att, ssm_w_glu[l].astype(bf16), w_out[l].astype(bf16),
                      ln1_g[l].reshape(1, d_model), ln1_b[l].reshape(1, d_model))
        u_q, u_unscale = _quantize_table(peer_u[l])
        h = _peer(h1, peer_w_q[l].astype(bf16), peer_sub_keys[l], u_q, u_unscale,
                  peer_v[l].T.astype(bf16), ln2_g[l].reshape(1, d_model), ln2_b[l].reshape(1, d_model))
    return h.reshape(bsz, seq, d_model)
```

```python
import functools
import math

import jax
import jax.numpy as jnp
from jax import lax
from jax.experimental import pallas as pl
from jax.experimental.pallas import tpu as pltpu

f32 = jnp.float32
bf16 = jnp.bfloat16
fp8 = jnp.float8_e4m3fn
FP8_TARGET_MAX = 256.0
FP8_MAX_SHIFT = 48.0

D_SSM = 512
SSM_GROUP = 16
SSM_STATE = 64
N_ATT_HEADS = 8
ATT_HEAD_DIM = 64
D_ATT = N_ATT_HEADS * ATT_HEAD_DIM
MOBA_BLOCK = 256
MOBA_TOPK = 3
PEER_HEADS = 8
PEER_HALF = 128
PEER_N_KEYS = 128
PEER_TOPK = 16
LN_EPS = 1e-5
DEPTH = 1
DN_ALPHA = (2.0 * DEPTH) ** 0.25
NEG = -1e30

LANES = 128
SUBLANES = 8
BF16_ROWS = 2 * SUBLANES
VMEM_LIMIT = 48 * 1024 * 1024

TM_PROJ = 1024
S5_CHUNK = 8
S5_LANE_GROUPS = LANES // SSM_GROUP
TT_PEER = 512
EB_PEER = 2048


def _dot(a, b):
    return jnp.dot(a, b, preferred_element_type=f32)


def _dot_nt(a, b):
    return lax.dot_general(a, b, (((1,), (1,)), ((), ())), preferred_element_type=f32)


def _split_bf16(x):
    hi = x.astype(bf16)
    lo = (x - hi.astype(f32)).astype(bf16)
    return hi, lo


def _layernorm(r, g, b):
    mu = jnp.mean(r, axis=-1, keepdims=True)
    d = r - mu
    var = jnp.mean(d * d, axis=-1, keepdims=True)
    return d * lax.rsqrt(var + LN_EPS) * g + b


def _inproj_kernel(x_ref, wuk_ref, wqvt_ref, u_ref, k_ref, qt_ref, vt_ref, km_ref):
    xb = x_ref[...].astype(bf16)
    acc = _dot(xb, wuk_ref[...])
    u_ref[...] = acc[:, :D_SSM]
    kk = acc[:, D_SSM:]
    k_ref[...] = kk.astype(bf16)
    qvt = _dot_nt(wqvt_ref[...], xb)
    for r in range(TM_PROJ // MOBA_BLOCK):
        cols = slice(r * MOBA_BLOCK, (r + 1) * MOBA_BLOCK)
        km_ref[r] = jnp.mean(kk[cols], axis=0, keepdims=True)
        qt_ref[r] = qvt[:D_ATT, cols].astype(bf16)
        vt_ref[r] = qvt[D_ATT:, cols].astype(bf16)


def _inproj(x2, w_in):
    n, d_model = x2.shape
    nb = TM_PROJ // MOBA_BLOCK
    w_uk = jnp.concatenate([w_in[:, :D_SSM], w_in[:, D_SSM + D_ATT:D_SSM + 2 * D_ATT]], axis=1).astype(bf16)
    w_qvt = jnp.concatenate([w_in[:, D_SSM:D_SSM + D_ATT], w_in[:, D_SSM + 2 * D_ATT:]], axis=1).T.astype(bf16)
    return pl.pallas_call(
        _inproj_kernel,
        grid=(n // TM_PROJ,),
        in_specs=[pl.BlockSpec((TM_PROJ, d_model), lambda i: (i, 0)),
                  pl.BlockSpec(w_uk.shape, lambda i: (0, 0)),
                  pl.BlockSpec(w_qvt.shape, lambda i: (0, 0))],
        out_specs=[pl.BlockSpec((TM_PROJ, D_SSM), lambda i: (i, 0)),
                   pl.BlockSpec((TM_PROJ, D_ATT), lambda i: (i, 0)),
                   pl.BlockSpec((nb, D_ATT, MOBA_BLOCK), lambda i: (i, 0, 0)),
                   pl.BlockSpec((nb, D_ATT, MOBA_BLOCK), lambda i: (i, 0, 0)),
                   pl.BlockSpec((nb, 1, D_ATT), lambda i: (i, 0, 0))],
        out_shape=[jax.ShapeDtypeStruct((n, D_SSM), f32),
                   jax.ShapeDtypeStruct((n, D_ATT), bf16),
                   jax.ShapeDtypeStruct((n // MOBA_BLOCK, D_ATT, MOBA_BLOCK), bf16),
                   jax.ShapeDtypeStruct((n // MOBA_BLOCK, D_ATT, MOBA_BLOCK), bf16),
                   jax.ShapeDtypeStruct((n // MOBA_BLOCK, 1, D_ATT), f32)],
        compiler_params=pltpu.CompilerParams(dimension_semantics=("parallel",),
                                             vmem_limit_bytes=VMEM_LIMIT),
        name="inproj",
    )(x2, w_uk, w_qvt)


def _s5_operators(a_re, a_im, log_dt, b_re, b_im, c_re, c_im, d_skip):
    T = S5_CHUNK
    G, P = a_re.shape
    H = SSM_GROUP
    nj = G // S5_LANE_GROUPS
    g8 = S5_LANE_GROUPS
    lam = lax.complex(a_re.astype(f32), a_im.astype(f32))
    dt = jnp.exp(log_dt.astype(f32))[:, None]
    lam_bar = jnp.exp(lam * dt)
    b_c = lax.complex(b_re.astype(f32), b_im.astype(f32))
    b_bar = ((lam_bar - 1.0) / lam)[:, :, None] * b_c
    c_c = lax.complex(c_re.astype(f32), c_im.astype(f32))
    pows = jnp.exp((lam * dt)[None] * jnp.arange(T + 1, dtype=f32)[:, None, None])
    def widen(x, reps):
        n = x.shape[-1]
        rep = jnp.tile(jnp.eye(n, dtype=f32), (1, reps))
        return jnp.dot(x.reshape(-1, n), rep).reshape(x.shape[:-1] + (reps * n,))

    same = jnp.arange(g8)[:, None] == (jnp.arange(g8 * H) // H)[None, :]
    same_p = jnp.arange(g8)[:, None] == (jnp.arange(g8 * P) // P)[None, :]
    kern = jnp.real(jnp.einsum('ghp,dgp,gpk->dgkh', c_c, pows[:T], b_bar))
    dd = jnp.arange(T)[None, :] - jnp.arange(T)[:, None]
    kst = jnp.where((dd >= 0)[:, :, None, None, None], kern[jnp.clip(dd, 0, T - 1)], 0.0)
    kst = widen(kst.reshape(T, T, nj, g8, H, H), g8)
    kst = jnp.where(same[None, None, None, :, None, :], kst, 0.0)
    toep = kst.transpose(2, 0, 3, 4, 1, 5).reshape(nj, T * LANES, T * LANES)
    win = pows[T - 1 - jnp.arange(T)][:, :, None, :] * jnp.swapaxes(b_bar, 1, 2)[None]
    win = jnp.stack([jnp.real(win), jnp.imag(win)], 0).reshape(2, T, nj, g8, H, P)
    win = jnp.where(same_p[None, None, None, :, None, :], widen(win, g8), 0.0)
    m_in = win.transpose(2, 1, 3, 4, 0, 5).reshape(nj, T * LANES, 2 * g8 * P)
    wout = jnp.swapaxes(c_c, 1, 2)[None] * pows[1:T + 1][:, :, :, None]
    wout = jnp.stack([jnp.real(wout), -jnp.imag(wout)], 0).reshape(2, T, nj, g8, P, H)
    wout = jnp.where(same[None, None, None, :, None, :], widen(wout, g8), 0.0)
    m_out = wout.transpose(2, 0, 3, 4, 1, 5).reshape(nj, 2 * g8 * P, T * LANES)
    a_chunk = pows[T].reshape(nj, 1, g8 * P)
    a_ri = jnp.concatenate([jnp.real(a_chunk), jnp.imag(a_chunk)], axis=1)
    d_flat = jnp.tile(d_skip.astype(f32).reshape(nj, 1, LANES), (1, 1, T))
    return toep.astype(bf16), m_in.astype(bf16), m_out.astype(bf16), a_ri, d_flat


def _s5_kernel(u_ref, toep_ref, min_ref, mout_ref, a_ref, d_ref, y_ref, in_scr, sp_scr):
    T = S5_CHUNK
    n_c = u_ref.shape[0] // T
    half = in_scr.shape[1] // 2
    uf = jnp.concatenate([u_ref[pl.ds(t, n_c, stride=T), :] for t in range(T)], axis=1)
    ufb = uf.astype(bf16)
    y = _dot(ufb, toep_ref[0]) + uf * d_ref[0]
    in_scr[...] = _dot(ufb, min_ref[0])
    ar = a_ref[0, 0:1, :]
    ai = a_ref[0, 1:2, :]

    def step(c, carry):
        sr, si = carry
        sp_scr[pl.ds(c, 1), :half] = sr
        sp_scr[pl.ds(c, 1), half:] = si
        ir = in_scr[pl.ds(c, 1), :half]
        ii = in_scr[pl.ds(c, 1), half:]
        return ar * sr - ai * si + ir, ar * si + ai * sr + ii

    z = jnp.zeros((1, half), f32)
    lax.fori_loop(0, n_c, step, (z, z), unroll=4)
    y = y + _dot(sp_scr[...].astype(bf16), mout_ref[0])
    yg = jax.nn.gelu(y)
    for t in range(T):
        y_ref[pl.ds(t, n_c, stride=T), :] = yg[:, t * LANES:(t + 1) * LANES]


def _s5(u2, bsz, seq, ops):
    toep, m_in, m_out, a_ri, d_flat = ops
    nj = toep.shape[0]
    n_c = seq // S5_CHUNK
    kdim = S5_CHUNK * LANES
    sdim = m_in.shape[2]
    return pl.pallas_call(
        _s5_kernel,
        grid=(nj, bsz),
        in_specs=[pl.BlockSpec((seq, LANES), lambda j, b: (b, j)),
                  pl.BlockSpec((1, kdim, kdim), lambda j, b: (j, 0, 0)),
                  pl.BlockSpec((1, kdim, sdim), lambda j, b: (j, 0, 0)),
                  pl.BlockSpec((1, sdim, kdim), lambda j, b: (j, 0, 0)),
                  pl.BlockSpec((1, 2, sdim // 2), lambda j, b: (j, 0, 0)),
                  pl.BlockSpec((1, 1, kdim), lambda j, b: (j, 0, 0))],
        out_specs=pl.BlockSpec((seq, LANES), lambda j, b: (b, j)),
        out_shape=jax.ShapeDtypeStruct(u2.shape, f32),
        scratch_shapes=[pltpu.VMEM((n_c, sdim), f32), pltpu.VMEM((n_c, sdim), f32)],
        compiler_params=pltpu.CompilerParams(dimension_semantics=("parallel", "parallel"),
                                             vmem_limit_bytes=VMEM_LIMIT),
        name="s5",
    )(u2, toep, m_in, m_out, a_ri, d_flat)


def _moba_kernel(qt_ref, k_ref, vt_ref, km_ref, o_ref, kaug_ref, s0_scr, s1_scr):
    blk = MOBA_BLOCK
    hd = ATT_HEAD_DIM
    nb = k_ref.shape[0] // blk
    i = pl.program_id(2)

    @pl.when(i == 0)
    def _build():
        lane = lax.broadcasted_iota(jnp.int32, (blk, LANES), 1)

        def build(j, carry):
            rows = pl.ds(pl.multiple_of(j * blk, blk), blk)
            kaug_ref[rows, :LANES] = k_ref[rows, :]
            kaug_ref[rows, LANES:] = jnp.where(lane == j, 1.0, 0.0).astype(bf16)
            return carry
        lax.fori_loop(0, nb, build, 0)

    qt = qt_ref[0]
    qtf = qt.astype(f32) * (hd ** -0.5)
    km = km_ref[0]
    lane_nb = lax.broadcasted_iota(jnp.int32, (nb, LANES), 1)
    feat = lax.broadcasted_iota(jnp.int32, (LANES, blk), 0)
    blk_id = lax.broadcasted_iota(jnp.int32, (nb, blk), 0)
    blk_f = blk_id.astype(f32)
    q_cols, bias_cols = [], []
    for hh in range(2):
        km_hi, km_lo = _split_bf16(jnp.where((lane_nb < hd) if hh == 0 else (lane_nb >= hd), km, 0.0))
        gate = _dot(km_hi, qt) + _dot(km_lo, qt)
        g = jnp.where(blk_id < i, gate, -jnp.inf)
        sel = blk_id == i
        for _ in range(MOBA_TOPK):
            mx = jnp.max(g, axis=0, keepdims=True)
            first = jnp.min(jnp.where(g == mx, blk_f, float(nb)), axis=0, keepdims=True)
            pick = (blk_f == first) & (mx > -jnp.inf)
            sel = sel | pick
            g = jnp.where(pick, -jnp.inf, g)
        bias_cols.append(jnp.where(sel, 0.0, NEG))
        q_cols.append(jnp.where((feat < hd) if hh == 0 else (feat >= hd), qtf, 0.0))
    qa = jnp.concatenate([jnp.concatenate(q_cols, axis=1), jnp.concatenate(bias_cols, axis=1),
                          jnp.zeros((LANES - nb, 2 * blk), f32)], axis=0).astype(bf16)

    key_pos = lax.broadcasted_iota(jnp.int32, (blk, 2 * blk), 0)
    q_pos = lax.broadcasted_iota(jnp.int32, (blk, 2 * blk), 1) % blk

    def scores(j):
        j = jnp.minimum(j, nb - 1)
        return _dot(kaug_ref[pl.ds(pl.multiple_of(j * blk, blk), blk), :], qa)

    def update(state, blocks):
        m, l, acc = state
        m_new = m
        for _, s in blocks:
            m_new = jnp.maximum(m_new, jnp.max(s, axis=0, keepdims=True))
        alpha = jnp.exp(m - m_new)
        l = alpha * l
        acc = alpha * acc
        for j, s in blocks:
            p = jnp.exp(s - m_new)
            l = l + jnp.sum(p, axis=0, keepdims=True)
            pb = p.astype(bf16)
            acc = acc + jnp.concatenate([_dot(vt_ref[j, :hd, :], pb[:, :blk]),
                                         _dot(vt_ref[j, hd:, :], pb[:, blk:])], axis=1)
        return m_new, l, acc

    def pair_scores(p, dst):
        dst[0] = scores(2 * p)
        dst[1] = scores(2 * p + 1)

    def pair_update(state, p, src, masked):
        blocks = []
        for r in range(2):
            j = 2 * p + r
            s = src[r]
            if masked:
                off = jnp.where(j < i, blk, jnp.where(j == i, 0, -2 * blk))
                s = jnp.where(key_pos <= q_pos + off, s, NEG)
            blocks.append((jnp.minimum(j, nb - 1), s))
        return update(state, blocks)

    def quad_body(q, state):
        pair_scores(2 * q + 1, s1_scr)
        state = pair_update(state, 2 * q, s0_scr, False)
        pair_scores(2 * q + 2, s0_scr)
        return pair_update(state, 2 * q + 1, s1_scr, False)

    state = (jnp.full((1, 2 * blk), -jnp.inf, f32), jnp.zeros((1, 2 * blk), f32), jnp.zeros((hd, 2 * blk), f32))
    n_quad = i // 4
    pair_scores(0, s0_scr)
    state = lax.fori_loop(0, n_quad, quad_body, state)
    pair_scores(2 * n_quad + 1, s1_scr)
    state = pair_update(state, 2 * n_quad, s0_scr, True)
    m, l, acc = pair_update(state, 2 * n_quad + 1, s1_scr, True)
    out = acc / l
    o_ref[...] = jnp.concatenate([out[:, :blk], out[:, blk:]], axis=0).T.astype(o_ref.dtype)


def _moba(qt, k2, vt, kmean, bsz, seq):
    nb = seq // MOBA_BLOCK
    assert seq % MOBA_BLOCK == 0 and nb % SUBLANES == 0 and nb <= LANES
    npair = D_ATT // LANES
    return pl.pallas_call(
        _moba_kernel,
        grid=(bsz, npair, nb),
        in_specs=[pl.BlockSpec((1, LANES, MOBA_BLOCK), lambda b, h, i: (b * nb + i, h, 0)),
                  pl.BlockSpec((seq, LANES), lambda b, h, i: (b, h)),
                  pl.BlockSpec((nb, LANES, MOBA_BLOCK), lambda b, h, i: (b, h, 0)),
                  pl.BlockSpec((1, nb, LANES), lambda b, h, i: (b, 0, h))],
        out_specs=pl.BlockSpec((MOBA_BLOCK, LANES), lambda b, h, i: (b * nb + i, h)),
        out_shape=jax.ShapeDtypeStruct(k2.shape, bf16),
        scratch_shapes=[pltpu.VMEM((seq, 2 * LANES), bf16),
                        pltpu.VMEM((2, MOBA_BLOCK, 2 * MOBA_BLOCK), f32),
                        pltpu.VMEM((2, MOBA_BLOCK, 2 * MOBA_BLOCK), f32)],
        compiler_params=pltpu.CompilerParams(dimension_semantics=("parallel", "parallel", "arbitrary"),
                                             vmem_limit_bytes=VMEM_LIMIT),
        name="moba",
    )(qt, k2, vt, kmean)


def _outproj_kernel(x_ref, yg_ref, att_ref, wglu_ref, wout_ref, g_ref, b_ref, h_ref):
    yg = yg_ref[...]
    z = yg * jax.nn.sigmoid(_dot(yg.astype(bf16), wglu_ref[...]))
    cat = jnp.concatenate([z.astype(bf16), att_ref[...]], axis=1)
    r = DN_ALPHA * x_ref[...] + _dot(cat, wout_ref[...])
    h_ref[...] = _layernorm(r, g_ref[...], b_ref[...])


def _outproj(x2, yg, att, wglu_b, wout_b, g, b):
    n, d_model = x2.shape
    return pl.pallas_call(
        _outproj_kernel,
        grid=(n // TM_PROJ,),
        in_specs=[pl.BlockSpec((TM_PROJ, d_model), lambda i: (i, 0)),
                  pl.BlockSpec((TM_PROJ, D_SSM), lambda i: (i, 0)),
                  pl.BlockSpec((TM_PROJ, D_ATT), lambda i: (i, 0)),
                  pl.BlockSpec(wglu_b.shape, lambda i: (0, 0)),
                  pl.BlockSpec(wout_b.shape, lambda i: (0, 0)),
                  pl.BlockSpec((1, d_model), lambda i: (0, 0)),
                  pl.BlockSpec((1, d_model), lambda i: (0, 0))],
        out_specs=pl.BlockSpec((TM_PROJ, d_model), lambda i: (i, 0)),
        out_shape=jax.ShapeDtypeStruct((n, d_model), f32),
        compiler_params=pltpu.CompilerParams(dimension_semantics=("parallel",),
                                             vmem_limit_bytes=VMEM_LIMIT),
        name="outproj",
    )(x2, yg, att, wglu_b, wout_b, g, b)


def _bitonic_sort_desc(v):
    n = len(v)
    k = 2
    while k <= n:
        j = k // 2
        while j >= 1:
            for a in range(n):
                b = a ^ j
                if b > a:
                    hi = jnp.maximum(v[a], v[b])
                    lo = jnp.minimum(v[a], v[b])
                    v[a], v[b] = (hi, lo) if (a & k) == 0 else (lo, hi)
            j //= 2
        k *= 2
    return v


def _bitonic_merge_desc(v):
    j = len(v) // 2
    while j >= 1:
        for a in range(len(v)):
            b = a ^ j
            if b > a:
                v[a], v[b] = jnp.maximum(v[a], v[b]), jnp.minimum(v[a], v[b])
        j //= 2
    return v


def _top16_sorted(st):
    k = PEER_TOPK
    x = _bitonic_sort_desc([st[SUBLANES * a:SUBLANES * (a + 1), :] for a in range(PEER_N_KEYS // SUBLANES)])
    for shift in (4, 2, 1):
        y = [pltpu.roll(x[k - 1 - a], shift, 0) for a in range(k)]
        x = _bitonic_merge_desc([jnp.maximum(x[a], y[a]) for a in range(k)])
    return x


def _dup_bf16(x):
    u = pltpu.bitcast(x.astype(bf16).astype(f32), jnp.uint32)
    return pltpu.bitcast(u | (u >> 16), f32)


def _bf16_rows(row):
    return pltpu.bitcast(jnp.broadcast_to(row, (SUBLANES, row.shape[1])), bf16)[None]


def _peer_kernel(n_exp, us_ref, h_ref, wq_ref, sk_ref, u_ref, vt_ref, g_ref, b_ref, o_ref,
                 q_scr, pr_scr, rq_scr, srt_scr, row_scr, acc_scr, hbt_scr, a_scr, unscale_scr):
    eb = pl.program_id(1)
    n_blk = n_exp // EB_PEER
    tt = h_ref.shape[0]
    nk = PEER_N_KEYS
    k = PEER_TOPK

    chunk = 2 * nk
    n_chunk = EB_PEER // chunk

    def activations():
        for s in range(n_chunk):
            a_scr[s] = _dot(u_ref[s * chunk:(s + 1) * chunk, :], hbt_scr[...])

    @pl.when(eb == 0)
    def _prologue():
        ht = h_ref[...].T
        amax = jnp.maximum(jnp.max(jnp.abs(ht), keepdims=True), jnp.finfo(f32).tiny)
        e_tile = jnp.minimum(jnp.floor(jnp.log2(FP8_TARGET_MAX / amax)), FP8_MAX_SHIFT)
        hbt_scr[...] = (ht * jnp.exp2(e_tile)).astype(fp8)
        unscale_scr[0] = (jnp.exp2(-e_tile) * us_ref[0, 0])[0, 0]
        q = _dot(h_ref[...].astype(bf16), wq_ref[...])
        for hc in range(2 * PEER_HEADS):
            q_scr[hc] = q[:, hc * PEER_HALF:(hc + 1) * PEER_HALF]

        def head_sort(h, carry):
            for c in range(2):
                q_hi, q_lo = _split_bf16(q_scr[2 * h + c])
                k_hi, k_lo = _split_bf16(sk_ref[2 * h + c])
                st = _dot_nt(k_hi, q_hi) + _dot_nt(k_hi, q_lo) + _dot_nt(k_lo, q_hi)
                pr_scr[c, h] = st
                srt = _top16_sorted(st)
                for a in range(k):
                    srt_scr[c, a, pl.ds(h, 1), :] = srt[a][0:1, :]
            return carry
        lax.fori_loop(0, PEER_HEADS, head_sort, 0)

        a_s = [srt_scr[0, a] for a in range(k)]
        b_s = [srt_scr[1, a] for a in range(k)]
        cands = [a_s[a] + b_s[b] for a in range(k) for b in range(k) if (a + 1) * (b + 1) <= k]
        cands += [jnp.full_like(cands[0], -jnp.inf)] * (64 - len(cands))
        cs = _bitonic_sort_desc(cands)
        zsum = jnp.zeros_like(cs[0])
        for a in range(k):
            zsum = zsum + jnp.exp(cs[a] - cs[0])
        row_scr[0] = 0.5 * (cs[k - 1] + cs[k])
        row_scr[1] = 1.0 / zsum
        row_scr[2] = a_s[0]
        row_scr[3] = b_s[0]
        row_scr[4] = a_s[k - 1]
        row_scr[5] = b_s[k - 1]

        def head_prep(h, carry):
            hrow = pl.ds(h, 1)
            at = pr_scr[0, h]
            bt = pr_scr[1, h]
            thr = row_scr[0, hrow, :] - at
            r_cnt = jnp.zeros_like(at)
            b_rank = jnp.zeros_like(bt)
            for a in range(k):
                b_a = srt_scr[1, a, hrow, :]
                r_cnt = jnp.where(b_a > thr, float(a + 1), r_cnt)
                b_rank = jnp.where(b_a > bt, float(a + 1), b_rank)
            in_a = at >= row_scr[4, hrow, :]
            p_w = jnp.where(in_a, jnp.exp(at - row_scr[2, hrow, :]), 0.0) * row_scr[1, hrow, :]
            pr_scr[0, h] = _dup_bf16(p_w)
            pr_scr[1, h] = _dup_bf16(jnp.where(in_a, r_cnt, 0.0))
            q_w = jnp.where(bt >= row_scr[5, hrow, :], jnp.exp(bt - row_scr[3, hrow, :]), 0.0)
            rq_scr[0, h] = b_rank.astype(bf16).reshape(nk // BF16_ROWS, BF16_ROWS, tt)
            rq_scr[1, h] = q_w.astype(bf16).reshape(nk // BF16_ROWS, BF16_ROWS, tt)
            return carry
        lax.fori_loop(0, PEER_HEADS, head_prep, 0)
        acc_scr[...] = jnp.zeros_like(acc_scr)

    activations()
    for s in range(n_chunk):
        halves = []
        for half in range(chunk // nk):
            n1 = pl.ds((EB_PEER // nk) * eb + s * (chunk // nk) + half, 1)
            g = jnp.zeros((nk // BF16_ROWS, BF16_ROWS, tt), bf16)
            for h in range(PEER_HEADS):
                p_row = _bf16_rows(pr_scr[0, h, n1, :])
                r_row = _bf16_rows(pr_scr[1, h, n1, :])
                g = g + p_row * jnp.where(rq_scr[0, h] < r_row, rq_scr[1, h], 0)
            halves.append(g.reshape(nk, tt))
        act = a_scr[s].astype(bf16) * unscale_scr[0].astype(bf16)
        w = jnp.concatenate(halves, axis=0) * jax.nn.gelu(act)
        acc_scr[...] += _dot(vt_ref[:, s * chunk:(s + 1) * chunk], w)

    @pl.when(eb == n_blk - 1)
    def _last():
        r = DN_ALPHA * h_ref[...] + acc_scr[...].T
        o_ref[...] = _layernorm(r, g_ref[...], b_ref[...])


def _peer(h1, wq_b, sub_keys, u_q, u_unscale, vt_b, g, b):
    u_b = u_q
    n, d_model = h1.shape
    n_exp = u_b.shape[0]
    tt = TT_PEER
    sk = sub_keys.reshape(2 * PEER_HEADS, PEER_N_KEYS, PEER_HALF).astype(f32)
    n_blk = n_exp // EB_PEER
    const = dict(pipeline_mode=pl.Buffered(1))
    return pl.pallas_call(
        functools.partial(_peer_kernel, n_exp),
        grid=(n // tt, n_blk),
        in_specs=[pl.BlockSpec(memory_space=pltpu.SMEM),
                  pl.BlockSpec((tt, d_model), lambda t, e: (t, 0)),
                  pl.BlockSpec(wq_b.shape, lambda t, e: (0, 0), **const),
                  pl.BlockSpec(sk.shape, lambda t, e: (0, 0, 0), **const),
                  pl.BlockSpec((EB_PEER, d_model), lambda t, e: (e, 0)),
                  pl.BlockSpec((d_model, EB_PEER), lambda t, e: (0, e)),
                  pl.BlockSpec((1, d_model), lambda t, e: (0, 0), **const),
                  pl.BlockSpec((1, d_model), lambda t, e: (0, 0), **const)],
        out_specs=pl.BlockSpec((tt, d_model), lambda t, e: (t, 0)),
        out_shape=jax.ShapeDtypeStruct((n, d_model), f32),
        scratch_shapes=[pltpu.VMEM((2 * PEER_HEADS, tt, PEER_HALF), f32),
                        pltpu.VMEM((2, PEER_HEADS, PEER_N_KEYS, tt), f32),
                        pltpu.VMEM((2, PEER_HEADS, PEER_N_KEYS // BF16_ROWS, BF16_ROWS, tt), bf16),
                        pltpu.VMEM((2, PEER_TOPK, PEER_HEADS, tt), f32),
                        pltpu.VMEM((6, PEER_HEADS, tt), f32),
                        pltpu.VMEM((d_model, tt), f32),
                        pltpu.VMEM((d_model, tt), fp8),
                        pltpu.VMEM((EB_PEER // (2 * PEER_N_KEYS), 2 * PEER_N_KEYS, tt), f32),
                        pltpu.SMEM((1,), f32)],
        compiler_params=pltpu.CompilerParams(dimension_semantics=("parallel", "arbitrary"),
                                             vmem_limit_bytes=VMEM_LIMIT),
        name="peer",
    )(u_unscale.reshape(1, 1), h1, wq_b, sk, u_b, vt_b, g, b)


def _quantize_table(t):
    amax = jnp.maximum(jnp.max(jnp.abs(t)), jnp.finfo(f32).tiny)
    e = jnp.minimum(jnp.floor(jnp.log2(FP8_TARGET_MAX / amax)), FP8_MAX_SHIFT)
    return (t * jnp.exp2(e)).astype(fp8), jnp.exp2(-e).astype(f32)


def kernel(x, w_in, ssm_a_re, ssm_a_im, ssm_log_dt, ssm_b_re, ssm_b_im, ssm_c_re, ssm_c_im, ssm_d,
           ssm_w_glu, w_out, ln1_g, ln1_b, peer_w_q, peer_sub_keys, peer_u, peer_v, ln2_g, ln2_b):
    bsz, seq, d_model = x.shape
    n = bsz * seq
    h = x.reshape(n, d_model)
    for l in range(w_in.shape[0]):
        u2, k2, qt, vt, kmean = _inproj(h, w_in[l])
        ops = _s5_operators(ssm_a_re[l], ssm_a_im[l], ssm_log_dt[l], ssm_b_re[l], ssm_b_im[l],
                            ssm_c_re[l], ssm_c_im[l], ssm_d[l])
        yg = _s5(u2, bsz, seq, ops)
        att = _moba(qt, k2, vt, kmean.reshape(bsz, seq // MOBA_BLOCK, D_ATT), bsz, seq)
        h1 = _outproj(h, yg, att, ssm_w_glu[l].astype(bf16), w_out[l].astype(bf16),
                      ln1_g[l].reshape(1, d_model), ln1_b[l].reshape(1, d_model))
        u_q, u_unscale = _quantize_table(peer_u[l])
        h = _peer(h1, peer_w_q[l].astype(bf16), peer_sub_keys[l], u_q, u_unscale,
                  peer_v[l].T.astype(bf16), ln2_g[l].reshape(1, d_model), ln2_b[l].reshape(1, d_model))
    return h.reshape(bsz, seq, d_model)
```

```python
import functools
import math

import jax
import jax.numpy as jnp
from jax import lax
from jax.experimental import pallas as pl
from jax.experimental.pallas import tpu as pltpu

f32 = jnp.float32
bf16 = jnp.bfloat16
fp8 = jnp.float8_e4m3fn
FP8_TARGET_MAX = 256.0
FP8_MAX_SHIFT = 48.0

D_SSM = 512
SSM_GROUP = 16
SSM_STATE = 64
N_ATT_HEADS = 8
ATT_HEAD_DIM = 64
D_ATT = N_ATT_HEADS * ATT_HEAD_DIM
MOBA_BLOCK = 256
MOBA_TOPK = 3
PEER_HEADS = 8
PEER_HALF = 128
PEER_N_KEYS = 128
PEER_TOPK = 16
LN_EPS = 1e-5
DEPTH = 1
DN_ALPHA = (2.0 * DEPTH) ** 0.25
NEG = -1e30

LANES = 128
SUBLANES = 8
BF16_ROWS = 2 * SUBLANES
VMEM_LIMIT = 48 * 1024 * 1024

TM_PROJ = 1024
S5_CHUNK = 8
S5_LANE_GROUPS = LANES // SSM_GROUP
TT_PEER = 512
EB_PEER = 2048


def _dot(a, b):
    return jnp.dot(a, b, preferred_element_type=f32)


def _dot_nt(a, b):
    return lax.dot_general(a, b, (((1,), (1,)), ((), ())), preferred_element_type=f32)


def _split_bf16(x):
    hi = x.astype(bf16)
    lo = (x - hi.astype(f32)).astype(bf16)
    return hi, lo


def _layernorm(r, g, b):
    mu = jnp.mean(r, axis=-1, keepdims=True)
    d = r - mu
    var = jnp.mean(d * d, axis=-1, keepdims=True)
    return d * lax.rsqrt(var + LN_EPS) * g + b


def _inproj_kernel(x_ref, wuk_ref, wqvt_ref, u_ref, k_ref, qt_ref, vt_ref, km_ref):
    xb = x_ref[...].astype(bf16)
    acc = _dot(xb, wuk_ref[...])
    u_ref[...] = acc[:, :D_SSM]
    kk = acc[:, D_SSM:]
    k_ref[...] = kk.astype(bf16)
    qvt = _dot_nt(wqvt_ref[...], xb)
    for r in range(TM_PROJ // MOBA_BLOCK):
        cols = slice(r * MOBA_BLOCK, (r + 1) * MOBA_BLOCK)
        km_ref[r] = jnp.mean(kk[cols], axis=0, keepdims=True)
        qt_ref[r] = qvt[:D_ATT, cols].astype(bf16)
        vt_ref[r] = qvt[D_ATT:, cols].astype(bf16)


def _inproj(x2, w_in):
    n, d_model = x2.shape
    nb = TM_PROJ // MOBA_BLOCK
    w_uk = jnp.concatenate([w_in[:, :D_SSM], w_in[:, D_SSM + D_ATT:D_SSM + 2 * D_ATT]], axis=1).astype(bf16)
    w_qvt = jnp.concatenate([w_in[:, D_SSM:D_SSM + D_ATT], w_in[:, D_SSM + 2 * D_ATT:]], axis=1).T.astype(bf16)
    return pl.pallas_call(
        _inproj_kernel,
        grid=(n // TM_PROJ,),
        in_specs=[pl.BlockSpec((TM_PROJ, d_model), lambda i: (i, 0)),
                  pl.BlockSpec(w_uk.shape, lambda i: (0, 0)),
                  pl.BlockSpec(w_qvt.shape, lambda i: (0, 0))],
        out_specs=[pl.BlockSpec((TM_PROJ, D_SSM), lambda i: (i, 0)),
                   pl.BlockSpec((TM_PROJ, D_ATT), lambda i: (i, 0)),
                   pl.BlockSpec((nb, D_ATT, MOBA_BLOCK), lambda i: (i, 0, 0)),
                   pl.BlockSpec((nb, D_ATT, MOBA_BLOCK), lambda i: (i, 0, 0)),
                   pl.BlockSpec((nb, 1, D_ATT), lambda i: (i, 0, 0))],
        out_shape=[jax.ShapeDtypeStruct((n, D_SSM), f32),
                   jax.ShapeDtypeStruct((n, D_ATT), bf16),
                   jax.ShapeDtypeStruct((n // MOBA_BLOCK, D_ATT, MOBA_BLOCK), bf16),
                   jax.ShapeDtypeStruct((n // MOBA_BLOCK, D_ATT, MOBA_BLOCK), bf16),
                   jax.ShapeDtypeStruct((n // MOBA_BLOCK, 1, D_ATT), f32)],
        compiler_params=pltpu.CompilerParams(dimension_semantics=("parallel",),
                                             vmem_limit_bytes=VMEM_LIMIT),
        name="inproj",
    )(x2, w_uk, w_qvt)


def _s5_operators(a_re, a_im, log_dt, b_re, b_im, c_re, c_im, d_skip):
    T = S5_CHUNK
    G, P = a_re.shape
    H = SSM_GROUP
    nj = G // S5_LANE_GROUPS
    g8 = S5_LANE_GROUPS
    lam = lax.complex(a_re.astype(f32), a_im.astype(f32))
    dt = jnp.exp(log_dt.astype(f32))[:, None]
    lam_bar = jnp.exp(lam * dt)
    b_c = lax.complex(b_re.astype(f32), b_im.astype(f32))
    b_bar = ((lam_bar - 1.0) / lam)[:, :, None] * b_c
    c_c = lax.complex(c_re.astype(f32), c_im.astype(f32))
    pows = jnp.exp((lam * dt)[None] * jnp.arange(T + 1, dtype=f32)[:, None, None])
    def widen(x, reps):
        n = x.shape[-1]
        rep = jnp.tile(jnp.eye(n, dtype=f32), (1, reps))
        return jnp.dot(x.reshape(-1, n), rep).reshape(x.shape[:-1] + (reps * n,))

    same = jnp.arange(g8)[:, None] == (jnp.arange(g8 * H) // H)[None, :]
    same_p = jnp.arange(g8)[:, None] == (jnp.arange(g8 * P) // P)[None, :]
    kern = jnp.real(jnp.einsum('ghp,dgp,gpk->dgkh', c_c, pows[:T], b_bar))
    dd = jnp.arange(T)[None, :] - jnp.arange(T)[:, None]
    kst = jnp.where((dd >= 0)[:, :, None, None, None], kern[jnp.clip(dd, 0, T - 1)], 0.0)
    kst = widen(kst.reshape(T, T, nj, g8, H, H), g8)
    kst = jnp.where(same[None, None, None, :, None, :], kst, 0.0)
    toep = kst.transpose(2, 0, 3, 4, 1, 5).reshape(nj, T * LANES, T * LANES)
    win = pows[T - 1 - jnp.arange(T)][:, :, None, :] * jnp.swapaxes(b_bar, 1, 2)[None]
    win = jnp.stack([jnp.real(win), jnp.imag(win)], 0).reshape(2, T, nj, g8, H, P)
    win = jnp.where(same_p[None, None, None, :, None, :], widen(win, g8), 0.0)
    m_in = win.transpose(2, 1, 3, 4, 0, 5).reshape(nj, T * LANES, 2 * g8 * P)
    wout = jnp.swapaxes(c_c, 1, 2)[None] * pows[1:T + 1][:, :, :, None]
    wout = jnp.stack([jnp.real(wout), -jnp.imag(wout)], 0).reshape(2, T, nj, g8, P, H)
    wout = jnp.where(same[None, None, None, :, None, :], widen(wout, g8), 0.0)
    m_out = wout.transpose(2, 0, 3, 4, 1, 5).reshape(nj, 2 * g8 * P, T * LANES)
    a_chunk = pows[T].reshape(nj, 1, g8 * P)
    a_ri = jnp.concatenate([jnp.real(a_chunk), jnp.imag(a_chunk)], axis=1)
    d_flat = jnp.tile(d_skip.astype(f32).reshape(nj, 1, LANES), (1, 1, T))
    return toep.astype(bf16), m_in.astype(bf16), m_out.astype(bf16), a_ri, d_flat


def _s5_kernel(u_ref, toep_ref, min_ref, mout_ref, a_ref, d_ref, y_ref, in_scr, sp_scr):
    T = S5_CHUNK
    n_c = u_ref.shape[0] // T
    half = in_scr.shape[1] // 2
    uf = jnp.concatenate([u_ref[pl.ds(t, n_c, stride=T), :] for t in range(T)], axis=1)
    ufb = uf.astype(bf16)
    y = _dot(ufb, toep_ref[0]) + uf * d_ref[0]
    in_scr[...] = _dot(ufb, min_ref[0])
    ar = a_ref[0, 0:1, :]
    ai = a_ref[0, 1:2, :]

    def step(c, carry):
        sr, si = carry
        sp_scr[pl.ds(c, 1), :half] = sr
        sp_scr[pl.ds(c, 1), half:] = si
        ir = in_scr[pl.ds(c, 1), :half]
        ii = in_scr[pl.ds(c, 1), half:]
        return ar * sr - ai * si + ir, ar * si + ai * sr + ii

    z = jnp.zeros((1, half), f32)
    lax.fori_loop(0, n_c, step, (z, z), unroll=4)
    y = y + _dot(sp_scr[...].astype(bf16), mout_ref[0])
    yg = jax.nn.gelu(y)
    for t in range(T):
        y_ref[pl.ds(t, n_c, stride=T), :] = yg[:, t * LANES:(t + 1) * LANES]


def _s5(u2, bsz, seq, ops):
    toep, m_in, m_out, a_ri, d_flat = ops
    nj = toep.shape[0]
    n_c = seq // S5_CHUNK
    kdim = S5_CHUNK * LANES
    sdim = m_in.shape[2]
    return pl.pallas_call(
        _s5_kernel,
        grid=(nj, bsz),
        in_specs=[pl.BlockSpec((seq, LANES), lambda j, b: (b, j)),
                  pl.BlockSpec((1, kdim, kdim), lambda j, b: (j, 0, 0)),
                  pl.BlockSpec((1, kdim, sdim), lambda j, b: (j, 0, 0)),
                  pl.BlockSpec((1, sdim, kdim), lambda j, b: (j, 0, 0)),
                  pl.BlockSpec((1, 2, sdim // 2), lambda j, b: (j, 0, 0)),
                  pl.BlockSpec((1, 1, kdim), lambda j, b: (j, 0, 0))],
        out_specs=pl.BlockSpec((seq, LANES), lambda j, b: (b, j)),
        out_shape=jax.ShapeDtypeStruct(u2.shape, f32),
        scratch_shapes=[pltpu.VMEM((n_c, sdim), f32), pltpu.VMEM((n_c, sdim), f32)],
        compiler_params=pltpu.CompilerParams(dimension_semantics=("parallel", "parallel"),
                                             vmem_limit_bytes=VMEM_LIMIT),
        name="s5",
    )(u2, toep, m_in, m_out, a_ri, d_flat)


def _moba_kernel(qt_ref, k_ref, vt_ref, km_ref, o_ref, kaug_ref, s0_scr, s1_scr):
    blk = MOBA_BLOCK
    hd = ATT_HEAD_DIM
    nb = k_ref.shape[0] // blk
    i = pl.program_id(2)

    @pl.when(i == 0)
    def _build():
        lane = lax.broadcasted_iota(jnp.int32, (blk, LANES), 1)

        def build(j, carry):
            rows = pl.ds(pl.multiple_of(j * blk, blk), blk)
            kaug_ref[rows, :LANES] = k_ref[rows, :]
            kaug_ref[rows, LANES:] = jnp.where(lane == j, 1.0, 0.0).astype(bf16)
            return carry
        lax.fori_loop(0, nb, build, 0)

    qt = qt_ref[0]
    qtf = qt.astype(f32) * (hd ** -0.5 * math.log2(math.e))
    km = km_ref[0]
    lane_nb = lax.broadcasted_iota(jnp.int32, (nb, LANES), 1)
    feat = lax.broadcasted_iota(jnp.int32, (LANES, blk), 0)
    blk_id = lax.broadcasted_iota(jnp.int32, (nb, blk), 0)
    blk_f = blk_id.astype(f32)
    q_cols, bias_cols = [], []
    for hh in range(2):
        km_hi, km_lo = _split_bf16(jnp.where((lane_nb < hd) if hh == 0 else (lane_nb >= hd), km, 0.0))
        gate = _dot(km_hi, qt) + _dot(km_lo, qt)
        g = jnp.where(blk_id < i, gate, -jnp.inf)
        sel = blk_id == i
        for _ in range(MOBA_TOPK):
            mx = jnp.max(g, axis=0, keepdims=True)
            first = jnp.min(jnp.where(g == mx, blk_f, float(nb)), axis=0, keepdims=True)
            pick = (blk_f == first) & (mx > -jnp.inf)
            sel = sel | pick
            g = jnp.where(pick, -jnp.inf, g)
        bias_cols.append(jnp.where(sel, 0.0, NEG))
        q_cols.append(jnp.where((feat < hd) if hh == 0 else (feat >= hd), qtf, 0.0))
    qa = jnp.concatenate([jnp.concatenate(q_cols, axis=1), jnp.concatenate(bias_cols, axis=1),
                          jnp.zeros((LANES - nb, 2 * blk), f32)], axis=0).astype(bf16)

    key_pos = lax.broadcasted_iota(jnp.int32, (blk, 2 * blk), 0)
    q_pos = lax.broadcasted_iota(jnp.int32, (blk, 2 * blk), 1) % blk

    def scores(j):
        j = jnp.minimum(j, nb - 1)
        return _dot(kaug_ref[pl.ds(pl.multiple_of(j * blk, blk), blk), :], qa)

    def update(state, blocks):
        m, l, acc = state
        m_new = m
        for _, s in blocks:
            m_new = jnp.maximum(m_new, jnp.max(s, axis=0, keepdims=True))
        alpha = jnp.exp2(m - m_new)
        l = alpha * l
        acc = alpha * acc
        for j, s in blocks:
            p = jnp.exp2(s - m_new)
            l = l + jnp.sum(p, axis=0, keepdims=True)
            pb = p.astype(bf16)
            acc = acc + jnp.concatenate([_dot(vt_ref[j, :hd, :], pb[:, :blk]),
                                         _dot(vt_ref[j, hd:, :], pb[:, blk:])], axis=1)
        return m_new, l, acc

    def pair_scores(p, dst):
        dst[0] = scores(2 * p)
        dst[1] = scores(2 * p + 1)

    def pair_update(state, p, src, masked):
        blocks = []
        for r in range(2):
            j = 2 * p + r
            s = src[r]
            if masked:
                off = jnp.where(j < i, blk, jnp.where(j == i, 0, -2 * blk))
                s = jnp.where(key_pos <= q_pos + off, s, NEG)
            blocks.append((jnp.minimum(j, nb - 1), s))
        return update(state, blocks)

    def quad_body(q, state):
        pair_scores(2 * q + 1, s1_scr)
        state = pair_update(state, 2 * q, s0_scr, False)
        pair_scores(2 * q + 2, s0_scr)
        return pair_update(state, 2 * q + 1, s1_scr, False)

    state = (jnp.full((1, 2 * blk), -jnp.inf, f32), jnp.zeros((1, 2 * blk), f32), jnp.zeros((hd, 2 * blk), f32))
    n_quad = i // 4
    pair_scores(0, s0_scr)
    state = lax.fori_loop(0, n_quad, quad_body, state)
    pair_scores(2 * n_quad + 1, s1_scr)
    state = pair_update(state, 2 * n_quad, s0_scr, True)
    m, l, acc = pair_update(state, 2 * n_quad + 1, s1_scr, True)
    out = acc / l
    o_ref[...] = jnp.concatenate([out[:, :blk], out[:, blk:]], axis=0).T.astype(o_ref.dtype)


def _moba(qt, k2, vt, kmean, bsz, seq):
    nb = seq // MOBA_BLOCK
    assert seq % MOBA_BLOCK == 0 and nb % SUBLANES == 0 and nb <= LANES
    npair = D_ATT // LANES
    return pl.pallas_call(
        _moba_kernel,
        grid=(bsz, npair, nb),
        in_specs=[pl.BlockSpec((1, LANES, MOBA_BLOCK), lambda b, h, i: (b * nb + i, h, 0)),
                  pl.BlockSpec((seq, LANES), lambda b, h, i: (b, h)),
                  pl.BlockSpec((nb, LANES, MOBA_BLOCK), lambda b, h, i: (b, h, 0)),
                  pl.BlockSpec((1, nb, LANES), lambda b, h, i: (b, 0, h))],
        out_specs=pl.BlockSpec((MOBA_BLOCK, LANES), lambda b, h, i: (b * nb + i, h)),
        out_shape=jax.ShapeDtypeStruct(k2.shape, bf16),
        scratch_shapes=[pltpu.VMEM((seq, 2 * LANES), bf16),
                        pltpu.VMEM((2, MOBA_BLOCK, 2 * MOBA_BLOCK), f32),
                        pltpu.VMEM((2, MOBA_BLOCK, 2 * MOBA_BLOCK), f32)],
        compiler_params=pltpu.CompilerParams(dimension_semantics=("parallel", "parallel", "arbitrary"),
                                             vmem_limit_bytes=VMEM_LIMIT),
        name="moba",
    )(qt, k2, vt, kmean)


def _outproj_kernel(x_ref, yg_ref, att_ref, wglu_ref, wout_ref, g_ref, b_ref, h_ref):
    yg = yg_ref[...]
    z = yg * jax.nn.sigmoid(_dot(yg.astype(bf16), wglu_ref[...]))
    cat = jnp.concatenate([z.astype(bf16), att_ref[...]], axis=1)
    r = DN_ALPHA * x_ref[...] + _dot(cat, wout_ref[...])
    h_ref[...] = _layernorm(r, g_ref[...], b_ref[...])


def _outproj(x2, yg, att, wglu_b, wout_b, g, b):
    n, d_model = x2.shape
    return pl.pallas_call(
        _outproj_kernel,
        grid=(n // TM_PROJ,),
        in_specs=[pl.BlockSpec((TM_PROJ, d_model), lambda i: (i, 0)),
                  pl.BlockSpec((TM_PROJ, D_SSM), lambda i: (i, 0)),
                  pl.BlockSpec((TM_PROJ, D_ATT), lambda i: (i, 0)),
                  pl.BlockSpec(wglu_b.shape, lambda i: (0, 0)),
                  pl.BlockSpec(wout_b.shape, lambda i: (0, 0)),
                  pl.BlockSpec((1, d_model), lambda i: (0, 0)),
                  pl.BlockSpec((1, d_model), lambda i: (0, 0))],
        out_specs=pl.BlockSpec((TM_PROJ, d_model), lambda i: (i, 0)),
        out_shape=jax.ShapeDtypeStruct((n, d_model), f32),
        compiler_params=pltpu.CompilerParams(dimension_semantics=("parallel",),
                                             vmem_limit_bytes=VMEM_LIMIT),
        name="outproj",
    )(x2, yg, att, wglu_b, wout_b, g, b)


def _bitonic_sort_desc(v):
    n = len(v)
    k = 2
    while k <= n:
        j = k // 2
        while j >= 1:
            for a in range(n):
                b = a ^ j
                if b > a:
                    hi = jnp.maximum(v[a], v[b])
                    lo = jnp.minimum(v[a], v[b])
                    v[a], v[b] = (hi, lo) if (a & k) == 0 else (lo, hi)
            j //= 2
        k *= 2
    return v


def _bitonic_merge_desc(v):
    j = len(v) // 2
    while j >= 1:
        for a in range(len(v)):
            b = a ^ j
            if b > a:
                v[a], v[b] = jnp.maximum(v[a], v[b]), jnp.minimum(v[a], v[b])
        j //= 2
    return v


def _top16_sorted(st):
    k = PEER_TOPK
    x = _bitonic_sort_desc([st[SUBLANES * a:SUBLANES * (a + 1), :] for a in range(PEER_N_KEYS // SUBLANES)])
    for shift in (4, 2, 1):
        y = [pltpu.roll(x[k - 1 - a], shift, 0) for a in range(k)]
        x = _bitonic_merge_desc([jnp.maximum(x[a], y[a]) for a in range(k)])
    return x


def _dup_bf16(x):
    u = pltpu.bitcast(x.astype(bf16).astype(f32), jnp.uint32)
    return pltpu.bitcast(u | (u >> 16), f32)


def _bf16_rows(row):
    return pltpu.bitcast(jnp.broadcast_to(row, (SUBLANES, row.shape[1])), bf16)[None]


def _peer_kernel(n_exp, us_ref, h_ref, wq_ref, sk_ref, u_ref, vt_ref, g_ref, b_ref, o_ref,
                 q_scr, pr_scr, rq_scr, srt_scr, row_scr, acc_scr, hbt_scr, a_scr, unscale_scr):
    eb = pl.program_id(1)
    n_blk = n_exp // EB_PEER
    tt = h_ref.shape[0]
    nk = PEER_N_KEYS
    k = PEER_TOPK

    chunk = 2 * nk
    n_chunk = EB_PEER // chunk

    def activations():
        for s in range(n_chunk):
            a_scr[s] = _dot(u_ref[s * chunk:(s + 1) * chunk, :], hbt_scr[...])

    @pl.when(eb == 0)
    def _prologue():
        ht = h_ref[...].T
        amax = jnp.maximum(jnp.max(jnp.abs(ht), keepdims=True), jnp.finfo(f32).tiny)
        e_tile = jnp.minimum(jnp.floor(jnp.log2(FP8_TARGET_MAX / amax)), FP8_MAX_SHIFT)
        hbt_scr[...] = (ht * jnp.exp2(e_tile)).astype(fp8)
        unscale_scr[0] = (jnp.exp2(-e_tile) * us_ref[0, 0])[0, 0]
        q = _dot(h_ref[...].astype(bf16), wq_ref[...])
        for hc in range(2 * PEER_HEADS):
            q_scr[hc] = q[:, hc * PEER_HALF:(hc + 1) * PEER_HALF]

        def head_sort(h, carry):
            for c in range(2):
                q_hi, q_lo = _split_bf16(q_scr[2 * h + c])
                k_hi, k_lo = _split_bf16(sk_ref[2 * h + c])
                st = _dot_nt(k_hi, q_hi) + _dot_nt(k_hi, q_lo) + _dot_nt(k_lo, q_hi)
                pr_scr[c, h] = st
                srt = _top16_sorted(st)
                for a in range(k):
                    srt_scr[c, a, pl.ds(h, 1), :] = srt[a][0:1, :]
            return carry
        lax.fori_loop(0, PEER_HEADS, head_sort, 0)

        a_s = [srt_scr[0, a] for a in range(k)]
        b_s = [srt_scr[1, a] for a in range(k)]
        cands = [a_s[a] + b_s[b] for a in range(k) for b in range(k) if (a + 1) * (b + 1) <= k]
        cands += [jnp.full_like(cands[0], -jnp.inf)] * (64 - len(cands))
        cs = _bitonic_sort_desc(cands)
        zsum = jnp.zeros_like(cs[0])
        for a in range(k):
            zsum = zsum + jnp.exp(cs[a] - cs[0])
        row_scr[0] = 0.5 * (cs[k - 1] + cs[k])
        row_scr[1] = 1.0 / zsum
        row_scr[2] = a_s[0]
        row_scr[3] = b_s[0]
        row_scr[4] = a_s[k - 1]
        row_scr[5] = b_s[k - 1]

        def head_prep(h, carry):
            hrow = pl.ds(h, 1)
            at = pr_scr[0, h]
            bt = pr_scr[1, h]
            thr = row_scr[0, hrow, :] - at
            r_cnt = jnp.zeros_like(at)
            b_rank = jnp.zeros_like(bt)
            for a in range(k):
                b_a = srt_scr[1, a, hrow, :]
                r_cnt = jnp.where(b_a > thr, float(a + 1), r_cnt)
                b_rank = jnp.where(b_a > bt, float(a + 1), b_rank)
            in_a = at >= row_scr[4, hrow, :]
            p_w = jnp.where(in_a, jnp.exp(at - row_scr[2, hrow, :]), 0.0) * row_scr[1, hrow, :]
            pr_scr[0, h] = _dup_bf16(p_w)
            pr_scr[1, h] = _dup_bf16(jnp.where(in_a, r_cnt, 0.0))
            q_w = jnp.where(bt >= row_scr[5, hrow, :], jnp.exp(bt - row_scr[3, hrow, :]), 0.0)
            rq_scr[0, h] = b_rank.astype(bf16).reshape(nk // BF16_ROWS, BF16_ROWS, tt)
            rq_scr[1, h] = q_w.astype(bf16).reshape(nk // BF16_ROWS, BF16_ROWS, tt)
            return carry
        lax.fori_loop(0, PEER_HEADS, head_prep, 0)
        acc_scr[...] = jnp.zeros_like(acc_scr)

    activations()
    for s in range(n_chunk):
        halves = []
        for half in range(chunk // nk):
            n1 = pl.ds((EB_PEER // nk) * eb + s * (chunk // nk) + half, 1)
            g = jnp.zeros((nk // BF16_ROWS, BF16_ROWS, tt), bf16)
            for h in range(PEER_HEADS):
                p_row = _bf16_rows(pr_scr[0, h, n1, :])
                r_row = _bf16_rows(pr_scr[1, h, n1, :])
                g = g + p_row * jnp.where(rq_scr[0, h] < r_row, rq_scr[1, h], 0)
            halves.append(g.reshape(nk, tt))
        act = a_scr[s].astype(bf16) * unscale_scr[0].astype(bf16)
        w = jnp.concatenate(halves, axis=0) * jax.nn.gelu(act)
        acc_scr[...] += _dot(vt_ref[:, s * chunk:(s + 1) * chunk], w)

    @pl.when(eb == n_blk - 1)
    def _last():
        r = DN_ALPHA * h_ref[...] + acc_scr[...].T
        o_ref[...] = _layernorm(r, g_ref[...], b_ref[...])


def _peer(h1, wq_b, sub_keys, u_q, u_unscale, vt_b, g, b):
    u_b = u_q
    n, d_model = h1.shape
    n_exp = u_b.shape[0]
    tt = TT_PEER
    sk = sub_keys.reshape(2 * PEER_HEADS, PEER_N_KEYS, PEER_HALF).astype(f32)
    n_blk = n_exp // EB_PEER
    const = dict(pipeline_mode=pl.Buffered(1))
    return pl.pallas_call(
        functools.partial(_peer_kernel, n_exp),
        grid=(n // tt, n_blk),
        in_specs=[pl.BlockSpec(memory_space=pltpu.SMEM),
                  pl.BlockSpec((tt, d_model), lambda t, e: (t, 0)),
                  pl.BlockSpec(wq_b.shape, lambda t, e: (0, 0), **const),
                  pl.BlockSpec(sk.shape, lambda t, e: (0, 0, 0), **const),
                  pl.BlockSpec((EB_PEER, d_model), lambda t, e: (e, 0)),
                  pl.BlockSpec((d_model, EB_PEER), lambda t, e: (0, e)),
                  pl.BlockSpec((1, d_model), lambda t, e: (0, 0), **const),
                  pl.BlockSpec((1, d_model), lambda t, e: (0, 0), **const)],
        out_specs=pl.BlockSpec((tt, d_model), lambda t, e: (t, 0)),
        out_shape=jax.ShapeDtypeStruct((n, d_model), f32),
        scratch_shapes=[pltpu.VMEM((2 * PEER_HEADS, tt, PEER_HALF), f32),
                        pltpu.VMEM((2, PEER_HEADS, PEER_N_KEYS, tt), f32),
                        pltpu.VMEM((2, PEER_HEADS, PEER_N_KEYS // BF16_ROWS, BF16_ROWS, tt), bf16),
                        pltpu.VMEM((2, PEER_TOPK, PEER_HEADS, tt), f32),
                        pltpu.VMEM((6, PEER_HEADS, tt), f32),
                        pltpu.VMEM((d_model, tt), f32),
                        pltpu.VMEM((d_model, tt), fp8),
                        pltpu.VMEM((EB_PEER // (2 * PEER_N_KEYS), 2 * PEER_N_KEYS, tt), f32),
                        pltpu.SMEM((1,), f32)],
        compiler_params=pltpu.CompilerParams(dimension_semantics=("parallel", "arbitrary"),
                                             vmem_limit_bytes=VMEM_LIMIT),
        name="peer",
    )(u_unscale.reshape(1, 1), h1, wq_b, sk, u_b, vt_b, g, b)


def _quantize_table(t):
    amax = jnp.maximum(jnp.max(jnp.abs(t)), jnp.finfo(f32).tiny)
    e = jnp.minimum(jnp.floor(jnp.log2(FP8_TARGET_MAX / amax)), FP8_MAX_SHIFT)
    return (t * jnp.exp2(e)).astype(fp8), jnp.exp2(-e).astype(f32)


def kernel(x, w_in, ssm_a_re, ssm_a_im, ssm_log_dt, ssm_b_re, ssm_b_im, ssm_c_re, ssm_c_im, ssm_d,
           ssm_w_glu, w_out, ln1_g, ln1_b, peer_w_q, peer_sub_keys, peer_u, peer_v, ln2_g, ln2_b):
    bsz, seq, d_model = x.shape
    n = bsz * seq
    h = x.reshape(n, d_model)
    for l in range(w_in.shape[0]):
        u2, k2, qt, vt, kmean = _inproj(h, w_in[l])
        ops = _s5_operators(ssm_a_re[l], ssm_a_im[l], ssm_log_dt[l], ssm_b_re[l], ssm_b_im[l],
                            ssm_c_re[l], ssm_c_im[l], ssm_d[l])
        yg = _s5(u2, bsz, seq, ops)
        att = _moba(qt, k2, vt, kmean.reshape(bsz, seq // MOBA_BLOCK, D_ATT), bsz, seq)
        h1 = _outproj(h, yg, att, ssm_w_glu[l].astype(bf16), w_out[l].astype(bf16),
                      ln1_g[l].reshape(1, d_model), ln1_b[l].reshape(1, d_model))
        u_q, u_unscale = _quantize_table(peer_u[l])
        h = _peer(h1, peer_w_q[l].astype(bf16), peer_sub_keys[l], u_q, u_unscale,
                  peer_v[l].T.astype(bf16), ln2_g[l].reshape(1, d_model), ln2_b[l].reshape(1, d_model))
    return h.reshape(bsz, seq, d_model)
```

```python
import functools
import math

import jax
import jax.numpy as jnp
from jax import lax
from jax.experimental import pallas as pl
from jax.experimental.pallas import tpu as pltpu

f32 = jnp.float32
bf16 = jnp.bfloat16
fp8 = jnp.float8_e4m3fn
FP8_TARGET_MAX = 256.0
FP8_MAX_SHIFT = 48.0

D_SSM = 512
SSM_GROUP = 16
SSM_STATE = 64
N_ATT_HEADS = 8
ATT_HEAD_DIM = 64
D_ATT = N_ATT_HEADS * ATT_HEAD_DIM
MOBA_BLOCK = 256
MOBA_TOPK = 3
PEER_HEADS = 8
PEER_HALF = 128
PEER_N_KEYS = 128
PEER_TOPK = 16
LN_EPS = 1e-5
DEPTH = 1
DN_ALPHA = (2.0 * DEPTH) ** 0.25
NEG = -1e30

LANES = 128
SUBLANES = 8
BF16_ROWS = 2 * SUBLANES
VMEM_LIMIT = 48 * 1024 * 1024

TM_PROJ = 1024
S5_CHUNK = 8
S5_LANE_GROUPS = LANES // SSM_GROUP
TT_PEER = 512
EB_PEER = 2048


def _dot(a, b):
    return jnp.dot(a, b, preferred_element_type=f32)


def _dot_nt(a, b):
    return lax.dot_general(a, b, (((1,), (1,)), ((), ())), preferred_element_type=f32)


def _split_bf16(x):
    hi = x.astype(bf16)
    lo = (x - hi.astype(f32)).astype(bf16)
    return hi, lo


def _layernorm(r, g, b):
    mu = jnp.mean(r, axis=-1, keepdims=True)
    d = r - mu
    var = jnp.mean(d * d, axis=-1, keepdims=True)
    return d * lax.rsqrt(var + LN_EPS) * g + b


def _inproj_kernel(x_ref, wuk_ref, wqvt_ref, u_ref, k_ref, qt_ref, vt_ref, km_ref):
    xb = x_ref[...].astype(bf16)
    acc = _dot(xb, wuk_ref[...])
    u_ref[...] = acc[:, :D_SSM]
    kk = acc[:, D_SSM:]
    k_ref[...] = kk.astype(bf16)
    qvt = _dot_nt(wqvt_ref[...], xb)
    for r in range(TM_PROJ // MOBA_BLOCK):
        cols = slice(r * MOBA_BLOCK, (r + 1) * MOBA_BLOCK)
        km_ref[r] = jnp.mean(kk[cols], axis=0, keepdims=True)
        qt_ref[r] = qvt[:D_ATT, cols].astype(bf16)
        vt_ref[r] = qvt[D_ATT:, cols].astype(bf16)


def _inproj(x2, w_in):
    n, d_model = x2.shape
    nb = TM_PROJ // MOBA_BLOCK
    w_uk = jnp.concatenate([w_in[:, :D_SSM], w_in[:, D_SSM + D_ATT:D_SSM + 2 * D_ATT]], axis=1).astype(bf16)
    w_qvt = jnp.concatenate([w_in[:, D_SSM:D_SSM + D_ATT], w_in[:, D_SSM + 2 * D_ATT:]], axis=1).T.astype(bf16)
    return pl.pallas_call(
        _inproj_kernel,
        grid=(n // TM_PROJ,),
        in_specs=[pl.BlockSpec((TM_PROJ, d_model), lambda i: (i, 0)),
                  pl.BlockSpec(w_uk.shape, lambda i: (0, 0)),
                  pl.BlockSpec(w_qvt.shape, lambda i: (0, 0))],
        out_specs=[pl.BlockSpec((TM_PROJ, D_SSM), lambda i: (i, 0)),
                   pl.BlockSpec((TM_PROJ, D_ATT), lambda i: (i, 0)),
                   pl.BlockSpec((nb, D_ATT, MOBA_BLOCK), lambda i: (i, 0, 0)),
                   pl.BlockSpec((nb, D_ATT, MOBA_BLOCK), lambda i: (i, 0, 0)),
                   pl.BlockSpec((nb, 1, D_ATT), lambda i: (i, 0, 0))],
        out_shape=[jax.ShapeDtypeStruct((n, D_SSM), f32),
                   jax.ShapeDtypeStruct((n, D_ATT), bf16),
                   jax.ShapeDtypeStruct((n // MOBA_BLOCK, D_ATT, MOBA_BLOCK), bf16),
                   jax.ShapeDtypeStruct((n // MOBA_BLOCK, D_ATT, MOBA_BLOCK), bf16),
                   jax.ShapeDtypeStruct((n // MOBA_BLOCK, 1, D_ATT), f32)],
        compiler_params=pltpu.CompilerParams(dimension_semantics=("parallel",),
                                             vmem_limit_bytes=VMEM_LIMIT),
        name="inproj",
    )(x2, w_uk, w_qvt)


def _s5_operators(a_re, a_im, log_dt, b_re, b_im, c_re, c_im, d_skip):
    T = S5_CHUNK
    G, P = a_re.shape
    H = SSM_GROUP
    nj = G // S5_LANE_GROUPS
    g8 = S5_LANE_GROUPS
    lam = lax.complex(a_re.astype(f32), a_im.astype(f32))
    dt = jnp.exp(log_dt.astype(f32))[:, None]
    lam_bar = jnp.exp(lam * dt)
    b_c = lax.complex(b_re.astype(f32), b_im.astype(f32))
    b_bar = ((lam_bar - 1.0) / lam)[:, :, None] * b_c
    c_c = lax.complex(c_re.astype(f32), c_im.astype(f32))
    pows = jnp.exp((lam * dt)[None] * jnp.arange(T + 1, dtype=f32)[:, None, None])
    def widen(x, reps):
        n = x.shape[-1]
        rep = jnp.tile(jnp.eye(n, dtype=f32), (1, reps))
        return jnp.dot(x.reshape(-1, n), rep).reshape(x.shape[:-1] + (reps * n,))

    same = jnp.arange(g8)[:, None] == (jnp.arange(g8 * H) // H)[None, :]
    same_p = jnp.arange(g8)[:, None] == (jnp.arange(g8 * P) // P)[None, :]
    kern = jnp.real(jnp.einsum('ghp,dgp,gpk->dgkh', c_c, pows[:T], b_bar))
    dd = jnp.arange(T)[None, :] - jnp.arange(T)[:, None]
    kst = jnp.where((dd >= 0)[:, :, None, None, None], kern[jnp.clip(dd, 0, T - 1)], 0.0)
    kst = widen(kst.reshape(T, T, nj, g8, H, H), g8)
    kst = jnp.where(same[None, None, None, :, None, :], kst, 0.0)
    toep = kst.transpose(2, 0, 3, 4, 1, 5).reshape(nj, T * LANES, T * LANES)
    win = pows[T - 1 - jnp.arange(T)][:, :, None, :] * jnp.swapaxes(b_bar, 1, 2)[None]
    win = jnp.stack([jnp.real(win), jnp.imag(win)], 0).reshape(2, T, nj, g8, H, P)
    win = jnp.where(same_p[None, None, None, :, None, :], widen(win, g8), 0.0)
    m_in = win.transpose(2, 1, 3, 4, 0, 5).reshape(nj, T * LANES, 2 * g8 * P)
    wout = jnp.swapaxes(c_c, 1, 2)[None] * pows[1:T + 1][:, :, :, None]
    wout = jnp.stack([jnp.real(wout), -jnp.imag(wout)], 0).reshape(2, T, nj, g8, P, H)
    wout = jnp.where(same[None, None, None, :, None, :], widen(wout, g8), 0.0)
    m_out = wout.transpose(2, 0, 3, 4, 1, 5).reshape(nj, 2 * g8 * P, T * LANES)
    a_chunk = pows[T].reshape(nj, 1, g8 * P)
    a_ri = jnp.concatenate([jnp.real(a_chunk), jnp.imag(a_chunk)], axis=1)
    d_flat = jnp.tile(d_skip.astype(f32).reshape(nj, 1, LANES), (1, 1, T))
    return toep.astype(bf16), m_in.astype(bf16), m_out.astype(bf16), a_ri, d_flat


def _s5_kernel(u_ref, toep_ref, min_ref, mout_ref, a_ref, d_ref, y_ref, in_scr, sp_scr):
    T = S5_CHUNK
    n_c = u_ref.shape[0] // T
    half = in_scr.shape[1] // 2
    uf = jnp.concatenate([u_ref[pl.ds(t, n_c, stride=T), :] for t in range(T)], axis=1)
    ufb = uf.astype(bf16)
    y = _dot(ufb, toep_ref[0]) + uf * d_ref[0]
    in_scr[...] = _dot(ufb, min_ref[0])
    ar = a_ref[0, 0:1, :]
    ai = a_ref[0, 1:2, :]

    def step(c, carry):
        sr, si = carry
        sp_scr[pl.ds(c, 1), :half] = sr
        sp_scr[pl.ds(c, 1), half:] = si
        ir = in_scr[pl.ds(c, 1), :half]
        ii = in_scr[pl.ds(c, 1), half:]
        return ar * sr - ai * si + ir, ar * si + ai * sr + ii

    z = jnp.zeros((1, half), f32)
    lax.fori_loop(0, n_c, step, (z, z), unroll=4)
    y = y + _dot(sp_scr[...].astype(bf16), mout_ref[0])
    yg = jax.nn.gelu(y)
    for t in range(T):
        y_ref[pl.ds(t, n_c, stride=T), :] = yg[:, t * LANES:(t + 1) * LANES]


def _s5(u2, bsz, seq, ops):
    toep, m_in, m_out, a_ri, d_flat = ops
    nj = toep.shape[0]
    n_c = seq // S5_CHUNK
    kdim = S5_CHUNK * LANES
    sdim = m_in.shape[2]
    return pl.pallas_call(
        _s5_kernel,
        grid=(nj, bsz),
        in_specs=[pl.BlockSpec((seq, LANES), lambda j, b: (b, j)),
                  pl.BlockSpec((1, kdim, kdim), lambda j, b: (j, 0, 0)),
                  pl.BlockSpec((1, kdim, sdim), lambda j, b: (j, 0, 0)),
                  pl.BlockSpec((1, sdim, kdim), lambda j, b: (j, 0, 0)),
                  pl.BlockSpec((1, 2, sdim // 2), lambda j, b: (j, 0, 0)),
                  pl.BlockSpec((1, 1, kdim), lambda j, b: (j, 0, 0))],
        out_specs=pl.BlockSpec((seq, LANES), lambda j, b: (b, j)),
        out_shape=jax.ShapeDtypeStruct(u2.shape, f32),
        scratch_shapes=[pltpu.VMEM((n_c, sdim), f32), pltpu.VMEM((n_c, sdim), f32)],
        compiler_params=pltpu.CompilerParams(dimension_semantics=("parallel", "parallel"),
                                             vmem_limit_bytes=VMEM_LIMIT),
        name="s5",
    )(u2, toep, m_in, m_out, a_ri, d_flat)


def _moba_kernel(qt_ref, k_ref, vt_ref, km_ref, o_ref, kaug_ref, s0_scr, s1_scr):
    blk = MOBA_BLOCK
    hd = ATT_HEAD_DIM
    nb = k_ref.shape[0] // blk
    i = pl.program_id(2)

    @pl.when(i == 0)
    def _build():
        lane = lax.broadcasted_iota(jnp.int32, (blk, LANES), 1)

        def build(j, carry):
            rows = pl.ds(pl.multiple_of(j * blk, blk), blk)
            kaug_ref[rows, :LANES] = k_ref[rows, :]
            kaug_ref[rows, LANES:] = jnp.where(lane == j, 1.0, 0.0).astype(bf16)
            return carry
        lax.fori_loop(0, nb, build, 0)

    qt = qt_ref[0]
    qtf = qt.astype(f32) * (hd ** -0.5 * math.log2(math.e))
    km = km_ref[0]
    lane_nb = lax.broadcasted_iota(jnp.int32, (nb, LANES), 1)
    feat = lax.broadcasted_iota(jnp.int32, (LANES, blk), 0)
    blk_id = lax.broadcasted_iota(jnp.int32, (nb, blk), 0)
    blk_f = blk_id.astype(f32)
    q_cols, bias_cols = [], []
    for hh in range(2):
        km_hi, km_lo = _split_bf16(jnp.where((lane_nb < hd) if hh == 0 else (lane_nb >= hd), km, 0.0))
        gate = _dot(km_hi, qt) + _dot(km_lo, qt)
        g = jnp.where(blk_id < i, gate, -jnp.inf)
        sel = blk_id == i
        for _ in range(MOBA_TOPK):
            mx = jnp.max(g, axis=0, keepdims=True)
            first = jnp.min(jnp.where(g == mx, blk_f, float(nb)), axis=0, keepdims=True)
            pick = (blk_f == first) & (mx > -jnp.inf)
            sel = sel | pick
            g = jnp.where(pick, -jnp.inf, g)
        bias_cols.append(jnp.where(sel, 0.0, NEG))
        q_cols.append(jnp.where((feat < hd) if hh == 0 else (feat >= hd), qtf, 0.0))
    qa = jnp.concatenate([jnp.concatenate(q_cols, axis=1), jnp.concatenate(bias_cols, axis=1),
                          jnp.zeros((LANES - nb, 2 * blk), f32)], axis=0).astype(bf16)

    ones_rows = jnp.ones((BF16_ROWS, blk), bf16)
    key_pos = lax.broadcasted_iota(jnp.int32, (blk, 2 * blk), 0)
    q_pos = lax.broadcasted_iota(jnp.int32, (blk, 2 * blk), 1) % blk

    def scores(j):
        j = jnp.minimum(j, nb - 1)
        return _dot(kaug_ref[pl.ds(pl.multiple_of(j * blk, blk), blk), :], qa)

    def update(state, blocks):
        m, acc = state
        m_new = m
        for _, s in blocks:
            m_new = jnp.maximum(m_new, jnp.max(s, axis=0, keepdims=True))
        acc = jnp.exp2(m - m_new) * acc
        for j, s in blocks:
            pb = jnp.exp2(s - m_new).astype(bf16)
            acc = acc + jnp.concatenate(
                [_dot(jnp.concatenate([vt_ref[j, :hd, :], ones_rows], axis=0), pb[:, :blk]),
                 _dot(jnp.concatenate([vt_ref[j, hd:, :], ones_rows], axis=0), pb[:, blk:])], axis=1)
        return m_new, acc

    def pair_scores(p, dst):
        dst[0] = scores(2 * p)
        dst[1] = scores(2 * p + 1)

    def pair_update(state, p, src, masked):
        blocks = []
        for r in range(2):
            j = 2 * p + r
            s = src[r]
            if masked:
                off = jnp.where(j < i, blk, jnp.where(j == i, 0, -2 * blk))
                s = jnp.where(key_pos <= q_pos + off, s, NEG)
            blocks.append((jnp.minimum(j, nb - 1), s))
        return update(state, blocks)

    def quad_body(q, state):
        pair_scores(2 * q + 1, s1_scr)
        state = pair_update(state, 2 * q, s0_scr, False)
        pair_scores(2 * q + 2, s0_scr)
        return pair_update(state, 2 * q + 1, s1_scr, False)

    state = (jnp.full((1, 2 * blk), -jnp.inf, f32), jnp.zeros((hd + BF16_ROWS, 2 * blk), f32))
    n_quad = i // 4
    pair_scores(0, s0_scr)
    state = lax.fori_loop(0, n_quad, quad_body, state)
    pair_scores(2 * n_quad + 1, s1_scr)
    state = pair_update(state, 2 * n_quad, s0_scr, True)
    m, acc = pair_update(state, 2 * n_quad + 1, s1_scr, True)
    out = acc[:hd] / acc[hd:hd + 1]
    o_ref[...] = jnp.concatenate([out[:, :blk], out[:, blk:]], axis=0).T.astype(o_ref.dtype)


def _moba(qt, k2, vt, kmean, bsz, seq):
    nb = seq // MOBA_BLOCK
    assert seq % MOBA_BLOCK == 0 and nb % SUBLANES == 0 and nb <= LANES
    npair = D_ATT // LANES
    return pl.pallas_call(
        _moba_kernel,
        grid=(bsz, npair, nb),
        in_specs=[pl.BlockSpec((1, LANES, MOBA_BLOCK), lambda b, h, i: (b * nb + i, h, 0)),
                  pl.BlockSpec((seq, LANES), lambda b, h, i: (b, h)),
                  pl.BlockSpec((nb, LANES, MOBA_BLOCK), lambda b, h, i: (b, h, 0)),
                  pl.BlockSpec((1, nb, LANES), lambda b, h, i: (b, 0, h))],
        out_specs=pl.BlockSpec((MOBA_BLOCK, LANES), lambda b, h, i: (b * nb + i, h)),
        out_shape=jax.ShapeDtypeStruct(k2.shape, bf16),
        scratch_shapes=[pltpu.VMEM((seq, 2 * LANES), bf16),
                        pltpu.VMEM((2, MOBA_BLOCK, 2 * MOBA_BLOCK), f32),
                        pltpu.VMEM((2, MOBA_BLOCK, 2 * MOBA_BLOCK), f32)],
        compiler_params=pltpu.CompilerParams(dimension_semantics=("parallel", "parallel", "arbitrary"),
                                             vmem_limit_bytes=VMEM_LIMIT),
        name="moba",
    )(qt, k2, vt, kmean)


def _outproj_kernel(x_ref, yg_ref, att_ref, wglu_ref, wout_ref, g_ref, b_ref, h_ref):
    yg = yg_ref[...]
    z = yg * jax.nn.sigmoid(_dot(yg.astype(bf16), wglu_ref[...]))
    cat = jnp.concatenate([z.astype(bf16), att_ref[...]], axis=1)
    r = DN_ALPHA * x_ref[...] + _dot(cat, wout_ref[...])
    h_ref[...] = _layernorm(r, g_ref[...], b_ref[...])


def _outproj(x2, yg, att, wglu_b, wout_b, g, b):
    n, d_model = x2.shape
    return pl.pallas_call(
        _outproj_kernel,
        grid=(n // TM_PROJ,),
        in_specs=[pl.BlockSpec((TM_PROJ, d_model), lambda i: (i, 0)),
                  pl.BlockSpec((TM_PROJ, D_SSM), lambda i: (i, 0)),
                  pl.BlockSpec((TM_PROJ, D_ATT), lambda i: (i, 0)),
                  pl.BlockSpec(wglu_b.shape, lambda i: (0, 0)),
                  pl.BlockSpec(wout_b.shape, lambda i: (0, 0)),
                  pl.BlockSpec((1, d_model), lambda i: (0, 0)),
                  pl.BlockSpec((1, d_model), lambda i: (0, 0))],
        out_specs=pl.BlockSpec((TM_PROJ, d_model), lambda i: (i, 0)),
        out_shape=jax.ShapeDtypeStruct((n, d_model), f32),
        compiler_params=pltpu.CompilerParams(dimension_semantics=("parallel",),
                                             vmem_limit_bytes=VMEM_LIMIT),
        name="outproj",
    )(x2, yg, att, wglu_b, wout_b, g, b)


def _bitonic_sort_desc(v):
    n = len(v)
    k = 2
    while k <= n:
        j = k // 2
        while j >= 1:
            for a in range(n):
                b = a ^ j
                if b > a:
                    hi = jnp.maximum(v[a], v[b])
                    lo = jnp.minimum(v[a], v[b])
                    v[a], v[b] = (hi, lo) if (a & k) == 0 else (lo, hi)
            j //= 2
        k *= 2
    return v


def _bitonic_merge_desc(v):
    j = len(v) // 2
    while j >= 1:
        for a in range(len(v)):
            b = a ^ j
            if b > a:
                v[a], v[b] = jnp.maximum(v[a], v[b]), jnp.minimum(v[a], v[b])
        j //= 2
    return v


def _top16_sorted(st):
    k = PEER_TOPK
    x = _bitonic_sort_desc([st[SUBLANES * a:SUBLANES * (a + 1), :] for a in range(PEER_N_KEYS // SUBLANES)])
    for shift in (4, 2, 1):
        y = [pltpu.roll(x[k - 1 - a], shift, 0) for a in range(k)]
        x = _bitonic_merge_desc([jnp.maximum(x[a], y[a]) for a in range(k)])
    return x


def _dup_bf16(x):
    u = pltpu.bitcast(x.astype(bf16).astype(f32), jnp.uint32)
    return pltpu.bitcast(u | (u >> 16), f32)


def _bf16_rows(row):
    return pltpu.bitcast(jnp.broadcast_to(row, (SUBLANES, row.shape[1])), bf16)[None]


def _peer_kernel(n_exp, us_ref, h_ref, wq_ref, sk_ref, u_ref, vt_ref, g_ref, b_ref, o_ref,
                 q_scr, pr_scr, rq_scr, srt_scr, row_scr, acc_scr, hbt_scr, a_scr, unscale_scr):
    eb = pl.program_id(1)
    n_blk = n_exp // EB_PEER
    tt = h_ref.shape[0]
    nk = PEER_N_KEYS
    k = PEER_TOPK

    chunk = 2 * nk
    n_chunk = EB_PEER // chunk

    def activations():
        for s in range(n_chunk):
            a_scr[s] = _dot(u_ref[s * chunk:(s + 1) * chunk, :], hbt_scr[...])

    @pl.when(eb == 0)
    def _prologue():
        ht = h_ref[...].T
        amax = jnp.maximum(jnp.max(jnp.abs(ht), keepdims=True), jnp.finfo(f32).tiny)
        e_tile = jnp.minimum(jnp.floor(jnp.log2(FP8_TARGET_MAX / amax)), FP8_MAX_SHIFT)
        hbt_scr[...] = (ht * jnp.exp2(e_tile)).astype(fp8)
        unscale_scr[0] = (jnp.exp2(-e_tile) * us_ref[0, 0])[0, 0]
        q = _dot(h_ref[...].astype(bf16), wq_ref[...])
        for hc in range(2 * PEER_HEADS):
            q_scr[hc] = q[:, hc * PEER_HALF:(hc + 1) * PEER_HALF]

        def head_sort(h, carry):
            for c in range(2):
                q_hi, q_lo = _split_bf16(q_scr[2 * h + c])
                k_hi, k_lo = _split_bf16(sk_ref[2 * h + c])
                st = _dot_nt(k_hi, q_hi) + _dot_nt(k_hi, q_lo) + _dot_nt(k_lo, q_hi)
                pr_scr[c, h] = st
                srt = _top16_sorted(st)
                for a in range(k):
                    srt_scr[c, a, pl.ds(h, 1), :] = srt[a][0:1, :]
            return carry
        lax.fori_loop(0, PEER_HEADS, head_sort, 0)

        a_s = [srt_scr[0, a] for a in range(k)]
        b_s = [srt_scr[1, a] for a in range(k)]
        cands = [a_s[a] + b_s[b] for a in range(k) for b in range(k) if (a + 1) * (b + 1) <= k]
        cands += [jnp.full_like(cands[0], -jnp.inf)] * (64 - len(cands))
        cs = _bitonic_sort_desc(cands)
        zsum = jnp.zeros_like(cs[0])
        for a in range(k):
            zsum = zsum + jnp.exp(cs[a] - cs[0])
        row_scr[0] = 0.5 * (cs[k - 1] + cs[k])
        row_scr[1] = 1.0 / zsum
        row_scr[2] = a_s[0]
        row_scr[3] = b_s[0]
        row_scr[4] = a_s[k - 1]
        row_scr[5] = b_s[k - 1]

        def head_prep(h, carry):
            hrow = pl.ds(h, 1)
            at = pr_scr[0, h]
            bt = pr_scr[1, h]
            thr = row_scr[0, hrow, :] - at
            r_cnt = jnp.zeros_like(at)
            b_rank = jnp.zeros_like(bt)
            for a in range(k):
                b_a = srt_scr[1, a, hrow, :]
                r_cnt = jnp.where(b_a > thr, float(a + 1), r_cnt)
                b_rank = jnp.where(b_a > bt, float(a + 1), b_rank)
            in_a = at >= row_scr[4, hrow, :]
            p_w = jnp.where(in_a, jnp.exp(at - row_scr[2, hrow, :]), 0.0) * row_scr[1, hrow, :]
            pr_scr[0, h] = _dup_bf16(p_w)
            pr_scr[1, h] = _dup_bf16(jnp.where(in_a, r_cnt, 0.0))
            q_w = jnp.where(bt >= row_scr[5, hrow, :], jnp.exp(bt - row_scr[3, hrow, :]), 0.0)
            rq_scr[0, h] = b_rank.astype(bf16).reshape(nk // BF16_ROWS, BF16_ROWS, tt)
            rq_scr[1, h] = q_w.astype(bf16).reshape(nk // BF16_ROWS, BF16_ROWS, tt)
            return carry
        lax.fori_loop(0, PEER_HEADS, head_prep, 0)
        acc_scr[...] = jnp.zeros_like(acc_scr)

    activations()
    for s in range(n_chunk):
        halves = []
        for half in range(chunk // nk):
            n1 = pl.ds((EB_PEER // nk) * eb + s * (chunk // nk) + half, 1)
            g = jnp.zeros((nk // BF16_ROWS, BF16_ROWS, tt), bf16)
            for h in range(PEER_HEADS):
                p_row = _bf16_rows(pr_scr[0, h, n1, :])
                r_row = _bf16_rows(pr_scr[1, h, n1, :])
                g = g + p_row * jnp.where(rq_scr[0, h] < r_row, rq_scr[1, h], 0)
            halves.append(g.reshape(nk, tt))
        act = a_scr[s].astype(bf16) * unscale_scr[0].astype(bf16)
        w = jnp.concatenate(halves, axis=0) * jax.nn.gelu(act)
        acc_scr[...] += _dot(vt_ref[:, s * chunk:(s + 1) * chunk], w)

    @pl.when(eb == n_blk - 1)
    def _last():
        r = DN_ALPHA * h_ref[...] + acc_scr[...].T
        o_ref[...] = _layernorm(r, g_ref[...], b_ref[...])


def _peer(h1, wq_b, sub_keys, u_q, u_unscale, vt_b, g, b):
    u_b = u_q
    n, d_model = h1.shape
    n_exp = u_b.shape[0]
    tt = TT_PEER
    sk = sub_keys.reshape(2 * PEER_HEADS, PEER_N_KEYS, PEER_HALF).astype(f32)
    n_blk = n_exp // EB_PEER
    const = dict(pipeline_mode=pl.Buffered(1))
    return pl.pallas_call(
        functools.partial(_peer_kernel, n_exp),
        grid=(n // tt, n_blk),
        in_specs=[pl.BlockSpec(memory_space=pltpu.SMEM),
                  pl.BlockSpec((tt, d_model), lambda t, e: (t, 0)),
                  pl.BlockSpec(wq_b.shape, lambda t, e: (0, 0), **const),
                  pl.BlockSpec(sk.shape, lambda t, e: (0, 0, 0), **const),
                  pl.BlockSpec((EB_PEER, d_model), lambda t, e: (e, 0)),
                  pl.BlockSpec((d_model, EB_PEER), lambda t, e: (0, e)),
                  pl.BlockSpec((1, d_model), lambda t, e: (0, 0), **const),
                  pl.BlockSpec((1, d_model), lambda t, e: (0, 0), **const)],
        out_specs=pl.BlockSpec((tt, d_model), lambda t, e: (t, 0)),
        out_shape=jax.ShapeDtypeStruct((n, d_model), f32),
        scratch_shapes=[pltpu.VMEM((2 * PEER_HEADS, tt, PEER_HALF), f32),
                        pltpu.VMEM((2, PEER_HEADS, PEER_N_KEYS, tt), f32),
                        pltpu.VMEM((2, PEER_HEADS, PEER_N_KEYS // BF16_ROWS, BF16_ROWS, tt), bf16),
                        pltpu.VMEM((2, PEER_TOPK, PEER_HEADS, tt), f32),
                        pltpu.VMEM((6, PEER_HEADS, tt), f32),
                        pltpu.VMEM((d_model, tt), f32),
                        pltpu.VMEM((d_model, tt), fp8),
                        pltpu.VMEM((EB_PEER // (2 * PEER_N_KEYS), 2 * PEER_N_KEYS, tt), f32),
                        pltpu.SMEM((1,), f32)],
        compiler_params=pltpu.CompilerParams(dimension_semantics=("parallel", "arbitrary"),
                                             vmem_limit_bytes=VMEM_LIMIT),
        name="peer",
    )(u_unscale.reshape(1, 1), h1, wq_b, sk, u_b, vt_b, g, b)


def _quantize_table(t):
    amax = jnp.maximum(jnp.max(jnp.abs(t)), jnp.finfo(f32).tiny)
    e = jnp.minimum(jnp.floor(jnp.log2(FP8_TARGET_MAX / amax)), FP8_MAX_SHIFT)
    return (t * jnp.exp2(e)).astype(fp8), jnp.exp2(-e).astype(f32)


def kernel(x, w_in, ssm_a_re, ssm_a_im, ssm_log_dt, ssm_b_re, ssm_b_im, ssm_c_re, ssm_c_im, ssm_d,
           ssm_w_glu, w_out, ln1_g, ln1_b, peer_w_q, peer_sub_keys, peer_u, peer_v, ln2_g, ln2_b):
    bsz, seq, d_model = x.shape
    n = bsz * seq
    h = x.reshape(n, d_model)
    for l in range(w_in.shape[0]):
        u2, k2, qt, vt, kmean = _inproj(h, w_in[l])
        ops = _s5_operators(ssm_a_re[l], ssm_a_im[l], ssm_log_dt[l], ssm_b_re[l], ssm_b_im[l],
                            ssm_c_re[l], ssm_c_im[l], ssm_d[l])
        yg = _s5(u2, bsz, seq, ops)
        att = _moba(qt, k2, vt, kmean.reshape(bsz, seq // MOBA_BLOCK, D_ATT), bsz, seq)
        h1 = _outproj(h, yg, att, ssm_w_glu[l].astype(bf16), w_out[l].astype(bf16),
                      ln1_g[l].reshape(1, d_model), ln1_b[l].reshape(1, d_model))
        u_q, u_unscale = _quantize_table(peer_u[l])
        h = _peer(h1, peer_w_q[l].astype(bf16), peer_sub_keys[l], u_q, u_unscale,
                  peer_v[l].T.astype(bf16), ln2_g[l].reshape(1, d_model), ln2_b[l].reshape(1, d_model))
    return h.reshape(bsz, seq, d_model)
```

```python
import functools
import math

import jax
import jax.numpy as jnp
from jax import lax
from jax.experimental import pallas as pl
from jax.experimental.pallas import tpu as pltpu

f32 = jnp.float32
bf16 = jnp.bfloat16
fp8 = jnp.float8_e4m3fn
FP8_TARGET_MAX = 256.0
FP8_MAX_SHIFT = 48.0

D_SSM = 512
SSM_GROUP = 16
SSM_STATE = 64
N_ATT_HEADS = 8
ATT_HEAD_DIM = 64
D_ATT = N_ATT_HEADS * ATT_HEAD_DIM
MOBA_BLOCK = 256
MOBA_TOPK = 3
PEER_HEADS = 8
PEER_HALF = 128
PEER_N_KEYS = 128
PEER_TOPK = 16
LN_EPS = 1e-5
DEPTH = 1
DN_ALPHA = (2.0 * DEPTH) ** 0.25
NEG = -1e30

LANES = 128
SUBLANES = 8
BF16_ROWS = 2 * SUBLANES
VMEM_LIMIT = 48 * 1024 * 1024

TM_PROJ = 1024
S5_CHUNK = 8
S5_LANE_GROUPS = LANES // SSM_GROUP
TT_PEER = 512
EB_PEER = 2048


def _dot(a, b):
    return jnp.dot(a, b, preferred_element_type=f32)


def _dot_nt(a, b):
    return lax.dot_general(a, b, (((1,), (1,)), ((), ())), preferred_element_type=f32)


def _split_bf16(x):
    hi = x.astype(bf16)
    lo = (x - hi.astype(f32)).astype(bf16)
    return hi, lo


def _layernorm(r, g, b):
    mu = jnp.mean(r, axis=-1, keepdims=True)
    d = r - mu
    var = jnp.mean(d * d, axis=-1, keepdims=True)
    return d * lax.rsqrt(var + LN_EPS) * g + b


def _inproj_kernel(x_ref, wuk_ref, wqvt_ref, u_ref, k_ref, qt_ref, vt_ref, km_ref):
    xb = x_ref[...].astype(bf16)
    acc = _dot(xb, wuk_ref[...])
    u_ref[...] = acc[:, :D_SSM]
    kk = acc[:, D_SSM:]
    k_ref[...] = kk.astype(bf16)
    qvt = _dot_nt(wqvt_ref[...], xb)
    for r in range(TM_PROJ // MOBA_BLOCK):
        cols = slice(r * MOBA_BLOCK, (r + 1) * MOBA_BLOCK)
        km_ref[r] = jnp.mean(kk[cols], axis=0, keepdims=True)
        qt_ref[r] = qvt[:D_ATT, cols].astype(bf16)
        vt_ref[r] = qvt[D_ATT:, cols].astype(bf16)


def _inproj(x2, w_in):
    n, d_model = x2.shape
    nb = TM_PROJ // MOBA_BLOCK
    w_uk = jnp.concatenate([w_in[:, :D_SSM], w_in[:, D_SSM + D_ATT:D_SSM + 2 * D_ATT]], axis=1).astype(bf16)
    w_qvt = jnp.concatenate([w_in[:, D_SSM:D_SSM + D_ATT], w_in[:, D_SSM + 2 * D_ATT:]], axis=1).T.astype(bf16)
    return pl.pallas_call(
        _inproj_kernel,
        grid=(n // TM_PROJ,),
        in_specs=[pl.BlockSpec((TM_PROJ, d_model), lambda i: (i, 0)),
                  pl.BlockSpec(w_uk.shape, lambda i: (0, 0)),
                  pl.BlockSpec(w_qvt.shape, lambda i: (0, 0))],
        out_specs=[pl.BlockSpec((TM_PROJ, D_SSM), lambda i: (i, 0)),
                   pl.BlockSpec((TM_PROJ, D_ATT), lambda i: (i, 0)),
                   pl.BlockSpec((nb, D_ATT, MOBA_BLOCK), lambda i: (i, 0, 0)),
                   pl.BlockSpec((nb, D_ATT, MOBA_BLOCK), lambda i: (i, 0, 0)),
                   pl.BlockSpec((nb, 1, D_ATT), lambda i: (i, 0, 0))],
        out_shape=[jax.ShapeDtypeStruct((n, D_SSM), f32),
                   jax.ShapeDtypeStruct((n, D_ATT), bf16),
                   jax.ShapeDtypeStruct((n // MOBA_BLOCK, D_ATT, MOBA_BLOCK), bf16),
                   jax.ShapeDtypeStruct((n // MOBA_BLOCK, D_ATT, MOBA_BLOCK), bf16),
                   jax.ShapeDtypeStruct((n // MOBA_BLOCK, 1, D_ATT), f32)],
        compiler_params=pltpu.CompilerParams(dimension_semantics=("parallel",),
                                             vmem_limit_bytes=VMEM_LIMIT),
        name="inproj",
    )(x2, w_uk, w_qvt)


def _s5_operators(a_re, a_im, log_dt, b_re, b_im, c_re, c_im, d_skip):
    T = S5_CHUNK
    G, P = a_re.shape
    H = SSM_GROUP
    nj = G // S5_LANE_GROUPS
    g8 = S5_LANE_GROUPS
    lam = lax.complex(a_re.astype(f32), a_im.astype(f32))
    dt = jnp.exp(log_dt.astype(f32))[:, None]
    lam_bar = jnp.exp(lam * dt)
    b_c = lax.complex(b_re.astype(f32), b_im.astype(f32))
    b_bar = ((lam_bar - 1.0) / lam)[:, :, None] * b_c
    c_c = lax.complex(c_re.astype(f32), c_im.astype(f32))
    pows = jnp.exp((lam * dt)[None] * jnp.arange(T + 1, dtype=f32)[:, None, None])
    def widen(x, reps):
        n = x.shape[-1]
        rep = jnp.tile(jnp.eye(n, dtype=f32), (1, reps))
        return jnp.dot(x.reshape(-1, n), rep).reshape(x.shape[:-1] + (reps * n,))

    same = jnp.arange(g8)[:, None] == (jnp.arange(g8 * H) // H)[None, :]
    same_p = jnp.arange(g8)[:, None] == (jnp.arange(g8 * P) // P)[None, :]
    kern = jnp.real(jnp.einsum('ghp,dgp,gpk->dgkh', c_c, pows[:T], b_bar))
    dd = jnp.arange(T)[None, :] - jnp.arange(T)[:, None]
    kst = jnp.where((dd >= 0)[:, :, None, None, None], kern[jnp.clip(dd, 0, T - 1)], 0.0)
    kst = widen(kst.reshape(T, T, nj, g8, H, H), g8)
    kst = jnp.where(same[None, None, None, :, None, :], kst, 0.0)
    toep = kst.transpose(2, 0, 3, 4, 1, 5).reshape(nj, T * LANES, T * LANES)
    win = pows[T - 1 - jnp.arange(T)][:, :, None, :] * jnp.swapaxes(b_bar, 1, 2)[None]
    win = jnp.stack([jnp.real(win), jnp.imag(win)], 0).reshape(2, T, nj, g8, H, P)
    win = jnp.where(same_p[None, None, None, :, None, :], widen(win, g8), 0.0)
    m_in = win.transpose(2, 1, 3, 4, 0, 5).reshape(nj, T * LANES, 2 * g8 * P)
    wout = jnp.swapaxes(c_c, 1, 2)[None] * pows[1:T + 1][:, :, :, None]
    wout = jnp.stack([jnp.real(wout), -jnp.imag(wout)], 0).reshape(2, T, nj, g8, P, H)
    wout = jnp.where(same[None, None, None, :, None, :], widen(wout, g8), 0.0)
    m_out = wout.transpose(2, 0, 3, 4, 1, 5).reshape(nj, 2 * g8 * P, T * LANES)
    a_chunk = pows[T].reshape(nj, 1, g8 * P)
    a_ri = jnp.concatenate([jnp.real(a_chunk), jnp.imag(a_chunk)], axis=1)
    d_flat = jnp.tile(d_skip.astype(f32).reshape(nj, 1, LANES), (1, 1, T))
    return toep.astype(bf16), m_in.astype(bf16), m_out.astype(bf16), a_ri, d_flat


def _s5_kernel(u_ref, toep_ref, min_ref, mout_ref, a_ref, d_ref, y_ref, in_scr, sp_scr):
    T = S5_CHUNK
    n_c = u_ref.shape[0] // T
    half = in_scr.shape[1] // 2
    uf = jnp.concatenate([u_ref[pl.ds(t, n_c, stride=T), :] for t in range(T)], axis=1)
    ufb = uf.astype(bf16)
    y = _dot(ufb, toep_ref[0]) + uf * d_ref[0]
    in_scr[...] = _dot(ufb, min_ref[0])
    ar = a_ref[0, 0:1, :]
    ai = a_ref[0, 1:2, :]

    def step(c, carry):
        sr, si = carry
        sp_scr[pl.ds(c, 1), :half] = sr
        sp_scr[pl.ds(c, 1), half:] = si
        ir = in_scr[pl.ds(c, 1), :half]
        ii = in_scr[pl.ds(c, 1), half:]
        return ar * sr - ai * si + ir, ar * si + ai * sr + ii

    z = jnp.zeros((1, half), f32)
    lax.fori_loop(0, n_c, step, (z, z), unroll=4)
    y = y + _dot(sp_scr[...].astype(bf16), mout_ref[0])
    yg = jax.nn.gelu(y)
    for t in range(T):
        y_ref[pl.ds(t, n_c, stride=T), :] = yg[:, t * LANES:(t + 1) * LANES]


def _s5(u2, bsz, seq, ops):
    toep, m_in, m_out, a_ri, d_flat = ops
    nj = toep.shape[0]
    n_c = seq // S5_CHUNK
    kdim = S5_CHUNK * LANES
    sdim = m_in.shape[2]
    return pl.pallas_call(
        _s5_kernel,
        grid=(nj, bsz),
        in_specs=[pl.BlockSpec((seq, LANES), lambda j, b: (b, j)),
                  pl.BlockSpec((1, kdim, kdim), lambda j, b: (j, 0, 0)),
                  pl.BlockSpec((1, kdim, sdim), lambda j, b: (j, 0, 0)),
                  pl.BlockSpec((1, sdim, kdim), lambda j, b: (j, 0, 0)),
                  pl.BlockSpec((1, 2, sdim // 2), lambda j, b: (j, 0, 0)),
                  pl.BlockSpec((1, 1, kdim), lambda j, b: (j, 0, 0))],
        out_specs=pl.BlockSpec((seq, LANES), lambda j, b: (b, j)),
        out_shape=jax.ShapeDtypeStruct(u2.shape, f32),
        scratch_shapes=[pltpu.VMEM((n_c, sdim), f32), pltpu.VMEM((n_c, sdim), f32)],
        compiler_params=pltpu.CompilerParams(dimension_semantics=("parallel", "parallel"),
                                             vmem_limit_bytes=VMEM_LIMIT),
        name="s5",
    )(u2, toep, m_in, m_out, a_ri, d_flat)


def _moba_kernel(qt_ref, k_ref, vt_ref, km_ref, o_ref, kaug_ref, s0_scr, s1_scr):
    blk = MOBA_BLOCK
    hd = ATT_HEAD_DIM
    nb = k_ref.shape[0] // blk
    i = pl.program_id(2)

    @pl.when(i == 0)
    def _build():
        lane = lax.broadcasted_iota(jnp.int32, (blk, LANES), 1)

        def build(j, carry):
            rows = pl.ds(pl.multiple_of(j * blk, blk), blk)
            kaug_ref[rows, :LANES] = k_ref[rows, :]
            kaug_ref[rows, LANES:] = jnp.where(lane == j, 1.0, 0.0).astype(bf16)
            return carry
        lax.fori_loop(0, nb, build, 0)

    qt = qt_ref[0]
    qtf = qt.astype(f32) * (hd ** -0.5 * math.log2(math.e))
    km = km_ref[0]
    lane_nb = lax.broadcasted_iota(jnp.int32, (nb, LANES), 1)
    feat = lax.broadcasted_iota(jnp.int32, (LANES, blk), 0)
    blk_id = lax.broadcasted_iota(jnp.int32, (nb, blk), 0)
    blk_f = blk_id.astype(f32)
    q_cols, bias_cols = [], []
    for hh in range(2):
        km_hi, km_lo = _split_bf16(jnp.where((lane_nb < hd) if hh == 0 else (lane_nb >= hd), km, 0.0))
        gate = _dot(km_hi, qt) + _dot(km_lo, qt)
        g = jnp.where(blk_id < i, gate, -jnp.inf)
        sel = blk_id == i
        for _ in range(MOBA_TOPK):
            mx = jnp.max(g, axis=0, keepdims=True)
            first = jnp.min(jnp.where(g == mx, blk_f, float(nb)), axis=0, keepdims=True)
            pick = (blk_f == first) & (mx > -jnp.inf)
            sel = sel | pick
            g = jnp.where(pick, -jnp.inf, g)
        bias_cols.append(jnp.where(sel, 0.0, NEG))
        q_cols.append(jnp.where((feat < hd) if hh == 0 else (feat >= hd), qtf, 0.0))
    qa = jnp.concatenate([jnp.concatenate(q_cols, axis=1), jnp.concatenate(bias_cols, axis=1),
                          jnp.zeros((LANES - nb, 2 * blk), f32)], axis=0).astype(bf16)

    ones_rows = jnp.ones((BF16_ROWS, blk), bf16)
    key_pos = lax.broadcasted_iota(jnp.int32, (blk, 2 * blk), 0)
    q_pos = lax.broadcasted_iota(jnp.int32, (blk, 2 * blk), 1) % blk

    def scores(j):
        j = jnp.minimum(j, nb - 1)
        return _dot(kaug_ref[pl.ds(pl.multiple_of(j * blk, blk), blk), :], qa)

    def update(state, blocks):
        m, acc = state
        m_new = m
        for _, s in blocks:
            m_new = jnp.maximum(m_new, jnp.max(s, axis=0, keepdims=True))
        acc = jnp.exp2(m - m_new) * acc
        for j, s in blocks:
            pb = jnp.exp2(s - m_new).astype(bf16)
            acc = acc + jnp.concatenate(
                [_dot(jnp.concatenate([vt_ref[j, :hd, :], ones_rows], axis=0), pb[:, :blk]),
                 _dot(jnp.concatenate([vt_ref[j, hd:, :], ones_rows], axis=0), pb[:, blk:])], axis=1)
        return m_new, acc

    def pair_scores(p, dst):
        dst[0] = scores(2 * p)
        dst[1] = scores(2 * p + 1)

    def pair_update(state, p, src, masked):
        blocks = []
        for r in range(2):
            j = 2 * p + r
            s = src[r]
            if masked:
                off = jnp.where(j < i, blk, jnp.where(j == i, 0, -2 * blk))
                s = jnp.where(key_pos <= q_pos + off, s, NEG)
            blocks.append((jnp.minimum(j, nb - 1), s))
        return update(state, blocks)

    def quad_body(q, state):
        pair_scores(2 * q + 1, s1_scr)
        state = pair_update(state, 2 * q, s0_scr, False)
        pair_scores(2 * q + 2, s0_scr)
        return pair_update(state, 2 * q + 1, s1_scr, False)

    state = (jnp.full((1, 2 * blk), -jnp.inf, f32), jnp.zeros((hd + BF16_ROWS, 2 * blk), f32))
    n_quad = i // 4
    pair_scores(0, s0_scr)
    state = lax.fori_loop(0, n_quad, quad_body, state)
    pair_scores(2 * n_quad + 1, s1_scr)
    state = pair_update(state, 2 * n_quad, s0_scr, True)
    m, acc = pair_update(state, 2 * n_quad + 1, s1_scr, True)
    out = acc[:hd] / acc[hd:hd + 1]
    o_ref[...] = jnp.concatenate([out[:, :blk], out[:, blk:]], axis=0).T.astype(o_ref.dtype)


def _moba(qt, k2, vt, kmean, bsz, seq):
    nb = seq // MOBA_BLOCK
    assert seq % MOBA_BLOCK == 0 and nb % SUBLANES == 0 and nb <= LANES
    npair = D_ATT // LANES
    return pl.pallas_call(
        _moba_kernel,
        grid=(bsz, npair, nb),
        in_specs=[pl.BlockSpec((1, LANES, MOBA_BLOCK), lambda b, h, i: (b * nb + i, h, 0)),
                  pl.BlockSpec((seq, LANES), lambda b, h, i: (b, h)),
                  pl.BlockSpec((nb, LANES, MOBA_BLOCK), lambda b, h, i: (b, h, 0)),
                  pl.BlockSpec((1, nb, LANES), lambda b, h, i: (b, 0, h))],
        out_specs=pl.BlockSpec((MOBA_BLOCK, LANES), lambda b, h, i: (b * nb + i, h)),
        out_shape=jax.ShapeDtypeStruct(k2.shape, bf16),
        scratch_shapes=[pltpu.VMEM((seq, 2 * LANES), bf16),
                        pltpu.VMEM((2, MOBA_BLOCK, 2 * MOBA_BLOCK), f32),
                        pltpu.VMEM((2, MOBA_BLOCK, 2 * MOBA_BLOCK), f32)],
        compiler_params=pltpu.CompilerParams(dimension_semantics=("parallel", "parallel", "arbitrary"),
                                             vmem_limit_bytes=VMEM_LIMIT),
        name="moba",
    )(qt, k2, vt, kmean)


def _outproj_kernel(x_ref, yg_ref, att_ref, wglu_ref, wout_ref, g_ref, b_ref, h_ref):
    yg = yg_ref[...]
    z = yg * jax.nn.sigmoid(_dot(yg.astype(bf16), wglu_ref[...]))
    cat = jnp.concatenate([z.astype(bf16), att_ref[...]], axis=1)
    r = DN_ALPHA * x_ref[...] + _dot(cat, wout_ref[...])
    h_ref[...] = _layernorm(r, g_ref[...], b_ref[...])


def _outproj(x2, yg, att, wglu_b, wout_b, g, b):
    n, d_model = x2.shape
    return pl.pallas_call(
        _outproj_kernel,
        grid=(n // TM_PROJ,),
        in_specs=[pl.BlockSpec((TM_PROJ, d_model), lambda i: (i, 0)),
                  pl.BlockSpec((TM_PROJ, D_SSM), lambda i: (i, 0)),
                  pl.BlockSpec((TM_PROJ, D_ATT), lambda i: (i, 0)),
                  pl.BlockSpec(wglu_b.shape, lambda i: (0, 0)),
                  pl.BlockSpec(wout_b.shape, lambda i: (0, 0)),
                  pl.BlockSpec((1, d_model), lambda i: (0, 0)),
                  pl.BlockSpec((1, d_model), lambda i: (0, 0))],
        out_specs=pl.BlockSpec((TM_PROJ, d_model), lambda i: (i, 0)),
        out_shape=jax.ShapeDtypeStruct((n, d_model), f32),
        compiler_params=pltpu.CompilerParams(dimension_semantics=("parallel",),
                                             vmem_limit_bytes=VMEM_LIMIT),
        name="outproj",
    )(x2, yg, att, wglu_b, wout_b, g, b)


def _bitonic_sort_desc(v):
    n = len(v)
    k = 2
    while k <= n:
        j = k // 2
        while j >= 1:
            for a in range(n):
                b = a ^ j
                if b > a:
                    hi = jnp.maximum(v[a], v[b])
                    lo = jnp.minimum(v[a], v[b])
                    v[a], v[b] = (hi, lo) if (a & k) == 0 else (lo, hi)
            j //= 2
        k *= 2
    return v


def _bitonic_merge_desc(v):
    j = len(v) // 2
    while j >= 1:
        for a in range(len(v)):
            b = a ^ j
            if b > a:
                v[a], v[b] = jnp.maximum(v[a], v[b]), jnp.minimum(v[a], v[b])
        j //= 2
    return v


def _top16_sorted(st):
    k = PEER_TOPK
    x = _bitonic_sort_desc([st[SUBLANES * a:SUBLANES * (a + 1), :] for a in range(PEER_N_KEYS // SUBLANES)])
    for shift in (4, 2, 1):
        y = [pltpu.roll(x[k - 1 - a], shift, 0) for a in range(k)]
        x = _bitonic_merge_desc([jnp.maximum(x[a], y[a]) for a in range(k)])
    return x


def _dup_bf16(x):
    u = pltpu.bitcast(x.astype(bf16).astype(f32), jnp.uint32)
    return pltpu.bitcast(u | (u >> 16), f32)


def _bf16_rows(row):
    return pltpu.bitcast(jnp.broadcast_to(row, (SUBLANES, row.shape[1])), bf16)[None]


def _peer_kernel(n_exp, us_ref, h_ref, wq_ref, sk_ref, u_ref, vt_ref, g_ref, b_ref, o_ref,
                 q_scr, pr_scr, rq_scr, srt_scr, row_scr, acc_scr, hbt_scr, a_scr, unscale_scr):
    eb = pl.program_id(1)
    n_blk = n_exp // EB_PEER
    tt = h_ref.shape[0]
    nk = PEER_N_KEYS
    k = PEER_TOPK

    chunk = 2 * nk
    n_chunk = EB_PEER // chunk

    def activations():
        for s in range(n_chunk):
            a_scr[s] = _dot(u_ref[s * chunk:(s + 1) * chunk, :], hbt_scr[...])

    @pl.when(eb == 0)
    def _prologue():
        ht = h_ref[...].T
        amax = jnp.maximum(jnp.max(jnp.abs(ht), keepdims=True), jnp.finfo(f32).tiny)
        e_tile = jnp.minimum(jnp.floor(jnp.log2(FP8_TARGET_MAX / amax)), FP8_MAX_SHIFT)
        hbt_scr[...] = (ht * jnp.exp2(e_tile)).astype(fp8)
        unscale_scr[0] = (jnp.exp2(-e_tile) * us_ref[0, 0])[0, 0]
        q = _dot(h_ref[...].astype(bf16), wq_ref[...])
        for hc in range(2 * PEER_HEADS):
            q_scr[hc] = q[:, hc * PEER_HALF:(hc + 1) * PEER_HALF]

        def head_sort(h, carry):
            for c in range(2):
                q_hi, q_lo = _split_bf16(q_scr[2 * h + c])
                k_hi, k_lo = _split_bf16(sk_ref[2 * h + c])
                st = _dot_nt(k_hi, q_hi) + _dot_nt(k_hi, q_lo) + _dot_nt(k_lo, q_hi)
                pr_scr[c, h] = st
                srt = _top16_sorted(st)
                for a in range(k):
                    srt_scr[c, a, pl.ds(h, 1), :] = srt[a][0:1, :]
            return carry
        lax.fori_loop(0, PEER_HEADS, head_sort, 0)

        a_s = [srt_scr[0, a] for a in range(k)]
        b_s = [srt_scr[1, a] for a in range(k)]
        cands = [a_s[a] + b_s[b] for a in range(k) for b in range(k) if (a + 1) * (b + 1) <= k]
        cands += [jnp.full_like(cands[0], -jnp.inf)] * (64 - len(cands))
        cs = _bitonic_sort_desc(cands)
        zsum = jnp.zeros_like(cs[0])
        for a in range(k):
            zsum = zsum + jnp.exp(cs[a] - cs[0])
        row_scr[0] = 0.5 * (cs[k - 1] + cs[k])
        row_scr[1] = 1.0 / zsum
        row_scr[2] = a_s[0]
        row_scr[3] = b_s[0]
        row_scr[4] = a_s[k - 1]
        row_scr[5] = b_s[k - 1]

        def head_prep(h, carry):
            hrow = pl.ds(h, 1)
            at = pr_scr[0, h]
            bt = pr_scr[1, h]
            thr = row_scr[0, hrow, :] - at
            r_cnt = jnp.zeros_like(at)
            b_rank = jnp.zeros_like(bt)
            for a in range(k):
                b_a = srt_scr[1, a, hrow, :]
                r_cnt = jnp.where(b_a > thr, float(a + 1), r_cnt)
                b_rank = jnp.where(b_a > bt, float(a + 1), b_rank)
            in_a = at >= row_scr[4, hrow, :]
            p_w = jnp.where(in_a, jnp.exp(at - row_scr[2, hrow, :]), 0.0) * row_scr[1, hrow, :]
            pr_scr[0, h] = _dup_bf16(p_w)
            pr_scr[1, h] = _dup_bf16(jnp.where(in_a, r_cnt, 0.0))
            q_w = jnp.where(bt >= row_scr[5, hrow, :], jnp.exp(bt - row_scr[3, hrow, :]), 0.0)
            rq_scr[0, h] = b_rank.astype(bf16).reshape(nk // BF16_ROWS, BF16_ROWS, tt)
            rq_scr[1, h] = q_w.astype(bf16).reshape(nk // BF16_ROWS, BF16_ROWS, tt)
            return carry
        lax.fori_loop(0, PEER_HEADS, head_prep, 0)
        acc_scr[...] = jnp.zeros_like(acc_scr)

    activations()
    for s in range(n_chunk):
        halves = []
        for half in range(chunk // nk):
            n1 = pl.ds((EB_PEER // nk) * eb + s * (chunk // nk) + half, 1)
            g = jnp.zeros((nk // BF16_ROWS, BF16_ROWS, tt), bf16)
            for h in range(PEER_HEADS):
                p_row = _bf16_rows(pr_scr[0, h, n1, :])
                r_row = _bf16_rows(pr_scr[1, h, n1, :])
                g = g + p_row * jnp.where(rq_scr[0, h] < r_row, rq_scr[1, h], 0)
            halves.append(g.reshape(nk, tt))
        act = a_scr[s].astype(bf16) * unscale_scr[0].astype(bf16)
        w = jnp.concatenate(halves, axis=0) * jax.nn.gelu(act)
        acc_scr[...] += _dot(vt_ref[:, s * chunk:(s + 1) * chunk], w)

    @pl.when(eb == n_blk - 1)
    def _last():
        r = DN_ALPHA * h_ref[...] + acc_scr[...].T
        o_ref[...] = _layernorm(r, g_ref[...], b_ref[...])


def _peer(h1, wq_b, sub_keys, u_q, u_unscale, vt_b, g, b):
    n, d_model = h1.shape
    n_exp = u_q.shape[0]
    tt = TT_PEER
    sk = sub_keys.reshape(2 * PEER_HEADS, PEER_N_KEYS, PEER_HALF).astype(f32)
    n_blk = n_exp // EB_PEER
    const = dict(pipeline_mode=pl.Buffered(1))
    return pl.pallas_call(
        functools.partial(_peer_kernel, n_exp),
        grid=(n // tt, n_blk),
        in_specs=[pl.BlockSpec(memory_space=pltpu.SMEM),
                  pl.BlockSpec((tt, d_model), lambda t, e: (t, 0)),
                  pl.BlockSpec(wq_b.shape, lambda t, e: (0, 0), **const),
                  pl.BlockSpec(sk.shape, lambda t, e: (0, 0, 0), **const),
                  pl.BlockSpec((EB_PEER, d_model), lambda t, e: (e, 0)),
                  pl.BlockSpec((d_model, EB_PEER), lambda t, e: (0, e)),
                  pl.BlockSpec((1, d_model), lambda t, e: (0, 0), **const),
                  pl.BlockSpec((1, d_model), lambda t, e: (0, 0), **const)],
        out_specs=pl.BlockSpec((tt, d_model), lambda t, e: (t, 0)),
        out_shape=jax.ShapeDtypeStruct((n, d_model), f32),
        scratch_shapes=[pltpu.VMEM((2 * PEER_HEADS, tt, PEER_HALF), f32),
                        pltpu.VMEM((2, PEER_HEADS, PEER_N_KEYS, tt), f32),
                        pltpu.VMEM((2, PEER_HEADS, PEER_N_KEYS // BF16_ROWS, BF16_ROWS, tt), bf16),
                        pltpu.VMEM((2, PEER_TOPK, PEER_HEADS, tt), f32),
                        pltpu.VMEM((6, PEER_HEADS, tt), f32),
                        pltpu.VMEM((d_model, tt), f32),
                        pltpu.VMEM((d_model, tt), fp8),
                        pltpu.VMEM((EB_PEER // (2 * PEER_N_KEYS), 2 * PEER_N_KEYS, tt), f32),
                        pltpu.SMEM((1,), f32)],
        compiler_params=pltpu.CompilerParams(dimension_semantics=("parallel", "arbitrary"),
                                             vmem_limit_bytes=VMEM_LIMIT),
        name="peer",
    )(u_unscale.reshape(1, 1), h1, wq_b, sk, u_q, vt_b, g, b)


def _quantize_table(t):
    amax = jnp.maximum(jnp.max(jnp.abs(t)), jnp.finfo(f32).tiny)
    e = jnp.minimum(jnp.floor(jnp.log2(FP8_TARGET_MAX / amax)), FP8_MAX_SHIFT)
    return (t * jnp.exp2(e)).astype(fp8), jnp.exp2(-e).astype(f32)


def kernel(x, w_in, ssm_a_re, ssm_a_im, ssm_log_dt, ssm_b_re, ssm_b_im, ssm_c_re, ssm_c_im, ssm_d,
           ssm_w_glu, w_out, ln1_g, ln1_b, peer_w_q, peer_sub_keys, peer_u, peer_v, ln2_g, ln2_b):
    bsz, seq, d_model = x.shape
    n = bsz * seq
    h = x.reshape(n, d_model)
    for l in range(w_in.shape[0]):
        u2, k2, qt, vt, kmean = _inproj(h, w_in[l])
        ops = _s5_operators(ssm_a_re[l], ssm_a_im[l], ssm_log_dt[l], ssm_b_re[l], ssm_b_im[l],
                            ssm_c_re[l], ssm_c_im[l], ssm_d[l])
        yg = _s5(u2, bsz, seq, ops)
        att = _moba(qt, k2, vt, kmean.reshape(bsz, seq // MOBA_BLOCK, D_ATT), bsz, seq)
        h1 = _outproj(h, yg, att, ssm_w_glu[l].astype(bf16), w_out[l].astype(bf16),
                      ln1_g[l].reshape(1, d_model), ln1_b[l].reshape(1, d_model))
        u_q, u_unscale = _quantize_table(peer_u[l])
        h = _peer(h1, peer_w_q[l].astype(bf16), peer_sub_keys[l], u_q, u_unscale,
                  peer_v[l].T.astype(bf16), ln2_g[l].reshape(1, d_model), ln2_b[l].reshape(1, d_model))
    return h.reshape(bsz, seq, d_model)
```

```python
import functools
import math

import jax
import jax.numpy as jnp
from jax import lax
from jax.experimental import pallas as pl
from jax.experimental.pallas import tpu as pltpu

f32 = jnp.float32
bf16 = jnp.bfloat16
fp8 = jnp.float8_e4m3fn
FP8_TARGET_MAX = 256.0
FP8_MAX_SHIFT = 48.0

D_SSM = 512
SSM_GROUP = 16
SSM_STATE = 64
N_ATT_HEADS = 8
ATT_HEAD_DIM = 64
D_ATT = N_ATT_HEADS * ATT_HEAD_DIM
MOBA_BLOCK = 256
MOBA_TOPK = 3
PEER_HEADS = 8
PEER_HALF = 128
PEER_N_KEYS = 128
PEER_TOPK = 16
LN_EPS = 1e-5
DEPTH = 1
DN_ALPHA = (2.0 * DEPTH) ** 0.25
NEG = -1e30

LANES = 128
SUBLANES = 8
BF16_ROWS = 2 * SUBLANES
VMEM_LIMIT = 48 * 1024 * 1024

TM_PROJ = 1024
S5_CHUNK = 8
S5_LANE_GROUPS = LANES // SSM_GROUP
TT_PEER = 512
EB_PEER = 2048


def _dot(a, b):
    return jnp.dot(a, b, preferred_element_type=f32)


def _dot_nt(a, b):
    return lax.dot_general(a, b, (((1,), (1,)), ((), ())), preferred_element_type=f32)


def _split_bf16(x):
    hi = x.astype(bf16)
    lo = (x - hi.astype(f32)).astype(bf16)
    return hi, lo


def _layernorm(r, g, b):
    mu = jnp.mean(r, axis=-1, keepdims=True)
    d = r - mu
    var = jnp.mean(d * d, axis=-1, keepdims=True)
    return d * lax.rsqrt(var + LN_EPS) * g + b


def _inproj_kernel(x_ref, wuk_ref, wqvt_ref, u_ref, k_ref, qt_ref, vt_ref, km_ref):
    xb = x_ref[...].astype(bf16)
    acc = _dot(xb, wuk_ref[...])
    u_ref[...] = acc[:, :D_SSM]
    kk = acc[:, D_SSM:]
    k_ref[...] = kk.astype(bf16)
    qvt = _dot_nt(wqvt_ref[...], xb)
    for r in range(TM_PROJ // MOBA_BLOCK):
        cols = slice(r * MOBA_BLOCK, (r + 1) * MOBA_BLOCK)
        km_ref[r] = jnp.mean(kk[cols], axis=0, keepdims=True)
        qt_ref[r] = qvt[:D_ATT, cols].astype(bf16)
        vt_ref[r] = qvt[D_ATT:, cols].astype(bf16)


def _inproj(x2, w_in):
    n, d_model = x2.shape
    nb = TM_PROJ // MOBA_BLOCK
    w_uk = jnp.concatenate([w_in[:, :D_SSM], w_in[:, D_SSM + D_ATT:D_SSM + 2 * D_ATT]], axis=1).astype(bf16)
    w_qvt = jnp.concatenate([w_in[:, D_SSM:D_SSM + D_ATT], w_in[:, D_SSM + 2 * D_ATT:]], axis=1).T.astype(bf16)
    return pl.pallas_call(
        _inproj_kernel,
        grid=(n // TM_PROJ,),
        in_specs=[pl.BlockSpec((TM_PROJ, d_model), lambda i: (i, 0)),
                  pl.BlockSpec(w_uk.shape, lambda i: (0, 0)),
                  pl.BlockSpec(w_qvt.shape, lambda i: (0, 0))],
        out_specs=[pl.BlockSpec((TM_PROJ, D_SSM), lambda i: (i, 0)),
                   pl.BlockSpec((TM_PROJ, D_ATT), lambda i: (i, 0)),
                   pl.BlockSpec((nb, D_ATT, MOBA_BLOCK), lambda i: (i, 0, 0)),
                   pl.BlockSpec((nb, D_ATT, MOBA_BLOCK), lambda i: (i, 0, 0)),
                   pl.BlockSpec((nb, 1, D_ATT), lambda i: (i, 0, 0))],
        out_shape=[jax.ShapeDtypeStruct((n, D_SSM), f32),
                   jax.ShapeDtypeStruct((n, D_ATT), bf16),
                   jax.ShapeDtypeStruct((n // MOBA_BLOCK, D_ATT, MOBA_BLOCK), bf16),
                   jax.ShapeDtypeStruct((n // MOBA_BLOCK, D_ATT, MOBA_BLOCK), bf16),
                   jax.ShapeDtypeStruct((n // MOBA_BLOCK, 1, D_ATT), f32)],
        compiler_params=pltpu.CompilerParams(dimension_semantics=("parallel",),
                                             vmem_limit_bytes=VMEM_LIMIT),
        name="inproj",
    )(x2, w_uk, w_qvt)


def _s5_operators(a_re, a_im, log_dt, b_re, b_im, c_re, c_im, d_skip):
    T = S5_CHUNK
    G, P = a_re.shape
    H = SSM_GROUP
    nj = G // S5_LANE_GROUPS
    g8 = S5_LANE_GROUPS
    lam = lax.complex(a_re.astype(f32), a_im.astype(f32))
    dt = jnp.exp(log_dt.astype(f32))[:, None]
    lam_bar = jnp.exp(lam * dt)
    b_c = lax.complex(b_re.astype(f32), b_im.astype(f32))
    b_bar = ((lam_bar - 1.0) / lam)[:, :, None] * b_c
    c_c = lax.complex(c_re.astype(f32), c_im.astype(f32))
    pows = jnp.exp((lam * dt)[None] * jnp.arange(T + 1, dtype=f32)[:, None, None])
    def widen(x, reps):
        n = x.shape[-1]
        rep = jnp.tile(jnp.eye(n, dtype=f32), (1, reps))
        return jnp.dot(x.reshape(-1, n), rep).reshape(x.shape[:-1] + (reps * n,))

    same = jnp.arange(g8)[:, None] == (jnp.arange(g8 * H) // H)[None, :]
    same_p = jnp.arange(g8)[:, None] == (jnp.arange(g8 * P) // P)[None, :]
    kern = jnp.real(jnp.einsum('ghp,dgp,gpk->dgkh', c_c, pows[:T], b_bar))
    dd = jnp.arange(T)[None, :] - jnp.arange(T)[:, None]
    kst = jnp.where((dd >= 0)[:, :, None, None, None], kern[jnp.clip(dd, 0, T - 1)], 0.0)
    kst = widen(kst.reshape(T, T, nj, g8, H, H), g8)
    kst = jnp.where(same[None, None, None, :, None, :], kst, 0.0)
    toep = kst.transpose(2, 0, 3, 4, 1, 5).reshape(nj, T * LANES, T * LANES)
    win = pows[T - 1 - jnp.arange(T)][:, :, None, :] * jnp.swapaxes(b_bar, 1, 2)[None]
    win = jnp.stack([jnp.real(win), jnp.imag(win)], 0).reshape(2, T, nj, g8, H, P)
    win = jnp.where(same_p[None, None, None, :, None, :], widen(win, g8), 0.0)
    m_in = win.transpose(2, 1, 3, 4, 0, 5).reshape(nj, T * LANES, 2 * g8 * P)
    wout = jnp.swapaxes(c_c, 1, 2)[None] * pows[1:T + 1][:, :, :, None]
    wout = jnp.stack([jnp.real(wout), -jnp.imag(wout)], 0).reshape(2, T, nj, g8, P, H)
    wout = jnp.where(same[None, None, None, :, None, :], widen(wout, g8), 0.0)
    m_out = wout.transpose(2, 0, 3, 4, 1, 5).reshape(nj, 2 * g8 * P, T * LANES)
    a_chunk = pows[T].reshape(nj, 1, g8 * P)
    a_ri = jnp.concatenate([jnp.real(a_chunk), jnp.imag(a_chunk)], axis=1)
    d_flat = jnp.tile(d_skip.astype(f32).reshape(nj, 1, LANES), (1, 1, T))
    return toep.astype(bf16), m_in.astype(bf16), m_out.astype(bf16), a_ri, d_flat


def _s5_kernel(u_ref, toep_ref, min_ref, mout_ref, a_ref, d_ref, y_ref, in_scr, sp_scr):
    T = S5_CHUNK
    n_c = u_ref.shape[0] // T
    half = in_scr.shape[1] // 2
    uf = jnp.concatenate([u_ref[pl.ds(t, n_c, stride=T), :] for t in range(T)], axis=1)
    ufb = uf.astype(bf16)
    y = _dot(ufb, toep_ref[0]) + uf * d_ref[0]
    in_scr[...] = _dot(ufb, min_ref[0])
    ar = a_ref[0, 0:1, :]
    ai = a_ref[0, 1:2, :]

    def step(c, carry):
        sr, si = carry
        sp_scr[pl.ds(c, 1), :half] = sr
        sp_scr[pl.ds(c, 1), half:] = si
        ir = in_scr[pl.ds(c, 1), :half]
        ii = in_scr[pl.ds(c, 1), half:]
        return ar * sr - ai * si + ir, ar * si + ai * sr + ii

    z = jnp.zeros((1, half), f32)
    lax.fori_loop(0, n_c, step, (z, z), unroll=4)
    y = y + _dot(sp_scr[...].astype(bf16), mout_ref[0])
    yg = jax.nn.gelu(y)
    for t in range(T):
        y_ref[pl.ds(t, n_c, stride=T), :] = yg[:, t * LANES:(t + 1) * LANES]


def _s5(u2, bsz, seq, ops):
    toep, m_in, m_out, a_ri, d_flat = ops
    nj = toep.shape[0]
    n_c = seq // S5_CHUNK
    kdim = S5_CHUNK * LANES
    sdim = m_in.shape[2]
    return pl.pallas_call(
        _s5_kernel,
        grid=(nj, bsz),
        in_specs=[pl.BlockSpec((seq, LANES), lambda j, b: (b, j)),
                  pl.BlockSpec((1, kdim, kdim), lambda j, b: (j, 0, 0)),
                  pl.BlockSpec((1, kdim, sdim), lambda j, b: (j, 0, 0)),
                  pl.BlockSpec((1, sdim, kdim), lambda j, b: (j, 0, 0)),
                  pl.BlockSpec((1, 2, sdim // 2), lambda j, b: (j, 0, 0)),
                  pl.BlockSpec((1, 1, kdim), lambda j, b: (j, 0, 0))],
        out_specs=pl.BlockSpec((seq, LANES), lambda j, b: (b, j)),
        out_shape=jax.ShapeDtypeStruct(u2.shape, f32),
        scratch_shapes=[pltpu.VMEM((n_c, sdim), f32), pltpu.VMEM((n_c, sdim), f32)],
        compiler_params=pltpu.CompilerParams(dimension_semantics=("parallel", "parallel"),
                                             vmem_limit_bytes=VMEM_LIMIT),
        name="s5",
    )(u2, toep, m_in, m_out, a_ri, d_flat)


def _moba_kernel(qt_ref, k_ref, vt_ref, km_ref, o_ref, kaug_ref, s0_scr, s1_scr, acc_scr):
    blk = MOBA_BLOCK
    hd = ATT_HEAD_DIM
    nb = k_ref.shape[0] // blk
    i = pl.program_id(2)

    @pl.when(i == 0)
    def _build():
        lane = lax.broadcasted_iota(jnp.int32, (blk, LANES), 1)

        def build(j, carry):
            rows = pl.ds(pl.multiple_of(j * blk, blk), blk)
            kaug_ref[rows, :LANES] = k_ref[rows, :]
            kaug_ref[rows, LANES:] = jnp.where(lane == j, 1.0, 0.0).astype(bf16)
            return carry
        lax.fori_loop(0, nb, build, 0)

    def finish():
        acc = acc_scr[...]
        out = acc[:hd] / acc[hd:hd + 1]
        o_ref[...] = jnp.concatenate([out[:, :blk], out[:, blk:]], axis=0).T.astype(o_ref.dtype)

    def attend():
        _moba_attend(qt_ref, k_ref, vt_ref, km_ref, kaug_ref, s0_scr, s1_scr, acc_scr)

    @pl.when(i == 0)
    def _first():
        attend()

    @pl.when((i > 0) & (i < nb))
    def _middle():
        finish()
        attend()

    @pl.when(i == nb)
    def _last():
        finish()


def _moba_attend(qt_ref, k_ref, vt_ref, km_ref, kaug_ref, s0_scr, s1_scr, acc_scr):
    blk = MOBA_BLOCK
    hd = ATT_HEAD_DIM
    nb = k_ref.shape[0] // blk
    i = pl.program_id(2)
    qt = qt_ref[0]
    qtf = qt.astype(f32) * (hd ** -0.5 * math.log2(math.e))
    km = km_ref[0]
    lane_nb = lax.broadcasted_iota(jnp.int32, (nb, LANES), 1)
    feat = lax.broadcasted_iota(jnp.int32, (LANES, blk), 0)
    blk_id = lax.broadcasted_iota(jnp.int32, (nb, blk), 0)
    blk_f = blk_id.astype(f32)
    q_cols, bias_cols = [], []
    for hh in range(2):
        km_hi, km_lo = _split_bf16(jnp.where((lane_nb < hd) if hh == 0 else (lane_nb >= hd), km, 0.0))
        gate = _dot(km_hi, qt) + _dot(km_lo, qt)
        g = jnp.where(blk_id < i, gate, -jnp.inf)
        sel = blk_id == i
        for _ in range(MOBA_TOPK):
            mx = jnp.max(g, axis=0, keepdims=True)
            first = jnp.min(jnp.where(g == mx, blk_f, float(nb)), axis=0, keepdims=True)
            pick = (blk_f == first) & (mx > -jnp.inf)
            sel = sel | pick
            g = jnp.where(pick, -jnp.inf, g)
        bias_cols.append(jnp.where(sel, 0.0, NEG))
        q_cols.append(jnp.where((feat < hd) if hh == 0 else (feat >= hd), qtf, 0.0))
    qa = jnp.concatenate([jnp.concatenate(q_cols, axis=1), jnp.concatenate(bias_cols, axis=1),
                          jnp.zeros((LANES - nb, 2 * blk), f32)], axis=0).astype(bf16)

    ones_rows = jnp.ones((BF16_ROWS, blk), bf16)
    key_pos = lax.broadcasted_iota(jnp.int32, (blk, 2 * blk), 0)
    q_pos = lax.broadcasted_iota(jnp.int32, (blk, 2 * blk), 1) % blk

    def scores(j):
        j = jnp.minimum(j, nb - 1)
        return _dot(kaug_ref[pl.ds(pl.multiple_of(j * blk, blk), blk), :], qa)

    def update(state, blocks):
        m, acc = state
        m_new = m
        for _, s in blocks:
            m_new = jnp.maximum(m_new, jnp.max(s, axis=0, keepdims=True))
        acc = jnp.exp2(m - m_new) * acc
        for j, s in blocks:
            pb = jnp.exp2(s - m_new).astype(bf16)
            acc = acc + jnp.concatenate(
                [_dot(jnp.concatenate([vt_ref[j, :hd, :], ones_rows], axis=0), pb[:, :blk]),
                 _dot(jnp.concatenate([vt_ref[j, hd:, :], ones_rows], axis=0), pb[:, blk:])], axis=1)
        return m_new, acc

    def pair_scores(p, dst):
        dst[0] = scores(2 * p)
        dst[1] = scores(2 * p + 1)

    def pair_update(state, p, src, masked):
        blocks = []
        for r in range(2):
            j = 2 * p + r
            s = src[r]
            if masked:
                off = jnp.where(j < i, blk, jnp.where(j == i, 0, -2 * blk))
                s = jnp.where(key_pos <= q_pos + off, s, NEG)
            blocks.append((jnp.minimum(j, nb - 1), s))
        return update(state, blocks)

    def quad_body(q, state):
        pair_scores(2 * q + 1, s1_scr)
        state = pair_update(state, 2 * q, s0_scr, False)
        pair_scores(2 * q + 2, s0_scr)
        return pair_update(state, 2 * q + 1, s1_scr, False)

    state = (jnp.full((1, 2 * blk), -jnp.inf, f32), jnp.zeros((hd + BF16_ROWS, 2 * blk), f32))
    n_quad = i // 4
    pair_scores(0, s0_scr)
    state = lax.fori_loop(0, n_quad, quad_body, state)
    pair_scores(2 * n_quad + 1, s1_scr)
    state = pair_update(state, 2 * n_quad, s0_scr, True)
    m, acc = pair_update(state, 2 * n_quad + 1, s1_scr, True)
    acc_scr[...] = acc


def _moba(qt, k2, vt, kmean, bsz, seq):
    nb = seq // MOBA_BLOCK
    assert seq % MOBA_BLOCK == 0 and nb % SUBLANES == 0 and nb <= LANES
    npair = D_ATT // LANES
    return pl.pallas_call(
        _moba_kernel,
        grid=(bsz, npair, nb + 1),
        in_specs=[pl.BlockSpec((1, LANES, MOBA_BLOCK), lambda b, h, i: (b * nb + jnp.minimum(i, nb - 1), h, 0)),
                  pl.BlockSpec((seq, LANES), lambda b, h, i: (b, h)),
                  pl.BlockSpec((nb, LANES, MOBA_BLOCK), lambda b, h, i: (b, h, 0)),
                  pl.BlockSpec((1, nb, LANES), lambda b, h, i: (b, 0, h))],
        out_specs=pl.BlockSpec((MOBA_BLOCK, LANES), lambda b, h, i: (b * nb + jnp.maximum(i - 1, 0), h)),
        out_shape=jax.ShapeDtypeStruct(k2.shape, bf16),
        scratch_shapes=[pltpu.VMEM((seq, 2 * LANES), bf16),
                        pltpu.VMEM((2, MOBA_BLOCK, 2 * MOBA_BLOCK), f32),
                        pltpu.VMEM((2, MOBA_BLOCK, 2 * MOBA_BLOCK), f32),
                        pltpu.VMEM((ATT_HEAD_DIM + BF16_ROWS, 2 * MOBA_BLOCK), f32)],
        compiler_params=pltpu.CompilerParams(dimension_semantics=("parallel", "parallel", "arbitrary"),
                                             vmem_limit_bytes=VMEM_LIMIT),
        name="moba",
    )(qt, k2, vt, kmean)


def _outproj_kernel(x_ref, yg_ref, att_ref, wglu_ref, wout_ref, g_ref, b_ref, h_ref):
    yg = yg_ref[...]
    z = yg * jax.nn.sigmoid(_dot(yg.astype(bf16), wglu_ref[...]))
    cat = jnp.concatenate([z.astype(bf16), att_ref[...]], axis=1)
    r = DN_ALPHA * x_ref[...] + _dot(cat, wout_ref[...])
    h_ref[...] = _layernorm(r, g_ref[...], b_ref[...])


def _outproj(x2, yg, att, wglu_b, wout_b, g, b):
    n, d_model = x2.shape
    return pl.pallas_call(
        _outproj_kernel,
        grid=(n // TM_PROJ,),
        in_specs=[pl.BlockSpec((TM_PROJ, d_model), lambda i: (i, 0)),
                  pl.BlockSpec((TM_PROJ, D_SSM), lambda i: (i, 0)),
                  pl.BlockSpec((TM_PROJ, D_ATT), lambda i: (i, 0)),
                  pl.BlockSpec(wglu_b.shape, lambda i: (0, 0)),
                  pl.BlockSpec(wout_b.shape, lambda i: (0, 0)),
                  pl.BlockSpec((1, d_model), lambda i: (0, 0)),
                  pl.BlockSpec((1, d_model), lambda i: (0, 0))],
        out_specs=pl.BlockSpec((TM_PROJ, d_model), lambda i: (i, 0)),
        out_shape=jax.ShapeDtypeStruct((n, d_model), f32),
        compiler_params=pltpu.CompilerParams(dimension_semantics=("parallel",),
                                             vmem_limit_bytes=VMEM_LIMIT),
        name="outproj",
    )(x2, yg, att, wglu_b, wout_b, g, b)


def _bitonic_sort_desc(v):
    n = len(v)
    k = 2
    while k <= n:
        j = k // 2
        while j >= 1:
            for a in range(n):
                b = a ^ j
                if b > a:
                    hi = jnp.maximum(v[a], v[b])
                    lo = jnp.minimum(v[a], v[b])
                    v[a], v[b] = (hi, lo) if (a & k) == 0 else (lo, hi)
            j //= 2
        k *= 2
    return v


def _bitonic_merge_desc(v):
    j = len(v) // 2
    while j >= 1:
        for a in range(len(v)):
            b = a ^ j
            if b > a:
                v[a], v[b] = jnp.maximum(v[a], v[b]), jnp.minimum(v[a], v[b])
        j //= 2
    return v


def _top16_sorted(st):
    k = PEER_TOPK
    x = _bitonic_sort_desc([st[SUBLANES * a:SUBLANES * (a + 1), :] for a in range(PEER_N_KEYS // SUBLANES)])
    for shift in (4, 2, 1):
        y = [pltpu.roll(x[k - 1 - a], shift, 0) for a in range(k)]
        x = _bitonic_merge_desc([jnp.maximum(x[a], y[a]) for a in range(k)])
    return x


def _dup_bf16(x):
    u = pltpu.bitcast(x.astype(bf16).astype(f32), jnp.uint32)
    return pltpu.bitcast(u | (u >> 16), f32)


def _bf16_rows(row):
    return pltpu.bitcast(jnp.broadcast_to(row, (SUBLANES, row.shape[1])), bf16)[None]


def _peer_kernel(n_exp, us_ref, h_ref, wq_ref, sk_ref, u_ref, vt_ref, g_ref, b_ref, o_ref,
                 q_scr, pr_scr, rq_scr, srt_scr, row_scr, acc_scr, hbt_scr, a_scr, unscale_scr):
    eb = pl.program_id(1)
    n_blk = n_exp // EB_PEER
    tt = h_ref.shape[0]
    nk = PEER_N_KEYS
    k = PEER_TOPK

    chunk = 2 * nk
    n_chunk = EB_PEER // chunk

    def activations():
        for s in range(n_chunk):
            a_scr[s] = _dot(u_ref[s * chunk:(s + 1) * chunk, :], hbt_scr[...])

    @pl.when(eb == 0)
    def _prologue():
        ht = h_ref[...].T
        amax = jnp.maximum(jnp.max(jnp.abs(ht), keepdims=True), jnp.finfo(f32).tiny)
        e_tile = jnp.minimum(jnp.floor(jnp.log2(FP8_TARGET_MAX / amax)), FP8_MAX_SHIFT)
        hbt_scr[...] = (ht * jnp.exp2(e_tile)).astype(fp8)
        unscale_scr[0] = (jnp.exp2(-e_tile) * us_ref[0, 0])[0, 0]
        q = _dot(h_ref[...].astype(bf16), wq_ref[...])
        for hc in range(2 * PEER_HEADS):
            q_scr[hc] = q[:, hc * PEER_HALF:(hc + 1) * PEER_HALF]

        def head_sort(h, carry):
            for c in range(2):
                q_hi, q_lo = _split_bf16(q_scr[2 * h + c])
                k_hi, k_lo = _split_bf16(sk_ref[2 * h + c])
                st = _dot_nt(k_hi, q_hi) + _dot_nt(k_hi, q_lo) + _dot_nt(k_lo, q_hi)
                pr_scr[c, h] = st
                srt = _top16_sorted(st)
                for a in range(k):
                    srt_scr[c, a, pl.ds(h, 1), :] = srt[a][0:1, :]
            return carry
        lax.fori_loop(0, PEER_HEADS, head_sort, 0)

        a_s = [srt_scr[0, a] for a in range(k)]
        b_s = [srt_scr[1, a] for a in range(k)]
        cands = [a_s[a] + b_s[b] for a in range(k) for b in range(k) if (a + 1) * (b + 1) <= k]
        cands += [jnp.full_like(cands[0], -jnp.inf)] * (64 - len(cands))
        cs = _bitonic_sort_desc(cands)
        zsum = jnp.zeros_like(cs[0])
        for a in range(k):
            zsum = zsum + jnp.exp(cs[a] - cs[0])
        row_scr[0] = 0.5 * (cs[k - 1] + cs[k])
        row_scr[1] = 1.0 / zsum
        row_scr[2] = a_s[0]
        row_scr[3] = b_s[0]
        row_scr[4] = a_s[k - 1]
        row_scr[5] = b_s[k - 1]

        def head_prep(h, carry):
            hrow = pl.ds(h, 1)
            at = pr_scr[0, h]
            bt = pr_scr[1, h]
            thr = row_scr[0, hrow, :] - at
            r_cnt = jnp.zeros_like(at)
            b_rank = jnp.zeros_like(bt)
            for a in range(k):
                b_a = srt_scr[1, a, hrow, :]
                r_cnt = jnp.where(b_a > thr, float(a + 1), r_cnt)
                b_rank = jnp.where(b_a > bt, float(a + 1), b_rank)
            in_a = at >= row_scr[4, hrow, :]
            p_w = jnp.where(in_a, jnp.exp(at - row_scr[2, hrow, :]), 0.0) * row_scr[1, hrow, :]
            pr_scr[0, h] = _dup_bf16(p_w)
            pr_scr[1, h] = _dup_bf16(jnp.where(in_a, r_cnt, 0.0))
            q_w = jnp.where(bt >= row_scr[5, hrow, :], jnp.exp(bt - row_scr[3, hrow, :]), 0.0)
            rq_scr[0, h] = b_rank.astype(bf16).reshape(nk // BF16_ROWS, BF16_ROWS, tt)
            rq_scr[1, h] = q_w.astype(bf16).reshape(nk // BF16_ROWS, BF16_ROWS, tt)
            return carry
        lax.fori_loop(0, PEER_HEADS, head_prep, 0)
        acc_scr[...] = jnp.zeros_like(acc_scr)

    activations()
    for s in range(n_chunk):
        halves = []
        for half in range(chunk // nk):
            n1 = pl.ds((EB_PEER // nk) * eb + s * (chunk // nk) + half, 1)
            g = jnp.zeros((nk // BF16_ROWS, BF16_ROWS, tt), bf16)
            for h in range(PEER_HEADS):
                p_row = _bf16_rows(pr_scr[0, h, n1, :])
                r_row = _bf16_rows(pr_scr[1, h, n1, :])
                g = g + p_row * jnp.where(rq_scr[0, h] < r_row, rq_scr[1, h], 0)
            halves.append(g.reshape(nk, tt))
        act = a_scr[s].astype(bf16) * unscale_scr[0].astype(bf16)
        w = jnp.concatenate(halves, axis=0) * jax.nn.gelu(act)
        acc_scr[...] += _dot(vt_ref[:, s * chunk:(s + 1) * chunk], w)

    @pl.when(eb == n_blk - 1)
    def _last():
        r = DN_ALPHA * h_ref[...] + acc_scr[...].T
        o_ref[...] = _layernorm(r, g_ref[...], b_ref[...])


def _peer(h1, wq_b, sub_keys, u_q, u_unscale, vt_b, g, b):
    n, d_model = h1.shape
    n_exp = u_q.shape[0]
    tt = TT_PEER
    sk = sub_keys.reshape(2 * PEER_HEADS, PEER_N_KEYS, PEER_HALF).astype(f32)
    n_blk = n_exp // EB_PEER
    const = dict(pipeline_mode=pl.Buffered(1))
    return pl.pallas_call(
        functools.partial(_peer_kernel, n_exp),
        grid=(n // tt, n_blk),
        in_specs=[pl.BlockSpec(memory_space=pltpu.SMEM),
                  pl.BlockSpec((tt, d_model), lambda t, e: (t, 0)),
                  pl.BlockSpec(wq_b.shape, lambda t, e: (0, 0), **const),
                  pl.BlockSpec(sk.shape, lambda t, e: (0, 0, 0), **const),
                  pl.BlockSpec((EB_PEER, d_model), lambda t, e: (e, 0)),
                  pl.BlockSpec((d_model, EB_PEER), lambda t, e: (0, e)),
                  pl.BlockSpec((1, d_model), lambda t, e: (0, 0), **const),
                  pl.BlockSpec((1, d_model), lambda t, e: (0, 0), **const)],
        out_specs=pl.BlockSpec((tt, d_model), lambda t, e: (t, 0)),
        out_shape=jax.ShapeDtypeStruct((n, d_model), f32),
        scratch_shapes=[pltpu.VMEM((2 * PEER_HEADS, tt, PEER_HALF), f32),
                        pltpu.VMEM((2, PEER_HEADS, PEER_N_KEYS, tt), f32),
                        pltpu.VMEM((2, PEER_HEADS, PEER_N_KEYS // BF16_ROWS, BF16_ROWS, tt), bf16),
                        pltpu.VMEM((2, PEER_TOPK, PEER_HEADS, tt), f32),
                        pltpu.VMEM((6, PEER_HEADS, tt), f32),
                        pltpu.VMEM((d_model, tt), f32),
                        pltpu.VMEM((d_model, tt), fp8),
                        pltpu.VMEM((EB_PEER // (2 * PEER_N_KEYS), 2 * PEER_N_KEYS, tt), f32),
                        pltpu.SMEM((1,), f32)],
        compiler_params=pltpu.CompilerParams(dimension_semantics=("parallel", "arbitrary"),
                                             vmem_limit_bytes=VMEM_LIMIT),
        name="peer",
    )(u_unscale.reshape(1, 1), h1, wq_b, sk, u_q, vt_b, g, b)


def _quantize_table(t):
    amax = jnp.maximum(jnp.max(jnp.abs(t)), jnp.finfo(f32).tiny)
    e = jnp.minimum(jnp.floor(jnp.log2(FP8_TARGET_MAX / amax)), FP8_MAX_SHIFT)
    return (t * jnp.exp2(e)).astype(fp8), jnp.exp2(-e).astype(f32)


def kernel(x, w_in, ssm_a_re, ssm_a_im, ssm_log_dt, ssm_b_re, ssm_b_im, ssm_c_re, ssm_c_im, ssm_d,
           ssm_w_glu, w_out, ln1_g, ln1_b, peer_w_q, peer_sub_keys, peer_u, peer_v, ln2_g, ln2_b):
    bsz, seq, d_model = x.shape
    n = bsz * seq
    h = x.reshape(n, d_model)
    for l in range(w_in.shape[0]):
        u2, k2, qt, vt, kmean = _inproj(h, w_in[l])
        ops = _s5_operators(ssm_a_re[l], ssm_a_im[l], ssm_log_dt[l], ssm_b_re[l], ssm_b_im[l],
                            ssm_c_re[l], ssm_c_im[l], ssm_d[l])
        yg = _s5(u2, bsz, seq, ops)
        att = _moba(qt, k2, vt, kmean.reshape(bsz, seq // MOBA_BLOCK, D_ATT), bsz, seq)
        h1 = _outproj(h, yg, att, ssm_w_glu[l].astype(bf16), w_out[l].astype(bf16),
                      ln1_g[l].reshape(1, d_model), ln1_b[l].reshape(1, d_model))
        u_q, u_unscale = _quantize_table(peer_u[l])
        h = _peer(h1, peer_w_q[l].astype(bf16), peer_sub_keys[l], u_q, u_unscale,
                  peer_v[l].T.astype(bf16), ln2_g[l].reshape(1, d_model), ln2_b[l].reshape(1, d_model))
    return h.reshape(bsz, seq, d_model)
```

```python
import functools
import math

import jax
import jax.numpy as jnp
from jax import lax
from jax.experimental import pallas as pl
from jax.experimental.pallas import tpu as pltpu

f32 = jnp.float32
bf16 = jnp.bfloat16
fp8 = jnp.float8_e4m3fn
FP8_TARGET_MAX = 256.0
FP8_MAX_SHIFT = 48.0

D_SSM = 512
SSM_GROUP = 16
SSM_STATE = 64
N_ATT_HEADS = 8
ATT_HEAD_DIM = 64
D_ATT = N_ATT_HEADS * ATT_HEAD_DIM
MOBA_BLOCK = 256
MOBA_TOPK = 3
PEER_HEADS = 8
PEER_HALF = 128
PEER_N_KEYS = 128
PEER_TOPK = 16
LN_EPS = 1e-5
DEPTH = 1
DN_ALPHA = (2.0 * DEPTH) ** 0.25
NEG = -1e30

LANES = 128
SUBLANES = 8
BF16_ROWS = 2 * SUBLANES
VMEM_LIMIT = 48 * 1024 * 1024

TM_PROJ = 1024
S5_CHUNK = 8
S5_LANE_GROUPS = LANES // SSM_GROUP
TT_PEER = 512
EB_PEER = 2048


def _dot(a, b):
    return jnp.dot(a, b, preferred_element_type=f32)


def _dot_nt(a, b):
    return lax.dot_general(a, b, (((1,), (1,)), ((), ())), preferred_element_type=f32)


def _split_bf16(x):
    hi = x.astype(bf16)
    lo = (x - hi.astype(f32)).astype(bf16)
    return hi, lo


def _layernorm(r, g, b):
    mu = jnp.mean(r, axis=-1, keepdims=True)
    d = r - mu
    var = jnp.mean(d * d, axis=-1, keepdims=True)
    return d * lax.rsqrt(var + LN_EPS) * g + b


def _inproj_kernel(x_ref, wuk_ref, wqvt_ref, u_ref, k_ref, qt_ref, vt_ref, km_ref):
    xb = x_ref[...].astype(bf16)
    acc = _dot(xb, wuk_ref[...])
    u_ref[...] = acc[:, :D_SSM]
    kk = acc[:, D_SSM:]
    k_ref[...] = kk.astype(bf16)
    qvt = _dot_nt(wqvt_ref[...], xb)
    for r in range(TM_PROJ // MOBA_BLOCK):
        cols = slice(r * MOBA_BLOCK, (r + 1) * MOBA_BLOCK)
        km_ref[r] = jnp.mean(kk[cols], axis=0, keepdims=True)
        qt_ref[r] = qvt[:D_ATT, cols].astype(bf16)
        vt_ref[r] = qvt[D_ATT:, cols].astype(bf16)


def _inproj(x2, w_in):
    n, d_model = x2.shape
    nb = TM_PROJ // MOBA_BLOCK
    w_uk = jnp.concatenate([w_in[:, :D_SSM], w_in[:, D_SSM + D_ATT:D_SSM + 2 * D_ATT]], axis=1).astype(bf16)
    w_qvt = jnp.concatenate([w_in[:, D_SSM:D_SSM + D_ATT], w_in[:, D_SSM + 2 * D_ATT:]], axis=1).T.astype(bf16)
    return pl.pallas_call(
        _inproj_kernel,
        grid=(n // TM_PROJ,),
        in_specs=[pl.BlockSpec((TM_PROJ, d_model), lambda i: (i, 0)),
                  pl.BlockSpec(w_uk.shape, lambda i: (0, 0)),
                  pl.BlockSpec(w_qvt.shape, lambda i: (0, 0))],
        out_specs=[pl.BlockSpec((TM_PROJ, D_SSM), lambda i: (i, 0)),
                   pl.BlockSpec((TM_PROJ, D_ATT), lambda i: (i, 0)),
                   pl.BlockSpec((nb, D_ATT, MOBA_BLOCK), lambda i: (i, 0, 0)),
                   pl.BlockSpec((nb, D_ATT, MOBA_BLOCK), lambda i: (i, 0, 0)),
                   pl.BlockSpec((nb, 1, D_ATT), lambda i: (i, 0, 0))],
        out_shape=[jax.ShapeDtypeStruct((n, D_SSM), f32),
                   jax.ShapeDtypeStruct((n, D_ATT), bf16),
                   jax.ShapeDtypeStruct((n // MOBA_BLOCK, D_ATT, MOBA_BLOCK), bf16),
                   jax.ShapeDtypeStruct((n // MOBA_BLOCK, D_ATT, MOBA_BLOCK), bf16),
                   jax.ShapeDtypeStruct((n // MOBA_BLOCK, 1, D_ATT), f32)],
        compiler_params=pltpu.CompilerParams(dimension_semantics=("parallel",),
                                             vmem_limit_bytes=VMEM_LIMIT),
        name="inproj",
    )(x2, w_uk, w_qvt)


def _s5_operators(a_re, a_im, log_dt, b_re, b_im, c_re, c_im, d_skip):
    T = S5_CHUNK
    G, P = a_re.shape
    H = SSM_GROUP
    nj = G // S5_LANE_GROUPS
    g8 = S5_LANE_GROUPS
    lam = lax.complex(a_re.astype(f32), a_im.astype(f32))
    dt = jnp.exp(log_dt.astype(f32))[:, None]
    lam_bar = jnp.exp(lam * dt)
    b_c = lax.complex(b_re.astype(f32), b_im.astype(f32))
    b_bar = ((lam_bar - 1.0) / lam)[:, :, None] * b_c
    c_c = lax.complex(c_re.astype(f32), c_im.astype(f32))
    pows = jnp.exp((lam * dt)[None] * jnp.arange(T + 1, dtype=f32)[:, None, None])
    def widen(x, reps):
        n = x.shape[-1]
        rep = jnp.tile(jnp.eye(n, dtype=f32), (1, reps))
        return jnp.dot(x.reshape(-1, n), rep).reshape(x.shape[:-1] + (reps * n,))

    same = jnp.arange(g8)[:, None] == (jnp.arange(g8 * H) // H)[None, :]
    same_p = jnp.arange(g8)[:, None] == (jnp.arange(g8 * P) // P)[None, :]
    kern = jnp.real(jnp.einsum('ghp,dgp,gpk->dgkh', c_c, pows[:T], b_bar))
    dd = jnp.arange(T)[None, :] - jnp.arange(T)[:, None]
    kst = jnp.where((dd >= 0)[:, :, None, None, None], kern[jnp.clip(dd, 0, T - 1)], 0.0)
    kst = widen(kst.reshape(T, T, nj, g8, H, H), g8)
    kst = jnp.where(same[None, None, None, :, None, :], kst, 0.0)
    toep = kst.transpose(2, 0, 3, 4, 1, 5).reshape(nj, T * LANES, T * LANES)
    win = pows[T - 1 - jnp.arange(T)][:, :, None, :] * jnp.swapaxes(b_bar, 1, 2)[None]
    win = jnp.stack([jnp.real(win), jnp.imag(win)], 0).reshape(2, T, nj, g8, H, P)
    win = jnp.where(same_p[None, None, None, :, None, :], widen(win, g8), 0.0)
    m_in = win.transpose(2, 1, 3, 4, 0, 5).reshape(nj, T * LANES, 2 * g8 * P)
    wout = jnp.swapaxes(c_c, 1, 2)[None] * pows[1:T + 1][:, :, :, None]
    wout = jnp.stack([jnp.real(wout), -jnp.imag(wout)], 0).reshape(2, T, nj, g8, P, H)
    wout = jnp.where(same[None, None, None, :, None, :], widen(wout, g8), 0.0)
    m_out = wout.transpose(2, 0, 3, 4, 1, 5).reshape(nj, 2 * g8 * P, T * LANES)
    a_chunk = pows[T].reshape(nj, 1, g8 * P)
    a_ri = jnp.concatenate([jnp.real(a_chunk), jnp.imag(a_chunk)], axis=1)
    d_flat = jnp.tile(d_skip.astype(f32).reshape(nj, 1, LANES), (1, 1, T))
    return toep.astype(bf16), m_in.astype(bf16), m_out.astype(bf16), a_ri, d_flat


def _s5_kernel(u_ref, toep_ref, min_ref, mout_ref, a_ref, d_ref, y_ref, in_scr, sp_scr):
    T = S5_CHUNK
    n_c = u_ref.shape[0] // T
    half = in_scr.shape[1] // 2
    uf = jnp.concatenate([u_ref[pl.ds(t, n_c, stride=T), :] for t in range(T)], axis=1)
    ufb = uf.astype(bf16)
    y = _dot(ufb, toep_ref[0]) + uf * d_ref[0]
    in_scr[...] = _dot(ufb, min_ref[0])
    ar = a_ref[0, 0:1, :]
    ai = a_ref[0, 1:2, :]

    def step(c, carry):
        sr, si = carry
        sp_scr[pl.ds(c, 1), :half] = sr
        sp_scr[pl.ds(c, 1), half:] = si
        ir = in_scr[pl.ds(c, 1), :half]
        ii = in_scr[pl.ds(c, 1), half:]
        return ar * sr - ai * si + ir, ar * si + ai * sr + ii

    z = jnp.zeros((1, half), f32)
    lax.fori_loop(0, n_c, step, (z, z), unroll=4)
    y = y + _dot(sp_scr[...].astype(bf16), mout_ref[0])
    yg = jax.nn.gelu(y)
    for t in range(T):
        y_ref[pl.ds(t, n_c, stride=T), :] = yg[:, t * LANES:(t + 1) * LANES]


def _s5(u2, bsz, seq, ops):
    toep, m_in, m_out, a_ri, d_flat = ops
    nj = toep.shape[0]
    n_c = seq // S5_CHUNK
    kdim = S5_CHUNK * LANES
    sdim = m_in.shape[2]
    return pl.pallas_call(
        _s5_kernel,
        grid=(nj, bsz),
        in_specs=[pl.BlockSpec((seq, LANES), lambda j, b: (b, j)),
                  pl.BlockSpec((1, kdim, kdim), lambda j, b: (j, 0, 0)),
                  pl.BlockSpec((1, kdim, sdim), lambda j, b: (j, 0, 0)),
                  pl.BlockSpec((1, sdim, kdim), lambda j, b: (j, 0, 0)),
                  pl.BlockSpec((1, 2, sdim // 2), lambda j, b: (j, 0, 0)),
                  pl.BlockSpec((1, 1, kdim), lambda j, b: (j, 0, 0))],
        out_specs=pl.BlockSpec((seq, LANES), lambda j, b: (b, j)),
        out_shape=jax.ShapeDtypeStruct(u2.shape, f32),
        scratch_shapes=[pltpu.VMEM((n_c, sdim), f32), pltpu.VMEM((n_c, sdim), f32)],
        compiler_params=pltpu.CompilerParams(dimension_semantics=("parallel", "parallel"),
                                             vmem_limit_bytes=VMEM_LIMIT),
        name="s5",
    )(u2, toep, m_in, m_out, a_ri, d_flat)


def _moba_kernel(qt_ref, k_ref, vt_ref, km_ref, o_ref, kaug_ref, s0_scr, s1_scr):
    blk = MOBA_BLOCK
    hd = ATT_HEAD_DIM
    nb = k_ref.shape[0] // blk
    i = pl.program_id(2)

    @pl.when(i == 0)
    def _build():
        lane = lax.broadcasted_iota(jnp.int32, (blk, LANES), 1)

        def build(j, carry):
            rows = pl.ds(pl.multiple_of(j * blk, blk), blk)
            kaug_ref[rows, :LANES] = k_ref[rows, :]
            kaug_ref[rows, LANES:] = jnp.where(lane == j, 1.0, 0.0).astype(bf16)
            return carry
        lax.fori_loop(0, nb, build, 0)

    qt = qt_ref[0]
    qtf = qt.astype(f32) * (hd ** -0.5 * math.log2(math.e))
    km = km_ref[0]
    lane_nb = lax.broadcasted_iota(jnp.int32, (nb, LANES), 1)
    feat = lax.broadcasted_iota(jnp.int32, (LANES, blk), 0)
    blk_id = lax.broadcasted_iota(jnp.int32, (nb, blk), 0)
    blk_f = blk_id.astype(f32)
    q_cols, bias_cols = [], []
    for hh in range(2):
        km_hi, km_lo = _split_bf16(jnp.where((lane_nb < hd) if hh == 0 else (lane_nb >= hd), km, 0.0))
        gate = _dot(km_hi, qt) + _dot(km_lo, qt)
        g = jnp.where(blk_id < i, gate, -jnp.inf)
        sel = blk_id == i
        for _ in range(MOBA_TOPK):
            mx = jnp.max(g, axis=0, keepdims=True)
            first = jnp.min(jnp.where(g == mx, blk_f, float(nb)), axis=0, keepdims=True)
            pick = (blk_f == first) & (mx > -jnp.inf)
            sel = sel | pick
            g = jnp.where(pick, -jnp.inf, g)
        bias_cols.append(jnp.where(sel, 0.0, NEG))
        q_cols.append(jnp.where((feat < hd) if hh == 0 else (feat >= hd), qtf, 0.0))
    qa = jnp.concatenate([jnp.concatenate(q_cols, axis=1), jnp.concatenate(bias_cols, axis=1),
                          jnp.zeros((LANES - nb, 2 * blk), f32)], axis=0).astype(bf16)

    ones_rows = jnp.ones((BF16_ROWS, blk), bf16)
    key_pos = lax.broadcasted_iota(jnp.int32, (blk, 2 * blk), 0)
    q_pos = lax.broadcasted_iota(jnp.int32, (blk, 2 * blk), 1) % blk

    def scores(j):
        j = jnp.minimum(j, nb - 1)
        return _dot(kaug_ref[pl.ds(pl.multiple_of(j * blk, blk), blk), :], qa)

    def update(state, blocks):
        m, acc = state
        m_new = m
        for _, s in blocks:
            m_new = jnp.maximum(m_new, jnp.max(s, axis=0, keepdims=True))
        acc = jnp.exp2(m - m_new) * acc
        for j, s in blocks:
            pb = jnp.exp2(s - m_new).astype(bf16)
            acc = acc + jnp.concatenate(
                [_dot(jnp.concatenate([vt_ref[j, :hd, :], ones_rows], axis=0), pb[:, :blk]),
                 _dot(jnp.concatenate([vt_ref[j, hd:, :], ones_rows], axis=0), pb[:, blk:])], axis=1)
        return m_new, acc

    def pair_scores(p, dst):
        dst[0] = scores(2 * p)
        dst[1] = scores(2 * p + 1)

    def pair_update(state, p, src, masked):
        blocks = []
        for r in range(2):
            j = 2 * p + r
            s = src[r]
            if masked:
                off = jnp.where(j < i, blk, jnp.where(j == i, 0, -2 * blk))
                s = jnp.where(key_pos <= q_pos + off, s, NEG)
            blocks.append((jnp.minimum(j, nb - 1), s))
        return update(state, blocks)

    def quad_body(q, state):
        pair_scores(2 * q + 1, s1_scr)
        state = pair_update(state, 2 * q, s0_scr, False)
        pair_scores(2 * q + 2, s0_scr)
        return pair_update(state, 2 * q + 1, s1_scr, False)

    state = (jnp.full((1, 2 * blk), -jnp.inf, f32), jnp.zeros((hd + BF16_ROWS, 2 * blk), f32))
    n_quad = i // 4
    pair_scores(0, s0_scr)
    state = lax.fori_loop(0, n_quad, quad_body, state)
    pair_scores(2 * n_quad + 1, s1_scr)
    state = pair_update(state, 2 * n_quad, s0_scr, True)
    m, acc = lax.cond(4 * n_quad + 2 <= i,
                      lambda st: pair_update(st, 2 * n_quad + 1, s1_scr, True), lambda st: st, state)
    out = acc[:hd] / acc[hd:hd + 1]
    o_ref[...] = jnp.concatenate([out[:, :blk], out[:, blk:]], axis=0).T.astype(o_ref.dtype)


def _moba(qt, k2, vt, kmean, bsz, seq):
    nb = seq // MOBA_BLOCK
    assert seq % MOBA_BLOCK == 0 and nb % SUBLANES == 0 and nb <= LANES
    npair = D_ATT // LANES
    return pl.pallas_call(
        _moba_kernel,
        grid=(bsz, npair, nb),
        in_specs=[pl.BlockSpec((1, LANES, MOBA_BLOCK), lambda b, h, i: (b * nb + i, h, 0)),
                  pl.BlockSpec((seq, LANES), lambda b, h, i: (b, h)),
                  pl.BlockSpec((nb, LANES, MOBA_BLOCK), lambda b, h, i: (b, h, 0)),
                  pl.BlockSpec((1, nb, LANES), lambda b, h, i: (b, 0, h))],
        out_specs=pl.BlockSpec((MOBA_BLOCK, LANES), lambda b, h, i: (b * nb + i, h)),
        out_shape=jax.ShapeDtypeStruct(k2.shape, bf16),
        scratch_shapes=[pltpu.VMEM((seq, 2 * LANES), bf16),
                        pltpu.VMEM((2, MOBA_BLOCK, 2 * MOBA_BLOCK), f32),
                        pltpu.VMEM((2, MOBA_BLOCK, 2 * MOBA_BLOCK), f32)],
        compiler_params=pltpu.CompilerParams(dimension_semantics=("parallel", "parallel", "arbitrary"),
                                             vmem_limit_bytes=VMEM_LIMIT),
        name="moba",
    )(qt, k2, vt, kmean)


def _outproj_kernel(x_ref, yg_ref, att_ref, wglu_ref, wout_ref, g_ref, b_ref, h_ref):
    yg = yg_ref[...]
    z = yg * jax.nn.sigmoid(_dot(yg.astype(bf16), wglu_ref[...]))
    cat = jnp.concatenate([z.astype(bf16), att_ref[...]], axis=1)
    r = DN_ALPHA * x_ref[...] + _dot(cat, wout_ref[...])
    h_ref[...] = _layernorm(r, g_ref[...], b_ref[...])


def _outproj(x2, yg, att, wglu_b, wout_b, g, b):
    n, d_model = x2.shape
    return pl.pallas_call(
        _outproj_kernel,
        grid=(n // TM_PROJ,),
        in_specs=[pl.BlockSpec((TM_PROJ, d_model), lambda i: (i, 0)),
                  pl.BlockSpec((TM_PROJ, D_SSM), lambda i: (i, 0)),
                  pl.BlockSpec((TM_PROJ, D_ATT), lambda i: (i, 0)),
                  pl.BlockSpec(wglu_b.shape, lambda i: (0, 0)),
                  pl.BlockSpec(wout_b.shape, lambda i: (0, 0)),
                  pl.BlockSpec((1, d_model), lambda i: (0, 0)),
                  pl.BlockSpec((1, d_model), lambda i: (0, 0))],
        out_specs=pl.BlockSpec((TM_PROJ, d_model), lambda i: (i, 0)),
        out_shape=jax.ShapeDtypeStruct((n, d_model), f32),
        compiler_params=pltpu.CompilerParams(dimension_semantics=("parallel",),
                                             vmem_limit_bytes=VMEM_LIMIT),
        name="outproj",
    )(x2, yg, att, wglu_b, wout_b, g, b)


def _bitonic_sort_desc(v):
    n = len(v)
    k = 2
    while k <= n:
        j = k // 2
        while j >= 1:
            for a in range(n):
                b = a ^ j
                if b > a:
                    hi = jnp.maximum(v[a], v[b])
                    lo = jnp.minimum(v[a], v[b])
                    v[a], v[b] = (hi, lo) if (a & k) == 0 else (lo, hi)
            j //= 2
        k *= 2
    return v


def _bitonic_merge_desc(v):
    j = len(v) // 2
    while j >= 1:
        for a in range(len(v)):
            b = a ^ j
            if b > a:
                v[a], v[b] = jnp.maximum(v[a], v[b]), jnp.minimum(v[a], v[b])
        j //= 2
    return v


def _top16_sorted(st):
    k = PEER_TOPK
    x = _bitonic_sort_desc([st[SUBLANES * a:SUBLANES * (a + 1), :] for a in range(PEER_N_KEYS // SUBLANES)])
    for shift in (4, 2, 1):
        y = [pltpu.roll(x[k - 1 - a], shift, 0) for a in range(k)]
        x = _bitonic_merge_desc([jnp.maximum(x[a], y[a]) for a in range(k)])
    return x


def _dup_bf16(x):
    u = pltpu.bitcast(x.astype(bf16).astype(f32), jnp.uint32)
    return pltpu.bitcast(u | (u >> 16), f32)


def _bf16_rows(row):
    return pltpu.bitcast(jnp.broadcast_to(row, (SUBLANES, row.shape[1])), bf16)[None]


def _peer_kernel(n_exp, us_ref, h_ref, wq_ref, sk_ref, u_ref, vt_ref, g_ref, b_ref, o_ref,
                 q_scr, pr_scr, rq_scr, srt_scr, row_scr, acc_scr, hbt_scr, a_scr, unscale_scr):
    eb = pl.program_id(1)
    n_blk = n_exp // EB_PEER
    tt = h_ref.shape[0]
    nk = PEER_N_KEYS
    k = PEER_TOPK

    chunk = 2 * nk
    n_chunk = EB_PEER // chunk

    def activations():
        for s in range(n_chunk):
            a_scr[s] = _dot(u_ref[s * chunk:(s + 1) * chunk, :], hbt_scr[...])

    @pl.when(eb == 0)
    def _prologue():
        ht = h_ref[...].T
        amax = jnp.maximum(jnp.max(jnp.abs(ht), keepdims=True), jnp.finfo(f32).tiny)
        e_tile = jnp.minimum(jnp.floor(jnp.log2(FP8_TARGET_MAX / amax)), FP8_MAX_SHIFT)
        hbt_scr[...] = (ht * jnp.exp2(e_tile)).astype(fp8)
        unscale_scr[0] = (jnp.exp2(-e_tile) * us_ref[0, 0])[0, 0]
        q = _dot(h_ref[...].astype(bf16), wq_ref[...])
        for hc in range(2 * PEER_HEADS):
            q_scr[hc] = q[:, hc * PEER_HALF:(hc + 1) * PEER_HALF]

        def head_sort(h, carry):
            for c in range(2):
                q_hi, q_lo = _split_bf16(q_scr[2 * h + c])
                k_hi, k_lo = _split_bf16(sk_ref[2 * h + c])
                st = _dot_nt(k_hi, q_hi) + _dot_nt(k_hi, q_lo) + _dot_nt(k_lo, q_hi)
                pr_scr[c, h] = st
                srt = _top16_sorted(st)
                for a in range(k):
                    srt_scr[c, a, pl.ds(h, 1), :] = srt[a][0:1, :]
            return carry
        lax.fori_loop(0, PEER_HEADS, head_sort, 0)

        a_s = [srt_scr[0, a] for a in range(k)]
        b_s = [srt_scr[1, a] for a in range(k)]
        cands = [a_s[a] + b_s[b] for a in range(k) for b in range(k) if (a + 1) * (b + 1) <= k]
        cands += [jnp.full_like(cands[0], -jnp.inf)] * (64 - len(cands))
        cs = _bitonic_sort_desc(cands)
        zsum = jnp.zeros_like(cs[0])
        for a in range(k):
            zsum = zsum + jnp.exp(cs[a] - cs[0])
        row_scr[0] = 0.5 * (cs[k - 1] + cs[k])
        row_scr[1] = 1.0 / zsum
        row_scr[2] = a_s[0]
        row_scr[3] = b_s[0]
        row_scr[4] = a_s[k - 1]
        row_scr[5] = b_s[k - 1]

        def head_prep(h, carry):
            hrow = pl.ds(h, 1)
            at = pr_scr[0, h]
            bt = pr_scr[1, h]
            thr = row_scr[0, hrow, :] - at
            r_cnt = jnp.zeros_like(at)
            b_rank = jnp.zeros_like(bt)
            for a in range(k):
                b_a = srt_scr[1, a, hrow, :]
                r_cnt = jnp.where(b_a > thr, float(a + 1), r_cnt)
                b_rank = jnp.where(b_a > bt, float(a + 1), b_rank)
            in_a = at >= row_scr[4, hrow, :]
            p_w = jnp.where(in_a, jnp.exp(at - row_scr[2, hrow, :]), 0.0) * row_scr[1, hrow, :]
            pr_scr[0, h] = _dup_bf16(p_w)
            pr_scr[1, h] = _dup_bf16(jnp.where(in_a, r_cnt, 0.0))
            q_w = jnp.where(bt >= row_scr[5, hrow, :], jnp.exp(bt - row_scr[3, hrow, :]), 0.0)
            rq_scr[0, h] = b_rank.astype(bf16).reshape(nk // BF16_ROWS, BF16_ROWS, tt)
            rq_scr[1, h] = q_w.astype(bf16).reshape(nk // BF16_ROWS, BF16_ROWS, tt)
            return carry
        lax.fori_loop(0, PEER_HEADS, head_prep, 0)
        acc_scr[...] = jnp.zeros_like(acc_scr)

    activations()
    for s in range(n_chunk):
        halves = []
        for half in range(chunk // nk):
            n1 = pl.ds((EB_PEER // nk) * eb + s * (chunk // nk) + half, 1)
            g = jnp.zeros((nk // BF16_ROWS, BF16_ROWS, tt), bf16)
            for h in range(PEER_HEADS):
                p_row = _bf16_rows(pr_scr[0, h, n1, :])
                r_row = _bf16_rows(pr_scr[1, h, n1, :])
                g = g + p_row * jnp.where(rq_scr[0, h] < r_row, rq_scr[1, h], 0)
            halves.append(g.reshape(nk, tt))
        act = a_scr[s].astype(bf16) * unscale_scr[0].astype(bf16)
        w = jnp.concatenate(halves, axis=0) * jax.nn.gelu(act)
        acc_scr[...] += _dot(vt_ref[:, s * chunk:(s + 1) * chunk], w)

    @pl.when(eb == n_blk - 1)
    def _last():
        r = DN_ALPHA * h_ref[...] + acc_scr[...].T
        o_ref[...] = _layernorm(r, g_ref[...], b_ref[...])


def _peer(h1, wq_b, sub_keys, u_q, u_unscale, vt_b, g, b):
    n, d_model = h1.shape
    n_exp = u_q.shape[0]
    tt = TT_PEER
    sk = sub_keys.reshape(2 * PEER_HEADS, PEER_N_KEYS, PEER_HALF).astype(f32)
    n_blk = n_exp // EB_PEER
    const = dict(pipeline_mode=pl.Buffered(1))
    return pl.pallas_call(
        functools.partial(_peer_kernel, n_exp),
        grid=(n // tt, n_blk),
        in_specs=[pl.BlockSpec(memory_space=pltpu.SMEM),
                  pl.BlockSpec((tt, d_model), lambda t, e: (t, 0)),
                  pl.BlockSpec(wq_b.shape, lambda t, e: (0, 0), **const),
                  pl.BlockSpec(sk.shape, lambda t, e: (0, 0, 0), **const),
                  pl.BlockSpec((EB_PEER, d_model), lambda t, e: (e, 0)),
                  pl.BlockSpec((d_model, EB_PEER), lambda t, e: (0, e)),
                  pl.BlockSpec((1, d_model), lambda t, e: (0, 0), **const),
                  pl.BlockSpec((1, d_model), lambda t, e: (0, 0), **const)],
        out_specs=pl.BlockSpec((tt, d_model), lambda t, e: (t, 0)),
        out_shape=jax.ShapeDtypeStruct((n, d_model), f32),
        scratch_shapes=[pltpu.VMEM((2 * PEER_HEADS, tt, PEER_HALF), f32),
                        pltpu.VMEM((2, PEER_HEADS, PEER_N_KEYS, tt), f32),
                        pltpu.VMEM((2, PEER_HEADS, PEER_N_KEYS // BF16_ROWS, BF16_ROWS, tt), bf16),
                        pltpu.VMEM((2, PEER_TOPK, PEER_HEADS, tt), f32),
                        pltpu.VMEM((6, PEER_HEADS, tt), f32),
                        pltpu.VMEM((d_model, tt), f32),
                        pltpu.VMEM((d_model, tt), fp8),
                        pltpu.VMEM((EB_PEER // (2 * PEER_N_KEYS), 2 * PEER_N_KEYS, tt), f32),
                        pltpu.SMEM((1,), f32)],
        compiler_params=pltpu.CompilerParams(dimension_semantics=("parallel", "arbitrary"),
                                             vmem_limit_bytes=VMEM_LIMIT),
        name="peer",
    )(u_unscale.reshape(1, 1), h1, wq_b, sk, u_q, vt_b, g, b)


def _quantize_table(t):
    amax = jnp.maximum(jnp.max(jnp.abs(t)), jnp.finfo(f32).tiny)
    e = jnp.minimum(jnp.floor(jnp.log2(FP8_TARGET_MAX / amax)), FP8_MAX_SHIFT)
    return (t * jnp.exp2(e)).astype(fp8), jnp.exp2(-e).astype(f32)


def kernel(x, w_in, ssm_a_re, ssm_a_im, ssm_log_dt, ssm_b_re, ssm_b_im, ssm_c_re, ssm_c_im, ssm_d,
           ssm_w_glu, w_out, ln1_g, ln1_b, peer_w_q, peer_sub_keys, peer_u, peer_v, ln2_g, ln2_b):
    bsz, seq, d_model = x.shape
    n = bsz * seq
    h = x.reshape(n, d_model)
    for l in range(w_in.shape[0]):
        u2, k2, qt, vt, kmean = _inproj(h, w_in[l])
        ops = _s5_operators(ssm_a_re[l], ssm_a_im[l], ssm_log_dt[l], ssm_b_re[l], ssm_b_im[l],
                            ssm_c_re[l], ssm_c_im[l], ssm_d[l])
        yg = _s5(u2, bsz, seq, ops)
        att = _moba(qt, k2, vt, kmean.reshape(bsz, seq // MOBA_BLOCK, D_ATT), bsz, seq)
        h1 = _outproj(h, yg, att, ssm_w_glu[l].astype(bf16), w_out[l].astype(bf16),
                      ln1_g[l].reshape(1, d_model), ln1_b[l].reshape(1, d_model))
        u_q, u_unscale = _quantize_table(peer_u[l])
        h = _peer(h1, peer_w_q[l].astype(bf16), peer_sub_keys[l], u_q, u_unscale,
                  peer_v[l].T.astype(bf16), ln2_g[l].reshape(1, d_model), ln2_b[l].reshape(1, d_model))
    return h.reshape(bsz, seq, d_model)
```

```python
import functools
import math

import jax
import jax.numpy as jnp
from jax import lax
from jax.experimental import pallas as pl
from jax.experimental.pallas import tpu as pltpu

f32 = jnp.float32
bf16 = jnp.bfloat16
fp8 = jnp.float8_e4m3fn
FP8_TARGET_MAX = 256.0
FP8_MAX_SHIFT = 48.0

D_SSM = 512
SSM_GROUP = 16
SSM_STATE = 64
N_ATT_HEADS = 8
ATT_HEAD_DIM = 64
D_ATT = N_ATT_HEADS * ATT_HEAD_DIM
MOBA_BLOCK = 256
MOBA_TOPK = 3
PEER_HEADS = 8
PEER_HALF = 128
PEER_N_KEYS = 128
PEER_TOPK = 16
LN_EPS = 1e-5
DEPTH = 1
DN_ALPHA = (2.0 * DEPTH) ** 0.25
NEG = -1e30

LANES = 128
SUBLANES = 8
BF16_ROWS = 2 * SUBLANES
VMEM_LIMIT = 48 * 1024 * 1024

TM_PROJ = 1024
S5_CHUNK = 8
S5_LANE_GROUPS = LANES // SSM_GROUP
TT_PEER = 512
EB_PEER = 2048


def _dot(a, b):
    return jnp.dot(a, b, preferred_element_type=f32)


def _dot_nt(a, b):
    return lax.dot_general(a, b, (((1,), (1,)), ((), ())), preferred_element_type=f32)


def _split_bf16(x):
    hi = x.astype(bf16)
    lo = (x - hi.astype(f32)).astype(bf16)
    return hi, lo


def _layernorm(r, g, b):
    mu = jnp.mean(r, axis=-1, keepdims=True)
    d = r - mu
    var = jnp.mean(d * d, axis=-1, keepdims=True)
    return d * lax.rsqrt(var + LN_EPS) * g + b


def _inproj_kernel(x_ref, wuk_ref, wqvt_ref, u_ref, k_ref, qt_ref, vt_ref, km_ref):
    xb = x_ref[...].astype(bf16)
    acc = _dot(xb, wuk_ref[...])
    u_ref[...] = acc[:, :D_SSM]
    kk = acc[:, D_SSM:]
    k_ref[...] = kk.astype(bf16)
    qvt = _dot_nt(wqvt_ref[...], xb)
    for r in range(TM_PROJ // MOBA_BLOCK):
        cols = slice(r * MOBA_BLOCK, (r + 1) * MOBA_BLOCK)
        km_ref[r] = jnp.mean(kk[cols], axis=0, keepdims=True)
        qt_ref[r] = qvt[:D_ATT, cols].astype(bf16)
        vt_ref[r] = qvt[D_ATT:, cols].astype(bf16)


def _inproj(x2, w_in):
    n, d_model = x2.shape
    nb = TM_PROJ // MOBA_BLOCK
    w_uk = jnp.concatenate([w_in[:, :D_SSM], w_in[:, D_SSM + D_ATT:D_SSM + 2 * D_ATT]], axis=1).astype(bf16)
    w_qvt = jnp.concatenate([w_in[:, D_SSM:D_SSM + D_ATT], w_in[:, D_SSM + 2 * D_ATT:]], axis=1).T.astype(bf16)
    return pl.pallas_call(
        _inproj_kernel,
        grid=(n // TM_PROJ,),
        in_specs=[pl.BlockSpec((TM_PROJ, d_model), lambda i: (i, 0)),
                  pl.BlockSpec(w_uk.shape, lambda i: (0, 0)),
                  pl.BlockSpec(w_qvt.shape, lambda i: (0, 0))],
        out_specs=[pl.BlockSpec((TM_PROJ, D_SSM), lambda i: (i, 0)),
                   pl.BlockSpec((TM_PROJ, D_ATT), lambda i: (i, 0)),
                   pl.BlockSpec((nb, D_ATT, MOBA_BLOCK), lambda i: (i, 0, 0)),
                   pl.BlockSpec((nb, D_ATT, MOBA_BLOCK), lambda i: (i, 0, 0)),
                   pl.BlockSpec((nb, 1, D_ATT), lambda i: (i, 0, 0))],
        out_shape=[jax.ShapeDtypeStruct((n, D_SSM), f32),
                   jax.ShapeDtypeStruct((n, D_ATT), bf16),
                   jax.ShapeDtypeStruct((n // MOBA_BLOCK, D_ATT, MOBA_BLOCK), bf16),
                   jax.ShapeDtypeStruct((n // MOBA_BLOCK, D_ATT, MOBA_BLOCK), bf16),
                   jax.ShapeDtypeStruct((n // MOBA_BLOCK, 1, D_ATT), f32)],
        compiler_params=pltpu.CompilerParams(dimension_semantics=("parallel",),
                                             vmem_limit_bytes=VMEM_LIMIT),
        name="inproj",
    )(x2, w_uk, w_qvt)


def _s5_operators(a_re, a_im, log_dt, b_re, b_im, c_re, c_im, d_skip):
    T = S5_CHUNK
    G, P = a_re.shape
    H = SSM_GROUP
    nj = G // S5_LANE_GROUPS
    g8 = S5_LANE_GROUPS
    lam = lax.complex(a_re.astype(f32), a_im.astype(f32))
    dt = jnp.exp(log_dt.astype(f32))[:, None]
    lam_bar = jnp.exp(lam * dt)
    b_c = lax.complex(b_re.astype(f32), b_im.astype(f32))
    b_bar = ((lam_bar - 1.0) / lam)[:, :, None] * b_c
    c_c = lax.complex(c_re.astype(f32), c_im.astype(f32))
    pows = jnp.exp((lam * dt)[None] * jnp.arange(T + 1, dtype=f32)[:, None, None])
    def widen(x, reps):
        n = x.shape[-1]
        rep = jnp.tile(jnp.eye(n, dtype=f32), (1, reps))
        return jnp.dot(x.reshape(-1, n), rep).reshape(x.shape[:-1] + (reps * n,))

    same = jnp.arange(g8)[:, None] == (jnp.arange(g8 * H) // H)[None, :]
    same_p = jnp.arange(g8)[:, None] == (jnp.arange(g8 * P) // P)[None, :]
    kern = jnp.real(jnp.einsum('ghp,dgp,gpk->dgkh', c_c, pows[:T], b_bar))
    dd = jnp.arange(T)[None, :] - jnp.arange(T)[:, None]
    kst = jnp.where((dd >= 0)[:, :, None, None, None], kern[jnp.clip(dd, 0, T - 1)], 0.0)
    kst = widen(kst.reshape(T, T, nj, g8, H, H), g8)
    kst = jnp.where(same[None, None, None, :, None, :], kst, 0.0)
    toep = kst.transpose(2, 0, 3, 4, 1, 5).reshape(nj, T * LANES, T * LANES)
    win = pows[T - 1 - jnp.arange(T)][:, :, None, :] * jnp.swapaxes(b_bar, 1, 2)[None]
    win = jnp.stack([jnp.real(win), jnp.imag(win)], 0).reshape(2, T, nj, g8, H, P)
    win = jnp.where(same_p[None, None, None, :, None, :], widen(win, g8), 0.0)
    m_in = win.transpose(2, 1, 3, 4, 0, 5).reshape(nj, T * LANES, 2 * g8 * P)
    wout = jnp.swapaxes(c_c, 1, 2)[None] * pows[1:T + 1][:, :, :, None]
    wout = jnp.stack([jnp.real(wout), -jnp.imag(wout)], 0).reshape(2, T, nj, g8, P, H)
    wout = jnp.where(same[None, None, None, :, None, :], widen(wout, g8), 0.0)
    m_out = wout.transpose(2, 0, 3, 4, 1, 5).reshape(nj, 2 * g8 * P, T * LANES)
    a_chunk = pows[T].reshape(nj, 1, g8 * P)
    a_ri = jnp.concatenate([jnp.real(a_chunk), jnp.imag(a_chunk)], axis=1)
    d_flat = jnp.tile(d_skip.astype(f32).reshape(nj, 1, LANES), (1, 1, T))
    return toep.astype(bf16), m_in.astype(bf16), m_out.astype(bf16), a_ri, d_flat


def _s5_kernel(u_ref, toep_ref, min_ref, mout_ref, a_ref, d_ref, y_ref, in_scr, sp_scr):
    T = S5_CHUNK
    n_c = u_ref.shape[0] // T
    half = in_scr.shape[1] // 2
    uf = jnp.concatenate([u_ref[pl.ds(t, n_c, stride=T), :] for t in range(T)], axis=1)
    ufb = uf.astype(bf16)
    y = _dot(ufb, toep_ref[0]) + uf * d_ref[0]
    in_scr[...] = _dot(ufb, min_ref[0])
    ar = a_ref[0, 0:1, :]
    ai = a_ref[0, 1:2, :]

    def step(c, carry):
        sr, si = carry
        sp_scr[pl.ds(c, 1), :half] = sr
        sp_scr[pl.ds(c, 1), half:] = si
        ir = in_scr[pl.ds(c, 1), :half]
        ii = in_scr[pl.ds(c, 1), half:]
        return ar * sr - ai * si + ir, ar * si + ai * sr + ii

    z = jnp.zeros((1, half), f32)
    lax.fori_loop(0, n_c, step, (z, z), unroll=4)
    y = y + _dot(sp_scr[...].astype(bf16), mout_ref[0])
    yg = jax.nn.gelu(y)
    for t in range(T):
        y_ref[pl.ds(t, n_c, stride=T), :] = yg[:, t * LANES:(t + 1) * LANES]


def _s5(u2, bsz, seq, ops):
    toep, m_in, m_out, a_ri, d_flat = ops
    nj = toep.shape[0]
    n_c = seq // S5_CHUNK
    kdim = S5_CHUNK * LANES
    sdim = m_in.shape[2]
    return pl.pallas_call(
        _s5_kernel,
        grid=(nj, bsz),
        in_specs=[pl.BlockSpec((seq, LANES), lambda j, b: (b, j)),
                  pl.BlockSpec((1, kdim, kdim), lambda j, b: (j, 0, 0)),
                  pl.BlockSpec((1, kdim, sdim), lambda j, b: (j, 0, 0)),
                  pl.BlockSpec((1, sdim, kdim), lambda j, b: (j, 0, 0)),
                  pl.BlockSpec((1, 2, sdim // 2), lambda j, b: (j, 0, 0)),
                  pl.BlockSpec((1, 1, kdim), lambda j, b: (j, 0, 0))],
        out_specs=pl.BlockSpec((seq, LANES), lambda j, b: (b, j)),
        out_shape=jax.ShapeDtypeStruct(u2.shape, f32),
        scratch_shapes=[pltpu.VMEM((n_c, sdim), f32), pltpu.VMEM((n_c, sdim), f32)],
        compiler_params=pltpu.CompilerParams(dimension_semantics=("parallel", "parallel"),
                                             vmem_limit_bytes=VMEM_LIMIT),
        name="s5",
    )(u2, toep, m_in, m_out, a_ri, d_flat)


def _moba_kernel(qt_ref, k_ref, vt_ref, km_ref, o_ref, kaug_ref, s0_scr, s1_scr):
    blk = MOBA_BLOCK
    hd = ATT_HEAD_DIM
    nb = k_ref.shape[0] // blk
    i = pl.program_id(2)

    @pl.when(i == 0)
    def _build():
        lane = lax.broadcasted_iota(jnp.int32, (blk, LANES), 1)

        def build(j, carry):
            rows = pl.ds(pl.multiple_of(j * blk, blk), blk)
            kaug_ref[rows, :LANES] = k_ref[rows, :]
            kaug_ref[rows, LANES:] = jnp.where(lane == j, 1.0, 0.0).astype(bf16)
            return carry
        lax.fori_loop(0, nb, build, 0)

    qt = qt_ref[0]
    qtf = qt.astype(f32) * (hd ** -0.5 * math.log2(math.e))
    km = km_ref[0]
    lane_nb = lax.broadcasted_iota(jnp.int32, (nb, LANES), 1)
    feat = lax.broadcasted_iota(jnp.int32, (LANES, blk), 0)
    blk_id = lax.broadcasted_iota(jnp.int32, (nb, blk), 0)
    blk_f = blk_id.astype(f32)
    q_cols, bias_cols = [], []
    for hh in range(2):
        km_hi, km_lo = _split_bf16(jnp.where((lane_nb < hd) if hh == 0 else (lane_nb >= hd), km, 0.0))
        gate = _dot(km_hi, qt) + _dot(km_lo, qt)
        g = jnp.where(blk_id < i, gate, -jnp.inf)
        sel = blk_id == i
        for _ in range(MOBA_TOPK):
            mx = jnp.max(g, axis=0, keepdims=True)
            first = jnp.min(jnp.where(g == mx, blk_f, float(nb)), axis=0, keepdims=True)
            pick = (blk_f == first) & (mx > -jnp.inf)
            sel = sel | pick
            g = jnp.where(pick, -jnp.inf, g)
        bias_cols.append(jnp.where(sel, 0.0, NEG))
        q_cols.append(jnp.where((feat < hd) if hh == 0 else (feat >= hd), qtf, 0.0))
    qa = jnp.concatenate([jnp.concatenate(q_cols, axis=1), jnp.concatenate(bias_cols, axis=1),
                          jnp.zeros((LANES - nb, 2 * blk), f32)], axis=0).astype(bf16)

    ones_rows = jnp.ones((BF16_ROWS, blk), bf16)
    key_pos = lax.broadcasted_iota(jnp.int32, (blk, 2 * blk), 0)
    q_pos = lax.broadcasted_iota(jnp.int32, (blk, 2 * blk), 1) % blk

    def scores(j):
        j = jnp.minimum(j, nb - 1)
        return _dot(kaug_ref[pl.ds(pl.multiple_of(j * blk, blk), blk), :], qa)

    def update(state, blocks):
        m, acc = state
        m_new = m
        for _, s in blocks:
            m_new = jnp.maximum(m_new, jnp.max(s, axis=0, keepdims=True))
        acc = jnp.exp2(m - m_new) * acc
        for j, s in blocks:
            pb = jnp.exp2((s - m_new).astype(bf16))
            acc = acc + jnp.concatenate(
                [_dot(jnp.concatenate([vt_ref[j, :hd, :], ones_rows], axis=0), pb[:, :blk]),
                 _dot(jnp.concatenate([vt_ref[j, hd:, :], ones_rows], axis=0), pb[:, blk:])], axis=1)
        return m_new, acc

    def pair_scores(p, dst):
        dst[0] = scores(2 * p)
        dst[1] = scores(2 * p + 1)

    def pair_update(state, p, src, masked):
        blocks = []
        for r in range(2):
            j = 2 * p + r
            s = src[r]
            if masked:
                off = jnp.where(j < i, blk, jnp.where(j == i, 0, -2 * blk))
                s = jnp.where(key_pos <= q_pos + off, s, NEG)
            blocks.append((jnp.minimum(j, nb - 1), s))
        return update(state, blocks)

    def quad_body(q, state):
        pair_scores(2 * q + 1, s1_scr)
        state = pair_update(state, 2 * q, s0_scr, False)
        pair_scores(2 * q + 2, s0_scr)
        return pair_update(state, 2 * q + 1, s1_scr, False)

    state = (jnp.full((1, 2 * blk), -jnp.inf, f32), jnp.zeros((hd + BF16_ROWS, 2 * blk), f32))
    n_quad = i // 4
    pair_scores(0, s0_scr)
    state = lax.fori_loop(0, n_quad, quad_body, state)
    pair_scores(2 * n_quad + 1, s1_scr)
    state = pair_update(state, 2 * n_quad, s0_scr, True)
    m, acc = pair_update(state, 2 * n_quad + 1, s1_scr, True)
    out = acc[:hd] / acc[hd:hd + 1]
    o_ref[...] = jnp.concatenate([out[:, :blk], out[:, blk:]], axis=0).T.astype(o_ref.dtype)


def _moba(qt, k2, vt, kmean, bsz, seq):
    nb = seq // MOBA_BLOCK
    assert seq % MOBA_BLOCK == 0 and nb % SUBLANES == 0 and nb <= LANES
    npair = D_ATT // LANES
    return pl.pallas_call(
        _moba_kernel,
        grid=(bsz, npair, nb),
        in_specs=[pl.BlockSpec((1, LANES, MOBA_BLOCK), lambda b, h, i: (b * nb + i, h, 0)),
                  pl.BlockSpec((seq, LANES), lambda b, h, i: (b, h)),
                  pl.BlockSpec((nb, LANES, MOBA_BLOCK), lambda b, h, i: (b, h, 0)),
                  pl.BlockSpec((1, nb, LANES), lambda b, h, i: (b, 0, h))],
        out_specs=pl.BlockSpec((MOBA_BLOCK, LANES), lambda b, h, i: (b * nb + i, h)),
        out_shape=jax.ShapeDtypeStruct(k2.shape, bf16),
        scratch_shapes=[pltpu.VMEM((seq, 2 * LANES), bf16),
                        pltpu.VMEM((2, MOBA_BLOCK, 2 * MOBA_BLOCK), f32),
                        pltpu.VMEM((2, MOBA_BLOCK, 2 * MOBA_BLOCK), f32)],
        compiler_params=pltpu.CompilerParams(dimension_semantics=("parallel", "parallel", "arbitrary"),
                                             vmem_limit_bytes=VMEM_LIMIT),
        name="moba",
    )(qt, k2, vt, kmean)


def _outproj_kernel(x_ref, yg_ref, att_ref, wglu_ref, wout_ref, g_ref, b_ref, h_ref):
    yg = yg_ref[...]
    z = yg * jax.nn.sigmoid(_dot(yg.astype(bf16), wglu_ref[...]))
    cat = jnp.concatenate([z.astype(bf16), att_ref[...]], axis=1)
    r = DN_ALPHA * x_ref[...] + _dot(cat, wout_ref[...])
    h_ref[...] = _layernorm(r, g_ref[...], b_ref[...])


def _outproj(x2, yg, att, wglu_b, wout_b, g, b):
    n, d_model = x2.shape
    return pl.pallas_call(
        _outproj_kernel,
        grid=(n // TM_PROJ,),
        in_specs=[pl.BlockSpec((TM_PROJ, d_model), lambda i: (i, 0)),
                  pl.BlockSpec((TM_PROJ, D_SSM), lambda i: (i, 0)),
                  pl.BlockSpec((TM_PROJ, D_ATT), lambda i: (i, 0)),
                  pl.BlockSpec(wglu_b.shape, lambda i: (0, 0)),
                  pl.BlockSpec(wout_b.shape, lambda i: (0, 0)),
                  pl.BlockSpec((1, d_model), lambda i: (0, 0)),
                  pl.BlockSpec((1, d_model), lambda i: (0, 0))],
        out_specs=pl.BlockSpec((TM_PROJ, d_model), lambda i: (i, 0)),
        out_shape=jax.ShapeDtypeStruct((n, d_model), f32),
        compiler_params=pltpu.CompilerParams(dimension_semantics=("parallel",),
                                             vmem_limit_bytes=VMEM_LIMIT),
        name="outproj",
    )(x2, yg, att, wglu_b, wout_b, g, b)


def _bitonic_sort_desc(v):
    n = len(v)
    k = 2
    while k <= n:
        j = k // 2
        while j >= 1:
            for a in range(n):
                b = a ^ j
                if b > a:
                    hi = jnp.maximum(v[a], v[b])
                    lo = jnp.minimum(v[a], v[b])
                    v[a], v[b] = (hi, lo) if (a & k) == 0 else (lo, hi)
            j //= 2
        k *= 2
    return v


def _bitonic_merge_desc(v):
    j = len(v) // 2
    while j >= 1:
        for a in range(len(v)):
            b = a ^ j
            if b > a:
                v[a], v[b] = jnp.maximum(v[a], v[b]), jnp.minimum(v[a], v[b])
        j //= 2
    return v


def _top16_sorted(st):
    k = PEER_TOPK
    x = _bitonic_sort_desc([st[SUBLANES * a:SUBLANES * (a + 1), :] for a in range(PEER_N_KEYS // SUBLANES)])
    for shift in (4, 2, 1):
        y = [pltpu.roll(x[k - 1 - a], shift, 0) for a in range(k)]
        x = _bitonic_merge_desc([jnp.maximum(x[a], y[a]) for a in range(k)])
    return x


def _dup_bf16(x):
    u = pltpu.bitcast(x.astype(bf16).astype(f32), jnp.uint32)
    return pltpu.bitcast(u | (u >> 16), f32)


def _bf16_rows(row):
    return pltpu.bitcast(jnp.broadcast_to(row, (SUBLANES, row.shape[1])), bf16)[None]


def _peer_kernel(n_exp, us_ref, h_ref, wq_ref, sk_ref, u_ref, vt_ref, g_ref, b_ref, o_ref,
                 q_scr, pr_scr, rq_scr, srt_scr, row_scr, acc_scr, hbt_scr, a_scr, unscale_scr):
    eb = pl.program_id(1)
    n_blk = n_exp // EB_PEER
    tt = h_ref.shape[0]
    nk = PEER_N_KEYS
    k = PEER_TOPK

    chunk = 2 * nk
    n_chunk = EB_PEER // chunk

    def activations():
        for s in range(n_chunk):
            a_scr[s] = _dot(u_ref[s * chunk:(s + 1) * chunk, :], hbt_scr[...])

    @pl.when(eb == 0)
    def _prologue():
        ht = h_ref[...].T
        amax = jnp.maximum(jnp.max(jnp.abs(ht), keepdims=True), jnp.finfo(f32).tiny)
        e_tile = jnp.minimum(jnp.floor(jnp.log2(FP8_TARGET_MAX / amax)), FP8_MAX_SHIFT)
        hbt_scr[...] = (ht * jnp.exp2(e_tile)).astype(fp8)
        unscale_scr[0] = (jnp.exp2(-e_tile) * us_ref[0, 0])[0, 0]
        q = _dot(h_ref[...].astype(bf16), wq_ref[...])
        for hc in range(2 * PEER_HEADS):
            q_scr[hc] = q[:, hc * PEER_HALF:(hc + 1) * PEER_HALF]

        def head_sort(h, carry):
            for c in range(2):
                q_hi, q_lo = _split_bf16(q_scr[2 * h + c])
                k_hi, k_lo = _split_bf16(sk_ref[2 * h + c])
                st = _dot_nt(k_hi, q_hi) + _dot_nt(k_hi, q_lo) + _dot_nt(k_lo, q_hi)
                pr_scr[c, h] = st
                srt = _top16_sorted(st)
                for a in range(k):
                    srt_scr[c, a, pl.ds(h, 1), :] = srt[a][0:1, :]
            return carry
        lax.fori_loop(0, PEER_HEADS, head_sort, 0)

        a_s = [srt_scr[0, a] for a in range(k)]
        b_s = [srt_scr[1, a] for a in range(k)]
        cands = [a_s[a] + b_s[b] for a in range(k) for b in range(k) if (a + 1) * (b + 1) <= k]
        cands += [jnp.full_like(cands[0], -jnp.inf)] * (64 - len(cands))
        cs = _bitonic_sort_desc(cands)
        zsum = jnp.zeros_like(cs[0])
        for a in range(k):
            zsum = zsum + jnp.exp(cs[a] - cs[0])
        row_scr[0] = 0.5 * (cs[k - 1] + cs[k])
        row_scr[1] = 1.0 / zsum
        row_scr[2] = a_s[0]
        row_scr[3] = b_s[0]
        row_scr[4] = a_s[k - 1]
        row_scr[5] = b_s[k - 1]

        def head_prep(h, carry):
            hrow = pl.ds(h, 1)
            at = pr_scr[0, h]
            bt = pr_scr[1, h]
            thr = row_scr[0, hrow, :] - at
            r_cnt = jnp.zeros_like(at)
            b_rank = jnp.zeros_like(bt)
            for a in range(k):
                b_a = srt_scr[1, a, hrow, :]
                r_cnt = jnp.where(b_a > thr, float(a + 1), r_cnt)
                b_rank = jnp.where(b_a > bt, float(a + 1), b_rank)
            in_a = at >= row_scr[4, hrow, :]
            p_w = jnp.where(in_a, jnp.exp(at - row_scr[2, hrow, :]), 0.0) * row_scr[1, hrow, :]
            pr_scr[0, h] = _dup_bf16(p_w)
            pr_scr[1, h] = _dup_bf16(jnp.where(in_a, r_cnt, 0.0))
            q_w = jnp.where(bt >= row_scr[5, hrow, :], jnp.exp(bt - row_scr[3, hrow, :]), 0.0)
            rq_scr[0, h] = b_rank.astype(bf16).reshape(nk // BF16_ROWS, BF16_ROWS, tt)
            rq_scr[1, h] = q_w.astype(bf16).reshape(nk // BF16_ROWS, BF16_ROWS, tt)
            return carry
        lax.fori_loop(0, PEER_HEADS, head_prep, 0)
        acc_scr[...] = jnp.zeros_like(acc_scr)

    activations()
    for s in range(n_chunk):
        halves = []
        for half in range(chunk // nk):
            n1 = pl.ds((EB_PEER // nk) * eb + s * (chunk // nk) + half, 1)
            g = jnp.zeros((nk // BF16_ROWS, BF16_ROWS, tt), bf16)
            for h in range(PEER_HEADS):
                p_row = _bf16_rows(pr_scr[0, h, n1, :])
                r_row = _bf16_rows(pr_scr[1, h, n1, :])
                g = g + p_row * jnp.where(rq_scr[0, h] < r_row, rq_scr[1, h], 0)
            halves.append(g.reshape(nk, tt))
        act = a_scr[s].astype(bf16) * unscale_scr[0].astype(bf16)
        w = jnp.concatenate(halves, axis=0) * jax.nn.gelu(act)
        acc_scr[...] += _dot(vt_ref[:, s * chunk:(s + 1) * chunk], w)

    @pl.when(eb == n_blk - 1)
    def _last():
        r = DN_ALPHA * h_ref[...] + acc_scr[...].T
        o_ref[...] = _layernorm(r, g_ref[...], b_ref[...])


def _peer(h1, wq_b, sub_keys, u_q, u_unscale, vt_b, g, b):
    n, d_model = h1.shape
    n_exp = u_q.shape[0]
    tt = TT_PEER
    sk = sub_keys.reshape(2 * PEER_HEADS, PEER_N_KEYS, PEER_HALF).astype(f32)
    n_blk = n_exp // EB_PEER
    const = dict(pipeline_mode=pl.Buffered(1))
    return pl.pallas_call(
        functools.partial(_peer_kernel, n_exp),
        grid=(n // tt, n_blk),
        in_specs=[pl.BlockSpec(memory_space=pltpu.SMEM),
                  pl.BlockSpec((tt, d_model), lambda t, e: (t, 0)),
                  pl.BlockSpec(wq_b.shape, lambda t, e: (0, 0), **const),
                  pl.BlockSpec(sk.shape, lambda t, e: (0, 0, 0), **const),
                  pl.BlockSpec((EB_PEER, d_model), lambda t, e: (e, 0)),
                  pl.BlockSpec((d_model, EB_PEER), lambda t, e: (0, e)),
                  pl.BlockSpec((1, d_model), lambda t, e: (0, 0), **const),
                  pl.BlockSpec((1, d_model), lambda t, e: (0, 0), **const)],
        out_specs=pl.BlockSpec((tt, d_model), lambda t, e: (t, 0)),
        out_shape=jax.ShapeDtypeStruct((n, d_model), f32),
        scratch_shapes=[pltpu.VMEM((2 * PEER_HEADS, tt, PEER_HALF), f32),
                        pltpu.VMEM((2, PEER_HEADS, PEER_N_KEYS, tt), f32),
                        pltpu.VMEM((2, PEER_HEADS, PEER_N_KEYS // BF16_ROWS, BF16_ROWS, tt), bf16),
                        pltpu.VMEM((2, PEER_TOPK, PEER_HEADS, tt), f32),
                        pltpu.VMEM((6, PEER_HEADS, tt), f32),
                        pltpu.VMEM((d_model, tt), f32),
                        pltpu.VMEM((d_model, tt), fp8),
                        pltpu.VMEM((EB_PEER // (2 * PEER_N_KEYS), 2 * PEER_N_KEYS, tt), f32),
                        pltpu.SMEM((1,), f32)],
        compiler_params=pltpu.CompilerParams(dimension_semantics=("parallel", "arbitrary"),
                                             vmem_limit_bytes=VMEM_LIMIT),
        name="peer",
    )(u_unscale.reshape(1, 1), h1, wq_b, sk, u_q, vt_b, g, b)


def _quantize_table(t):
    amax = jnp.maximum(jnp.max(jnp.abs(t)), jnp.finfo(f32).tiny)
    e = jnp.minimum(jnp.floor(jnp.log2(FP8_TARGET_MAX / amax)), FP8_MAX_SHIFT)
    return (t * jnp.exp2(e)).astype(fp8), jnp.exp2(-e).astype(f32)


def kernel(x, w_in, ssm_a_re, ssm_a_im, ssm_log_dt, ssm_b_re, ssm_b_im, ssm_c_re, ssm_c_im, ssm_d,
           ssm_w_glu, w_out, ln1_g, ln1_b, peer_w_q, peer_sub_keys, peer_u, peer_v, ln2_g, ln2_b):
    bsz, seq, d_model = x.shape
    n = bsz * seq
    h = x.reshape(n, d_model)
    for l in range(w_in.shape[0]):
        u2, k2, qt, vt, kmean = _inproj(h, w_in[l])
        ops = _s5_operators(ssm_a_re[l], ssm_a_im[l], ssm_log_dt[l], ssm_b_re[l], ssm_b_im[l],
                            ssm_c_re[l], ssm_c_im[l], ssm_d[l])
        yg = _s5(u2, bsz, seq, ops)
        att = _moba(qt, k2, vt, kmean.reshape(bsz, seq // MOBA_BLOCK, D_ATT), bsz, seq)
        h1 = _outproj(h, yg, att, ssm_w_glu[l].astype(bf16), w_out[l].astype(bf16),
                      ln1_g[l].reshape(1, d_model), ln1_b[l].reshape(1, d_model))
        u_q, u_unscale = _quantize_table(peer_u[l])
        h = _peer(h1, peer_w_q[l].astype(bf16), peer_sub_keys[l], u_q, u_unscale,
                  peer_v[l].T.astype(bf16), ln2_g[l].reshape(1, d_model), ln2_b[l].reshape(1, d_model))
    return h.reshape(bsz, seq, d_model)
```

```python
import functools
import math

import jax
import jax.numpy as jnp
from jax import lax
from jax.experimental import pallas as pl
from jax.experimental.pallas import tpu as pltpu

f32 = jnp.float32
bf16 = jnp.bfloat16
fp8 = jnp.float8_e4m3fn
FP8_TARGET_MAX = 256.0
FP8_MAX_SHIFT = 48.0

D_SSM = 512
SSM_GROUP = 16
SSM_STATE = 64
N_ATT_HEADS = 8
ATT_HEAD_DIM = 64
D_ATT = N_ATT_HEADS * ATT_HEAD_DIM
MOBA_BLOCK = 256
MOBA_TOPK = 3
PEER_HEADS = 8
PEER_HALF = 128
PEER_N_KEYS = 128
PEER_TOPK = 16
LN_EPS = 1e-5
DEPTH = 1
DN_ALPHA = (2.0 * DEPTH) ** 0.25
NEG = -1e30

LANES = 128
SUBLANES = 8
BF16_ROWS = 2 * SUBLANES
VMEM_LIMIT = 48 * 1024 * 1024

TM_PROJ = 1024
S5_CHUNK = 8
S5_LANE_GROUPS = LANES // SSM_GROUP
TT_PEER = 512
EB_PEER = 2048


def _dot(a, b):
    return jnp.dot(a, b, preferred_element_type=f32)


def _dot_nt(a, b):
    return lax.dot_general(a, b, (((1,), (1,)), ((), ())), preferred_element_type=f32)


def _split_bf16(x):
    hi = x.astype(bf16)
    lo = (x - hi.astype(f32)).astype(bf16)
    return hi, lo


def _layernorm(r, g, b):
    mu = jnp.mean(r, axis=-1, keepdims=True)
    d = r - mu
    var = jnp.mean(d * d, axis=-1, keepdims=True)
    return d * lax.rsqrt(var + LN_EPS) * g + b


def _inproj_kernel(x_ref, wuk_ref, wqvt_ref, u_ref, k_ref, qt_ref, vt_ref, km_ref):
    xb = x_ref[...].astype(bf16)
    acc = _dot(xb, wuk_ref[...])
    u_ref[...] = acc[:, :D_SSM]
    kk = acc[:, D_SSM:]
    k_ref[...] = kk.astype(bf16)
    qvt = _dot_nt(wqvt_ref[...], xb)
    for r in range(TM_PROJ // MOBA_BLOCK):
        cols = slice(r * MOBA_BLOCK, (r + 1) * MOBA_BLOCK)
        km_ref[r] = jnp.mean(kk[cols], axis=0, keepdims=True)
        qt_ref[r] = qvt[:D_ATT, cols].astype(bf16)
        vt_ref[r] = qvt[D_ATT:, cols].astype(bf16)


def _inproj(x2, w_in):
    n, d_model = x2.shape
    nb = TM_PROJ // MOBA_BLOCK
    w_uk = jnp.concatenate([w_in[:, :D_SSM], w_in[:, D_SSM + D_ATT:D_SSM + 2 * D_ATT]], axis=1).astype(bf16)
    w_qvt = jnp.concatenate([w_in[:, D_SSM:D_SSM + D_ATT], w_in[:, D_SSM + 2 * D_ATT:]], axis=1).T.astype(bf16)
    return pl.pallas_call(
        _inproj_kernel,
        grid=(n // TM_PROJ,),
        in_specs=[pl.BlockSpec((TM_PROJ, d_model), lambda i: (i, 0)),
                  pl.BlockSpec(w_uk.shape, lambda i: (0, 0)),
                  pl.BlockSpec(w_qvt.shape, lambda i: (0, 0))],
        out_specs=[pl.BlockSpec((TM_PROJ, D_SSM), lambda i: (i, 0)),
                   pl.BlockSpec((TM_PROJ, D_ATT), lambda i: (i, 0)),
                   pl.BlockSpec((nb, D_ATT, MOBA_BLOCK), lambda i: (i, 0, 0)),
                   pl.BlockSpec((nb, D_ATT, MOBA_BLOCK), lambda i: (i, 0, 0)),
                   pl.BlockSpec((nb, 1, D_ATT), lambda i: (i, 0, 0))],
        out_shape=[jax.ShapeDtypeStruct((n, D_SSM), f32),
                   jax.ShapeDtypeStruct((n, D_ATT), bf16),
                   jax.ShapeDtypeStruct((n // MOBA_BLOCK, D_ATT, MOBA_BLOCK), bf16),
                   jax.ShapeDtypeStruct((n // MOBA_BLOCK, D_ATT, MOBA_BLOCK), bf16),
                   jax.ShapeDtypeStruct((n // MOBA_BLOCK, 1, D_ATT), f32)],
        compiler_params=pltpu.CompilerParams(dimension_semantics=("parallel",),
                                             vmem_limit_bytes=VMEM_LIMIT),
        name="inproj",
    )(x2, w_uk, w_qvt)


def _s5_operators(a_re, a_im, log_dt, b_re, b_im, c_re, c_im, d_skip):
    T = S5_CHUNK
    G, P = a_re.shape
    H = SSM_GROUP
    nj = G // S5_LANE_GROUPS
    g8 = S5_LANE_GROUPS
    lam = lax.complex(a_re.astype(f32), a_im.astype(f32))
    dt = jnp.exp(log_dt.astype(f32))[:, None]
    lam_bar = jnp.exp(lam * dt)
    b_c = lax.complex(b_re.astype(f32), b_im.astype(f32))
    b_bar = ((lam_bar - 1.0) / lam)[:, :, None] * b_c
    c_c = lax.complex(c_re.astype(f32), c_im.astype(f32))
    pows = jnp.exp((lam * dt)[None] * jnp.arange(T + 1, dtype=f32)[:, None, None])
    def widen(x, reps):
        n = x.shape[-1]
        rep = jnp.tile(jnp.eye(n, dtype=f32), (1, reps))
        return jnp.dot(x.reshape(-1, n), rep).reshape(x.shape[:-1] + (reps * n,))

    same = jnp.arange(g8)[:, None] == (jnp.arange(g8 * H) // H)[None, :]
    same_p = jnp.arange(g8)[:, None] == (jnp.arange(g8 * P) // P)[None, :]
    kern = jnp.real(jnp.einsum('ghp,dgp,gpk->dgkh', c_c, pows[:T], b_bar))
    dd = jnp.arange(T)[None, :] - jnp.arange(T)[:, None]
    kst = jnp.where((dd >= 0)[:, :, None, None, None], kern[jnp.clip(dd, 0, T - 1)], 0.0)
    kst = widen(kst.reshape(T, T, nj, g8, H, H), g8)
    kst = jnp.where(same[None, None, None, :, None, :], kst, 0.0)
    toep = kst.transpose(2, 0, 3, 4, 1, 5).reshape(nj, T * LANES, T * LANES)
    win = pows[T - 1 - jnp.arange(T)][:, :, None, :] * jnp.swapaxes(b_bar, 1, 2)[None]
    win = jnp.stack([jnp.real(win), jnp.imag(win)], 0).reshape(2, T, nj, g8, H, P)
    win = jnp.where(same_p[None, None, None, :, None, :], widen(win, g8), 0.0)
    m_in = win.transpose(2, 1, 3, 4, 0, 5).reshape(nj, T * LANES, 2 * g8 * P)
    wout = jnp.swapaxes(c_c, 1, 2)[None] * pows[1:T + 1][:, :, :, None]
    wout = jnp.stack([jnp.real(wout), -jnp.imag(wout)], 0).reshape(2, T, nj, g8, P, H)
    wout = jnp.where(same[None, None, None, :, None, :], widen(wout, g8), 0.0)
    m_out = wout.transpose(2, 0, 3, 4, 1, 5).reshape(nj, 2 * g8 * P, T * LANES)
    a_chunk = pows[T].reshape(nj, 1, g8 * P)
    a_ri = jnp.concatenate([jnp.real(a_chunk), jnp.imag(a_chunk)], axis=1)
    d_flat = jnp.tile(d_skip.astype(f32).reshape(nj, 1, LANES), (1, 1, T))
    return toep.astype(bf16), m_in.astype(bf16), m_out.astype(bf16), a_ri, d_flat


def _s5_kernel(u_ref, toep_ref, min_ref, mout_ref, a_ref, d_ref, y_ref, in_scr, sp_scr):
    T = S5_CHUNK
    n_c = u_ref.shape[0] // T
    half = in_scr.shape[1] // 2
    uf = jnp.concatenate([u_ref[pl.ds(t, n_c, stride=T), :] for t in range(T)], axis=1)
    ufb = uf.astype(bf16)
    kt = 2 * LANES
    y = jnp.concatenate([_dot(ufb[:, :(n + 1) * kt], toep_ref[0, :(n + 1) * kt, n * kt:(n + 1) * kt])
                         for n in range(T * LANES // kt)], axis=1) + uf * d_ref[0]
    in_scr[...] = _dot(ufb, min_ref[0])
    ar = a_ref[0, 0:1, :]
    ai = a_ref[0, 1:2, :]

    def step(c, carry):
        sr, si = carry
        sp_scr[pl.ds(c, 1), :half] = sr
        sp_scr[pl.ds(c, 1), half:] = si
        ir = in_scr[pl.ds(c, 1), :half]
        ii = in_scr[pl.ds(c, 1), half:]
        return ar * sr - ai * si + ir, ar * si + ai * sr + ii

    z = jnp.zeros((1, half), f32)
    lax.fori_loop(0, n_c, step, (z, z), unroll=4)
    y = y + _dot(sp_scr[...].astype(bf16), mout_ref[0])
    yg = jax.nn.gelu(y)
    for t in range(T):
        y_ref[pl.ds(t, n_c, stride=T), :] = yg[:, t * LANES:(t + 1) * LANES]


def _s5(u2, bsz, seq, ops):
    toep, m_in, m_out, a_ri, d_flat = ops
    nj = toep.shape[0]
    n_c = seq // S5_CHUNK
    kdim = S5_CHUNK * LANES
    sdim = m_in.shape[2]
    return pl.pallas_call(
        _s5_kernel,
        grid=(nj, bsz),
        in_specs=[pl.BlockSpec((seq, LANES), lambda j, b: (b, j)),
                  pl.BlockSpec((1, kdim, kdim), lambda j, b: (j, 0, 0)),
                  pl.BlockSpec((1, kdim, sdim), lambda j, b: (j, 0, 0)),
                  pl.BlockSpec((1, sdim, kdim), lambda j, b: (j, 0, 0)),
                  pl.BlockSpec((1, 2, sdim // 2), lambda j, b: (j, 0, 0)),
                  pl.BlockSpec((1, 1, kdim), lambda j, b: (j, 0, 0))],
        out_specs=pl.BlockSpec((seq, LANES), lambda j, b: (b, j)),
        out_shape=jax.ShapeDtypeStruct(u2.shape, f32),
        scratch_shapes=[pltpu.VMEM((n_c, sdim), f32), pltpu.VMEM((n_c, sdim), f32)],
        compiler_params=pltpu.CompilerParams(dimension_semantics=("parallel", "parallel"),
                                             vmem_limit_bytes=VMEM_LIMIT),
        name="s5",
    )(u2, toep, m_in, m_out, a_ri, d_flat)


def _moba_kernel(qt_ref, k_ref, vt_ref, km_ref, o_ref, kaug_ref, s0_scr, s1_scr):
    blk = MOBA_BLOCK
    hd = ATT_HEAD_DIM
    nb = k_ref.shape[0] // blk
    i = pl.program_id(2)

    @pl.when(i == 0)
    def _build():
        lane = lax.broadcasted_iota(jnp.int32, (blk, LANES), 1)

        def build(j, carry):
            rows = pl.ds(pl.multiple_of(j * blk, blk), blk)
            kaug_ref[rows, :LANES] = k_ref[rows, :]
            kaug_ref[rows, LANES:] = jnp.where(lane == j, 1.0, 0.0).astype(bf16)
            return carry
        lax.fori_loop(0, nb, build, 0)

    qt = qt_ref[0]
    qtf = qt.astype(f32) * (hd ** -0.5 * math.log2(math.e))
    km = km_ref[0]
    lane_nb = lax.broadcasted_iota(jnp.int32, (nb, LANES), 1)
    feat = lax.broadcasted_iota(jnp.int32, (LANES, blk), 0)
    blk_id = lax.broadcasted_iota(jnp.int32, (nb, blk), 0)
    blk_f = blk_id.astype(f32)
    q_cols, bias_cols = [], []
    for hh in range(2):
        km_hi, km_lo = _split_bf16(jnp.where((lane_nb < hd) if hh == 0 else (lane_nb >= hd), km, 0.0))
        gate = _dot(km_hi, qt) + _dot(km_lo, qt)
        g = jnp.where(blk_id < i, gate, -jnp.inf)
        sel = blk_id == i
        for _ in range(MOBA_TOPK):
            mx = jnp.max(g, axis=0, keepdims=True)
            first = jnp.min(jnp.where(g == mx, blk_f, float(nb)), axis=0, keepdims=True)
            pick = (blk_f == first) & (mx > -jnp.inf)
            sel = sel | pick
            g = jnp.where(pick, -jnp.inf, g)
        bias_cols.append(jnp.where(sel, 0.0, NEG))
        q_cols.append(jnp.where((feat < hd) if hh == 0 else (feat >= hd), qtf, 0.0))
    qa = jnp.concatenate([jnp.concatenate(q_cols, axis=1), jnp.concatenate(bias_cols, axis=1),
                          jnp.zeros((LANES - nb, 2 * blk), f32)], axis=0).astype(bf16)

    ones_rows = jnp.ones((BF16_ROWS, blk), bf16)
    key_pos = lax.broadcasted_iota(jnp.int32, (blk, 2 * blk), 0)
    q_pos = lax.broadcasted_iota(jnp.int32, (blk, 2 * blk), 1) % blk

    def scores(j):
        j = jnp.minimum(j, nb - 1)
        return _dot(kaug_ref[pl.ds(pl.multiple_of(j * blk, blk), blk), :], qa)

    def update(state, blocks):
        m, acc = state
        m_new = m
        for _, s in blocks:
            m_new = jnp.maximum(m_new, jnp.max(s, axis=0, keepdims=True))
        acc = jnp.exp2(m - m_new) * acc
        for j, s in blocks:
            pb = jnp.exp2(s - m_new).astype(bf16)
            acc = acc + jnp.concatenate(
                [_dot(jnp.concatenate([vt_ref[j, :hd, :], ones_rows], axis=0), pb[:, :blk]),
                 _dot(jnp.concatenate([vt_ref[j, hd:, :], ones_rows], axis=0), pb[:, blk:])], axis=1)
        return m_new, acc

    def pair_scores(p, dst):
        dst[0] = scores(2 * p)
        dst[1] = scores(2 * p + 1)

    def pair_update(state, p, src, masked):
        blocks = []
        for r in range(2):
            j = 2 * p + r
            s = src[r]
            if masked:
                off = jnp.where(j < i, blk, jnp.where(j == i, 0, -2 * blk))
                s = jnp.where(key_pos <= q_pos + off, s, NEG)
            blocks.append((jnp.minimum(j, nb - 1), s))
        return update(state, blocks)

    def quad_body(q, state):
        pair_scores(2 * q + 1, s1_scr)
        state = pair_update(state, 2 * q, s0_scr, False)
        pair_scores(2 * q + 2, s0_scr)
        return pair_update(state, 2 * q + 1, s1_scr, False)

    state = (jnp.full((1, 2 * blk), -jnp.inf, f32), jnp.zeros((hd + BF16_ROWS, 2 * blk), f32))
    n_quad = i // 4
    pair_scores(0, s0_scr)
    state = lax.fori_loop(0, n_quad, quad_body, state)
    pair_scores(2 * n_quad + 1, s1_scr)
    state = pair_update(state, 2 * n_quad, s0_scr, True)
    m, acc = lax.cond(4 * n_quad + 2 <= i,
                      lambda st: pair_update(st, 2 * n_quad + 1, s1_scr, True), lambda st: st, state)
    out = acc[:hd] / acc[hd:hd + 1]
    o_ref[...] = jnp.concatenate([out[:, :blk], out[:, blk:]], axis=0).T.astype(o_ref.dtype)


def _moba(qt, k2, vt, kmean, bsz, seq):
    nb = seq // MOBA_BLOCK
    assert seq % MOBA_BLOCK == 0 and nb % SUBLANES == 0 and nb <= LANES
    npair = D_ATT // LANES
    return pl.pallas_call(
        _moba_kernel,
        grid=(bsz, npair, nb),
        in_specs=[pl.BlockSpec((1, LANES, MOBA_BLOCK), lambda b, h, i: (b * nb + i, h, 0)),
                  pl.BlockSpec((seq, LANES), lambda b, h, i: (b, h)),
                  pl.BlockSpec((nb, LANES, MOBA_BLOCK), lambda b, h, i: (b, h, 0)),
                  pl.BlockSpec((1, nb, LANES), lambda b, h, i: (b, 0, h))],
        out_specs=pl.BlockSpec((MOBA_BLOCK, LANES), lambda b, h, i: (b * nb + i, h)),
        out_shape=jax.ShapeDtypeStruct(k2.shape, bf16),
        scratch_shapes=[pltpu.VMEM((seq, 2 * LANES), bf16),
                        pltpu.VMEM((2, MOBA_BLOCK, 2 * MOBA_BLOCK), f32),
                        pltpu.VMEM((2, MOBA_BLOCK, 2 * MOBA_BLOCK), f32)],
        compiler_params=pltpu.CompilerParams(dimension_semantics=("parallel", "parallel", "arbitrary"),
                                             vmem_limit_bytes=VMEM_LIMIT),
        name="moba",
    )(qt, k2, vt, kmean)


def _outproj_kernel(x_ref, yg_ref, att_ref, wglu_ref, wout_ref, g_ref, b_ref, h_ref):
    yg = yg_ref[...]
    z = yg * jax.nn.sigmoid(_dot(yg.astype(bf16), wglu_ref[...]))
    cat = jnp.concatenate([z.astype(bf16), att_ref[...]], axis=1)
    r = DN_ALPHA * x_ref[...] + _dot(cat, wout_ref[...])
    h_ref[...] = _layernorm(r, g_ref[...], b_ref[...])


def _outproj(x2, yg, att, wglu_b, wout_b, g, b):
    n, d_model = x2.shape
    return pl.pallas_call(
        _outproj_kernel,
        grid=(n // TM_PROJ,),
        in_specs=[pl.BlockSpec((TM_PROJ, d_model), lambda i: (i, 0)),
                  pl.BlockSpec((TM_PROJ, D_SSM), lambda i: (i, 0)),
                  pl.BlockSpec((TM_PROJ, D_ATT), lambda i: (i, 0)),
                  pl.BlockSpec(wglu_b.shape, lambda i: (0, 0)),
                  pl.BlockSpec(wout_b.shape, lambda i: (0, 0)),
                  pl.BlockSpec((1, d_model), lambda i: (0, 0)),
                  pl.BlockSpec((1, d_model), lambda i: (0, 0))],
        out_specs=pl.BlockSpec((TM_PROJ, d_model), lambda i: (i, 0)),
        out_shape=jax.ShapeDtypeStruct((n, d_model), f32),
        compiler_params=pltpu.CompilerParams(dimension_semantics=("parallel",),
                                             vmem_limit_bytes=VMEM_LIMIT),
        name="outproj",
    )(x2, yg, att, wglu_b, wout_b, g, b)


def _bitonic_sort_desc(v):
    n = len(v)
    k = 2
    while k <= n:
        j = k // 2
        while j >= 1:
            for a in range(n):
                b = a ^ j
                if b > a:
                    hi = jnp.maximum(v[a], v[b])
                    lo = jnp.minimum(v[a], v[b])
                    v[a], v[b] = (hi, lo) if (a & k) == 0 else (lo, hi)
            j //= 2
        k *= 2
    return v


def _bitonic_merge_desc(v):
    j = len(v) // 2
    while j >= 1:
        for a in range(len(v)):
            b = a ^ j
            if b > a:
                v[a], v[b] = jnp.maximum(v[a], v[b]), jnp.minimum(v[a], v[b])
        j //= 2
    return v


def _top16_sorted(st):
    k = PEER_TOPK
    x = _bitonic_sort_desc([st[SUBLANES * a:SUBLANES * (a + 1), :] for a in range(PEER_N_KEYS // SUBLANES)])
    for shift in (4, 2, 1):
        y = [pltpu.roll(x[k - 1 - a], shift, 0) for a in range(k)]
        x = _bitonic_merge_desc([jnp.maximum(x[a], y[a]) for a in range(k)])
    return x


def _dup_bf16(x):
    u = pltpu.bitcast(x.astype(bf16).astype(f32), jnp.uint32)
    return pltpu.bitcast(u | (u >> 16), f32)


def _bf16_rows(row):
    return pltpu.bitcast(jnp.broadcast_to(row, (SUBLANES, row.shape[1])), bf16)[None]


def _peer_kernel(n_exp, us_ref, h_ref, wq_ref, sk_ref, u_ref, vt_ref, g_ref, b_ref, o_ref,
                 q_scr, pr_scr, rq_scr, srt_scr, row_scr, acc_scr, hbt_scr, a_scr, unscale_scr):
    eb = pl.program_id(1)
    n_blk = n_exp // EB_PEER
    tt = h_ref.shape[0]
    nk = PEER_N_KEYS
    k = PEER_TOPK

    chunk = 2 * nk
    n_chunk = EB_PEER // chunk

    def activations():
        for s in range(n_chunk):
            a_scr[s] = _dot(u_ref[s * chunk:(s + 1) * chunk, :], hbt_scr[...])

    @pl.when(eb == 0)
    def _prologue():
        ht = h_ref[...].T
        amax = jnp.maximum(jnp.max(jnp.abs(ht), keepdims=True), jnp.finfo(f32).tiny)
        e_tile = jnp.minimum(jnp.floor(jnp.log2(FP8_TARGET_MAX / amax)), FP8_MAX_SHIFT)
        hbt_scr[...] = (ht * jnp.exp2(e_tile)).astype(fp8)
        unscale_scr[0] = (jnp.exp2(-e_tile) * us_ref[0, 0])[0, 0]
        q = _dot(h_ref[...].astype(bf16), wq_ref[...])
        for hc in range(2 * PEER_HEADS):
            q_scr[hc] = q[:, hc * PEER_HALF:(hc + 1) * PEER_HALF]

        def head_sort(h, carry):
            for c in range(2):
                q_hi, q_lo = _split_bf16(q_scr[2 * h + c])
                k_hi, k_lo = _split_bf16(sk_ref[2 * h + c])
                st = _dot_nt(k_hi, q_hi) + _dot_nt(k_hi, q_lo) + _dot_nt(k_lo, q_hi)
                pr_scr[c, h] = st
                srt = _top16_sorted(st)
                for a in range(k):
                    srt_scr[c, a, pl.ds(h, 1), :] = srt[a][0:1, :]
            return carry
        lax.fori_loop(0, PEER_HEADS, head_sort, 0)

        a_s = [srt_scr[0, a] for a in range(k)]
        b_s = [srt_scr[1, a] for a in range(k)]
        cands = [a_s[a] + b_s[b] for a in range(k) for b in range(k) if (a + 1) * (b + 1) <= k]
        cands += [jnp.full_like(cands[0], -jnp.inf)] * (64 - len(cands))
        cs = _bitonic_sort_desc(cands)
        zsum = jnp.zeros_like(cs[0])
        for a in range(k):
            zsum = zsum + jnp.exp(cs[a] - cs[0])
        row_scr[0] = 0.5 * (cs[k - 1] + cs[k])
        row_scr[1] = 1.0 / zsum
        row_scr[2] = a_s[0]
        row_scr[3] = b_s[0]
        row_scr[4] = a_s[k - 1]
        row_scr[5] = b_s[k - 1]

        def head_prep(h, carry):
            hrow = pl.ds(h, 1)
            at = pr_scr[0, h]
            bt = pr_scr[1, h]
            thr = row_scr[0, hrow, :] - at
            r_cnt = jnp.zeros_like(at)
            b_rank = jnp.zeros_like(bt)
            for a in range(k):
                b_a = srt_scr[1, a, hrow, :]
                r_cnt = jnp.where(b_a > thr, float(a + 1), r_cnt)
                b_rank = jnp.where(b_a > bt, float(a + 1), b_rank)
            in_a = at >= row_scr[4, hrow, :]
            p_w = jnp.where(in_a, jnp.exp(at - row_scr[2, hrow, :]), 0.0) * row_scr[1, hrow, :]
            pr_scr[0, h] = _dup_bf16(p_w)
            pr_scr[1, h] = _dup_bf16(jnp.where(in_a, r_cnt, 0.0))
            q_w = jnp.where(bt >= row_scr[5, hrow, :], jnp.exp(bt - row_scr[3, hrow, :]), 0.0)
            rq_scr[0, h] = b_rank.astype(bf16).reshape(nk // BF16_ROWS, BF16_ROWS, tt)
            rq_scr[1, h] = q_w.astype(bf16).reshape(nk // BF16_ROWS, BF16_ROWS, tt)
            return carry
        lax.fori_loop(0, PEER_HEADS, head_prep, 0)
        acc_scr[...] = jnp.zeros_like(acc_scr)

    activations()
    for s in range(n_chunk):
        halves = []
        for half in range(chunk // nk):
            n1 = pl.ds((EB_PEER // nk) * eb + s * (chunk // nk) + half, 1)
            g = jnp.zeros((nk // BF16_ROWS, BF16_ROWS, tt), bf16)
            for h in range(PEER_HEADS):
                p_row = _bf16_rows(pr_scr[0, h, n1, :])
                r_row = _bf16_rows(pr_scr[1, h, n1, :])
                g = g + p_row * jnp.where(rq_scr[0, h] < r_row, rq_scr[1, h], 0)
            halves.append(g.reshape(nk, tt))
        act = a_scr[s].astype(bf16) * unscale_scr[0].astype(bf16)
        w = jnp.concatenate(halves, axis=0) * jax.nn.gelu(act)
        acc_scr[...] += _dot(vt_ref[:, s * chunk:(s + 1) * chunk], w)

    @pl.when(eb == n_blk - 1)
    def _last():
        r = DN_ALPHA * h_ref[...] + acc_scr[...].T
        o_ref[...] = _layernorm(r, g_ref[...], b_ref[...])


def _peer(h1, wq_b, sub_keys, u_q, u_unscale, vt_b, g, b):
    n, d_model = h1.shape
    n_exp = u_q.shape[0]
    tt = TT_PEER
    sk = sub_keys.reshape(2 * PEER_HEADS, PEER_N_KEYS, PEER_HALF).astype(f32)
    n_blk = n_exp // EB_PEER
    const = dict(pipeline_mode=pl.Buffered(1))
    return pl.pallas_call(
        functools.partial(_peer_kernel, n_exp),
        grid=(n // tt, n_blk),
        in_specs=[pl.BlockSpec(memory_space=pltpu.SMEM),
                  pl.BlockSpec((tt, d_model), lambda t, e: (t, 0)),
                  pl.BlockSpec(wq_b.shape, lambda t, e: (0, 0), **const),
                  pl.BlockSpec(sk.shape, lambda t, e: (0, 0, 0), **const),
                  pl.BlockSpec((EB_PEER, d_model), lambda t, e: (e, 0)),
                  pl.BlockSpec((d_model, EB_PEER), lambda t, e: (0, e)),
                  pl.BlockSpec((1, d_model), lambda t, e: (0, 0), **const),
                  pl.BlockSpec((1, d_model), lambda t, e: (0, 0), **const)],
        out_specs=pl.BlockSpec((tt, d_model), lambda t, e: (t, 0)),
        out_shape=jax.ShapeDtypeStruct((n, d_model), f32),
        scratch_shapes=[pltpu.VMEM((2 * PEER_HEADS, tt, PEER_HALF), f32),
                        pltpu.VMEM((2, PEER_HEADS, PEER_N_KEYS, tt), f32),
                        pltpu.VMEM((2, PEER_HEADS, PEER_N_KEYS // BF16_ROWS, BF16_ROWS, tt), bf16),
                        pltpu.VMEM((2, PEER_TOPK, PEER_HEADS, tt), f32),
                        pltpu.VMEM((6, PEER_HEADS, tt), f32),
                        pltpu.VMEM((d_model, tt), f32),
                        pltpu.VMEM((d_model, tt), fp8),
                        pltpu.VMEM((EB_PEER // (2 * PEER_N_KEYS), 2 * PEER_N_KEYS, tt), f32),
                        pltpu.SMEM((1,), f32)],
        compiler_params=pltpu.CompilerParams(dimension_semantics=("parallel", "arbitrary"),
                                             vmem_limit_bytes=VMEM_LIMIT),
        name="peer",
    )(u_unscale.reshape(1, 1), h1, wq_b, sk, u_q, vt_b, g, b)


def _quantize_table(t):
    amax = jnp.maximum(jnp.max(jnp.abs(t)), jnp.finfo(f32).tiny)
    e = jnp.minimum(jnp.floor(jnp.log2(FP8_TARGET_MAX / amax)), FP8_MAX_SHIFT)
    return (t * jnp.exp2(e)).astype(fp8), jnp.exp2(-e).astype(f32)


def kernel(x, w_in, ssm_a_re, ssm_a_im, ssm_log_dt, ssm_b_re, ssm_b_im, ssm_c_re, ssm_c_im, ssm_d,
           ssm_w_glu, w_out, ln1_g, ln1_b, peer_w_q, peer_sub_keys, peer_u, peer_v, ln2_g, ln2_b):
    bsz, seq, d_model = x.shape
    n = bsz * seq
    h = x.reshape(n, d_model)
    for l in range(w_in.shape[0]):
        u2, k2, qt, vt, kmean = _inproj(h, w_in[l])
        ops = _s5_operators(ssm_a_re[l], ssm_a_im[l], ssm_log_dt[l], ssm_b_re[l], ssm_b_im[l],
                            ssm_c_re[l], ssm_c_im[l], ssm_d[l])
        yg = _s5(u2, bsz, seq, ops)
        att = _moba(qt, k2, vt, kmean.reshape(bsz, seq // MOBA_BLOCK, D_ATT), bsz, seq)
        h1 = _outproj(h, yg, att, ssm_w_glu[l].astype(bf16), w_out[l].astype(bf16),
                      ln1_g[l].reshape(1, d_model), ln1_b[l].reshape(1, d_model))
        u_q, u_unscale = _quantize_table(peer_u[l])
        h = _peer(h1, peer_w_q[l].astype(bf16), peer_sub_keys[l], u_q, u_unscale,
                  peer_v[l].T.astype(bf16), ln2_g[l].reshape(1, d_model), ln2_b[l].reshape(1, d_model))
    return h.reshape(bsz, seq, d_model)
```

```python
import functools
import math

import jax
import jax.numpy as jnp
from jax import lax
from jax.experimental import pallas as pl
from jax.experimental.pallas import tpu as pltpu

f32 = jnp.float32
bf16 = jnp.bfloat16
fp8 = jnp.float8_e4m3fn
FP8_TARGET_MAX = 256.0
FP8_MAX_SHIFT = 48.0

D_SSM = 512
SSM_GROUP = 16
SSM_STATE = 64
N_ATT_HEADS = 8
ATT_HEAD_DIM = 64
D_ATT = N_ATT_HEADS * ATT_HEAD_DIM
MOBA_BLOCK = 256
MOBA_TOPK = 3
PEER_HEADS = 8
PEER_HALF = 128
PEER_N_KEYS = 128
PEER_TOPK = 16
LN_EPS = 1e-5
DEPTH = 1
DN_ALPHA = (2.0 * DEPTH) ** 0.25
NEG = -1e30

LANES = 128
SUBLANES = 8
BF16_ROWS = 2 * SUBLANES
VMEM_LIMIT = 48 * 1024 * 1024

TM_PROJ = 1024
S5_CHUNK = 8
S5_LANE_GROUPS = LANES // SSM_GROUP
TT_PEER = 512
EB_PEER = 2048


def _dot(a, b):
    return jnp.dot(a, b, preferred_element_type=f32)


def _dot_nt(a, b):
    return lax.dot_general(a, b, (((1,), (1,)), ((), ())), preferred_element_type=f32)


def _split_bf16(x):
    hi = x.astype(bf16)
    lo = (x - hi.astype(f32)).astype(bf16)
    return hi, lo


def _layernorm(r, g, b):
    mu = jnp.mean(r, axis=-1, keepdims=True)
    d = r - mu
    var = jnp.mean(d * d, axis=-1, keepdims=True)
    return d * lax.rsqrt(var + LN_EPS) * g + b


def _inproj_kernel(x_ref, wuk_ref, wqvt_ref, u_ref, k_ref, qt_ref, vt_ref, km_ref):
    xb = x_ref[...].astype(bf16)
    acc = _dot(xb, wuk_ref[...])
    u_ref[...] = acc[:, :D_SSM]
    kk = acc[:, D_SSM:]
    k_ref[...] = kk.astype(bf16)
    qvt = _dot_nt(wqvt_ref[...], xb)
    for r in range(TM_PROJ // MOBA_BLOCK):
        cols = slice(r * MOBA_BLOCK, (r + 1) * MOBA_BLOCK)
        km_ref[r] = jnp.mean(kk[cols], axis=0, keepdims=True)
        qt_ref[r] = qvt[:D_ATT, cols].astype(bf16)
        vt_ref[r] = qvt[D_ATT:, cols].astype(bf16)


def _inproj(x2, w_in):
    n, d_model = x2.shape
    nb = TM_PROJ // MOBA_BLOCK
    w_uk = jnp.concatenate([w_in[:, :D_SSM], w_in[:, D_SSM + D_ATT:D_SSM + 2 * D_ATT]], axis=1).astype(bf16)
    w_qvt = jnp.concatenate([w_in[:, D_SSM:D_SSM + D_ATT], w_in[:, D_SSM + 2 * D_ATT:]], axis=1).T.astype(bf16)
    return pl.pallas_call(
        _inproj_kernel,
        grid=(n // TM_PROJ,),
        in_specs=[pl.BlockSpec((TM_PROJ, d_model), lambda i: (i, 0)),
                  pl.BlockSpec(w_uk.shape, lambda i: (0, 0)),
                  pl.BlockSpec(w_qvt.shape, lambda i: (0, 0))],
        out_specs=[pl.BlockSpec((TM_PROJ, D_SSM), lambda i: (i, 0)),
                   pl.BlockSpec((TM_PROJ, D_ATT), lambda i: (i, 0)),
                   pl.BlockSpec((nb, D_ATT, MOBA_BLOCK), lambda i: (i, 0, 0)),
                   pl.BlockSpec((nb, D_ATT, MOBA_BLOCK), lambda i: (i, 0, 0)),
                   pl.BlockSpec((nb, 1, D_ATT), lambda i: (i, 0, 0))],
        out_shape=[jax.ShapeDtypeStruct((n, D_SSM), f32),
                   jax.ShapeDtypeStruct((n, D_ATT), bf16),
                   jax.ShapeDtypeStruct((n // MOBA_BLOCK, D_ATT, MOBA_BLOCK), bf16),
                   jax.ShapeDtypeStruct((n // MOBA_BLOCK, D_ATT, MOBA_BLOCK), bf16),
                   jax.ShapeDtypeStruct((n // MOBA_BLOCK, 1, D_ATT), f32)],
        compiler_params=pltpu.CompilerParams(dimension_semantics=("parallel",),
                                             vmem_limit_bytes=VMEM_LIMIT),
        name="inproj",
    )(x2, w_uk, w_qvt)


def _s5_operators(a_re, a_im, log_dt, b_re, b_im, c_re, c_im, d_skip):
    T = S5_CHUNK
    G, P = a_re.shape
    H = SSM_GROUP
    nj = G // S5_LANE_GROUPS
    g8 = S5_LANE_GROUPS
    lam = lax.complex(a_re.astype(f32), a_im.astype(f32))
    dt = jnp.exp(log_dt.astype(f32))[:, None]
    lam_bar = jnp.exp(lam * dt)
    b_c = lax.complex(b_re.astype(f32), b_im.astype(f32))
    b_bar = ((lam_bar - 1.0) / lam)[:, :, None] * b_c
    c_c = lax.complex(c_re.astype(f32), c_im.astype(f32))
    pows = jnp.exp((lam * dt)[None] * jnp.arange(T + 1, dtype=f32)[:, None, None])
    def widen(x, reps):
        n = x.shape[-1]
        rep = jnp.tile(jnp.eye(n, dtype=f32), (1, reps))
        return jnp.dot(x.reshape(-1, n), rep).reshape(x.shape[:-1] + (reps * n,))

    same = jnp.arange(g8)[:, None] == (jnp.arange(g8 * H) // H)[None, :]
    same_p = jnp.arange(g8)[:, None] == (jnp.arange(g8 * P) // P)[None, :]
    kern = jnp.real(jnp.einsum('ghp,dgp,gpk->dgkh', c_c, pows[:T], b_bar))
    dd = jnp.arange(T)[None, :] - jnp.arange(T)[:, None]
    kst = jnp.where((dd >= 0)[:, :, None, None, None], kern[jnp.clip(dd, 0, T - 1)], 0.0)
    kst = widen(kst.reshape(T, T, nj, g8, H, H), g8)
    kst = jnp.where(same[None, None, None, :, None, :], kst, 0.0)
    toep = kst.transpose(2, 0, 3, 4, 1, 5).reshape(nj, T * LANES, T * LANES)
    win = pows[T - 1 - jnp.arange(T)][:, :, None, :] * jnp.swapaxes(b_bar, 1, 2)[None]
    win = jnp.stack([jnp.real(win), jnp.imag(win)], 0).reshape(2, T, nj, g8, H, P)
    win = jnp.where(same_p[None, None, None, :, None, :], widen(win, g8), 0.0)
    m_in = win.transpose(2, 1, 3, 4, 0, 5).reshape(nj, T * LANES, 2 * g8 * P)
    wout = jnp.swapaxes(c_c, 1, 2)[None] * pows[1:T + 1][:, :, :, None]
    wout = jnp.stack([jnp.real(wout), -jnp.imag(wout)], 0).reshape(2, T, nj, g8, P, H)
    wout = jnp.where(same[None, None, None, :, None, :], widen(wout, g8), 0.0)
    m_out = wout.transpose(2, 0, 3, 4, 1, 5).reshape(nj, 2 * g8 * P, T * LANES)
    a_chunk = pows[T].reshape(nj, 1, g8 * P)
    a_ri = jnp.concatenate([jnp.real(a_chunk), jnp.imag(a_chunk)], axis=1)
    d_flat = jnp.tile(d_skip.astype(f32).reshape(nj, 1, LANES), (1, 1, T))
    return toep.astype(bf16), m_in.astype(bf16), m_out.astype(bf16), a_ri, d_flat


def _s5_kernel(u_ref, toep_ref, min_ref, mout_ref, a_ref, d_ref, y_ref, in_scr, sp_scr):
    T = S5_CHUNK
    n_c = u_ref.shape[0] // T
    half = in_scr.shape[1] // 2
    uf = jnp.concatenate([u_ref[pl.ds(t, n_c, stride=T), :] for t in range(T)], axis=1)
    ufb = uf.astype(bf16)
    kt = 2 * LANES
    y = jnp.concatenate([_dot(ufb[:, :(n + 1) * kt], toep_ref[0, :(n + 1) * kt, n * kt:(n + 1) * kt])
                         for n in range(T * LANES // kt)], axis=1) + uf * d_ref[0]
    in_scr[...] = _dot(ufb, min_ref[0])
    ar = a_ref[0, 0:1, :]
    ai = a_ref[0, 1:2, :]

    def step(c, carry):
        sr, si = carry
        sp_scr[pl.ds(c, 1), :half] = sr
        sp_scr[pl.ds(c, 1), half:] = si
        ir = in_scr[pl.ds(c, 1), :half]
        ii = in_scr[pl.ds(c, 1), half:]
        return ar * sr - ai * si + ir, ar * si + ai * sr + ii

    z = jnp.zeros((1, half), f32)
    lax.fori_loop(0, n_c, step, (z, z), unroll=4)
    y = y + _dot(sp_scr[...].astype(bf16), mout_ref[0])
    yg = jax.nn.gelu(y)
    for t in range(T):
        y_ref[pl.ds(t, n_c, stride=T), :] = yg[:, t * LANES:(t + 1) * LANES]


def _s5(u2, bsz, seq, ops):
    toep, m_in, m_out, a_ri, d_flat = ops
    nj = toep.shape[0]
    n_c = seq // S5_CHUNK
    kdim = S5_CHUNK * LANES
    sdim = m_in.shape[2]
    return pl.pallas_call(
        _s5_kernel,
        grid=(nj, bsz),
        in_specs=[pl.BlockSpec((seq, LANES), lambda j, b: (b, j)),
                  pl.BlockSpec((1, kdim, kdim), lambda j, b: (j, 0, 0)),
                  pl.BlockSpec((1, kdim, sdim), lambda j, b: (j, 0, 0)),
                  pl.BlockSpec((1, sdim, kdim), lambda j, b: (j, 0, 0)),
                  pl.BlockSpec((1, 2, sdim // 2), lambda j, b: (j, 0, 0)),
                  pl.BlockSpec((1, 1, kdim), lambda j, b: (j, 0, 0))],
        out_specs=pl.BlockSpec((seq, LANES), lambda j, b: (b, j)),
        out_shape=jax.ShapeDtypeStruct(u2.shape, f32),
        scratch_shapes=[pltpu.VMEM((n_c, sdim), f32), pltpu.VMEM((n_c, sdim), f32)],
        compiler_params=pltpu.CompilerParams(dimension_semantics=("parallel", "parallel"),
                                             vmem_limit_bytes=VMEM_LIMIT),
        name="s5",
    )(u2, toep, m_in, m_out, a_ri, d_flat)


def _moba_kernel(qt_ref, k_ref, vt_ref, km_ref, o_ref, kaug_ref, s0_scr, s1_scr):
    blk = MOBA_BLOCK
    hd = ATT_HEAD_DIM
    nb = k_ref.shape[0] // blk
    i = pl.program_id(2)

    @pl.when(i == 0)
    def _build():
        lane = lax.broadcasted_iota(jnp.int32, (blk, LANES), 1)

        def build(j, carry):
            rows = pl.ds(pl.multiple_of(j * blk, blk), blk)
            kaug_ref[rows, :LANES] = k_ref[rows, :]
            kaug_ref[rows, LANES:] = jnp.where(lane == j, 1.0, 0.0).astype(bf16)
            return carry
        lax.fori_loop(0, nb, build, 0)

    qt = qt_ref[0]
    qtf = qt.astype(f32) * (hd ** -0.5 * math.log2(math.e))
    km = km_ref[0]
    lane_nb = lax.broadcasted_iota(jnp.int32, (nb, LANES), 1)
    feat = lax.broadcasted_iota(jnp.int32, (LANES, blk), 0)
    blk_id = lax.broadcasted_iota(jnp.int32, (nb, blk), 0)
    blk_f = blk_id.astype(f32)
    q_cols, bias_cols = [], []
    for hh in range(2):
        km_hi, km_lo = _split_bf16(jnp.where((lane_nb < hd) if hh == 0 else (lane_nb >= hd), km, 0.0))
        gate = _dot(km_hi, qt) + _dot(km_lo, qt)
        g = jnp.where(blk_id < i, gate, -jnp.inf)
        sel = blk_id == i
        for _ in range(MOBA_TOPK):
            mx = jnp.max(g, axis=0, keepdims=True)
            first = jnp.min(jnp.where(g == mx, blk_f, float(nb)), axis=0, keepdims=True)
            pick = (blk_f == first) & (mx > -jnp.inf)
            sel = sel | pick
            g = jnp.where(pick, -jnp.inf, g)
        bias_cols.append(jnp.where(sel, 0.0, NEG))
        q_cols.append(jnp.where((feat < hd) if hh == 0 else (feat >= hd), qtf, 0.0))
    qa = jnp.concatenate([jnp.concatenate(q_cols, axis=1), jnp.concatenate(bias_cols, axis=1),
                          jnp.zeros((LANES - nb, 2 * blk), f32)], axis=0).astype(bf16)

    ones_rows = jnp.ones((BF16_ROWS, blk), bf16)
    key_pos = lax.broadcasted_iota(jnp.int32, (blk, 2 * blk), 0)
    q_pos = lax.broadcasted_iota(jnp.int32, (blk, 2 * blk), 1) % blk

    def scores(j):
        j = jnp.minimum(j, nb - 1)
        return _dot(kaug_ref[pl.ds(pl.multiple_of(j * blk, blk), blk), :], qa)

    def update(state, blocks):
        m, acc = state
        m_new = m
        for _, s in blocks:
            m_new = jnp.maximum(m_new, jnp.max(s, axis=0, keepdims=True))
        acc = jnp.exp2(m - m_new) * acc
        for j, s in blocks:
            pb = jnp.exp2(s - m_new).astype(bf16)
            acc = acc + jnp.concatenate(
                [_dot(jnp.concatenate([vt_ref[j, :hd, :], ones_rows], axis=0), pb[:, :blk]),
                 _dot(jnp.concatenate([vt_ref[j, hd:, :], ones_rows], axis=0), pb[:, blk:])], axis=1)
        return m_new, acc

    def pair_scores(p, dst):
        dst[0] = scores(2 * p)
        dst[1] = scores(2 * p + 1)

    def pair_update(state, p, src, masked):
        blocks = []
        for r in range(2):
            j = 2 * p + r
            s = src[r]
            if masked:
                off = jnp.where(j < i, blk, jnp.where(j == i, 0, -2 * blk))
                s = jnp.where(key_pos <= q_pos + off, s, NEG)
            blocks.append((jnp.minimum(j, nb - 1), s))
        return update(state, blocks)

    def quad_body(q, state):
        pair_scores(2 * q + 1, s1_scr)
        state = pair_update(state, 2 * q, s0_scr, False)
        pair_scores(2 * q + 2, s0_scr)
        return pair_update(state, 2 * q + 1, s1_scr, False)

    state = (jnp.full((1, 2 * blk), -jnp.inf, f32), jnp.zeros((hd + BF16_ROWS, 2 * blk), f32))
    n_quad = i // 4
    pair_scores(0, s0_scr)
    state = lax.fori_loop(0, n_quad, quad_body, state)
    pair_scores(2 * n_quad + 1, s1_scr)
    state = pair_update(state, 2 * n_quad, s0_scr, True)
    m, acc = lax.cond(4 * n_quad + 2 <= i,
                      lambda st: pair_update(st, 2 * n_quad + 1, s1_scr, True), lambda st: st, state)
    out = acc[:hd] / acc[hd:hd + 1]
    o_ref[...] = jnp.concatenate([out[:, :blk], out[:, blk:]], axis=0).T.astype(o_ref.dtype)


def _moba(qt, k2, vt, kmean, bsz, seq):
    nb = seq // MOBA_BLOCK
    assert seq % MOBA_BLOCK == 0 and nb % SUBLANES == 0 and nb <= LANES
    npair = D_ATT // LANES
    return pl.pallas_call(
        _moba_kernel,
        grid=(bsz, npair, nb),
        in_specs=[pl.BlockSpec((1, LANES, MOBA_BLOCK), lambda b, h, i: (b * nb + i, h, 0)),
                  pl.BlockSpec((seq, LANES), lambda b, h, i: (b, h)),
                  pl.BlockSpec((nb, LANES, MOBA_BLOCK), lambda b, h, i: (b, h, 0)),
                  pl.BlockSpec((1, nb, LANES), lambda b, h, i: (b, 0, h))],
        out_specs=pl.BlockSpec((MOBA_BLOCK, LANES), lambda b, h, i: (b * nb + i, h)),
        out_shape=jax.ShapeDtypeStruct(k2.shape, bf16),
        scratch_shapes=[pltpu.VMEM((seq, 2 * LANES), bf16),
                        pltpu.VMEM((2, MOBA_BLOCK, 2 * MOBA_BLOCK), f32),
                        pltpu.VMEM((2, MOBA_BLOCK, 2 * MOBA_BLOCK), f32)],
        compiler_params=pltpu.CompilerParams(dimension_semantics=("parallel", "parallel", "arbitrary"),
                                             vmem_limit_bytes=VMEM_LIMIT),
        name="moba",
    )(qt, k2, vt, kmean)


def _outproj_kernel(x_ref, yg_ref, att_ref, wglu_ref, wout_ref, g_ref, b_ref, h_ref):
    yg = yg_ref[...]
    z = yg * jax.nn.sigmoid(_dot(yg.astype(bf16), wglu_ref[...]))
    cat = jnp.concatenate([z.astype(bf16), att_ref[...]], axis=1)
    r = DN_ALPHA * x_ref[...] + _dot(cat, wout_ref[...])
    h_ref[...] = _layernorm(r, g_ref[...], b_ref[...])


def _outproj(x2, yg, att, wglu_b, wout_b, g, b):
    n, d_model = x2.shape
    return pl.pallas_call(
        _outproj_kernel,
        grid=(n // TM_PROJ,),
        in_specs=[pl.BlockSpec((TM_PROJ, d_model), lambda i: (i, 0)),
                  pl.BlockSpec((TM_PROJ, D_SSM), lambda i: (i, 0)),
                  pl.BlockSpec((TM_PROJ, D_ATT), lambda i: (i, 0)),
                  pl.BlockSpec(wglu_b.shape, lambda i: (0, 0)),
                  pl.BlockSpec(wout_b.shape, lambda i: (0, 0)),
                  pl.BlockSpec((1, d_model), lambda i: (0, 0)),
                  pl.BlockSpec((1, d_model), lambda i: (0, 0))],
        out_specs=pl.BlockSpec((TM_PROJ, d_model), lambda i: (i, 0)),
        out_shape=jax.ShapeDtypeStruct((n, d_model), f32),
        compiler_params=pltpu.CompilerParams(dimension_semantics=("parallel",),
                                             vmem_limit_bytes=VMEM_LIMIT),
        name="outproj",
    )(x2, yg, att, wglu_b, wout_b, g, b)


def _bitonic_sort_desc(v):
    n = len(v)
    k = 2
    while k <= n:
        j = k // 2
        while j >= 1:
            for a in range(n):
                b = a ^ j
                if b > a:
                    hi = jnp.maximum(v[a], v[b])
                    lo = jnp.minimum(v[a], v[b])
                    v[a], v[b] = (hi, lo) if (a & k) == 0 else (lo, hi)
            j //= 2
        k *= 2
    return v


def _bitonic_merge_desc(v):
    j = len(v) // 2
    while j >= 1:
        for a in range(len(v)):
            b = a ^ j
            if b > a:
                v[a], v[b] = jnp.maximum(v[a], v[b]), jnp.minimum(v[a], v[b])
        j //= 2
    return v


def _top16_sorted(st):
    k = PEER_TOPK
    x = _bitonic_sort_desc([st[SUBLANES * a:SUBLANES * (a + 1), :] for a in range(PEER_N_KEYS // SUBLANES)])
    for shift in (4, 2, 1):
        y = [pltpu.roll(x[k - 1 - a], shift, 0) for a in range(k)]
        x = _bitonic_merge_desc([jnp.maximum(x[a], y[a]) for a in range(k)])
    return x


def _dup_bf16(x):
    u = pltpu.bitcast(x.astype(bf16).astype(f32), jnp.uint32)
    return pltpu.bitcast(u | (u >> 16), f32)


def _bf16_rows(row):
    return pltpu.bitcast(jnp.broadcast_to(row, (SUBLANES, row.shape[1])), bf16)[None]


def _peer_kernel(n_exp, us_ref, h_ref, wq_ref, sk_ref, u_ref, vt_ref, g_ref, b_ref, o_ref,
                 q_scr, pr_scr, rq_scr, srt_scr, row_scr, acc_scr, hbt_scr, a_scr, unscale_scr):
    eb = pl.program_id(1)
    n_blk = n_exp // EB_PEER
    tt = h_ref.shape[0]
    nk = PEER_N_KEYS
    k = PEER_TOPK

    chunk = 2 * nk
    n_chunk = EB_PEER // chunk

    def activations():
        for s in range(n_chunk):
            a_scr[s] = _dot(u_ref[s * chunk:(s + 1) * chunk, :], hbt_scr[...])

    @pl.when(eb == 0)
    def _prologue():
        ht = h_ref[...].T
        amax = jnp.maximum(jnp.max(jnp.abs(ht), keepdims=True), jnp.finfo(f32).tiny)
        e_tile = jnp.minimum(jnp.floor(jnp.log2(FP8_TARGET_MAX / amax)), FP8_MAX_SHIFT)
        hbt_scr[...] = (ht * jnp.exp2(e_tile)).astype(fp8)
        unscale_scr[0] = (jnp.exp2(-e_tile) * us_ref[0, 0])[0, 0]
        q = _dot(h_ref[...].astype(bf16), wq_ref[...])
        for hc in range(2 * PEER_HEADS):
            q_scr[hc] = q[:, hc * PEER_HALF:(hc + 1) * PEER_HALF]

        def head_sort(h, carry):
            for c in range(2):
                q_hi, q_lo = _split_bf16(q_scr[2 * h + c])
                k_hi, k_lo = _split_bf16(sk_ref[2 * h + c])
                st = _dot_nt(k_hi, q_hi) + _dot_nt(k_hi, q_lo) + _dot_nt(k_lo, q_hi)
                pr_scr[c, h] = st
                srt = _top16_sorted(st)
                for a in range(k):
                    srt_scr[c, a, pl.ds(h, 1), :] = srt[a][0:1, :]
            return carry
        lax.fori_loop(0, PEER_HEADS, head_sort, 0, unroll=4)

        a_s = [srt_scr[0, a] for a in range(k)]
        b_s = [srt_scr[1, a] for a in range(k)]
        cands = [a_s[a] + b_s[b] for a in range(k) for b in range(k) if (a + 1) * (b + 1) <= k]
        cands += [jnp.full_like(cands[0], -jnp.inf)] * (64 - len(cands))
        cs = _bitonic_sort_desc(cands)
        zsum = jnp.zeros_like(cs[0])
        for a in range(k):
            zsum = zsum + jnp.exp(cs[a] - cs[0])
        row_scr[0] = 0.5 * (cs[k - 1] + cs[k])
        row_scr[1] = 1.0 / zsum
        row_scr[2] = a_s[0]
        row_scr[3] = b_s[0]
        row_scr[4] = a_s[k - 1]
        row_scr[5] = b_s[k - 1]

        def head_prep(h, carry):
            hrow = pl.ds(h, 1)
            at = pr_scr[0, h]
            bt = pr_scr[1, h]
            thr = row_scr[0, hrow, :] - at
            r_cnt = jnp.zeros_like(at)
            b_rank = jnp.zeros_like(bt)
            for a in range(k):
                b_a = srt_scr[1, a, hrow, :]
                r_cnt = jnp.where(b_a > thr, float(a + 1), r_cnt)
                b_rank = jnp.where(b_a > bt, float(a + 1), b_rank)
            in_a = at >= row_scr[4, hrow, :]
            p_w = jnp.where(in_a, jnp.exp(at - row_scr[2, hrow, :]), 0.0) * row_scr[1, hrow, :]
            pr_scr[0, h] = _dup_bf16(p_w)
            pr_scr[1, h] = _dup_bf16(jnp.where(in_a, r_cnt, 0.0))
            q_w = jnp.where(bt >= row_scr[5, hrow, :], jnp.exp(bt - row_scr[3, hrow, :]), 0.0)
            rq_scr[0, h] = b_rank.astype(bf16).reshape(nk // BF16_ROWS, BF16_ROWS, tt)
            rq_scr[1, h] = q_w.astype(bf16).reshape(nk // BF16_ROWS, BF16_ROWS, tt)
            return carry
        lax.fori_loop(0, PEER_HEADS, head_prep, 0)
        acc_scr[...] = jnp.zeros_like(acc_scr)

    activations()
    for s in range(n_chunk):
        halves = []
        for half in range(chunk // nk):
            n1 = pl.ds((EB_PEER // nk) * eb + s * (chunk // nk) + half, 1)
            g = jnp.zeros((nk // BF16_ROWS, BF16_ROWS, tt), bf16)
            for h in range(PEER_HEADS):
                p_row = _bf16_rows(pr_scr[0, h, n1, :])
                r_row = _bf16_rows(pr_scr[1, h, n1, :])
                g = g + p_row * jnp.where(rq_scr[0, h] < r_row, rq_scr[1, h], 0)
            halves.append(g.reshape(nk, tt))
        act = a_scr[s].astype(bf16) * unscale_scr[0].astype(bf16)
        w = jnp.concatenate(halves, axis=0) * jax.nn.gelu(act)
        acc_scr[...] += _dot(vt_ref[:, s * chunk:(s + 1) * chunk], w)

    @pl.when(eb == n_blk - 1)
    def _last():
        r = DN_ALPHA * h_ref[...] + acc_scr[...].T
        o_ref[...] = _layernorm(r, g_ref[...], b_ref[...])


def _peer(h1, wq_b, sub_keys, u_q, u_unscale, vt_b, g, b):
    n, d_model = h1.shape
    n_exp = u_q.shape[0]
    tt = TT_PEER
    sk = sub_keys.reshape(2 * PEER_HEADS, PEER_N_KEYS, PEER_HALF).astype(f32)
    n_blk = n_exp // EB_PEER
    const = dict(pipeline_mode=pl.Buffered(1))
    return pl.pallas_call(
        functools.partial(_peer_kernel, n_exp),
        grid=(n // tt, n_blk),
        in_specs=[pl.BlockSpec(memory_space=pltpu.SMEM),
                  pl.BlockSpec((tt, d_model), lambda t, e: (t, 0)),
                  pl.BlockSpec(wq_b.shape, lambda t, e: (0, 0), **const),
                  pl.BlockSpec(sk.shape, lambda t, e: (0, 0, 0), **const),
                  pl.BlockSpec((EB_PEER, d_model), lambda t, e: (e, 0)),
                  pl.BlockSpec((d_model, EB_PEER), lambda t, e: (0, e)),
                  pl.BlockSpec((1, d_model), lambda t, e: (0, 0), **const),
                  pl.BlockSpec((1, d_model), lambda t, e: (0, 0), **const)],
        out_specs=pl.BlockSpec((tt, d_model), lambda t, e: (t, 0)),
        out_shape=jax.ShapeDtypeStruct((n, d_model), f32),
        scratch_shapes=[pltpu.VMEM((2 * PEER_HEADS, tt, PEER_HALF), f32),
                        pltpu.VMEM((2, PEER_HEADS, PEER_N_KEYS, tt), f32),
                        pltpu.VMEM((2, PEER_HEADS, PEER_N_KEYS // BF16_ROWS, BF16_ROWS, tt), bf16),
                        pltpu.VMEM((2, PEER_TOPK, PEER_HEADS, tt), f32),
                        pltpu.VMEM((6, PEER_HEADS, tt), f32),
                        pltpu.VMEM((d_model, tt), f32),
                        pltpu.VMEM((d_model, tt), fp8),
                        pltpu.VMEM((EB_PEER // (2 * PEER_N_KEYS), 2 * PEER_N_KEYS, tt), f32),
                        pltpu.SMEM((1,), f32)],
        compiler_params=pltpu.CompilerParams(dimension_semantics=("parallel", "arbitrary"),
                                             vmem_limit_bytes=VMEM_LIMIT),
        name="peer",
    )(u_unscale.reshape(1, 1), h1, wq_b, sk, u_q, vt_b, g, b)


def _quantize_table(t):
    amax = jnp.maximum(jnp.max(jnp.abs(t)), jnp.finfo(f32).tiny)
    e = jnp.minimum(jnp.floor(jnp.log2(FP8_TARGET_MAX / amax)), FP8_MAX_SHIFT)
    return (t * jnp.exp2(e)).astype(fp8), jnp.exp2(-e).astype(f32)


def kernel(x, w_in, ssm_a_re, ssm_a_im, ssm_log_dt, ssm_b_re, ssm_b_im, ssm_c_re, ssm_c_im, ssm_d,
           ssm_w_glu, w_out, ln1_g, ln1_b, peer_w_q, peer_sub_keys, peer_u, peer_v, ln2_g, ln2_b):
    bsz, seq, d_model = x.shape
    n = bsz * seq
    h = x.reshape(n, d_model)
    for l in range(w_in.shape[0]):
        u2, k2, qt, vt, kmean = _inproj(h, w_in[l])
        ops = _s5_operators(ssm_a_re[l], ssm_a_im[l], ssm_log_dt[l], ssm_b_re[l], ssm_b_im[l],
                            ssm_c_re[l], ssm_c_im[l], ssm_d[l])
        yg = _s5(u2, bsz, seq, ops)
        att = _moba(qt, k2, vt, kmean.reshape(bsz, seq // MOBA_BLOCK, D_ATT), bsz, seq)
        h1 = _outproj(h, yg, att, ssm_w_glu[l].astype(bf16), w_out[l].astype(bf16),
                      ln1_g[l].reshape(1, d_model), ln1_b[l].reshape(1, d_model))
        u_q, u_unscale = _quantize_table(peer_u[l])
        h = _peer(h1, peer_w_q[l].astype(bf16), peer_sub_keys[l], u_q, u_unscale,
                  peer_v[l].T.astype(bf16), ln2_g[l].reshape(1, d_model), ln2_b[l].reshape(1, d_model))
    return h.reshape(bsz, seq, d_model)
```

```python
import functools
import math

import jax
import jax.numpy as jnp
from jax import lax
from jax.experimental import pallas as pl
from jax.experimental.pallas import tpu as pltpu

f32 = jnp.float32
bf16 = jnp.bfloat16
fp8 = jnp.float8_e4m3fn
FP8_TARGET_MAX = 256.0
FP8_MAX_SHIFT = 48.0

D_SSM = 512
SSM_GROUP = 16
SSM_STATE = 64
N_ATT_HEADS = 8
ATT_HEAD_DIM = 64
D_ATT = N_ATT_HEADS * ATT_HEAD_DIM
MOBA_BLOCK = 256
MOBA_TOPK = 3
PEER_HEADS = 8
PEER_HALF = 128
PEER_N_KEYS = 128
PEER_TOPK = 16
LN_EPS = 1e-5
DEPTH = 1
DN_ALPHA = (2.0 * DEPTH) ** 0.25
NEG = -1e30

LANES = 128
SUBLANES = 8
BF16_ROWS = 2 * SUBLANES
VMEM_LIMIT = 48 * 1024 * 1024

TM_PROJ = 1024
S5_CHUNK = 8
S5_LANE_GROUPS = LANES // SSM_GROUP
TT_PEER = 512
EB_PEER = 2048


def _dot(a, b):
    return jnp.dot(a, b, preferred_element_type=f32)


def _dot_nt(a, b):
    return lax.dot_general(a, b, (((1,), (1,)), ((), ())), preferred_element_type=f32)


def _split_bf16(x):
    hi = x.astype(bf16)
    lo = (x - hi.astype(f32)).astype(bf16)
    return hi, lo


def _layernorm(r, g, b):
    mu = jnp.mean(r, axis=-1, keepdims=True)
    d = r - mu
    var = jnp.mean(d * d, axis=-1, keepdims=True)
    return d * lax.rsqrt(var + LN_EPS) * g + b


def _inproj_kernel(x_ref, wuk_ref, wqvt_ref, u_ref, k_ref, qt_ref, vt_ref, km_ref):
    xb = x_ref[...].astype(bf16)
    acc = _dot(xb, wuk_ref[...])
    u_ref[...] = acc[:, :D_SSM]
    kk = acc[:, D_SSM:]
    k_ref[...] = kk.astype(bf16)
    qvt = _dot_nt(wqvt_ref[...], xb)
    for r in range(TM_PROJ // MOBA_BLOCK):
        cols = slice(r * MOBA_BLOCK, (r + 1) * MOBA_BLOCK)
        km_ref[r] = jnp.mean(kk[cols], axis=0, keepdims=True)
        qt_ref[r] = qvt[:D_ATT, cols].astype(bf16)
        vt_ref[r] = qvt[D_ATT:, cols].astype(bf16)


def _inproj(x2, w_in):
    n, d_model = x2.shape
    nb = TM_PROJ // MOBA_BLOCK
    w_uk = jnp.concatenate([w_in[:, :D_SSM], w_in[:, D_SSM + D_ATT:D_SSM + 2 * D_ATT]], axis=1).astype(bf16)
    w_qvt = jnp.concatenate([w_in[:, D_SSM:D_SSM + D_ATT], w_in[:, D_SSM + 2 * D_ATT:]], axis=1).T.astype(bf16)
    return pl.pallas_call(
        _inproj_kernel,
        grid=(n // TM_PROJ,),
        in_specs=[pl.BlockSpec((TM_PROJ, d_model), lambda i: (i, 0)),
                  pl.BlockSpec(w_uk.shape, lambda i: (0, 0)),
                  pl.BlockSpec(w_qvt.shape, lambda i: (0, 0))],
        out_specs=[pl.BlockSpec((TM_PROJ, D_SSM), lambda i: (i, 0)),
                   pl.BlockSpec((TM_PROJ, D_ATT), lambda i: (i, 0)),
                   pl.BlockSpec((nb, D_ATT, MOBA_BLOCK), lambda i: (i, 0, 0)),
                   pl.BlockSpec((nb, D_ATT, MOBA_BLOCK), lambda i: (i, 0, 0)),
                   pl.BlockSpec((nb, 1, D_ATT), lambda i: (i, 0, 0))],
        out_shape=[jax.ShapeDtypeStruct((n, D_SSM), f32),
                   jax.ShapeDtypeStruct((n, D_ATT), bf16),
                   jax.ShapeDtypeStruct((n // MOBA_BLOCK, D_ATT, MOBA_BLOCK), bf16),
                   jax.ShapeDtypeStruct((n // MOBA_BLOCK, D_ATT, MOBA_BLOCK), bf16),
                   jax.ShapeDtypeStruct((n // MOBA_BLOCK, 1, D_ATT), f32)],
        compiler_params=pltpu.CompilerParams(dimension_semantics=("parallel",),
                                             vmem_limit_bytes=VMEM_LIMIT),
        name="inproj",
    )(x2, w_uk, w_qvt)


def _s5_operators(a_re, a_im, log_dt, b_re, b_im, c_re, c_im, d_skip):
    T = S5_CHUNK
    G, P = a_re.shape
    H = SSM_GROUP
    nj = G // S5_LANE_GROUPS
    g8 = S5_LANE_GROUPS
    lam = lax.complex(a_re.astype(f32), a_im.astype(f32))
    dt = jnp.exp(log_dt.astype(f32))[:, None]
    lam_bar = jnp.exp(lam * dt)
    b_c = lax.complex(b_re.astype(f32), b_im.astype(f32))
    b_bar = ((lam_bar - 1.0) / lam)[:, :, None] * b_c
    c_c = lax.complex(c_re.astype(f32), c_im.astype(f32))
    pows = jnp.exp((lam * dt)[None] * jnp.arange(T + 1, dtype=f32)[:, None, None])
    def widen(x, reps):
        n = x.shape[-1]
        rep = jnp.tile(jnp.eye(n, dtype=f32), (1, reps))
        return jnp.dot(x.reshape(-1, n), rep).reshape(x.shape[:-1] + (reps * n,))

    same = jnp.arange(g8)[:, None] == (jnp.arange(g8 * H) // H)[None, :]
    same_p = jnp.arange(g8)[:, None] == (jnp.arange(g8 * P) // P)[None, :]
    kern = jnp.real(jnp.einsum('ghp,dgp,gpk->dgkh', c_c, pows[:T], b_bar))
    dd = jnp.arange(T)[None, :] - jnp.arange(T)[:, None]
    kst = jnp.where((dd >= 0)[:, :, None, None, None], kern[jnp.clip(dd, 0, T - 1)], 0.0)
    kst = widen(kst.reshape(T, T, nj, g8, H, H), g8)
    kst = jnp.where(same[None, None, None, :, None, :], kst, 0.0)
    toep = kst.transpose(2, 0, 3, 4, 1, 5).reshape(nj, T * LANES, T * LANES)
    win = pows[T - 1 - jnp.arange(T)][:, :, None, :] * jnp.swapaxes(b_bar, 1, 2)[None]
    win = jnp.stack([jnp.real(win), jnp.imag(win)], 0).reshape(2, T, nj, g8, H, P)
    win = jnp.where(same_p[None, None, None, :, None, :], widen(win, g8), 0.0)
    m_in = win.transpose(2, 1, 3, 4, 0, 5).reshape(nj, T * LANES, 2 * g8 * P)
    wout = jnp.swapaxes(c_c, 1, 2)[None] * pows[1:T + 1][:, :, :, None]
    wout = jnp.stack([jnp.real(wout), -jnp.imag(wout)], 0).reshape(2, T, nj, g8, P, H)
    wout = jnp.where(same[None, None, None, :, None, :], widen(wout, g8), 0.0)
    m_out = wout.transpose(2, 0, 3, 4, 1, 5).reshape(nj, 2 * g8 * P, T * LANES)
    a_chunk = pows[T].reshape(nj, 1, g8 * P)
    a_ri = jnp.concatenate([jnp.real(a_chunk), jnp.imag(a_chunk)], axis=1)
    d_flat = jnp.tile(d_skip.astype(f32).reshape(nj, 1, LANES), (1, 1, T))
    return toep.astype(bf16), m_in.astype(bf16), m_out.astype(bf16), a_ri, d_flat


def _s5_kernel(u_ref, toep_ref, min_ref, mout_ref, a_ref, d_ref, y_ref, in_scr, sp_scr):
    T = S5_CHUNK
    n_c = u_ref.shape[0] // T
    half = in_scr.shape[1] // 2
    uf = jnp.concatenate([u_ref[pl.ds(t, n_c, stride=T), :] for t in range(T)], axis=1)
    ufb = uf.astype(bf16)
    kt = 2 * LANES
    y = jnp.concatenate([_dot(ufb[:, :(n + 1) * kt], toep_ref[0, :(n + 1) * kt, n * kt:(n + 1) * kt])
                         for n in range(T * LANES // kt)], axis=1) + uf * d_ref[0]
    in_scr[...] = _dot(ufb, min_ref[0])
    ar = a_ref[0, 0:1, :]
    ai = a_ref[0, 1:2, :]

    def step(c, carry):
        sr, si = carry
        sp_scr[pl.ds(c, 1), :half] = sr
        sp_scr[pl.ds(c, 1), half:] = si
        ir = in_scr[pl.ds(c, 1), :half]
        ii = in_scr[pl.ds(c, 1), half:]
        return ar * sr - ai * si + ir, ar * si + ai * sr + ii

    z = jnp.zeros((1, half), f32)
    lax.fori_loop(0, n_c, step, (z, z), unroll=4)
    y = y + _dot(sp_scr[...].astype(bf16), mout_ref[0])
    yg = jax.nn.gelu(y)
    for t in range(T):
        y_ref[pl.ds(t, n_c, stride=T), :] = yg[:, t * LANES:(t + 1) * LANES]


def _s5(u2, bsz, seq, ops):
    toep, m_in, m_out, a_ri, d_flat = ops
    nj = toep.shape[0]
    n_c = seq // S5_CHUNK
    kdim = S5_CHUNK * LANES
    sdim = m_in.shape[2]
    return pl.pallas_call(
        _s5_kernel,
        grid=(nj, bsz),
        in_specs=[pl.BlockSpec((seq, LANES), lambda j, b: (b, j)),
                  pl.BlockSpec((1, kdim, kdim), lambda j, b: (j, 0, 0)),
                  pl.BlockSpec((1, kdim, sdim), lambda j, b: (j, 0, 0)),
                  pl.BlockSpec((1, sdim, kdim), lambda j, b: (j, 0, 0)),
                  pl.BlockSpec((1, 2, sdim // 2), lambda j, b: (j, 0, 0)),
                  pl.BlockSpec((1, 1, kdim), lambda j, b: (j, 0, 0))],
        out_specs=pl.BlockSpec((seq, LANES), lambda j, b: (b, j)),
        out_shape=jax.ShapeDtypeStruct(u2.shape, f32),
        scratch_shapes=[pltpu.VMEM((n_c, sdim), f32), pltpu.VMEM((n_c, sdim), f32)],
        compiler_params=pltpu.CompilerParams(dimension_semantics=("parallel", "parallel"),
                                             vmem_limit_bytes=VMEM_LIMIT),
        name="s5",
    )(u2, toep, m_in, m_out, a_ri, d_flat)


def _moba_kernel(qt_lo_ref, qt_hi_ref, k_ref, vt_ref, km_ref, o_lo_ref, o_hi_ref, kaug_ref, s0_scr, s1_scr):
    blk = MOBA_BLOCK
    nb = k_ref.shape[0] // blk
    t = pl.program_id(2)

    @pl.when(t == 0)
    def _build():
        lane = lax.broadcasted_iota(jnp.int32, (blk, LANES), 1)

        def build(j, carry):
            rows = pl.ds(pl.multiple_of(j * blk, blk), blk)
            kaug_ref[rows, :LANES] = k_ref[rows, :]
            kaug_ref[rows, LANES:] = jnp.where(lane == j, 1.0, 0.0).astype(bf16)
            return carry
        lax.fori_loop(0, nb, build, 0)

    _moba_block(t, qt_lo_ref, vt_ref, km_ref, o_lo_ref, kaug_ref, s0_scr, s1_scr)
    _moba_block(nb - 1 - t, qt_hi_ref, vt_ref, km_ref, o_hi_ref, kaug_ref, s0_scr, s1_scr)


def _moba_block(i, qt_ref, vt_ref, km_ref, o_ref, kaug_ref, s0_scr, s1_scr):
    blk = MOBA_BLOCK
    hd = ATT_HEAD_DIM
    nb = kaug_ref.shape[0] // blk
    qt = qt_ref[0]
    qtf = qt.astype(f32) * (hd ** -0.5 * math.log2(math.e))
    km = km_ref[0]
    lane_nb = lax.broadcasted_iota(jnp.int32, (nb, LANES), 1)
    feat = lax.broadcasted_iota(jnp.int32, (LANES, blk), 0)
    blk_id = lax.broadcasted_iota(jnp.int32, (nb, blk), 0)
    blk_f = blk_id.astype(f32)
    q_cols, bias_cols = [], []
    for hh in range(2):
        km_hi, km_lo = _split_bf16(jnp.where((lane_nb < hd) if hh == 0 else (lane_nb >= hd), km, 0.0))
        gate = _dot(km_hi, qt) + _dot(km_lo, qt)
        g = jnp.where(blk_id < i, gate, -jnp.inf)
        sel = blk_id == i
        for _ in range(MOBA_TOPK):
            mx = jnp.max(g, axis=0, keepdims=True)
            first = jnp.min(jnp.where(g == mx, blk_f, float(nb)), axis=0, keepdims=True)
            pick = (blk_f == first) & (mx > -jnp.inf)
            sel = sel | pick
            g = jnp.where(pick, -jnp.inf, g)
        bias_cols.append(jnp.where(sel, 0.0, NEG))
        q_cols.append(jnp.where((feat < hd) if hh == 0 else (feat >= hd), qtf, 0.0))
    qa = jnp.concatenate([jnp.concatenate(q_cols, axis=1), jnp.concatenate(bias_cols, axis=1),
                          jnp.zeros((LANES - nb, 2 * blk), f32)], axis=0).astype(bf16)

    ones_rows = jnp.ones((BF16_ROWS, blk), bf16)
    key_pos = lax.broadcasted_iota(jnp.int32, (blk, 2 * blk), 0)
    q_pos = lax.broadcasted_iota(jnp.int32, (blk, 2 * blk), 1) % blk

    def scores(j):
        j = jnp.minimum(j, nb - 1)
        return _dot(kaug_ref[pl.ds(pl.multiple_of(j * blk, blk), blk), :], qa)

    def update(state, blocks):
        m, acc = state
        m_new = m
        for _, s in blocks:
            m_new = jnp.maximum(m_new, jnp.max(s, axis=0, keepdims=True))
        acc = jnp.exp2(m - m_new) * acc
        for j, s in blocks:
            pb = jnp.exp2(s - m_new).astype(bf16)
            acc = acc + jnp.concatenate(
                [_dot(jnp.concatenate([vt_ref[j, :hd, :], ones_rows], axis=0), pb[:, :blk]),
                 _dot(jnp.concatenate([vt_ref[j, hd:, :], ones_rows], axis=0), pb[:, blk:])], axis=1)
        return m_new, acc

    def pair_scores(p, dst):
        dst[0] = scores(2 * p)
        dst[1] = scores(2 * p + 1)

    def pair_update(state, p, src, masked):
        blocks = []
        for r in range(2):
            j = 2 * p + r
            s = src[r]
            if masked:
                off = jnp.where(j < i, blk, jnp.where(j == i, 0, -2 * blk))
                s = jnp.where(key_pos <= q_pos + off, s, NEG)
            blocks.append((jnp.minimum(j, nb - 1), s))
        return update(state, blocks)

    def quad_body(q, state):
        pair_scores(2 * q + 1, s1_scr)
        state = pair_update(state, 2 * q, s0_scr, False)
        pair_scores(2 * q + 2, s0_scr)
        return pair_update(state, 2 * q + 1, s1_scr, False)

    state = (jnp.full((1, 2 * blk), -jnp.inf, f32), jnp.zeros((hd + BF16_ROWS, 2 * blk), f32))
    n_quad = i // 4
    pair_scores(0, s0_scr)
    state = lax.fori_loop(0, n_quad, quad_body, state)
    pair_scores(2 * n_quad + 1, s1_scr)
    state = pair_update(state, 2 * n_quad, s0_scr, True)
    m, acc = lax.cond(4 * n_quad + 2 <= i,
                      lambda st: pair_update(st, 2 * n_quad + 1, s1_scr, True), lambda st: st, state)
    out = acc[:hd] / acc[hd:hd + 1]
    o_ref[...] = jnp.concatenate([out[:, :blk], out[:, blk:]], axis=0).T.astype(o_ref.dtype)


def _moba(qt, k2, vt, kmean, bsz, seq):
    nb = seq // MOBA_BLOCK
    assert seq % MOBA_BLOCK == 0 and nb % (2 * SUBLANES) == 0 and nb <= LANES
    npair = D_ATT // LANES
    nh = nb // 2
    half_shape = jax.ShapeDtypeStruct((k2.shape[0] // 2, k2.shape[1]), bf16)
    return pl.pallas_call(
        _moba_kernel,
        grid=(bsz, npair, nh),
        in_specs=[pl.BlockSpec((1, LANES, MOBA_BLOCK), lambda b, h, t: (b * nb + t, h, 0)),
                  pl.BlockSpec((1, LANES, MOBA_BLOCK), lambda b, h, t: (b * nb + nb - 1 - t, h, 0)),
                  pl.BlockSpec((seq, LANES), lambda b, h, t: (b, h)),
                  pl.BlockSpec((nb, LANES, MOBA_BLOCK), lambda b, h, t: (b, h, 0)),
                  pl.BlockSpec((1, nb, LANES), lambda b, h, t: (b, 0, h))],
        out_specs=[pl.BlockSpec((MOBA_BLOCK, LANES), lambda b, h, t: (b * nh + t, h)),
                   pl.BlockSpec((MOBA_BLOCK, LANES), lambda b, h, t: (b * nh + nh - 1 - t, h))],
        out_shape=[half_shape, half_shape],
        scratch_shapes=[pltpu.VMEM((seq, 2 * LANES), bf16),
                        pltpu.VMEM((2, MOBA_BLOCK, 2 * MOBA_BLOCK), f32),
                        pltpu.VMEM((2, MOBA_BLOCK, 2 * MOBA_BLOCK), f32)],
        compiler_params=pltpu.CompilerParams(dimension_semantics=("parallel", "parallel", "arbitrary"),
                                             vmem_limit_bytes=VMEM_LIMIT),
        name="moba",
    )(qt, qt, k2, vt, kmean)


def _outproj_kernel(tiles_per_half, x_ref, yg_ref, att_lo_ref, att_hi_ref, wglu_ref, wout_ref, g_ref, b_ref, h_ref):
    first_half = (pl.program_id(0) // tiles_per_half) % 2 == 0
    att = jnp.where(first_half, att_lo_ref[...], att_hi_ref[...])
    yg = yg_ref[...]
    z = yg * jax.nn.sigmoid(_dot(yg.astype(bf16), wglu_ref[...]))
    cat = jnp.concatenate([z.astype(bf16), att], axis=1)
    r = DN_ALPHA * x_ref[...] + _dot(cat, wout_ref[...])
    h_ref[...] = _layernorm(r, g_ref[...], b_ref[...])


def _outproj(x2, yg, att_lo, att_hi, seq, wglu_b, wout_b, g, b):
    n, d_model = x2.shape
    tph = seq // (2 * TM_PROJ)
    assert seq % (2 * TM_PROJ) == 0

    def half_tile(i, second):
        r = i % (2 * tph)
        return (i // (2 * tph)) * tph + (jnp.maximum(r - tph, 0) if second else jnp.minimum(r, tph - 1))

    return pl.pallas_call(
        functools.partial(_outproj_kernel, tph),
        grid=(n // TM_PROJ,),
        in_specs=[pl.BlockSpec((TM_PROJ, d_model), lambda i: (i, 0)),
                  pl.BlockSpec((TM_PROJ, D_SSM), lambda i: (i, 0)),
                  pl.BlockSpec((TM_PROJ, D_ATT), lambda i: (half_tile(i, False), 0)),
                  pl.BlockSpec((TM_PROJ, D_ATT), lambda i: (half_tile(i, True), 0)),
                  pl.BlockSpec(wglu_b.shape, lambda i: (0, 0)),
                  pl.BlockSpec(wout_b.shape, lambda i: (0, 0)),
                  pl.BlockSpec((1, d_model), lambda i: (0, 0)),
                  pl.BlockSpec((1, d_model), lambda i: (0, 0))],
        out_specs=pl.BlockSpec((TM_PROJ, d_model), lambda i: (i, 0)),
        out_shape=jax.ShapeDtypeStruct((n, d_model), f32),
        compiler_params=pltpu.CompilerParams(dimension_semantics=("parallel",),
                                             vmem_limit_bytes=VMEM_LIMIT),
        name="outproj",
    )(x2, yg, att_lo, att_hi, wglu_b, wout_b, g, b)


def _bitonic_sort_desc(v):
    n = len(v)
    k = 2
    while k <= n:
        j = k // 2
        while j >= 1:
            for a in range(n):
                b = a ^ j
                if b > a:
                    hi = jnp.maximum(v[a], v[b])
                    lo = jnp.minimum(v[a], v[b])
                    v[a], v[b] = (hi, lo) if (a & k) == 0 else (lo, hi)
            j //= 2
        k *= 2
    return v


def _bitonic_merge_desc(v):
    j = len(v) // 2
    while j >= 1:
        for a in range(len(v)):
            b = a ^ j
            if b > a:
                v[a], v[b] = jnp.maximum(v[a], v[b]), jnp.minimum(v[a], v[b])
        j //= 2
    return v


def _top16_sorted(st):
    k = PEER_TOPK
    x = _bitonic_sort_desc([st[SUBLANES * a:SUBLANES * (a + 1), :] for a in range(PEER_N_KEYS // SUBLANES)])
    for shift in (4, 2, 1):
        y = [pltpu.roll(x[k - 1 - a], shift, 0) for a in range(k)]
        x = _bitonic_merge_desc([jnp.maximum(x[a], y[a]) for a in range(k)])
    return x


def _dup_bf16(x):
    u = pltpu.bitcast(x.astype(bf16).astype(f32), jnp.uint32)
    return pltpu.bitcast(u | (u >> 16), f32)


def _bf16_rows(row):
    return pltpu.bitcast(jnp.broadcast_to(row, (SUBLANES, row.shape[1])), bf16)[None]


def _peer_kernel(n_exp, us_ref, h_ref, wq_ref, sk_ref, u_ref, vt_ref, g_ref, b_ref, o_ref,
                 q_scr, pr_scr, rq_scr, srt_scr, row_scr, acc_scr, hbt_scr, a_scr, unscale_scr):
    eb = pl.program_id(1)
    n_blk = n_exp // EB_PEER
    tt = h_ref.shape[0]
    nk = PEER_N_KEYS
    k = PEER_TOPK

    chunk = 2 * nk
    n_chunk = EB_PEER // chunk

    def activations():
        for s in range(n_chunk):
            a_scr[s] = _dot(u_ref[s * chunk:(s + 1) * chunk, :], hbt_scr[...])

    @pl.when(eb == 0)
    def _prologue():
        ht = h_ref[...].T
        amax = jnp.maximum(jnp.max(jnp.abs(ht), keepdims=True), jnp.finfo(f32).tiny)
        e_tile = jnp.minimum(jnp.floor(jnp.log2(FP8_TARGET_MAX / amax)), FP8_MAX_SHIFT)
        hbt_scr[...] = (ht * jnp.exp2(e_tile)).astype(fp8)
        unscale_scr[0] = (jnp.exp2(-e_tile) * us_ref[0, 0])[0, 0]
        q = _dot(h_ref[...].astype(bf16), wq_ref[...])
        for hc in range(2 * PEER_HEADS):
            q_scr[hc] = q[:, hc * PEER_HALF:(hc + 1) * PEER_HALF]

        def head_sort(h, carry):
            for c in range(2):
                q_hi, q_lo = _split_bf16(q_scr[2 * h + c])
                k_hi, k_lo = _split_bf16(sk_ref[2 * h + c])
                st = _dot_nt(k_hi, q_hi) + _dot_nt(k_hi, q_lo) + _dot_nt(k_lo, q_hi)
                pr_scr[c, h] = st
                srt = _top16_sorted(st)
                for a in range(k):
                    srt_scr[c, a, pl.ds(h, 1), :] = srt[a][0:1, :]
            return carry
        lax.fori_loop(0, PEER_HEADS, head_sort, 0, unroll=4)

        a_s = [srt_scr[0, a] for a in range(k)]
        b_s = [srt_scr[1, a] for a in range(k)]
        cands = [a_s[a] + b_s[b] for a in range(k) for b in range(k) if (a + 1) * (b + 1) <= k]
        cands += [jnp.full_like(cands[0], -jnp.inf)] * (64 - len(cands))
        cs = _bitonic_sort_desc(cands)
        zsum = jnp.zeros_like(cs[0])
        for a in range(k):
            zsum = zsum + jnp.exp(cs[a] - cs[0])
        row_scr[0] = 0.5 * (cs[k - 1] + cs[k])
        row_scr[1] = 1.0 / zsum
        row_scr[2] = a_s[0]
        row_scr[3] = b_s[0]
        row_scr[4] = a_s[k - 1]
        row_scr[5] = b_s[k - 1]

        def head_prep(h, carry):
            hrow = pl.ds(h, 1)
            at = pr_scr[0, h]
            bt = pr_scr[1, h]
            thr = row_scr[0, hrow, :] - at
            r_cnt = jnp.zeros_like(at)
            b_rank = jnp.zeros_like(bt)
            for a in range(k):
                b_a = srt_scr[1, a, hrow, :]
                r_cnt = jnp.where(b_a > thr, float(a + 1), r_cnt)
                b_rank = jnp.where(b_a > bt, float(a + 1), b_rank)
            in_a = at >= row_scr[4, hrow, :]
            p_w = jnp.where(in_a, jnp.exp(at - row_scr[2, hrow, :]), 0.0) * row_scr[1, hrow, :]
            pr_scr[0, h] = _dup_bf16(p_w)
            pr_scr[1, h] = _dup_bf16(jnp.where(in_a, r_cnt, 0.0))
            q_w = jnp.where(bt >= row_scr[5, hrow, :], jnp.exp(bt - row_scr[3, hrow, :]), 0.0)
            rq_scr[0, h] = b_rank.astype(bf16).reshape(nk // BF16_ROWS, BF16_ROWS, tt)
            rq_scr[1, h] = q_w.astype(bf16).reshape(nk // BF16_ROWS, BF16_ROWS, tt)
            return carry
        lax.fori_loop(0, PEER_HEADS, head_prep, 0)
        acc_scr[...] = jnp.zeros_like(acc_scr)

    activations()
    for s in range(n_chunk):
        halves = []
        for half in range(chunk // nk):
            n1 = pl.ds((EB_PEER // nk) * eb + s * (chunk // nk) + half, 1)
            g = jnp.zeros((nk // BF16_ROWS, BF16_ROWS, tt), bf16)
            for h in range(PEER_HEADS):
                p_row = _bf16_rows(pr_scr[0, h, n1, :])
                r_row = _bf16_rows(pr_scr[1, h, n1, :])
                g = g + p_row * jnp.where(rq_scr[0, h] < r_row, rq_scr[1, h], 0)
            halves.append(g.reshape(nk, tt))
        act = a_scr[s].astype(bf16) * unscale_scr[0].astype(bf16)
        w = jnp.concatenate(halves, axis=0) * jax.nn.gelu(act)
        acc_scr[...] += _dot(vt_ref[:, s * chunk:(s + 1) * chunk], w)

    @pl.when(eb == n_blk - 1)
    def _last():
        r = DN_ALPHA * h_ref[...] + acc_scr[...].T
        o_ref[...] = _layernorm(r, g_ref[...], b_ref[...])


def _peer(h1, wq_b, sub_keys, u_q, u_unscale, vt_b, g, b):
    n, d_model = h1.shape
    n_exp = u_q.shape[0]
    tt = TT_PEER
    sk = sub_keys.reshape(2 * PEER_HEADS, PEER_N_KEYS, PEER_HALF).astype(f32)
    n_blk = n_exp // EB_PEER
    const = dict(pipeline_mode=pl.Buffered(1))
    return pl.pallas_call(
        functools.partial(_peer_kernel, n_exp),
        grid=(n // tt, n_blk),
        in_specs=[pl.BlockSpec(memory_space=pltpu.SMEM),
                  pl.BlockSpec((tt, d_model), lambda t, e: (t, 0)),
                  pl.BlockSpec(wq_b.shape, lambda t, e: (0, 0), **const),
                  pl.BlockSpec(sk.shape, lambda t, e: (0, 0, 0), **const),
                  pl.BlockSpec((EB_PEER, d_model), lambda t, e: (e, 0)),
                  pl.BlockSpec((d_model, EB_PEER), lambda t, e: (0, e)),
                  pl.BlockSpec((1, d_model), lambda t, e: (0, 0), **const),
                  pl.BlockSpec((1, d_model), lambda t, e: (0, 0), **const)],
        out_specs=pl.BlockSpec((tt, d_model), lambda t, e: (t, 0)),
        out_shape=jax.ShapeDtypeStruct((n, d_model), f32),
        scratch_shapes=[pltpu.VMEM((2 * PEER_HEADS, tt, PEER_HALF), f32),
                        pltpu.VMEM((2, PEER_HEADS, PEER_N_KEYS, tt), f32),
                        pltpu.VMEM((2, PEER_HEADS, PEER_N_KEYS // BF16_ROWS, BF16_ROWS, tt), bf16),
                        pltpu.VMEM((2, PEER_TOPK, PEER_HEADS, tt), f32),
                        pltpu.VMEM((6, PEER_HEADS, tt), f32),
                        pltpu.VMEM((d_model, tt), f32),
                        pltpu.VMEM((d_model, tt), fp8),
                        pltpu.VMEM((EB_PEER // (2 * PEER_N_KEYS), 2 * PEER_N_KEYS, tt), f32),
                        pltpu.SMEM((1,), f32)],
        compiler_params=pltpu.CompilerParams(dimension_semantics=("parallel", "arbitrary"),
                                             vmem_limit_bytes=VMEM_LIMIT),
        name="peer",
    )(u_unscale.reshape(1, 1), h1, wq_b, sk, u_q, vt_b, g, b)


def _quantize_table(t):
    amax = jnp.maximum(jnp.max(jnp.abs(t)), jnp.finfo(f32).tiny)
    e = jnp.minimum(jnp.floor(jnp.log2(FP8_TARGET_MAX / amax)), FP8_MAX_SHIFT)
    return (t * jnp.exp2(e)).astype(fp8), jnp.exp2(-e).astype(f32)


def kernel(x, w_in, ssm_a_re, ssm_a_im, ssm_log_dt, ssm_b_re, ssm_b_im, ssm_c_re, ssm_c_im, ssm_d,
           ssm_w_glu, w_out, ln1_g, ln1_b, peer_w_q, peer_sub_keys, peer_u, peer_v, ln2_g, ln2_b):
    bsz, seq, d_model = x.shape
    n = bsz * seq
    h = x.reshape(n, d_model)
    for l in range(w_in.shape[0]):
        u2, k2, qt, vt, kmean = _inproj(h, w_in[l])
        ops = _s5_operators(ssm_a_re[l], ssm_a_im[l], ssm_log_dt[l], ssm_b_re[l], ssm_b_im[l],
                            ssm_c_re[l], ssm_c_im[l], ssm_d[l])
        yg = _s5(u2, bsz, seq, ops)
        att_lo, att_hi = _moba(qt, k2, vt, kmean.reshape(bsz, seq // MOBA_BLOCK, D_ATT), bsz, seq)
        h1 = _outproj(h, yg, att_lo, att_hi, seq,ssm_w_glu[l].astype(bf16), w_out[l].astype(bf16),
                      ln1_g[l].reshape(1, d_model), ln1_b[l].reshape(1, d_model))
        u_q, u_unscale = _quantize_table(peer_u[l])
        h = _peer(h1, peer_w_q[l].astype(bf16), peer_sub_keys[l], u_q, u_unscale,
                  peer_v[l].T.astype(bf16), ln2_g[l].reshape(1, d_model), ln2_b[l].reshape(1, d_model))
    return h.reshape(bsz, seq, d_model)
```

```python
import functools
import math

import jax
import jax.numpy as jnp
from jax import lax
from jax.experimental import pallas as pl
from jax.experimental.pallas import tpu as pltpu

f32 = jnp.float32
bf16 = jnp.bfloat16
fp8 = jnp.float8_e4m3fn
FP8_TARGET_MAX = 256.0
FP8_MAX_SHIFT = 48.0

D_SSM = 512
SSM_GROUP = 16
SSM_STATE = 64
N_ATT_HEADS = 8
ATT_HEAD_DIM = 64
D_ATT = N_ATT_HEADS * ATT_HEAD_DIM
MOBA_BLOCK = 256
MOBA_TOPK = 3
PEER_HEADS = 8
PEER_HALF = 128
PEER_N_KEYS = 128
PEER_TOPK = 16
LN_EPS = 1e-5
DEPTH = 1
DN_ALPHA = (2.0 * DEPTH) ** 0.25
NEG = -1e30

LANES = 128
SUBLANES = 8
BF16_ROWS = 2 * SUBLANES
VMEM_LIMIT = 48 * 1024 * 1024
VMEM_LIMIT_PEER = 60 * 1024 * 1024

TM_PROJ = 1024
S5_CHUNK = 8
S5_LANE_GROUPS = LANES // SSM_GROUP
TT_PEER = 512
EB_PEER = 4096


def _dot(a, b):
    return jnp.dot(a, b, preferred_element_type=f32)


def _dot_nt(a, b):
    return lax.dot_general(a, b, (((1,), (1,)), ((), ())), preferred_element_type=f32)


def _split_bf16(x):
    hi = x.astype(bf16)
    lo = (x - hi.astype(f32)).astype(bf16)
    return hi, lo


def _layernorm(r, g, b):
    mu = jnp.mean(r, axis=-1, keepdims=True)
    d = r - mu
    var = jnp.mean(d * d, axis=-1, keepdims=True)
    return d * lax.rsqrt(var + LN_EPS) * g + b


def _inproj_kernel(x_ref, wuk_ref, wqvt_ref, u_ref, k_ref, qt_ref, vt_ref, km_ref):
    xb = x_ref[...].astype(bf16)
    acc = _dot(xb, wuk_ref[...])
    u_ref[...] = acc[:, :D_SSM]
    kk = acc[:, D_SSM:]
    k_ref[...] = kk.astype(bf16)
    qvt = _dot_nt(wqvt_ref[...], xb)
    for r in range(TM_PROJ // MOBA_BLOCK):
        cols = slice(r * MOBA_BLOCK, (r + 1) * MOBA_BLOCK)
        km_ref[r] = jnp.mean(kk[cols], axis=0, keepdims=True)
        qt_ref[r] = qvt[:D_ATT, cols].astype(bf16)
        vt_ref[r] = qvt[D_ATT:, cols].astype(bf16)


def _inproj(x2, w_in):
    n, d_model = x2.shape
    nb = TM_PROJ // MOBA_BLOCK
    w_uk = jnp.concatenate([w_in[:, :D_SSM], w_in[:, D_SSM + D_ATT:D_SSM + 2 * D_ATT]], axis=1).astype(bf16)
    w_qvt = jnp.concatenate([w_in[:, D_SSM:D_SSM + D_ATT], w_in[:, D_SSM + 2 * D_ATT:]], axis=1).T.astype(bf16)
    return pl.pallas_call(
        _inproj_kernel,
        grid=(n // TM_PROJ,),
        in_specs=[pl.BlockSpec((TM_PROJ, d_model), lambda i: (i, 0)),
                  pl.BlockSpec(w_uk.shape, lambda i: (0, 0)),
                  pl.BlockSpec(w_qvt.shape, lambda i: (0, 0))],
        out_specs=[pl.BlockSpec((TM_PROJ, D_SSM), lambda i: (i, 0)),
                   pl.BlockSpec((TM_PROJ, D_ATT), lambda i: (i, 0)),
                   pl.BlockSpec((nb, D_ATT, MOBA_BLOCK), lambda i: (i, 0, 0)),
                   pl.BlockSpec((nb, D_ATT, MOBA_BLOCK), lambda i: (i, 0, 0)),
                   pl.BlockSpec((nb, 1, D_ATT), lambda i: (i, 0, 0))],
        out_shape=[jax.ShapeDtypeStruct((n, D_SSM), f32),
                   jax.ShapeDtypeStruct((n, D_ATT), bf16),
                   jax.ShapeDtypeStruct((n // MOBA_BLOCK, D_ATT, MOBA_BLOCK), bf16),
                   jax.ShapeDtypeStruct((n // MOBA_BLOCK, D_ATT, MOBA_BLOCK), bf16),
                   jax.ShapeDtypeStruct((n // MOBA_BLOCK, 1, D_ATT), f32)],
        compiler_params=pltpu.CompilerParams(dimension_semantics=("parallel",),
                                             vmem_limit_bytes=VMEM_LIMIT),
        name="inproj",
    )(x2, w_uk, w_qvt)


def _s5_operators(a_re, a_im, log_dt, b_re, b_im, c_re, c_im, d_skip):
    T = S5_CHUNK
    G, P = a_re.shape
    H = SSM_GROUP
    nj = G // S5_LANE_GROUPS
    g8 = S5_LANE_GROUPS
    lam = lax.complex(a_re.astype(f32), a_im.astype(f32))
    dt = jnp.exp(log_dt.astype(f32))[:, None]
    lam_bar = jnp.exp(lam * dt)
    b_c = lax.complex(b_re.astype(f32), b_im.astype(f32))
    b_bar = ((lam_bar - 1.0) / lam)[:, :, None] * b_c
    c_c = lax.complex(c_re.astype(f32), c_im.astype(f32))
    pows = jnp.exp((lam * dt)[None] * jnp.arange(T + 1, dtype=f32)[:, None, None])
    def widen(x, reps):
        n = x.shape[-1]
        rep = jnp.tile(jnp.eye(n, dtype=f32), (1, reps))
        return jnp.dot(x.reshape(-1, n), rep).reshape(x.shape[:-1] + (reps * n,))

    same = jnp.arange(g8)[:, None] == (jnp.arange(g8 * H) // H)[None, :]
    same_p = jnp.arange(g8)[:, None] == (jnp.arange(g8 * P) // P)[None, :]
    kern = jnp.real(jnp.einsum('ghp,dgp,gpk->dgkh', c_c, pows[:T], b_bar))
    dd = jnp.arange(T)[None, :] - jnp.arange(T)[:, None]
    kst = jnp.where((dd >= 0)[:, :, None, None, None], kern[jnp.clip(dd, 0, T - 1)], 0.0)
    kst = widen(kst.reshape(T, T, nj, g8, H, H), g8)
    kst = jnp.where(same[None, None, None, :, None, :], kst, 0.0)
    toep = kst.transpose(2, 0, 3, 4, 1, 5).reshape(nj, T * LANES, T * LANES)
    win = pows[T - 1 - jnp.arange(T)][:, :, None, :] * jnp.swapaxes(b_bar, 1, 2)[None]
    win = jnp.stack([jnp.real(win), jnp.imag(win)], 0).reshape(2, T, nj, g8, H, P)
    win = jnp.where(same_p[None, None, None, :, None, :], widen(win, g8), 0.0)
    m_in = win.transpose(2, 1, 3, 4, 0, 5).reshape(nj, T * LANES, 2 * g8 * P)
    wout = jnp.swapaxes(c_c, 1, 2)[None] * pows[1:T + 1][:, :, :, None]
    wout = jnp.stack([jnp.real(wout), -jnp.imag(wout)], 0).reshape(2, T, nj, g8, P, H)
    wout = jnp.where(same[None, None, None, :, None, :], widen(wout, g8), 0.0)
    m_out = wout.transpose(2, 0, 3, 4, 1, 5).reshape(nj, 2 * g8 * P, T * LANES)
    a_chunk = pows[T].reshape(nj, 1, g8 * P)
    a_ri = jnp.concatenate([jnp.real(a_chunk), jnp.imag(a_chunk)], axis=1)
    d_flat = jnp.tile(d_skip.astype(f32).reshape(nj, 1, LANES), (1, 1, T))
    return toep.astype(bf16), m_in.astype(bf16), m_out.astype(bf16), a_ri, d_flat


def _s5_kernel(u_ref, toep_ref, min_ref, mout_ref, a_ref, d_ref, y_ref, in_scr, sp_scr):
    T = S5_CHUNK
    n_c = u_ref.shape[0] // T
    half = in_scr.shape[1] // 2
    uf = jnp.concatenate([u_ref[pl.ds(t, n_c, stride=T), :] for t in range(T)], axis=1)
    ufb = uf.astype(bf16)
    kt = 2 * LANES
    y = jnp.concatenate([_dot(ufb[:, :(n + 1) * kt], toep_ref[0, :(n + 1) * kt, n * kt:(n + 1) * kt])
                         for n in range(T * LANES // kt)], axis=1) + uf * d_ref[0]
    in_scr[...] = _dot(ufb, min_ref[0])
    ar = a_ref[0, 0:1, :]
    ai = a_ref[0, 1:2, :]

    def step(c, carry):
        sr, si = carry
        sp_scr[pl.ds(c, 1), :half] = sr
        sp_scr[pl.ds(c, 1), half:] = si
        ir = in_scr[pl.ds(c, 1), :half]
        ii = in_scr[pl.ds(c, 1), half:]
        return ar * sr - ai * si + ir, ar * si + ai * sr + ii

    z = jnp.zeros((1, half), f32)
    lax.fori_loop(0, n_c, step, (z, z), unroll=4)
    y = y + _dot(sp_scr[...].astype(bf16), mout_ref[0])
    yg = jax.nn.gelu(y)
    for t in range(T):
        y_ref[pl.ds(t, n_c, stride=T), :] = yg[:, t * LANES:(t + 1) * LANES]


def _s5(u2, bsz, seq, ops):
    toep, m_in, m_out, a_ri, d_flat = ops
    nj = toep.shape[0]
    n_c = seq // S5_CHUNK
    kdim = S5_CHUNK * LANES
    sdim = m_in.shape[2]
    return pl.pallas_call(
        _s5_kernel,
        grid=(nj, bsz),
        in_specs=[pl.BlockSpec((seq, LANES), lambda j, b: (b, j)),
                  pl.BlockSpec((1, kdim, kdim), lambda j, b: (j, 0, 0)),
                  pl.BlockSpec((1, kdim, sdim), lambda j, b: (j, 0, 0)),
                  pl.BlockSpec((1, sdim, kdim), lambda j, b: (j, 0, 0)),
                  pl.BlockSpec((1, 2, sdim // 2), lambda j, b: (j, 0, 0)),
                  pl.BlockSpec((1, 1, kdim), lambda j, b: (j, 0, 0))],
        out_specs=pl.BlockSpec((seq, LANES), lambda j, b: (b, j)),
        out_shape=jax.ShapeDtypeStruct(u2.shape, f32),
        scratch_shapes=[pltpu.VMEM((n_c, sdim), f32), pltpu.VMEM((n_c, sdim), f32)],
        compiler_params=pltpu.CompilerParams(dimension_semantics=("parallel", "parallel"),
                                             vmem_limit_bytes=VMEM_LIMIT),
        name="s5",
    )(u2, toep, m_in, m_out, a_ri, d_flat)


def _moba_kernel(qt_lo_ref, qt_hi_ref, k_ref, vt_ref, km_ref, o_lo_ref, o_hi_ref, kaug_ref, s0_scr, s1_scr):
    blk = MOBA_BLOCK
    nb = k_ref.shape[0] // blk
    t = pl.program_id(2)

    @pl.when(t == 0)
    def _build():
        lane = lax.broadcasted_iota(jnp.int32, (blk, LANES), 1)

        def build(j, carry):
            rows = pl.ds(pl.multiple_of(j * blk, blk), blk)
            kaug_ref[rows, :LANES] = k_ref[rows, :]
            kaug_ref[rows, LANES:] = jnp.where(lane == j, 1.0, 0.0).astype(bf16)
            return carry
        lax.fori_loop(0, nb, build, 0)

    _moba_block(t, qt_lo_ref, vt_ref, km_ref, o_lo_ref, kaug_ref, s0_scr, s1_scr)
    _moba_block(nb - 1 - t, qt_hi_ref, vt_ref, km_ref, o_hi_ref, kaug_ref, s0_scr, s1_scr)


def _moba_block(i, qt_ref, vt_ref, km_ref, o_ref, kaug_ref, s0_scr, s1_scr):
    blk = MOBA_BLOCK
    hd = ATT_HEAD_DIM
    nb = kaug_ref.shape[0] // blk
    qt = qt_ref[0]
    qtf = qt.astype(f32) * (hd ** -0.5 * math.log2(math.e))
    km = km_ref[0]
    lane_nb = lax.broadcasted_iota(jnp.int32, (nb, LANES), 1)
    feat = lax.broadcasted_iota(jnp.int32, (LANES, blk), 0)
    blk_id = lax.broadcasted_iota(jnp.int32, (nb, blk), 0)
    blk_f = blk_id.astype(f32)
    q_cols, bias_cols = [], []
    for hh in range(2):
        km_hi, km_lo = _split_bf16(jnp.where((lane_nb < hd) if hh == 0 else (lane_nb >= hd), km, 0.0))
        gate = _dot(km_hi, qt) + _dot(km_lo, qt)
        g = jnp.where(blk_id < i, gate, -jnp.inf)
        sel = blk_id == i
        for _ in range(MOBA_TOPK):
            mx = jnp.max(g, axis=0, keepdims=True)
            first = jnp.min(jnp.where(g == mx, blk_f, float(nb)), axis=0, keepdims=True)
            pick = (blk_f == first) & (mx > -jnp.inf)
            sel = sel | pick
            g = jnp.where(pick, -jnp.inf, g)
        bias_cols.append(jnp.where(sel, 0.0, NEG))
        q_cols.append(jnp.where((feat < hd) if hh == 0 else (feat >= hd), qtf, 0.0))
    qa = jnp.concatenate([jnp.concatenate(q_cols, axis=1), jnp.concatenate(bias_cols, axis=1),
                          jnp.zeros((LANES - nb, 2 * blk), f32)], axis=0).astype(bf16)

    ones_rows = jnp.ones((BF16_ROWS, blk), bf16)
    key_pos = lax.broadcasted_iota(jnp.int32, (blk, 2 * blk), 0)
    q_pos = lax.broadcasted_iota(jnp.int32, (blk, 2 * blk), 1) % blk

    def scores(j):
        j = jnp.minimum(j, nb - 1)
        return _dot(kaug_ref[pl.ds(pl.multiple_of(j * blk, blk), blk), :], qa)

    def update(state, blocks):
        m, acc = state
        m_new = m
        for _, s in blocks:
            m_new = jnp.maximum(m_new, jnp.max(s, axis=0, keepdims=True))
        acc = jnp.exp2(m - m_new) * acc
        for j, s in blocks:
            pb = jnp.exp2(s - m_new).astype(bf16)
            acc = acc + jnp.concatenate(
                [_dot(jnp.concatenate([vt_ref[j, :hd, :], ones_rows], axis=0), pb[:, :blk]),
                 _dot(jnp.concatenate([vt_ref[j, hd:, :], ones_rows], axis=0), pb[:, blk:])], axis=1)
        return m_new, acc

    def pair_scores(p, dst):
        dst[0] = scores(2 * p)
        dst[1] = scores(2 * p + 1)

    def pair_update(state, p, src, masked):
        blocks = []
        for r in range(2):
            j = 2 * p + r
            s = src[r]
            if masked:
                off = jnp.where(j < i, blk, jnp.where(j == i, 0, -2 * blk))
                s = jnp.where(key_pos <= q_pos + off, s, NEG)
            blocks.append((jnp.minimum(j, nb - 1), s))
        return update(state, blocks)

    def quad_body(q, state):
        pair_scores(2 * q + 1, s1_scr)
        state = pair_update(state, 2 * q, s0_scr, False)
        pair_scores(2 * q + 2, s0_scr)
        return pair_update(state, 2 * q + 1, s1_scr, False)

    state = (jnp.full((1, 2 * blk), -jnp.inf, f32), jnp.zeros((hd + BF16_ROWS, 2 * blk), f32))
    n_quad = i // 4
    pair_scores(0, s0_scr)
    state = lax.fori_loop(0, n_quad, quad_body, state)
    pair_scores(2 * n_quad + 1, s1_scr)
    state = pair_update(state, 2 * n_quad, s0_scr, True)
    m, acc = lax.cond(4 * n_quad + 2 <= i,
                      lambda st: pair_update(st, 2 * n_quad + 1, s1_scr, True), lambda st: st, state)
    out = acc[:hd] / acc[hd:hd + 1]
    o_ref[...] = jnp.concatenate([out[:, :blk], out[:, blk:]], axis=0).T.astype(o_ref.dtype)


def _moba(qt, k2, vt, kmean, bsz, seq):
    nb = seq // MOBA_BLOCK
    assert seq % MOBA_BLOCK == 0 and nb % (2 * SUBLANES) == 0 and nb <= LANES
    npair = D_ATT // LANES
    nh = nb // 2
    half_shape = jax.ShapeDtypeStruct((k2.shape[0] // 2, k2.shape[1]), bf16)
    return pl.pallas_call(
        _moba_kernel,
        grid=(bsz, npair, nh),
        in_specs=[pl.BlockSpec((1, LANES, MOBA_BLOCK), lambda b, h, t: (b * nb + t, h, 0)),
                  pl.BlockSpec((1, LANES, MOBA_BLOCK), lambda b, h, t: (b * nb + nb - 1 - t, h, 0)),
                  pl.BlockSpec((seq, LANES), lambda b, h, t: (b, h)),
                  pl.BlockSpec((nb, LANES, MOBA_BLOCK), lambda b, h, t: (b, h, 0)),
                  pl.BlockSpec((1, nb, LANES), lambda b, h, t: (b, 0, h))],
        out_specs=[pl.BlockSpec((MOBA_BLOCK, LANES), lambda b, h, t: (b * nh + t, h)),
                   pl.BlockSpec((MOBA_BLOCK, LANES), lambda b, h, t: (b * nh + nh - 1 - t, h))],
        out_shape=[half_shape, half_shape],
        scratch_shapes=[pltpu.VMEM((seq, 2 * LANES), bf16),
                        pltpu.VMEM((2, MOBA_BLOCK, 2 * MOBA_BLOCK), f32),
                        pltpu.VMEM((2, MOBA_BLOCK, 2 * MOBA_BLOCK), f32)],
        compiler_params=pltpu.CompilerParams(dimension_semantics=("parallel", "parallel", "arbitrary"),
                                             vmem_limit_bytes=VMEM_LIMIT),
        name="moba",
    )(qt, qt, k2, vt, kmean)


def _outproj_kernel(tiles_per_half, x_ref, yg_ref, att_lo_ref, att_hi_ref, wglu_ref, wout_ref, g_ref, b_ref, h_ref):
    first_half = (pl.program_id(0) // tiles_per_half) % 2 == 0
    att = jnp.where(first_half, att_lo_ref[...], att_hi_ref[...])
    yg = yg_ref[...]
    z = yg * jax.nn.sigmoid(_dot(yg.astype(bf16), wglu_ref[...]))
    cat = jnp.concatenate([z.astype(bf16), att], axis=1)
    r = DN_ALPHA * x_ref[...] + _dot(cat, wout_ref[...])
    h_ref[...] = _layernorm(r, g_ref[...], b_ref[...])


def _outproj(x2, yg, att_lo, att_hi, seq, wglu_b, wout_b, g, b):
    n, d_model = x2.shape
    tph = seq // (2 * TM_PROJ)
    assert seq % (2 * TM_PROJ) == 0

    def half_tile(i, second):
        r = i % (2 * tph)
        return (i // (2 * tph)) * tph + (jnp.maximum(r - tph, 0) if second else jnp.minimum(r, tph - 1))

    return pl.pallas_call(
        functools.partial(_outproj_kernel, tph),
        grid=(n // TM_PROJ,),
        in_specs=[pl.BlockSpec((TM_PROJ, d_model), lambda i: (i, 0)),
                  pl.BlockSpec((TM_PROJ, D_SSM), lambda i: (i, 0)),
                  pl.BlockSpec((TM_PROJ, D_ATT), lambda i: (half_tile(i, False), 0)),
                  pl.BlockSpec((TM_PROJ, D_ATT), lambda i: (half_tile(i, True), 0)),
                  pl.BlockSpec(wglu_b.shape, lambda i: (0, 0)),
                  pl.BlockSpec(wout_b.shape, lambda i: (0, 0)),
                  pl.BlockSpec((1, d_model), lambda i: (0, 0)),
                  pl.BlockSpec((1, d_model), lambda i: (0, 0))],
        out_specs=pl.BlockSpec((TM_PROJ, d_model), lambda i: (i, 0)),
        out_shape=jax.ShapeDtypeStruct((n, d_model), f32),
        compiler_params=pltpu.CompilerParams(dimension_semantics=("parallel",),
                                             vmem_limit_bytes=VMEM_LIMIT),
        name="outproj",
    )(x2, yg, att_lo, att_hi, wglu_b, wout_b, g, b)


def _bitonic_sort_desc(v):
    n = len(v)
    k = 2
    while k <= n:
        j = k // 2
        while j >= 1:
            for a in range(n):
                b = a ^ j
                if b > a:
                    hi = jnp.maximum(v[a], v[b])
                    lo = jnp.minimum(v[a], v[b])
                    v[a], v[b] = (hi, lo) if (a & k) == 0 else (lo, hi)
            j //= 2
        k *= 2
    return v


def _bitonic_merge_desc(v):
    j = len(v) // 2
    while j >= 1:
        for a in range(len(v)):
            b = a ^ j
            if b > a:
                v[a], v[b] = jnp.maximum(v[a], v[b]), jnp.minimum(v[a], v[b])
        j //= 2
    return v


def _top16_sorted(st):
    k = PEER_TOPK
    x = _bitonic_sort_desc([st[SUBLANES * a:SUBLANES * (a + 1), :] for a in range(PEER_N_KEYS // SUBLANES)])
    for shift in (4, 2, 1):
        y = [pltpu.roll(x[k - 1 - a], shift, 0) for a in range(k)]
        x = _bitonic_merge_desc([jnp.maximum(x[a], y[a]) for a in range(k)])
    return x


def _dup_bf16(x):
    u = pltpu.bitcast(x.astype(bf16).astype(f32), jnp.uint32)
    return pltpu.bitcast(u | (u >> 16), f32)


def _bf16_rows(row):
    return pltpu.bitcast(jnp.broadcast_to(row, (SUBLANES, row.shape[1])), bf16)[None]


def _peer_kernel(n_exp, us_ref, h_ref, wq_ref, sk_ref, u_ref, vt_ref, g_ref, b_ref, o_ref,
                 q_scr, pr_scr, rq_scr, srt_scr, row_scr, acc_scr, hbt_scr, a_scr, unscale_scr):
    eb = pl.program_id(1)
    n_blk = n_exp // EB_PEER
    tt = h_ref.shape[0]
    nk = PEER_N_KEYS
    k = PEER_TOPK

    chunk = 2 * nk
    n_chunk = EB_PEER // chunk

    def activations():
        for s in range(n_chunk):
            a_scr[s] = _dot(u_ref[s * chunk:(s + 1) * chunk, :], hbt_scr[...])

    @pl.when(eb == 0)
    def _prologue():
        ht = h_ref[...].T
        amax = jnp.maximum(jnp.max(jnp.abs(ht), keepdims=True), jnp.finfo(f32).tiny)
        e_tile = jnp.minimum(jnp.floor(jnp.log2(FP8_TARGET_MAX / amax)), FP8_MAX_SHIFT)
        hbt_scr[...] = (ht * jnp.exp2(e_tile)).astype(fp8)
        unscale_scr[0] = (jnp.exp2(-e_tile) * us_ref[0, 0])[0, 0]
        q = _dot(h_ref[...].astype(bf16), wq_ref[...])
        for hc in range(2 * PEER_HEADS):
            q_scr[hc] = q[:, hc * PEER_HALF:(hc + 1) * PEER_HALF]

        def head_sort(h, carry):
            for c in range(2):
                q_hi, q_lo = _split_bf16(q_scr[2 * h + c])
                k_hi, k_lo = _split_bf16(sk_ref[2 * h + c])
                st = _dot_nt(k_hi, q_hi) + _dot_nt(k_hi, q_lo) + _dot_nt(k_lo, q_hi)
                pr_scr[c, h] = st
                srt = _top16_sorted(st)
                for a in range(k):
                    srt_scr[c, a, pl.ds(h, 1), :] = srt[a][0:1, :]
            return carry
        lax.fori_loop(0, PEER_HEADS, head_sort, 0, unroll=4)

        a_s = [srt_scr[0, a] for a in range(k)]
        b_s = [srt_scr[1, a] for a in range(k)]
        cands = [a_s[a] + b_s[b] for a in range(k) for b in range(k) if (a + 1) * (b + 1) <= k]
        cands += [jnp.full_like(cands[0], -jnp.inf)] * (64 - len(cands))
        cs = _bitonic_sort_desc(cands)
        zsum = jnp.zeros_like(cs[0])
        for a in range(k):
            zsum = zsum + jnp.exp(cs[a] - cs[0])
        row_scr[0] = 0.5 * (cs[k - 1] + cs[k])
        row_scr[1] = 1.0 / zsum
        row_scr[2] = a_s[0]
        row_scr[3] = b_s[0]
        row_scr[4] = a_s[k - 1]
        row_scr[5] = b_s[k - 1]

        def head_prep(h, carry):
            hrow = pl.ds(h, 1)
            at = pr_scr[0, h]
            bt = pr_scr[1, h]
            thr = row_scr[0, hrow, :] - at
            r_cnt = jnp.zeros_like(at)
            b_rank = jnp.zeros_like(bt)
            for a in range(k):
                b_a = srt_scr[1, a, hrow, :]
                r_cnt = jnp.where(b_a > thr, float(a + 1), r_cnt)
                b_rank = jnp.where(b_a > bt, float(a + 1), b_rank)
            in_a = at >= row_scr[4, hrow, :]
            p_w = jnp.where(in_a, jnp.exp(at - row_scr[2, hrow, :]), 0.0) * row_scr[1, hrow, :]
            pr_scr[0, h] = _dup_bf16(p_w)
            pr_scr[1, h] = _dup_bf16(jnp.where(in_a, r_cnt, 0.0))
            q_w = jnp.where(bt >= row_scr[5, hrow, :], jnp.exp(bt - row_scr[3, hrow, :]), 0.0)
            rq_scr[0, h] = b_rank.astype(bf16).reshape(nk // BF16_ROWS, BF16_ROWS, tt)
            rq_scr[1, h] = q_w.astype(bf16).reshape(nk // BF16_ROWS, BF16_ROWS, tt)
            return carry
        lax.fori_loop(0, PEER_HEADS, head_prep, 0)
        acc_scr[...] = jnp.zeros_like(acc_scr)

    activations()
    for s in range(n_chunk):
        halves = []
        for half in range(chunk // nk):
            n1 = pl.ds((EB_PEER // nk) * eb + s * (chunk // nk) + half, 1)
            g = jnp.zeros((nk // BF16_ROWS, BF16_ROWS, tt), bf16)
            for h in range(PEER_HEADS):
                p_row = _bf16_rows(pr_scr[0, h, n1, :])
                r_row = _bf16_rows(pr_scr[1, h, n1, :])
                g = g + p_row * jnp.where(rq_scr[0, h] < r_row, rq_scr[1, h], 0)
            halves.append(g.reshape(nk, tt))
        act = a_scr[s].astype(bf16) * unscale_scr[0].astype(bf16)
        w = jnp.concatenate(halves, axis=0) * jax.nn.gelu(act)
        acc_scr[...] += _dot(vt_ref[:, s * chunk:(s + 1) * chunk], w)

    @pl.when(eb == n_blk - 1)
    def _last():
        r = DN_ALPHA * h_ref[...] + acc_scr[...].T
        o_ref[...] = _layernorm(r, g_ref[...], b_ref[...])


def _peer(h1, wq_b, sub_keys, u_q, u_unscale, vt_b, g, b):
    n, d_model = h1.shape
    n_exp = u_q.shape[0]
    tt = TT_PEER
    sk = sub_keys.reshape(2 * PEER_HEADS, PEER_N_KEYS, PEER_HALF).astype(f32)
    n_blk = n_exp // EB_PEER
    const = dict(pipeline_mode=pl.Buffered(1))
    return pl.pallas_call(
        functools.partial(_peer_kernel, n_exp),
        grid=(n // tt, n_blk),
        in_specs=[pl.BlockSpec(memory_space=pltpu.SMEM),
                  pl.BlockSpec((tt, d_model), lambda t, e: (t, 0)),
                  pl.BlockSpec(wq_b.shape, lambda t, e: (0, 0), **const),
                  pl.BlockSpec(sk.shape, lambda t, e: (0, 0, 0), **const),
                  pl.BlockSpec((EB_PEER, d_model), lambda t, e: (e, 0)),
                  pl.BlockSpec((d_model, EB_PEER), lambda t, e: (0, e)),
                  pl.BlockSpec((1, d_model), lambda t, e: (0, 0), **const),
                  pl.BlockSpec((1, d_model), lambda t, e: (0, 0), **const)],
        out_specs=pl.BlockSpec((tt, d_model), lambda t, e: (t, 0)),
        out_shape=jax.ShapeDtypeStruct((n, d_model), f32),
        scratch_shapes=[pltpu.VMEM((2 * PEER_HEADS, tt, PEER_HALF), f32),
                        pltpu.VMEM((2, PEER_HEADS, PEER_N_KEYS, tt), f32),
                        pltpu.VMEM((2, PEER_HEADS, PEER_N_KEYS // BF16_ROWS, BF16_ROWS, tt), bf16),
                        pltpu.VMEM((2, PEER_TOPK, PEER_HEADS, tt), f32),
                        pltpu.VMEM((6, PEER_HEADS, tt), f32),
                        pltpu.VMEM((d_model, tt), f32),
                        pltpu.VMEM((d_model, tt), fp8),
                        pltpu.VMEM((EB_PEER // (2 * PEER_N_KEYS), 2 * PEER_N_KEYS, tt), f32),
                        pltpu.SMEM((1,), f32)],
        compiler_params=pltpu.CompilerParams(dimension_semantics=("parallel", "arbitrary"),
                                             vmem_limit_bytes=VMEM_LIMIT_PEER),
        name="peer",
    )(u_unscale.reshape(1, 1), h1, wq_b, sk, u_q, vt_b, g, b)


def _quantize_table(t):
    amax = jnp.maximum(jnp.max(jnp.abs(t)), jnp.finfo(f32).tiny)
    e = jnp.minimum(jnp.floor(jnp.log2(FP8_TARGET_MAX / amax)), FP8_MAX_SHIFT)
    return (t * jnp.exp2(e)).astype(fp8), jnp.exp2(-e).astype(f32)


def kernel(x, w_in, ssm_a_re, ssm_a_im, ssm_log_dt, ssm_b_re, ssm_b_im, ssm_c_re, ssm_c_im, ssm_d,
           ssm_w_glu, w_out, ln1_g, ln1_b, peer_w_q, peer_sub_keys, peer_u, peer_v, ln2_g, ln2_b):
    bsz, seq, d_model = x.shape
    n = bsz * seq
    h = x.reshape(n, d_model)
    for l in range(w_in.shape[0]):
        u2, k2, qt, vt, kmean = _inproj(h, w_in[l])
        ops = _s5_operators(ssm_a_re[l], ssm_a_im[l], ssm_log_dt[l], ssm_b_re[l], ssm_b_im[l],
                            ssm_c_re[l], ssm_c_im[l], ssm_d[l])
        yg = _s5(u2, bsz, seq, ops)
        att_lo, att_hi = _moba(qt, k2, vt, kmean.reshape(bsz, seq // MOBA_BLOCK, D_ATT), bsz, seq)
        h1 = _outproj(h, yg, att_lo, att_hi, seq,ssm_w_glu[l].astype(bf16), w_out[l].astype(bf16),
                      ln1_g[l].reshape(1, d_model), ln1_b[l].reshape(1, d_model))
        u_q, u_unscale = _quantize_table(peer_u[l])
        h = _peer(h1, peer_w_q[l].astype(bf16), peer_sub_keys[l], u_q, u_unscale,
                  peer_v[l].T.astype(bf16), ln2_g[l].reshape(1, d_model), ln2_b[l].reshape(1, d_model))
    return h.reshape(bsz, seq, d_model)
```

```python
import functools
import math

import jax
import jax.numpy as jnp
from jax import lax
from jax.experimental import pallas as pl
from jax.experimental.pallas import tpu as pltpu

f32 = jnp.float32
bf16 = jnp.bfloat16
fp8 = jnp.float8_e4m3fn
FP8_TARGET_MAX = 256.0
FP8_MAX_SHIFT = 48.0

D_SSM = 512
SSM_GROUP = 16
SSM_STATE = 64
N_ATT_HEADS = 8
ATT_HEAD_DIM = 64
D_ATT = N_ATT_HEADS * ATT_HEAD_DIM
MOBA_BLOCK = 256
MOBA_TOPK = 3
PEER_HEADS = 8
PEER_HALF = 128
PEER_N_KEYS = 128
PEER_TOPK = 16
LN_EPS = 1e-5
DEPTH = 1
DN_ALPHA = (2.0 * DEPTH) ** 0.25
NEG = -1e30

LANES = 128
SUBLANES = 8
BF16_ROWS = 2 * SUBLANES
VMEM_LIMIT = 48 * 1024 * 1024
VMEM_LIMIT_PEER = 60 * 1024 * 1024

TM_PROJ = 1024
S5_CHUNK = 8
S5_LANE_GROUPS = LANES // SSM_GROUP
TT_PEER = 512
EB_PEER = 4096


def _dot(a, b):
    return jnp.dot(a, b, preferred_element_type=f32)


def _dot_nt(a, b):
    return lax.dot_general(a, b, (((1,), (1,)), ((), ())), preferred_element_type=f32)


def _split_bf16(x):
    hi = x.astype(bf16)
    lo = (x - hi.astype(f32)).astype(bf16)
    return hi, lo


def _layernorm(r, g, b):
    mu = jnp.mean(r, axis=-1, keepdims=True)
    d = r - mu
    var = jnp.mean(d * d, axis=-1, keepdims=True)
    return d * lax.rsqrt(var + LN_EPS) * g + b


def _inproj_kernel(x_ref, wuk_ref, wqvt_ref, u_ref, k_ref, qt_ref, vt_ref, km_ref):
    xb = x_ref[...].astype(bf16)
    acc = _dot(xb, wuk_ref[...])
    u_ref[...] = acc[:, :D_SSM]
    kk = acc[:, D_SSM:]
    k_ref[...] = kk.astype(bf16)
    qvt = _dot_nt(wqvt_ref[...], xb)
    for r in range(TM_PROJ // MOBA_BLOCK):
        cols = slice(r * MOBA_BLOCK, (r + 1) * MOBA_BLOCK)
        km_ref[r] = jnp.mean(kk[cols], axis=0, keepdims=True)
        qt_ref[r] = qvt[:D_ATT, cols].astype(bf16)
        vt_ref[r] = qvt[D_ATT:, cols].astype(bf16)


def _inproj(x2, w_in):
    n, d_model = x2.shape
    nb = TM_PROJ // MOBA_BLOCK
    w_uk = jnp.concatenate([w_in[:, :D_SSM], w_in[:, D_SSM + D_ATT:D_SSM + 2 * D_ATT]], axis=1).astype(bf16)
    w_qvt = jnp.concatenate([w_in[:, D_SSM:D_SSM + D_ATT], w_in[:, D_SSM + 2 * D_ATT:]], axis=1).T.astype(bf16)
    return pl.pallas_call(
        _inproj_kernel,
        grid=(n // TM_PROJ,),
        in_specs=[pl.BlockSpec((TM_PROJ, d_model), lambda i: (i, 0)),
                  pl.BlockSpec(w_uk.shape, lambda i: (0, 0)),
                  pl.BlockSpec(w_qvt.shape, lambda i: (0, 0))],
        out_specs=[pl.BlockSpec((TM_PROJ, D_SSM), lambda i: (i, 0)),
                   pl.BlockSpec((TM_PROJ, D_ATT), lambda i: (i, 0)),
                   pl.BlockSpec((nb, D_ATT, MOBA_BLOCK), lambda i: (i, 0, 0)),
                   pl.BlockSpec((nb, D_ATT, MOBA_BLOCK), lambda i: (i, 0, 0)),
                   pl.BlockSpec((nb, 1, D_ATT), lambda i: (i, 0, 0))],
        out_shape=[jax.ShapeDtypeStruct((n, D_SSM), f32),
                   jax.ShapeDtypeStruct((n, D_ATT), bf16),
                   jax.ShapeDtypeStruct((n // MOBA_BLOCK, D_ATT, MOBA_BLOCK), bf16),
                   jax.ShapeDtypeStruct((n // MOBA_BLOCK, D_ATT, MOBA_BLOCK), bf16),
                   jax.ShapeDtypeStruct((n // MOBA_BLOCK, 1, D_ATT), f32)],
        compiler_params=pltpu.CompilerParams(dimension_semantics=("parallel",),
                                             vmem_limit_bytes=VMEM_LIMIT),
        name="inproj",
    )(x2, w_uk, w_qvt)


def _s5_operators(a_re, a_im, log_dt, b_re, b_im, c_re, c_im, d_skip):
    T = S5_CHUNK
    G, P = a_re.shape
    H = SSM_GROUP
    nj = G // S5_LANE_GROUPS
    g8 = S5_LANE_GROUPS
    lam = lax.complex(a_re.astype(f32), a_im.astype(f32))
    dt = jnp.exp(log_dt.astype(f32))[:, None]
    lam_bar = jnp.exp(lam * dt)
    b_c = lax.complex(b_re.astype(f32), b_im.astype(f32))
    b_bar = ((lam_bar - 1.0) / lam)[:, :, None] * b_c
    c_c = lax.complex(c_re.astype(f32), c_im.astype(f32))
    pows = jnp.exp((lam * dt)[None] * jnp.arange(T + 1, dtype=f32)[:, None, None])
    def widen(x, reps):
        n = x.shape[-1]
        rep = jnp.tile(jnp.eye(n, dtype=f32), (1, reps))
        return jnp.dot(x.reshape(-1, n), rep).reshape(x.shape[:-1] + (reps * n,))

    same = jnp.arange(g8)[:, None] == (jnp.arange(g8 * H) // H)[None, :]
    same_p = jnp.arange(g8)[:, None] == (jnp.arange(g8 * P) // P)[None, :]
    kern = jnp.real(jnp.einsum('ghp,dgp,gpk->dgkh', c_c, pows[:T], b_bar))
    dd = jnp.arange(T)[None, :] - jnp.arange(T)[:, None]
    kst = jnp.where((dd >= 0)[:, :, None, None, None], kern[jnp.clip(dd, 0, T - 1)], 0.0)
    kst = widen(kst.reshape(T, T, nj, g8, H, H), g8)
    kst = jnp.where(same[None, None, None, :, None, :], kst, 0.0)
    toep = kst.transpose(2, 0, 3, 4, 1, 5).reshape(nj, T * LANES, T * LANES)
    win = pows[T - 1 - jnp.arange(T)][:, :, None, :] * jnp.swapaxes(b_bar, 1, 2)[None]
    win = jnp.stack([jnp.real(win), jnp.imag(win)], 0).reshape(2, T, nj, g8, H, P)
    win = jnp.where(same_p[None, None, None, :, None, :], widen(win, g8), 0.0)
    m_in = win.transpose(2, 1, 3, 4, 0, 5).reshape(nj, T * LANES, 2 * g8 * P)
    wout = jnp.swapaxes(c_c, 1, 2)[None] * pows[1:T + 1][:, :, :, None]
    wout = jnp.stack([jnp.real(wout), -jnp.imag(wout)], 0).reshape(2, T, nj, g8, P, H)
    wout = jnp.where(same[None, None, None, :, None, :], widen(wout, g8), 0.0)
    m_out = wout.transpose(2, 0, 3, 4, 1, 5).reshape(nj, 2 * g8 * P, T * LANES)
    a_chunk = pows[T].reshape(nj, 1, g8 * P)
    a_ri = jnp.concatenate([jnp.real(a_chunk), jnp.imag(a_chunk)], axis=1)
    d_flat = jnp.tile(d_skip.astype(f32).reshape(nj, 1, LANES), (1, 1, T))
    return toep.astype(bf16), m_in.astype(bf16), m_out.astype(bf16), a_ri, d_flat


def _s5_kernel(u_ref, toep_ref, min_ref, mout_ref, a_ref, d_ref, y_ref, in_scr, sp_scr):
    T = S5_CHUNK
    n_c = u_ref.shape[0] // T
    half = in_scr.shape[1] // 2
    uf = jnp.concatenate([u_ref[pl.ds(t, n_c, stride=T), :] for t in range(T)], axis=1)
    ufb = uf.astype(bf16)
    kt = 2 * LANES
    y = jnp.concatenate([_dot(ufb[:, :(n + 1) * kt], toep_ref[0, :(n + 1) * kt, n * kt:(n + 1) * kt])
                         for n in range(T * LANES // kt)], axis=1) + uf * d_ref[0]
    in_scr[...] = _dot(ufb, min_ref[0])
    ar = a_ref[0, 0:1, :]
    ai = a_ref[0, 1:2, :]

    def step(c, carry):
        sr, si = carry
        sp_scr[pl.ds(c, 1), :half] = sr
        sp_scr[pl.ds(c, 1), half:] = si
        ir = in_scr[pl.ds(c, 1), :half]
        ii = in_scr[pl.ds(c, 1), half:]
        return ar * sr - ai * si + ir, ar * si + ai * sr + ii

    z = jnp.zeros((1, half), f32)
    lax.fori_loop(0, n_c, step, (z, z), unroll=4)
    y = y + _dot(sp_scr[...].astype(bf16), mout_ref[0])
    yg = jax.nn.gelu(y)
    for t in range(T):
        y_ref[pl.ds(t, n_c, stride=T), :] = yg[:, t * LANES:(t + 1) * LANES]


def _s5(u2, bsz, seq, ops):
    toep, m_in, m_out, a_ri, d_flat = ops
    nj = toep.shape[0]
    n_c = seq // S5_CHUNK
    kdim = S5_CHUNK * LANES
    sdim = m_in.shape[2]
    return pl.pallas_call(
        _s5_kernel,
        grid=(nj, bsz),
        in_specs=[pl.BlockSpec((seq, LANES), lambda j, b: (b, j)),
                  pl.BlockSpec((1, kdim, kdim), lambda j, b: (j, 0, 0)),
                  pl.BlockSpec((1, kdim, sdim), lambda j, b: (j, 0, 0)),
                  pl.BlockSpec((1, sdim, kdim), lambda j, b: (j, 0, 0)),
                  pl.BlockSpec((1, 2, sdim // 2), lambda j, b: (j, 0, 0)),
                  pl.BlockSpec((1, 1, kdim), lambda j, b: (j, 0, 0))],
        out_specs=pl.BlockSpec((seq, LANES), lambda j, b: (b, j)),
        out_shape=jax.ShapeDtypeStruct(u2.shape, f32),
        scratch_shapes=[pltpu.VMEM((n_c, sdim), f32), pltpu.VMEM((n_c, sdim), f32)],
        compiler_params=pltpu.CompilerParams(dimension_semantics=("parallel", "parallel"),
                                             vmem_limit_bytes=VMEM_LIMIT),
        name="s5",
    )(u2, toep, m_in, m_out, a_ri, d_flat)


def _moba_kernel(q0_ref, q1_ref, q2_ref, q3_ref, k_ref, vt_ref, km_ref, o0_ref, o1_ref, o2_ref, o3_ref,
                 kaug_ref, s0_scr, s1_scr):
    blk = MOBA_BLOCK
    nb = k_ref.shape[0] // blk
    t = pl.program_id(2)

    @pl.when(t == 0)
    def _build():
        lane = lax.broadcasted_iota(jnp.int32, (blk, LANES), 1)

        def build(j, carry):
            rows = pl.ds(pl.multiple_of(j * blk, blk), blk)
            kaug_ref[rows, :LANES] = k_ref[rows, :]
            kaug_ref[rows, LANES:] = jnp.where(lane == j, 1.0, 0.0).astype(bf16)
            return carry
        lax.fori_loop(0, nb, build, 0)

    for i, q_ref, o_ref in ((t, q0_ref, o0_ref), (nb // 2 - 1 - t, q1_ref, o1_ref),
                            (nb // 2 + t, q2_ref, o2_ref), (nb - 1 - t, q3_ref, o3_ref)):
        _moba_block(i, q_ref, vt_ref, km_ref, o_ref, kaug_ref, s0_scr, s1_scr)


def _moba_block(i, qt_ref, vt_ref, km_ref, o_ref, kaug_ref, s0_scr, s1_scr):
    blk = MOBA_BLOCK
    hd = ATT_HEAD_DIM
    nb = kaug_ref.shape[0] // blk
    qt = qt_ref[0]
    qtf = qt.astype(f32) * (hd ** -0.5 * math.log2(math.e))
    km = km_ref[0]
    lane_nb = lax.broadcasted_iota(jnp.int32, (nb, LANES), 1)
    feat = lax.broadcasted_iota(jnp.int32, (LANES, blk), 0)
    blk_id = lax.broadcasted_iota(jnp.int32, (nb, blk), 0)
    blk_f = blk_id.astype(f32)
    q_cols, bias_cols = [], []
    for hh in range(2):
        km_hi, km_lo = _split_bf16(jnp.where((lane_nb < hd) if hh == 0 else (lane_nb >= hd), km, 0.0))
        gate = _dot(km_hi, qt) + _dot(km_lo, qt)
        g = jnp.where(blk_id < i, gate, -jnp.inf)
        sel = blk_id == i
        for _ in range(MOBA_TOPK):
            mx = jnp.max(g, axis=0, keepdims=True)
            first = jnp.min(jnp.where(g == mx, blk_f, float(nb)), axis=0, keepdims=True)
            pick = (blk_f == first) & (mx > -jnp.inf)
            sel = sel | pick
            g = jnp.where(pick, -jnp.inf, g)
        bias_cols.append(jnp.where(sel, 0.0, NEG))
        q_cols.append(jnp.where((feat < hd) if hh == 0 else (feat >= hd), qtf, 0.0))
    qa = jnp.concatenate([jnp.concatenate(q_cols, axis=1), jnp.concatenate(bias_cols, axis=1),
                          jnp.zeros((LANES - nb, 2 * blk), f32)], axis=0).astype(bf16)

    ones_rows = jnp.ones((BF16_ROWS, blk), bf16)
    key_pos = lax.broadcasted_iota(jnp.int32, (blk, 2 * blk), 0)
    q_pos = lax.broadcasted_iota(jnp.int32, (blk, 2 * blk), 1) % blk

    def scores(j):
        j = jnp.minimum(j, nb - 1)
        return _dot(kaug_ref[pl.ds(pl.multiple_of(j * blk, blk), blk), :], qa)

    def update(state, blocks):
        m, acc = state
        m_new = m
        for _, s in blocks:
            m_new = jnp.maximum(m_new, jnp.max(s, axis=0, keepdims=True))
        acc = jnp.exp2(m - m_new) * acc
        for j, s in blocks:
            pb = jnp.exp2(s - m_new).astype(bf16)
            acc = acc + jnp.concatenate(
                [_dot(jnp.concatenate([vt_ref[j, :hd, :], ones_rows], axis=0), pb[:, :blk]),
                 _dot(jnp.concatenate([vt_ref[j, hd:, :], ones_rows], axis=0), pb[:, blk:])], axis=1)
        return m_new, acc

    def pair_scores(p, dst):
        dst[0] = scores(2 * p)
        dst[1] = scores(2 * p + 1)

    def pair_update(state, p, src, masked):
        blocks = []
        for r in range(2):
            j = 2 * p + r
            s = src[r]
            if masked:
                off = jnp.where(j < i, blk, jnp.where(j == i, 0, -2 * blk))
                s = jnp.where(key_pos <= q_pos + off, s, NEG)
            blocks.append((jnp.minimum(j, nb - 1), s))
        return update(state, blocks)

    def quad_body(q, state):
        pair_scores(2 * q + 1, s1_scr)
        state = pair_update(state, 2 * q, s0_scr, False)
        pair_scores(2 * q + 2, s0_scr)
        return pair_update(state, 2 * q + 1, s1_scr, False)

    state = (jnp.full((1, 2 * blk), -jnp.inf, f32), jnp.zeros((hd + BF16_ROWS, 2 * blk), f32))
    n_quad = i // 4
    pair_scores(0, s0_scr)
    state = lax.fori_loop(0, n_quad, quad_body, state)
    pair_scores(2 * n_quad + 1, s1_scr)
    state = pair_update(state, 2 * n_quad, s0_scr, True)
    m, acc = lax.cond(4 * n_quad + 2 <= i,
                      lambda st: pair_update(st, 2 * n_quad + 1, s1_scr, True), lambda st: st, state)
    out = acc[:hd] / acc[hd:hd + 1]
    o_ref[...] = jnp.concatenate([out[:, :blk], out[:, blk:]], axis=0).T.astype(o_ref.dtype)


def _moba(qt, k2, vt, kmean, bsz, seq):
    nb = seq // MOBA_BLOCK
    assert seq % MOBA_BLOCK == 0 and nb % (4 * SUBLANES) == 0 and nb <= LANES
    npair = D_ATT // LANES
    nq = nb // 4
    quarter = jax.ShapeDtypeStruct((k2.shape[0] // 4, k2.shape[1]), bf16)
    fwd = lambda b, h, t: (b * nq + t, h)
    bwd = lambda b, h, t: (b * nq + nq - 1 - t, h)
    return pl.pallas_call(
        _moba_kernel,
        grid=(bsz, npair, nq),
        in_specs=[pl.BlockSpec((1, LANES, MOBA_BLOCK), lambda b, h, t: (b * nb + t, h, 0)),
                  pl.BlockSpec((1, LANES, MOBA_BLOCK), lambda b, h, t: (b * nb + nb // 2 - 1 - t, h, 0)),
                  pl.BlockSpec((1, LANES, MOBA_BLOCK), lambda b, h, t: (b * nb + nb // 2 + t, h, 0)),
                  pl.BlockSpec((1, LANES, MOBA_BLOCK), lambda b, h, t: (b * nb + nb - 1 - t, h, 0)),
                  pl.BlockSpec((seq, LANES), lambda b, h, t: (b, h)),
                  pl.BlockSpec((nb, LANES, MOBA_BLOCK), lambda b, h, t: (b, h, 0)),
                  pl.BlockSpec((1, nb, LANES), lambda b, h, t: (b, 0, h))],
        out_specs=[pl.BlockSpec((MOBA_BLOCK, LANES), fwd), pl.BlockSpec((MOBA_BLOCK, LANES), bwd),
                   pl.BlockSpec((MOBA_BLOCK, LANES), fwd), pl.BlockSpec((MOBA_BLOCK, LANES), bwd)],
        out_shape=[quarter] * 4,
        scratch_shapes=[pltpu.VMEM((seq, 2 * LANES), bf16),
                        pltpu.VMEM((2, MOBA_BLOCK, 2 * MOBA_BLOCK), f32),
                        pltpu.VMEM((2, MOBA_BLOCK, 2 * MOBA_BLOCK), f32)],
        compiler_params=pltpu.CompilerParams(dimension_semantics=("parallel", "parallel", "arbitrary"),
                                             vmem_limit_bytes=VMEM_LIMIT),
        name="moba",
    )(qt, qt, qt, qt, k2, vt, kmean)


def _outproj_kernel(tiles_per_quarter, x_ref, yg_ref, a0_ref, a1_ref, a2_ref, a3_ref, wglu_ref, wout_ref,
                    g_ref, b_ref, h_ref):
    qtr = (pl.program_id(0) // tiles_per_quarter) % 4
    att = jnp.where(qtr < 2, jnp.where(qtr == 0, a0_ref[...], a1_ref[...]),
                    jnp.where(qtr == 2, a2_ref[...], a3_ref[...]))
    yg = yg_ref[...]
    z = yg * jax.nn.sigmoid(_dot(yg.astype(bf16), wglu_ref[...]))
    cat = jnp.concatenate([z.astype(bf16), att], axis=1)
    r = DN_ALPHA * x_ref[...] + _dot(cat, wout_ref[...])
    h_ref[...] = _layernorm(r, g_ref[...], b_ref[...])


def _outproj(x2, yg, att_quarters, seq, wglu_b, wout_b, g, b):
    n, d_model = x2.shape
    tpq = seq // (4 * TM_PROJ)
    assert seq % (4 * TM_PROJ) == 0

    def quarter_tile(q):
        def index(i):
            r = i % (4 * tpq) - q * tpq
            return (i // (4 * tpq)) * tpq + jnp.clip(r, 0, tpq - 1), 0
        return index

    return pl.pallas_call(
        functools.partial(_outproj_kernel, tpq),
        grid=(n // TM_PROJ,),
        in_specs=[pl.BlockSpec((TM_PROJ, d_model), lambda i: (i, 0)),
                  pl.BlockSpec((TM_PROJ, D_SSM), lambda i: (i, 0)),
                  pl.BlockSpec((TM_PROJ, D_ATT), quarter_tile(0)),
                  pl.BlockSpec((TM_PROJ, D_ATT), quarter_tile(1)),
                  pl.BlockSpec((TM_PROJ, D_ATT), quarter_tile(2)),
                  pl.BlockSpec((TM_PROJ, D_ATT), quarter_tile(3)),
                  pl.BlockSpec(wglu_b.shape, lambda i: (0, 0)),
                  pl.BlockSpec(wout_b.shape, lambda i: (0, 0)),
                  pl.BlockSpec((1, d_model), lambda i: (0, 0)),
                  pl.BlockSpec((1, d_model), lambda i: (0, 0))],
        out_specs=pl.BlockSpec((TM_PROJ, d_model), lambda i: (i, 0)),
        out_shape=jax.ShapeDtypeStruct((n, d_model), f32),
        compiler_params=pltpu.CompilerParams(dimension_semantics=("parallel",),
                                             vmem_limit_bytes=VMEM_LIMIT),
        name="outproj",
    )(x2, yg, *att_quarters, wglu_b, wout_b, g, b)


def _bitonic_sort_desc(v):
    n = len(v)
    k = 2
    while k <= n:
        j = k // 2
        while j >= 1:
            for a in range(n):
                b = a ^ j
                if b > a:
                    hi = jnp.maximum(v[a], v[b])
                    lo = jnp.minimum(v[a], v[b])
                    v[a], v[b] = (hi, lo) if (a & k) == 0 else (lo, hi)
            j //= 2
        k *= 2
    return v


def _bitonic_merge_desc(v):
    j = len(v) // 2
    while j >= 1:
        for a in range(len(v)):
            b = a ^ j
            if b > a:
                v[a], v[b] = jnp.maximum(v[a], v[b]), jnp.minimum(v[a], v[b])
        j //= 2
    return v


def _top16_sorted(st):
    k = PEER_TOPK
    x = _bitonic_sort_desc([st[SUBLANES * a:SUBLANES * (a + 1), :] for a in range(PEER_N_KEYS // SUBLANES)])
    for shift in (4, 2, 1):
        y = [pltpu.roll(x[k - 1 - a], shift, 0) for a in range(k)]
        x = _bitonic_merge_desc([jnp.maximum(x[a], y[a]) for a in range(k)])
    return x


def _dup_bf16(x):
    u = pltpu.bitcast(x.astype(bf16).astype(f32), jnp.uint32)
    return pltpu.bitcast(u | (u >> 16), f32)


def _bf16_rows(row):
    return pltpu.bitcast(jnp.broadcast_to(row, (SUBLANES, row.shape[1])), bf16)[None]


def _peer_kernel(n_exp, us_ref, h_ref, wq_ref, sk_ref, u_ref, vt_ref, g_ref, b_ref, o_ref,
                 q_scr, pr_scr, rq_scr, srt_scr, row_scr, acc_scr, hbt_scr, a_scr, unscale_scr):
    eb = pl.program_id(1)
    n_blk = n_exp // EB_PEER
    tt = h_ref.shape[0]
    nk = PEER_N_KEYS
    k = PEER_TOPK

    chunk = 2 * nk
    n_chunk = EB_PEER // chunk

    def activations():
        for s in range(n_chunk):
            a_scr[s] = _dot(u_ref[s * chunk:(s + 1) * chunk, :], hbt_scr[...])

    @pl.when(eb == 0)
    def _prologue():
        ht = h_ref[...].T
        amax = jnp.maximum(jnp.max(jnp.abs(ht), keepdims=True), jnp.finfo(f32).tiny)
        e_tile = jnp.minimum(jnp.floor(jnp.log2(FP8_TARGET_MAX / amax)), FP8_MAX_SHIFT)
        hbt_scr[...] = (ht * jnp.exp2(e_tile)).astype(fp8)
        unscale_scr[0] = (jnp.exp2(-e_tile) * us_ref[0, 0])[0, 0]
        q = _dot(h_ref[...].astype(bf16), wq_ref[...])
        for hc in range(2 * PEER_HEADS):
            q_scr[hc] = q[:, hc * PEER_HALF:(hc + 1) * PEER_HALF]

        def head_sort(h, carry):
            for c in range(2):
                q_hi, q_lo = _split_bf16(q_scr[2 * h + c])
                k_hi, k_lo = _split_bf16(sk_ref[2 * h + c])
                st = _dot_nt(k_hi, q_hi) + _dot_nt(k_hi, q_lo) + _dot_nt(k_lo, q_hi)
                pr_scr[c, h] = st
                srt = _top16_sorted(st)
                for a in range(k):
                    srt_scr[c, a, pl.ds(h, 1), :] = srt[a][0:1, :]
            return carry
        lax.fori_loop(0, PEER_HEADS, head_sort, 0, unroll=4)

        a_s = [srt_scr[0, a] for a in range(k)]
        b_s = [srt_scr[1, a] for a in range(k)]
        cands = [a_s[a] + b_s[b] for a in range(k) for b in range(k) if (a + 1) * (b + 1) <= k]
        cands += [jnp.full_like(cands[0], -jnp.inf)] * (64 - len(cands))
        cs = _bitonic_sort_desc(cands)
        zsum = jnp.zeros_like(cs[0])
        for a in range(k):
            zsum = zsum + jnp.exp(cs[a] - cs[0])
        row_scr[0] = 0.5 * (cs[k - 1] + cs[k])
        row_scr[1] = 1.0 / zsum
        row_scr[2] = a_s[0]
        row_scr[3] = b_s[0]
        row_scr[4] = a_s[k - 1]
        row_scr[5] = b_s[k - 1]

        def head_prep(h, carry):
            hrow = pl.ds(h, 1)
            at = pr_scr[0, h]
            bt = pr_scr[1, h]
            thr = row_scr[0, hrow, :] - at
            r_cnt = jnp.zeros_like(at)
            b_rank = jnp.zeros_like(bt)
            for a in range(k):
                b_a = srt_scr[1, a, hrow, :]
                r_cnt = jnp.where(b_a > thr, float(a + 1), r_cnt)
                b_rank = jnp.where(b_a > bt, float(a + 1), b_rank)
            in_a = at >= row_scr[4, hrow, :]
            p_w = jnp.where(in_a, jnp.exp(at - row_scr[2, hrow, :]), 0.0) * row_scr[1, hrow, :]
            pr_scr[0, h] = _dup_bf16(p_w)
            pr_scr[1, h] = _dup_bf16(jnp.where(in_a, r_cnt, 0.0))
            q_w = jnp.where(bt >= row_scr[5, hrow, :], jnp.exp(bt - row_scr[3, hrow, :]), 0.0)
            rq_scr[0, h] = b_rank.astype(bf16).reshape(nk // BF16_ROWS, BF16_ROWS, tt)
            rq_scr[1, h] = q_w.astype(bf16).reshape(nk // BF16_ROWS, BF16_ROWS, tt)
            return carry
        lax.fori_loop(0, PEER_HEADS, head_prep, 0)
        acc_scr[...] = jnp.zeros_like(acc_scr)

    activations()
    for s in range(n_chunk):
        halves = []
        for half in range(chunk // nk):
            n1 = pl.ds((EB_PEER // nk) * eb + s * (chunk // nk) + half, 1)
            g = jnp.zeros((nk // BF16_ROWS, BF16_ROWS, tt), bf16)
            for h in range(PEER_HEADS):
                p_row = _bf16_rows(pr_scr[0, h, n1, :])
                r_row = _bf16_rows(pr_scr[1, h, n1, :])
                g = g + p_row * jnp.where(rq_scr[0, h] < r_row, rq_scr[1, h], 0)
            halves.append(g.reshape(nk, tt))
        act = a_scr[s].astype(bf16) * unscale_scr[0].astype(bf16)
        w = jnp.concatenate(halves, axis=0) * jax.nn.gelu(act)
        acc_scr[...] += _dot(vt_ref[:, s * chunk:(s + 1) * chunk], w)

    @pl.when(eb == n_blk - 1)
    def _last():
        r = DN_ALPHA * h_ref[...] + acc_scr[...].T
        o_ref[...] = _layernorm(r, g_ref[...], b_ref[...])


def _peer(h1, wq_b, sub_keys, u_q, u_unscale, vt_b, g, b):
    n, d_model = h1.shape
    n_exp = u_q.shape[0]
    tt = TT_PEER
    sk = sub_keys.reshape(2 * PEER_HEADS, PEER_N_KEYS, PEER_HALF).astype(f32)
    n_blk = n_exp // EB_PEER
    const = dict(pipeline_mode=pl.Buffered(1))
    return pl.pallas_call(
        functools.partial(_peer_kernel, n_exp),
        grid=(n // tt, n_blk),
        in_specs=[pl.BlockSpec(memory_space=pltpu.SMEM),
                  pl.BlockSpec((tt, d_model), lambda t, e: (t, 0)),
                  pl.BlockSpec(wq_b.shape, lambda t, e: (0, 0), **const),
                  pl.BlockSpec(sk.shape, lambda t, e: (0, 0, 0), **const),
                  pl.BlockSpec((EB_PEER, d_model), lambda t, e: (e, 0)),
                  pl.BlockSpec((d_model, EB_PEER), lambda t, e: (0, e)),
                  pl.BlockSpec((1, d_model), lambda t, e: (0, 0), **const),
                  pl.BlockSpec((1, d_model), lambda t, e: (0, 0), **const)],
        out_specs=pl.BlockSpec((tt, d_model), lambda t, e: (t, 0)),
        out_shape=jax.ShapeDtypeStruct((n, d_model), f32),
        scratch_shapes=[pltpu.VMEM((2 * PEER_HEADS, tt, PEER_HALF), f32),
                        pltpu.VMEM((2, PEER_HEADS, PEER_N_KEYS, tt), f32),
                        pltpu.VMEM((2, PEER_HEADS, PEER_N_KEYS // BF16_ROWS, BF16_ROWS, tt), bf16),
                        pltpu.VMEM((2, PEER_TOPK, PEER_HEADS, tt), f32),
                        pltpu.VMEM((6, PEER_HEADS, tt), f32),
                        pltpu.VMEM((d_model, tt), f32),
                        pltpu.VMEM((d_model, tt), fp8),
                        pltpu.VMEM((EB_PEER // (2 * PEER_N_KEYS), 2 * PEER_N_KEYS, tt), f32),
                        pltpu.SMEM((1,), f32)],
        compiler_params=pltpu.CompilerParams(dimension_semantics=("parallel", "arbitrary"),
                                             vmem_limit_bytes=VMEM_LIMIT_PEER),
        name="peer",
    )(u_unscale.reshape(1, 1), h1, wq_b, sk, u_q, vt_b, g, b)


def _quantize_table(t):
    amax = jnp.maximum(jnp.max(jnp.abs(t)), jnp.finfo(f32).tiny)
    e = jnp.minimum(jnp.floor(jnp.log2(FP8_TARGET_MAX / amax)), FP8_MAX_SHIFT)
    return (t * jnp.exp2(e)).astype(fp8), jnp.exp2(-e).astype(f32)


def kernel(x, w_in, ssm_a_re, ssm_a_im, ssm_log_dt, ssm_b_re, ssm_b_im, ssm_c_re, ssm_c_im, ssm_d,
           ssm_w_glu, w_out, ln1_g, ln1_b, peer_w_q, peer_sub_keys, peer_u, peer_v, ln2_g, ln2_b):
    bsz, seq, d_model = x.shape
    n = bsz * seq
    h = x.reshape(n, d_model)
    for l in range(w_in.shape[0]):
        u2, k2, qt, vt, kmean = _inproj(h, w_in[l])
        ops = _s5_operators(ssm_a_re[l], ssm_a_im[l], ssm_log_dt[l], ssm_b_re[l], ssm_b_im[l],
                            ssm_c_re[l], ssm_c_im[l], ssm_d[l])
        yg = _s5(u2, bsz, seq, ops)
        att_quarters = _moba(qt, k2, vt, kmean.reshape(bsz, seq // MOBA_BLOCK, D_ATT), bsz, seq)
        h1 = _outproj(h, yg, att_quarters, seq,ssm_w_glu[l].astype(bf16), w_out[l].astype(bf16),
                      ln1_g[l].reshape(1, d_model), ln1_b[l].reshape(1, d_model))
        u_q, u_unscale = _quantize_table(peer_u[l])
        h = _peer(h1, peer_w_q[l].astype(bf16), peer_sub_keys[l], u_q, u_unscale,
                  peer_v[l].T.astype(bf16), ln2_g[l].reshape(1, d_model), ln2_b[l].reshape(1, d_model))
    return h.reshape(bsz, seq, d_model)
```
